```python
import math
import jax, jax.numpy as jnp
from jax import lax
import numpy as np

D_MODEL = 2048
BATCH = 4
SEQ = 2048
DEPTH = 4

ATTN_PATTERNS = ((128, 1), (512, 4), (2048, 16))
N_ATTN_GROUPS = len(ATTN_PATTERNS)
HEADS_PER_GROUP = 8
HEAD_DIM = 128
ATTN_QKV_WIDTH = N_ATTN_GROUPS * HEADS_PER_GROUP * HEAD_DIM
ATTN_OUT = HEADS_PER_GROUP * HEAD_DIM
BLOCK = 128

SSM_GROUP_CH = 16
SSM_STATE = 64
SSM_WIDTH = D_MODEL // 2
SSM_GROUPS = SSM_WIDTH // SSM_GROUP_CH

IN_COLS = 3 * ATTN_QKV_WIDTH + SSM_WIDTH + 2 * D_MODEL

D_FF = ((8 * D_MODEL // 3 + 255) // 256) * 256
N_EXPERTS = 8
TOP_K = 2
D_FF_EXPERT = D_FF // 2
N_DENSE = (DEPTH + 1) // 2
N_MOE = DEPTH // 2

RMS_EPS = 1e-6

kernel_name = "hybrid_dilated_attn_s5_moe_adaln"


def rmsnorm(x, g):
    xf = x.astype(jnp.float32)
    y = xf * lax.rsqrt(jnp.mean(xf * xf, axis=-1, keepdims=True) + RMS_EPS)
    return (y * g.astype(jnp.float32)).astype(x.dtype)


def modulate(h, shift, scale):
    return h * (1 + scale[:, None, :]) + shift[:, None, :]


def alibi_slopes(n_heads):
    return 2.0 ** (-8.0 * (jnp.arange(n_heads, dtype=jnp.float32) + 1.0) / n_heads)


def dilated_window_attention(q, k, v, window, dilation, slopes):
    b, s, h, e = q.shape
    n_w = window // dilation
    assert n_w <= BLOCK
    L = s // dilation
    nb = -(-L // BLOCK)
    lp = nb * BLOCK

    def to_strided(t):
        return t.astype(jnp.float32).reshape(b, L, dilation, h, e).transpose(0, 2, 1, 3, 4)

    qs, ks, vs = to_strided(q), to_strided(k), to_strided(v)
    qb = jnp.pad(qs, ((0, 0), (0, 0), (0, lp - L), (0, 0), (0, 0))).reshape(b, dilation, nb, BLOCK, h, e)

    def key_blocks(t):
        tp = jnp.pad(t, ((0, 0), (0, 0), (BLOCK, lp - L), (0, 0), (0, 0))).reshape(b, dilation, nb + 1, BLOCK, h, e)
        return jnp.concatenate([tp[:, :, :-1], tp[:, :, 1:]], axis=3)

    kb, vb = key_blocks(ks), key_blocks(vs)
    q_idx = jnp.arange(nb)[:, None] * BLOCK + jnp.arange(BLOCK)[None, :]
    k_idx = jnp.arange(nb)[:, None] * BLOCK - BLOCK + jnp.arange(2 * BLOCK)[None, :]
    dist = q_idx[:, :, None] - k_idx[:, None, :]
    valid = (dist >= 0) & (dist <= n_w) & (k_idx[:, None, :] >= 0)
    penalty = slopes[None, :, None, None] * (dist * dilation).astype(jnp.float32)[:, None]

    scores = jnp.einsum('brnqhe,brnkhe->brnhqk', qb, kb) * (e ** -0.5)
    scores = jnp.where(valid[:, None], scores - penalty, -jnp.inf)
    m = jnp.max(scores, axis=-1, keepdims=True)
    p = jnp.exp(scores - m)
    l = jnp.sum(p, axis=-1, keepdims=True)
    lse = (m + jnp.log(l))[..., 0].transpose(0, 1, 2, 4, 3)
    o = jnp.einsum('brnhqk,brnkhe->brnqhe', p, vb) / l[..., 0].transpose(0, 1, 2, 4, 3)[..., None]

    o = o.reshape(b, dilation, lp, h, e)[:, :, :L].transpose(0, 2, 1, 3, 4).reshape(b, s, h, e)
    lse = lse.reshape(b, dilation, lp, h)[:, :, :L].transpose(0, 2, 1, 3).reshape(b, s, h)
    return o, lse


def dilation_mixture_attention(q, k, v):
    slopes = alibi_slopes(HEADS_PER_GROUP)
    outs, lses = [], []
    for gi, (window, dilation) in enumerate(ATTN_PATTERNS):
        o, lse = dilated_window_attention(q[:, :, gi], k[:, :, gi], v[:, :, gi], window, dilation, slopes)
        outs.append(o)
        lses.append(lse)
    w = jax.nn.softmax(jnp.stack(lses, axis=0), axis=0)
    o = jnp.sum(w[..., None] * jnp.stack(outs, axis=0), axis=0)
    b, s = o.shape[:2]
    return o.reshape(b, s, ATTN_OUT)


def _ssm_combine(left, right):
    a_l, b_l = left
    a_r, b_r = right
    return a_l * a_r, a_r * b_l + b_r


def s5_ssm(u, a_re, a_im, log_dt, b_re, b_im, c_re, c_im, d_skip, w_glu, b_glu):
    bsz, s, _ = u.shape
    uf = u.astype(jnp.float32).reshape(bsz, s, SSM_GROUPS, SSM_GROUP_CH)
    lam = lax.complex(a_re.astype(jnp.float32), a_im.astype(jnp.float32))
    dt = jnp.exp(log_dt.astype(jnp.float32))[:, None]
    a_bar = jnp.exp(lam * dt)
    b_mat = lax.complex(b_re.astype(jnp.float32), b_im.astype(jnp.float32))
    b_bar = ((a_bar - 1.0) / lam)[:, :, None] * b_mat
    c_mat = lax.complex(c_re.astype(jnp.float32), c_im.astype(jnp.float32))
    bu = jnp.einsum('bsgc,gnc->bsgn', uf.astype(jnp.complex64), b_bar)
    a_seq = jnp.broadcast_to(a_bar, bu.shape)
    _, states = lax.associative_scan(_ssm_combine, (a_seq, bu), axis=1)
    y = jnp.einsum('bsgn,gcn->bsgc', states, c_mat).real
    y = y + d_skip.astype(jnp.float32).reshape(SSM_GROUPS, SSM_GROUP_CH) * uf
    z = jax.nn.gelu(y.reshape(bsz, s, SSM_WIDTH))
    ga, gb = jnp.split(z @ w_glu.astype(jnp.float32) + b_glu.astype(jnp.float32), 2, axis=-1)
    return (ga * jax.nn.sigmoid(gb)).astype(u.dtype)


def hybrid_mixer(h, w_in, a_re, a_im, log_dt, b_re, b_im, c_re, c_im, d_skip, w_glu, b_glu, w_branch, w_out):
    bsz, s, _ = h.shape
    proj = h @ w_in
    q, k, v, u, gates = jnp.split(
        proj, [ATTN_QKV_WIDTH, 2 * ATTN_QKV_WIDTH, 3 * ATTN_QKV_WIDTH, 3 * ATTN_QKV_WIDTH + SSM_WIDTH], axis=-1)
    hs = (bsz, s, N_ATTN_GROUPS, HEADS_PER_GROUP, HEAD_DIM)
    attn = dilation_mixture_attention(q.reshape(hs), k.reshape(hs), v.reshape(hs)).astype(h.dtype)
    ssm = s5_ssm(u, a_re, a_im, log_dt, b_re, b_im, c_re, c_im, d_skip, w_glu, b_glu)
    gate_a, gate_s = jnp.split(jax.nn.sigmoid(gates), 2, axis=-1)
    merged = gate_a * (attn @ w_branch[:ATTN_OUT]) + gate_s * (ssm @ w_branch[ATTN_OUT:])
    return merged @ w_out


def swiglu(h, w1, w3, w2):
    return (jax.nn.silu(h @ w1) * (h @ w3)) @ w2


def moe_swiglu(h, router_w, router_b, w1, w3, w2):
    logits = (h @ router_w + router_b).astype(jnp.float32)
    top_vals, top_idx = lax.top_k(logits, TOP_K)
    top_w = jax.nn.softmax(top_vals, axis=-1)
    combine = jnp.sum(jax.nn.one_hot(top_idx, N_EXPERTS, dtype=jnp.float32) * top_w[..., None], axis=-2)
    combine = combine.astype(h.dtype)
    out = jnp.zeros_like(h)
    for e in range(N_EXPERTS):
        out = out + combine[..., e:e + 1] * swiglu(h, w1[e], w3[e], w2[e])
    return out


def setup_inputs(seed: int = 0) -> dict:
    key = jax.random.key(seed)
    ks = jax.random.split(key, 32)
    f32 = jnp.float32

    def nrm(k, shape, scale):
        return jax.random.normal(k, shape, f32) * scale

    d = D_MODEL
    inp = {}
    inp["x"] = nrm(ks[0], (BATCH, SEQ, d), 1.0)
    inp["c"] = nrm(ks[1], (BATCH, d), 1.0)
    inp["ada_w"] = nrm(ks[2], (DEPTH, d, 6 * d), 0.5 * d ** -0.5)
    inp["ada_b"] = nrm(ks[3], (DEPTH, 6 * d), 0.01)
    inp["norm_mix_g"] = 1.0 + nrm(ks[4], (DEPTH, d), 0.05)
    inp["norm_ffn_g"] = 1.0 + nrm(ks[5], (DEPTH, d), 0.05)
    inp["final_norm_g"] = 1.0 + nrm(ks[6], (d,), 0.05)
    inp["w_in"] = nrm(ks[7], (DEPTH, d, IN_COLS), d ** -0.5)
    inp["ssm_a_re"] = -0.5 * jnp.exp(nrm(ks[8], (DEPTH, SSM_GROUPS, SSM_STATE), 0.05))
    inp["ssm_a_im"] = math.pi * jnp.arange(SSM_STATE, dtype=f32)[None, None, :] + nrm(ks[9], (DEPTH, SSM_GROUPS, SSM_STATE), 0.01)
    inp["ssm_log_dt"] = jax.random.uniform(ks[10], (DEPTH, SSM_GROUPS), f32, math.log(1e-3), math.log(1e-1))
    inp["ssm_b_re"] = nrm(ks[11], (DEPTH, SSM_GROUPS, SSM_STATE, SSM_GROUP_CH), (2 * SSM_GROUP_CH) ** -0.5)
    inp["ssm_b_im"] = nrm(ks[12], (DEPTH, SSM_GROUPS, SSM_STATE, SSM_GROUP_CH), (2 * SSM_GROUP_CH) ** -0.5)
    inp["ssm_c_re"] = nrm(ks[13], (DEPTH, SSM_GROUPS, SSM_GROUP_CH, SSM_STATE), SSM_STATE ** -0.5)
    inp["ssm_c_im"] = nrm(ks[14], (DEPTH, SSM_GROUPS, SSM_GROUP_CH, SSM_STATE), SSM_STATE ** -0.5)
    inp["ssm_d"] = nrm(ks[15], (DEPTH, SSM_WIDTH), 1.0)
    inp["w_glu"] = nrm(ks[16], (DEPTH, SSM_WIDTH, 2 * SSM_WIDTH), SSM_WIDTH ** -0.5)
    inp["b_glu"] = nrm(ks[17], (DEPTH, 2 * SSM_WIDTH), 0.01)
    inp["w_branch"] = nrm(ks[18], (DEPTH, ATTN_OUT + SSM_WIDTH, d), ATTN_OUT ** -0.5)
    inp["w_out"] = nrm(ks[19], (DEPTH, d, d), d ** -0.5)
    inp["ffn_w1"] = nrm(ks[20], (N_DENSE, d, D_FF), d ** -0.5)
    inp["ffn_w3"] = nrm(ks[21], (N_DENSE, d, D_FF), d ** -0.5)
    inp["ffn_w2"] = nrm(ks[22], (N_DENSE, D_FF, d), D_FF ** -0.5)
    inp["router_w"] = nrm(ks[23], (N_MOE, d, N_EXPERTS), d ** -0.5)
    inp["router_b"] = nrm(ks[24], (N_MOE, N_EXPERTS), 0.01)
    inp["moe_w1"] = nrm(ks[25], (N_MOE, N_EXPERTS, d, D_FF_EXPERT), d ** -0.5)
    inp["moe_w3"] = nrm(ks[26], (N_MOE, N_EXPERTS, d, D_FF_EXPERT), d ** -0.5)
    inp["moe_w2"] = nrm(ks[27], (N_MOE, N_EXPERTS, D_FF_EXPERT, d), D_FF_EXPERT ** -0.5)
    return inp


def reference(x, c, ada_w, ada_b, norm_mix_g, norm_ffn_g, final_norm_g, w_in,
              ssm_a_re, ssm_a_im, ssm_log_dt, ssm_b_re, ssm_b_im, ssm_c_re, ssm_c_im, ssm_d,
              w_glu, b_glu, w_branch, w_out, ffn_w1, ffn_w3, ffn_w2,
              router_w, router_b, moe_w1, moe_w3, moe_w2):
    c_act = jax.nn.silu(c)
    for layer in range(DEPTH):
        mod = c_act @ ada_w[layer] + ada_b[layer]
        sh1, sc1, g1, sh2, sc2, g2 = jnp.split(mod, 6, axis=-1)
        h = modulate(rmsnorm(x, norm_mix_g[layer]), sh1, sc1)
        mix = hybrid_mixer(h, w_in[layer], ssm_a_re[layer], ssm_a_im[layer], ssm_log_dt[layer],
                           ssm_b_re[layer], ssm_b_im[layer], ssm_c_re[layer], ssm_c_im[layer],
                           ssm_d[layer], w_glu[layer], b_glu[layer], w_branch[layer], w_out[layer])
        x = x + g1[:, None, :] * mix
        h = modulate(rmsnorm(x, norm_ffn_g[layer]), sh2, sc2)
        li = layer // 2
        if layer % 2 == 0:
            f = swiglu(h, ffn_w1[li], ffn_w3[li], ffn_w2[li])
        else:
            f = moe_swiglu(h, router_w[li], router_b[li], moe_w1[li], moe_w3[li], moe_w2[li])
        x = x + g2[:, None, :] * f
    return rmsnorm(x, final_norm_g)
```

```python
import functools
import math

import jax
import jax.numpy as jnp
from jax import lax
from jax.experimental import pallas as pl
from jax.experimental.pallas import tpu as pltpu

F32 = jnp.float32
BF16 = jnp.bfloat16

LANES = 128
SUBLANES = 8
VMEM_LIMIT_BYTES = 56 * 1024 * 1024

ATTN_PATTERNS = ((128, 1), (512, 4), (2048, 16))
N_GROUPS = len(ATTN_PATTERNS)
HEADS = 8
HEAD_DIM = 128
ATTN_OUT = HEADS * HEAD_DIM
QKV_WIDTH = N_GROUPS * ATTN_OUT
ATTN_BLOCK = 128
SSM_GROUP_CH = 16
SSM_STATE = 64
N_EXPERTS = 8
RMS_EPS = 1e-6
NEG_BIG = -1e30


def _params(*semantics):
    return pltpu.CompilerParams(dimension_semantics=semantics,
                                vmem_limit_bytes=VMEM_LIMIT_BYTES)


def _sigmoid(v):
    return 1.0 / (1.0 + jnp.exp(-v))


ADA_TN = 1024


def _ada_kernel(cb_ref, w_ref, b_ref, o_ref):
    nb = cb_ref.shape[0]
    tn = w_ref.shape[1]
    for b in range(nb):
        cb = cb_ref[b]
        act = cb * _sigmoid(cb)
        for j in range(tn // LANES):
            sl = slice(j * LANES, (j + 1) * LANES)
            col = jnp.sum(w_ref[:, sl] * act, axis=0, keepdims=True)
            o_ref[b:b + 1, sl] = col + b_ref[:, sl]


def _ada_modulation(c, ada_w, ada_b):
    depth, k, n = ada_w.shape
    nb = c.shape[0]
    cb = jnp.broadcast_to(c[:, :, None], (nb, k, LANES))
    return pl.pallas_call(
        _ada_kernel,
        grid=(depth, n // ADA_TN),
        in_specs=[
            pl.BlockSpec((nb, k, LANES), lambda l, j: (0, 0, 0)),
            pl.BlockSpec((None, k, ADA_TN), lambda l, j: (l, 0, j)),
            pl.BlockSpec((None, 1, ADA_TN), lambda l, j: (l, 0, j)),
        ],
        out_specs=pl.BlockSpec((None, nb, ADA_TN), lambda l, j: (l, 0, j)),
        out_shape=jax.ShapeDtypeStruct((depth, nb, n), F32),
        compiler_params=_params("arbitrary", "arbitrary"),
        name="ada_modulation",
    )(cb, ada_w, ada_b.reshape(depth, 1, n))


NORM_TS = 512


def _rms(x, g):
    return x * lax.rsqrt(jnp.mean(x * x, axis=-1, keepdims=True) + RMS_EPS) * g


def _norm_mod_kernel(x_ref, g_ref, sh_ref, sc_ref, o_ref):
    y = _rms(x_ref[...], g_ref[...])
    o_ref[...] = (y * (1.0 + sc_ref[...]) + sh_ref[...]).astype(o_ref.dtype)


def _norm_mod(x, g, shift, scale):
    b, s, d = x.shape
    vec = pl.BlockSpec((None, 1, d), lambda i, j: (i, 0, 0))
    out = pl.pallas_call(
        _norm_mod_kernel,
        grid=(b, s // NORM_TS),
        in_specs=[
            pl.BlockSpec((None, NORM_TS, d), lambda i, j: (i, j, 0)),
            pl.BlockSpec((1, d), lambda i, j: (0, 0)),
            vec, vec,
        ],
        out_specs=pl.BlockSpec((None, NORM_TS, d), lambda i, j: (i, j, 0)),
        out_shape=jax.ShapeDtypeStruct((b, s, d), BF16),
        compiler_params=_params("arbitrary", "arbitrary"),
        name="norm_mod",
    )(x, g.reshape(1, d), shift.reshape(b, 1, d), scale.reshape(b, 1, d))
    return out.reshape(b * s, d)


def _final_norm_kernel(x_ref, g_ref, o_ref):
    o_ref[...] = _rms(x_ref[...], g_ref[...])


def _final_norm(x, g):
    b, s, d = x.shape
    return pl.pallas_call(
        _final_norm_kernel,
        grid=(b, s // NORM_TS),
        in_specs=[
            pl.BlockSpec((None, NORM_TS, d), lambda i, j: (i, j, 0)),
            pl.BlockSpec((1, d), lambda i, j: (0, 0)),
        ],
        out_specs=pl.BlockSpec((None, NORM_TS, d), lambda i, j: (i, j, 0)),
        out_shape=jax.ShapeDtypeStruct((b, s, d), F32),
        compiler_params=_params("arbitrary", "arbitrary"),
        name="final_norm",
    )(x, g.reshape(1, d))


CAST_ROWS = 256


def _cast_weight(w_ref, wb_ref):
    def body(i, carry):
        r = pl.multiple_of(i * CAST_ROWS, CAST_ROWS)
        wb_ref[pl.ds(r, CAST_ROWS), :] = w_ref[pl.ds(r, CAST_ROWS), :].astype(BF16)
        return carry
    lax.fori_loop(0, w_ref.shape[0] // CAST_ROWS, body, 0)


def _dot(a, b):
    return jnp.dot(a, b, preferred_element_type=F32)


def _mm_plain_kernel(a_ref, w_ref, o_ref, wb_ref):
    @pl.when(pl.program_id(1) == 0)
    def _():
        _cast_weight(w_ref, wb_ref)
    o_ref[...] = _dot(a_ref[...], wb_ref[...]).astype(o_ref.dtype)


def _mm_plain(a, w, w_idx, tm, tn):
    m, k = a.shape
    n = w.shape[-1]
    lead = (None,) * len(w_idx)
    return pl.pallas_call(
        _mm_plain_kernel,
        grid=(n // tn, m // tm),
        in_specs=[
            pl.BlockSpec((tm, k), lambda j, i: (i, 0)),
            pl.BlockSpec(lead + (k, tn), lambda j, i: w_idx + (0, j)),
        ],
        out_specs=pl.BlockSpec((tm, tn), lambda j, i: (i, j)),
        out_shape=jax.ShapeDtypeStruct((m, n), BF16),
        scratch_shapes=[pltpu.VMEM((k, tn), BF16)],
        compiler_params=_params("arbitrary", "arbitrary"),
        name="mm_plain",
    )(a, w)


def _mm_swiglu_kernel(a_ref, w1_ref, w3_ref, rs_ref, o_ref, wb1_ref, wb3_ref):
    @pl.when(pl.program_id(1) == 0)
    def _():
        _cast_weight(w1_ref, wb1_ref)
        _cast_weight(w3_ref, wb3_ref)
    a = a_ref[...]
    u = _dot(a, wb1_ref[...])
    v = _dot(a, wb3_ref[...])
    o_ref[...] = (u * _sigmoid(u) * v * rs_ref[...]).astype(o_ref.dtype)


def _mm_swiglu(a, w1, w3, w_idx, row_scale, tm, tn):
    m, k = a.shape
    n = w1.shape[-1]
    lead = (None,) * len(w_idx)
    wspec = pl.BlockSpec(lead + (k, tn), lambda j, i: w_idx + (0, j))
    return pl.pallas_call(
        _mm_swiglu_kernel,
        grid=(n // tn, m // tm),
        in_specs=[
            pl.BlockSpec((tm, k), lambda j, i: (i, 0)),
            wspec, wspec,
            pl.BlockSpec((tm, 1), lambda j, i: (i, 0)),
        ],
        out_specs=pl.BlockSpec((tm, tn), lambda j, i: (i, j)),
        out_shape=jax.ShapeDtypeStruct((m, n), BF16),
        scratch_shapes=[pltpu.VMEM((k, tn), BF16), pltpu.VMEM((k, tn), BF16)],
        compiler_params=_params("arbitrary", "arbitrary"),
        name="mm_swiglu",
    )(a, w1, w3, row_scale)


def _mm_resid_kernel(a_ref, w_ref, x_ref, g_ref, o_ref, wb_ref):
    @pl.when(pl.program_id(1) == 0)
    def _():
        _cast_weight(w_ref, wb_ref)
    o_ref[...] = x_ref[...] + g_ref[...] * _dot(a_ref[...], wb_ref[...])


def _mm_resid(a, w, w_idx, x, gate, seq, tm, tn):
    m, k = a.shape
    n = w.shape[-1]
    nb = gate.shape[0]
    lead = (None,) * len(w_idx)
    return pl.pallas_call(
        _mm_resid_kernel,
        grid=(n // tn, m // tm),
        in_specs=[
            pl.BlockSpec((tm, k), lambda j, i: (i, 0)),
            pl.BlockSpec(lead + (k, tn), lambda j, i: w_idx + (0, j)),
            pl.BlockSpec((tm, tn), lambda j, i: (i, j)),
            pl.BlockSpec((None, 1, tn), lambda j, i: (i * tm // seq, 0, j)),
        ],
        out_specs=pl.BlockSpec((tm, tn), lambda j, i: (i, j)),
        out_shape=jax.ShapeDtypeStruct((m, n), F32),
        scratch_shapes=[pltpu.VMEM((k, tn), BF16)],
        compiler_params=_params("arbitrary", "arbitrary"),
        name="mm_resid",
    )(a, w, x, gate.reshape(nb, 1, n))


def _mm_branch_kernel(at_ref, ss_ref, wa_ref, ws_ref, ga_ref, gs_ref, o_ref,
                      wba_ref, wbs_ref):
    @pl.when(pl.program_id(1) == 0)
    def _():
        _cast_weight(wa_ref, wba_ref)
        _cast_weight(ws_ref, wbs_ref)
    pa = _dot(at_ref[...], wba_ref[...])
    ps = _dot(ss_ref[...], wbs_ref[...])
    ga = _sigmoid(ga_ref[...].astype(F32))
    gs = _sigmoid(gs_ref[...].astype(F32))
    o_ref[...] = (ga * pa + gs * ps).astype(o_ref.dtype)


def _mm_branch(attn, ssm, w_branch, layer, proj, gate_col, tm, tn):
    m, k = attn.shape
    n = w_branch.shape[-1]
    g0 = gate_col // tn
    g1 = (gate_col + n) // tn
    return pl.pallas_call(
        _mm_branch_kernel,
        grid=(n // tn, m // tm),
        in_specs=[
            pl.BlockSpec((tm, k), lambda j, i: (i, 0)),
            pl.BlockSpec((tm, k), lambda j, i: (i, 0)),
            pl.BlockSpec((None, k, tn), lambda j, i: (layer, 0, j)),
            pl.BlockSpec((None, k, tn), lambda j, i: (layer, 1, j)),
            pl.BlockSpec((tm, tn), lambda j, i: (i, g0 + j)),
            pl.BlockSpec((tm, tn), lambda j, i: (i, g1 + j)),
        ],
        out_specs=pl.BlockSpec((tm, tn), lambda j, i: (i, j)),
        out_shape=jax.ShapeDtypeStruct((m, n), BF16),
        scratch_shapes=[pltpu.VMEM((k, tn), BF16), pltpu.VMEM((k, tn), BF16)],
        compiler_params=_params("arbitrary", "arbitrary"),
        name="mm_branch",
    )(attn, ssm, w_branch, w_branch, proj, proj)


def _dot_nt(a, b):
    return lax.dot_general(a, b, (((1,), (1,)), ((), ())), preferred_element_type=F32)


def _attn_kernel(q_ref, kp_ref, kc_ref, vp_ref, vc_ref, o_ref, lse_ref, *, dilation):
    blk = ATTN_BLOCK
    has_prev = pl.program_id(2) > 0
    qi = lax.broadcasted_iota(jnp.int32, (blk, blk), 0)
    kj = lax.broadcasted_iota(jnp.int32, (blk, blk), 1)
    dist_c = qi - kj
    dist_p = dist_c + blk
    valid_c = dist_c >= 0
    valid_p = jnp.logical_and(dist_p <= blk, has_prev)
    pen_c = dist_c.astype(F32) * float(dilation)
    pen_p = dist_p.astype(F32) * float(dilation)
    scale = HEAD_DIM ** -0.5
    lse_tile = jnp.zeros((blk, LANES), F32)
    for h in range(HEADS):
        slope = 2.0 ** (-8.0 * (h + 1) / HEADS)
        sl = slice(h * HEAD_DIM, (h + 1) * HEAD_DIM)
        q = q_ref[:, sl]
        s_c = _dot_nt(q, kc_ref[:, sl]) * scale - slope * pen_c
        s_p = _dot_nt(q, kp_ref[:, sl]) * scale - slope * pen_p
        s_c = jnp.where(valid_c, s_c, NEG_BIG)
        s_p = jnp.where(valid_p, s_p, NEG_BIG)
        m = jnp.maximum(jnp.max(s_c, axis=1, keepdims=True),
                        jnp.max(s_p, axis=1, keepdims=True))
        p_c = jnp.exp(s_c - m)
        p_p = jnp.exp(s_p - m)
        l = jnp.sum(p_c, axis=1, keepdims=True) + jnp.sum(p_p, axis=1, keepdims=True)
        o = _dot(p_c.astype(BF16), vc_ref[:, sl]) + _dot(p_p.astype(BF16), vp_ref[:, sl])
        o_ref[:, sl] = (o / l).astype(o_ref.dtype)
        lse_tile = jnp.where(kj == h, m + jnp.log(l), lse_tile)
    lse_ref[...] = lse_tile


def _attention_group(proj, batch, seq, gi):
    window, dilation = ATTN_PATTERNS[gi]
    assert window // dilation == ATTN_BLOCK
    cols = proj.shape[-1]
    strided_len = seq // dilation
    nblk = strided_len // ATTN_BLOCK
    cb = cols // ATTN_OUT
    view = proj.reshape(batch, strided_len, dilation * cols)
    qcol, kcol, vcol = gi, N_GROUPS + gi, 2 * N_GROUPS + gi
    blk = (None, ATTN_BLOCK, ATTN_OUT)

    def cur(col):
        return pl.BlockSpec(blk, lambda b, r, n: (b, n, r * cb + col))

    def prev(col):
        return pl.BlockSpec(blk, lambda b, r, n: (b, jnp.maximum(n - 1, 0), r * cb + col))

    o, lse = pl.pallas_call(
        functools.partial(_attn_kernel, dilation=dilation),
        grid=(batch, dilation, nblk),
        in_specs=[cur(qcol), prev(kcol), cur(kcol), prev(vcol), cur(vcol)],
        out_specs=[
            pl.BlockSpec(blk, lambda b, r, n: (b, n, r)),
            pl.BlockSpec((None, ATTN_BLOCK, LANES), lambda b, r, n: (b, n, r)),
        ],
        out_shape=[
            jax.ShapeDtypeStruct((batch, strided_len, dilation * ATTN_OUT), BF16),
            jax.ShapeDtypeStruct((batch, strided_len, dilation * LANES), F32),
        ],
        compiler_params=_params("arbitrary", "arbitrary", "arbitrary"),
        name=f"attn_g{gi}",
    )(view, view, view, view, view)
    return o.reshape(batch, seq, ATTN_OUT), lse.reshape(batch, seq, LANES)


COMBINE_TS = 512


def _combine_kernel(o0_ref, o1_ref, o2_ref, l0_ref, l1_ref, l2_ref, out_ref):
    for h in range(HEADS):
        sl = slice(h * HEAD_DIM, (h + 1) * HEAD_DIM)
        a0 = l0_ref[:, h:h + 1]
        a1 = l1_ref[:, h:h + 1]
        a2 = l2_ref[:, h:h + 1]
        m = jnp.maximum(jnp.maximum(a0, a1), a2)
        e0 = jnp.exp(a0 - m)
        e1 = jnp.exp(a1 - m)
        e2 = jnp.exp(a2 - m)
        inv = 1.0 / (e0 + e1 + e2)
        mix = ((e0 * inv) * o0_ref[:, sl].astype(F32)
               + (e1 * inv) * o1_ref[:, sl].astype(F32)
               + (e2 * inv) * o2_ref[:, sl].astype(F32))
        out_ref[:, sl] = mix.astype(out_ref.dtype)


def _dilation_mixture_attention(proj, batch, seq):
    outs, lses = [], []
    for gi in range(N_GROUPS):
        o, lse = _attention_group(proj, batch, seq, gi)
        outs.append(o.reshape(batch * seq, ATTN_OUT))
        lses.append(lse.reshape(batch * seq, LANES))
    m = batch * seq
    ospec = pl.BlockSpec((COMBINE_TS, ATTN_OUT), lambda i: (i, 0))
    lspec = pl.BlockSpec((COMBINE_TS, LANES), lambda i: (i, 0))
    return pl.pallas_call(
        _combine_kernel,
        grid=(m // COMBINE_TS,),
        in_specs=[ospec] * 3 + [lspec] * 3,
        out_specs=ospec,
        out_shape=jax.ShapeDtypeStruct((m, ATTN_OUT), BF16),
        compiler_params=_params("arbitrary"),
        name="attn_combine",
    )(*outs, *lses)


SSM_LT = 128
SSM_JBLK = 256
SSM_SCAN_COLS = 1024


def _gelu_tanh(y):
    return 0.5 * y * (1.0 + jnp.tanh(math.sqrt(2.0 / math.pi) * (y + 0.044715 * (y * y * y))))


def _ssm_kernel(u_ref, bblk_ref, cblk_ref, are_ref, aim_ref, d_ref, wglu_ref, bglu_ref,
                o_ref, sre_ref, sim_ref, hre_ref, him_ref, *, nbatch):
    rows, width = u_ref.shape
    njb = width // SSM_JBLK
    jstates = bblk_ref.shape[2] // 2
    nstate = njb * jstates
    pairs = rows // SUBLANES
    per_tile = SUBLANES // nbatch

    @pl.when(pl.program_id(0) == 0)
    def _():
        hre_ref[...] = jnp.zeros_like(hre_ref)
        him_ref[...] = jnp.zeros_like(him_ref)

    for j in range(njb):
        bu = _dot(u_ref[:, j * SSM_JBLK:(j + 1) * SSM_JBLK], bblk_ref[j])
        cs = slice(j * jstates, (j + 1) * jstates)
        sre_ref[:, :, cs] = bu[:, :jstates].reshape(pairs, SUBLANES, jstates)
        sim_ref[:, :, cs] = bu[:, jstates:].reshape(pairs, SUBLANES, jstates)

    for c in range(nstate // SSM_SCAN_COLS):
        cs = slice(c * SSM_SCAN_COLS, (c + 1) * SSM_SCAN_COLS)
        ar = are_ref[0:nbatch, cs]
        ai = aim_ref[0:nbatch, cs]

        def step(k, carry):
            hr, hi = carry
            for t in range(per_tile):
                rs = slice(t * nbatch, (t + 1) * nbatch)
                br = sre_ref[k, rs, cs]
                bi = sim_ref[k, rs, cs]
                nr = ar * hr - ai * hi + br
                ni = ar * hi + ai * hr + bi
                sre_ref[k, rs, cs] = nr
                sim_ref[k, rs, cs] = ni
                hr, hi = nr, ni
            return hr, hi

        hr, hi = lax.fori_loop(0, pairs, step,
                               (hre_ref[0:nbatch, cs], him_ref[0:nbatch, cs]))
        hre_ref[0:nbatch, cs] = hr
        him_ref[0:nbatch, cs] = hi

    ys = []
    for j in range(njb):
        cs = slice(j * jstates, (j + 1) * jstates)
        hr = sre_ref[:, :, cs].reshape(rows, jstates).astype(BF16)
        hi = sim_ref[:, :, cs].reshape(rows, jstates).astype(BF16)
        ys.append(_dot(hr, cblk_ref[j, 0:jstates, :]) + _dot(hi, cblk_ref[j, jstates:, :]))
    y = jnp.concatenate(ys, axis=1) + d_ref[...] * u_ref[...].astype(F32)
    z = _gelu_tanh(y).astype(BF16)
    g = _dot(z, wglu_ref[...]) + bglu_ref[...]
    o_ref[...] = (g[:, :width] * _sigmoid(g[:, width:])).astype(o_ref.dtype)


def _ssm_tables(a_re, a_im, log_dt, b_re, b_im, c_re, c_im):
    groups, nst = a_re.shape
    gpb = SSM_JBLK // SSM_GROUP_CH
    njb = groups // gpb
    lam = lax.complex(a_re.astype(F32), a_im.astype(F32))
    dt = jnp.exp(log_dt.astype(F32))[:, None]
    a_bar = jnp.exp(lam * dt)
    b_mat = lax.complex(b_re.astype(F32), b_im.astype(F32))
    b_bar = ((a_bar - 1.0) / lam)[:, :, None] * b_mat
    eye = jnp.eye(gpb, dtype=F32)

    def in_blocks(t):
        t = t.reshape(njb, gpb, nst, SSM_GROUP_CH)
        return jnp.einsum('jgnc,gh->jgchn', t, eye).reshape(njb, gpb * SSM_GROUP_CH, gpb * nst)

    def out_blocks(t):
        t = t.reshape(njb, gpb, SSM_GROUP_CH, nst)
        return jnp.einsum('jgcn,gh->jhngc', t, eye).reshape(njb, gpb * nst, gpb * SSM_GROUP_CH)

    bblk = jnp.concatenate([in_blocks(jnp.real(b_bar)), in_blocks(jnp.imag(b_bar))], axis=2)
    cblk = jnp.concatenate([out_blocks(c_re.astype(F32)), out_blocks(-c_im.astype(F32))], axis=1)
    are = jnp.broadcast_to(jnp.real(a_bar).reshape(1, groups * nst), (SUBLANES, groups * nst))
    aim = jnp.broadcast_to(jnp.imag(a_bar).reshape(1, groups * nst), (SUBLANES, groups * nst))
    return bblk.astype(BF16), cblk.astype(BF16), are, aim


def _s5_ssm(u_tb, nbatch, tables, d_skip, w_glu, b_glu):
    bblk, cblk, are, aim = tables
    total, width = u_tb.shape
    rows = SSM_LT * nbatch
    nstate = are.shape[1]
    full = lambda a: pl.BlockSpec(a.shape, lambda i: (0,) * a.ndim)
    d2 = d_skip.reshape(1, width).astype(F32)
    wg = w_glu.astype(BF16)
    bg = b_glu.reshape(1, 2 * width).astype(F32)
    return pl.pallas_call(
        functools.partial(_ssm_kernel, nbatch=nbatch),
        grid=(total // rows,),
        in_specs=[pl.BlockSpec((rows, width), lambda i: (i, 0)),
                  full(bblk), full(cblk), full(are), full(aim), full(d2), full(wg), full(bg)],
        out_specs=pl.BlockSpec((rows, width), lambda i: (i, 0)),
        out_shape=jax.ShapeDtypeStruct((total, width), BF16),
        scratch_shapes=[
            pltpu.VMEM((rows // SUBLANES, SUBLANES, nstate), F32),
            pltpu.VMEM((rows // SUBLANES, SUBLANES, nstate), F32),
            pltpu.VMEM((SUBLANES, nstate), F32),
            pltpu.VMEM((SUBLANES, nstate), F32),
        ],
        compiler_params=_params("arbitrary"),
        name="s5_ssm",
    )(u_tb, bblk, cblk, are, aim, d2, wg, bg)


ROUTER_TS = 256


def _router_kernel(x_ref, g_ref, sh_ref, sc_ref, rw_ref, rb_ref, comb_ref):
    h = _rms(x_ref[...], g_ref[...]) * (1.0 + sc_ref[...]) + sh_ref[...]
    logits = jnp.dot(h, rw_ref[...], preferred_element_type=F32,
                     precision=lax.Precision.HIGHEST) + rb_ref[...]
    ne = logits.shape[1]
    idx = lax.broadcasted_iota(jnp.int32, logits.shape, 1)
    m1 = jnp.max(logits, axis=1, keepdims=True)
    i1 = jnp.min(jnp.where(logits == m1, idx, ne), axis=1, keepdims=True)
    rest = jnp.where(idx == i1, -jnp.inf, logits)
    m2 = jnp.max(rest, axis=1, keepdims=True)
    i2 = jnp.min(jnp.where(rest == m2, idx, ne), axis=1, keepdims=True)
    e = jnp.exp(m2 - m1)
    w1 = 1.0 / (1.0 + e)
    w2 = e / (1.0 + e)
    comb_ref[...] = jnp.where(idx == i1, w1, 0.0) + jnp.where(idx == i2, w2, 0.0)


def _router(x, g, shift, scale, router_w, router_b):
    b, s, d = x.shape
    ne = router_w.shape[-1]
    vec = pl.BlockSpec((None, 1, d), lambda i, j: (i, 0, 0))
    comb = pl.pallas_call(
        _router_kernel,
        grid=(b, s // ROUTER_TS),
        in_specs=[
            pl.BlockSpec((None, ROUTER_TS, d), lambda i, j: (i, j, 0)),
            pl.BlockSpec((1, d), lambda i, j: (0, 0)),
            vec, vec,
            pl.BlockSpec((d, ne), lambda i, j: (0, 0)),
            pl.BlockSpec((1, ne), lambda i, j: (0, 0)),
        ],
        out_specs=pl.BlockSpec((None, ROUTER_TS, ne), lambda i, j: (i, j, 0)),
        out_shape=jax.ShapeDtypeStruct((b, s, ne), F32),
        compiler_params=_params("arbitrary", "arbitrary"),
        name="router",
    )(x, g.reshape(1, d), shift.reshape(b, 1, d), scale.reshape(b, 1, d),
      router_w, router_b.reshape(1, ne))
    return comb.reshape(b * s, ne)


def kernel(x, c, ada_w, ada_b, norm_mix_g, norm_ffn_g, final_norm_g, w_in, ssm_a_re, ssm_a_im, ssm_log_dt, ssm_b_re, ssm_b_im, ssm_c_re, ssm_c_im, ssm_d, w_glu, b_glu, w_branch, w_out, ffn_w1, ffn_w3, ffn_w2, router_w, router_b, moe_w1, moe_w3, moe_w2):
    batch, seq, d = x.shape
    depth = ada_w.shape[0]
    tokens = batch * seq
    ssm_width = w_glu.shape[1]
    u_col = 3 * QKV_WIDTH
    gate_col = u_col + ssm_width

    mod = _ada_modulation(c, ada_w, ada_b).reshape(depth, batch, 6, d)
    ones = jnp.ones((tokens, 1), F32)
    x2 = x.reshape(tokens, d)
    for layer in range(depth):
        sh1, sc1, g1, sh2, sc2, g2 = (mod[layer, :, i] for i in range(6))

        h = _norm_mod(x2.reshape(batch, seq, d), norm_mix_g[layer], sh1, sc1)
        proj = _mm_plain(h, w_in, (layer,), tm=1024, tn=1024)
        attn = _dilation_mixture_attention(proj, batch, seq)
        u_tb = (proj[:, u_col:u_col + ssm_width].reshape(batch, seq, ssm_width)
                .transpose(1, 0, 2).reshape(tokens, ssm_width))
        tables = _ssm_tables(ssm_a_re[layer], ssm_a_im[layer], ssm_log_dt[layer],
                             ssm_b_re[layer], ssm_b_im[layer], ssm_c_re[layer], ssm_c_im[layer])
        ssm_tb = _s5_ssm(u_tb, batch, tables, ssm_d[layer], w_glu[layer], b_glu[layer])
        ssm = (ssm_tb.reshape(seq, batch, ssm_width).transpose(1, 0, 2)
               .reshape(tokens, ssm_width))
        merged = _mm_branch(attn, ssm, w_branch, layer, proj, gate_col, tm=1024, tn=1024)
        x2 = _mm_resid(merged, w_out, (layer,), x2, g1, seq, tm=1024, tn=1024)

        li = layer // 2
        h = _norm_mod(x2.reshape(batch, seq, d), norm_ffn_g[layer], sh2, sc2)
        if layer % 2 == 0:
            act = _mm_swiglu(h, ffn_w1, ffn_w3, (li,), ones, tm=1024, tn=512)
            x2 = _mm_resid(act, ffn_w2, (li,), x2, g2, seq, tm=512, tn=512)
        else:
            comb = _router(x2.reshape(batch, seq, d), norm_ffn_g[layer], sh2, sc2,
                           router_w[li], router_b[li])
            for e in range(N_EXPERTS):
                act = _mm_swiglu(h, moe_w1, moe_w3, (li, e), comb[:, e:e + 1], tm=1024, tn=256)
                x2 = _mm_resid(act, moe_w2, (li, e), x2, g2, seq, tm=1024, tn=512)
    return _final_norm(x2.reshape(batch, seq, d), final_norm_g)
```

```python
import functools
import math

import jax
import jax.numpy as jnp
from jax import lax
from jax.experimental import pallas as pl
from jax.experimental.pallas import tpu as pltpu

F32 = jnp.float32
BF16 = jnp.bfloat16

LANES = 128
SUBLANES = 8
VMEM_LIMIT_BYTES = 56 * 1024 * 1024

ATTN_PATTERNS = ((128, 1), (512, 4), (2048, 16))
N_GROUPS = len(ATTN_PATTERNS)
HEADS = 8
HEAD_DIM = 128
ATTN_OUT = HEADS * HEAD_DIM
QKV_WIDTH = N_GROUPS * ATTN_OUT
ATTN_BLOCK = 128
SSM_GROUP_CH = 16
SSM_STATE = 64
N_EXPERTS = 8
RMS_EPS = 1e-6
NEG_BIG = -1e30


def _params(*semantics):
    return pltpu.CompilerParams(dimension_semantics=semantics,
                                vmem_limit_bytes=VMEM_LIMIT_BYTES)


def _sigmoid(v):
    return 1.0 / (1.0 + jnp.exp(-v))


ADA_TN = 1024


def _ada_kernel(cb_ref, w_ref, b_ref, o_ref):
    nb = cb_ref.shape[0]
    tn = w_ref.shape[1]
    for b in range(nb):
        cb = cb_ref[b]
        act = cb * _sigmoid(cb)
        for j in range(tn // LANES):
            sl = slice(j * LANES, (j + 1) * LANES)
            col = jnp.sum(w_ref[:, sl] * act, axis=0, keepdims=True)
            o_ref[b:b + 1, sl] = col + b_ref[:, sl]


def _ada_modulation(c, ada_w, ada_b):
    depth, k, n = ada_w.shape
    nb = c.shape[0]
    cb = jnp.broadcast_to(c[:, :, None], (nb, k, LANES))
    return pl.pallas_call(
        _ada_kernel,
        grid=(depth, n // ADA_TN),
        in_specs=[
            pl.BlockSpec((nb, k, LANES), lambda l, j: (0, 0, 0)),
            pl.BlockSpec((None, k, ADA_TN), lambda l, j: (l, 0, j)),
            pl.BlockSpec((None, 1, ADA_TN), lambda l, j: (l, 0, j)),
        ],
        out_specs=pl.BlockSpec((None, nb, ADA_TN), lambda l, j: (l, 0, j)),
        out_shape=jax.ShapeDtypeStruct((depth, nb, n), F32),
        compiler_params=_params("arbitrary", "arbitrary"),
        name="ada_modulation",
    )(cb, ada_w, ada_b.reshape(depth, 1, n))


NORM_TS = 512


def _rms(x, g):
    return x * lax.rsqrt(jnp.mean(x * x, axis=-1, keepdims=True) + RMS_EPS) * g


def _norm_mod_kernel(x_ref, g_ref, sh_ref, sc_ref, o_ref):
    y = _rms(x_ref[...], g_ref[...])
    o_ref[...] = (y * (1.0 + sc_ref[...]) + sh_ref[...]).astype(o_ref.dtype)


def _norm_mod(x, g, shift, scale):
    b, s, d = x.shape
    vec = pl.BlockSpec((None, 1, d), lambda i, j: (i, 0, 0))
    out = pl.pallas_call(
        _norm_mod_kernel,
        grid=(b, s // NORM_TS),
        in_specs=[
            pl.BlockSpec((None, NORM_TS, d), lambda i, j: (i, j, 0)),
            pl.BlockSpec((1, d), lambda i, j: (0, 0)),
            vec, vec,
        ],
        out_specs=pl.BlockSpec((None, NORM_TS, d), lambda i, j: (i, j, 0)),
        out_shape=jax.ShapeDtypeStruct((b, s, d), BF16),
        compiler_params=_params("arbitrary", "arbitrary"),
        name="norm_mod",
    )(x, g.reshape(1, d), shift.reshape(b, 1, d), scale.reshape(b, 1, d))
    return out.reshape(b * s, d)


def _final_norm_kernel(x_ref, g_ref, o_ref):
    o_ref[...] = _rms(x_ref[...], g_ref[...])


def _final_norm(x, g):
    b, s, d = x.shape
    return pl.pallas_call(
        _final_norm_kernel,
        grid=(b, s // NORM_TS),
        in_specs=[
            pl.BlockSpec((None, NORM_TS, d), lambda i, j: (i, j, 0)),
            pl.BlockSpec((1, d), lambda i, j: (0, 0)),
        ],
        out_specs=pl.BlockSpec((None, NORM_TS, d), lambda i, j: (i, j, 0)),
        out_shape=jax.ShapeDtypeStruct((b, s, d), F32),
        compiler_params=_params("arbitrary", "arbitrary"),
        name="final_norm",
    )(x, g.reshape(1, d))


CAST_ROWS = 256


def _cast_weight(w_ref, wb_ref):
    def body(i, carry):
        r = pl.multiple_of(i * CAST_ROWS, CAST_ROWS)
        wb_ref[pl.ds(r, CAST_ROWS), :] = w_ref[pl.ds(r, CAST_ROWS), :].astype(BF16)
        return carry
    lax.fori_loop(0, w_ref.shape[0] // CAST_ROWS, body, 0)


def _dot(a, b):
    return jnp.dot(a, b, preferred_element_type=F32)


def _mm_plain_kernel(a_ref, w_ref, o_ref, wb_ref):
    @pl.when(pl.program_id(1) == 0)
    def _():
        _cast_weight(w_ref, wb_ref)
    o_ref[...] = _dot(a_ref[...], wb_ref[...]).astype(o_ref.dtype)


def _mm_plain(a, w, w_idx, col_tile, n_tiles, out_dtype, tm, tn):
    m, k = a.shape
    lead = (None,) * len(w_idx)
    return pl.pallas_call(
        _mm_plain_kernel,
        grid=(n_tiles, m // tm),
        in_specs=[
            pl.BlockSpec((tm, k), lambda j, i: (i, 0)),
            pl.BlockSpec(lead + (k, tn), lambda j, i: w_idx + (0, col_tile(j))),
        ],
        out_specs=pl.BlockSpec((tm, tn), lambda j, i: (i, j)),
        out_shape=jax.ShapeDtypeStruct((m, n_tiles * tn), out_dtype),
        scratch_shapes=[pltpu.VMEM((k, tn), BF16)],
        compiler_params=_params("arbitrary", "arbitrary"),
        name="mm_plain",
    )(a, w)


def _mm_swiglu_kernel(a_ref, w1_ref, w3_ref, rs_ref, o_ref, wb1_ref, wb3_ref):
    @pl.when(pl.program_id(1) == 0)
    def _():
        _cast_weight(w1_ref, wb1_ref)
        _cast_weight(w3_ref, wb3_ref)
    a = a_ref[...]
    u = _dot(a, wb1_ref[...])
    v = _dot(a, wb3_ref[...])
    o_ref[...] = (u * _sigmoid(u) * v * rs_ref[...]).astype(o_ref.dtype)


def _mm_swiglu(a, w1, w3, w_idx, row_scale, tm, tn):
    m, k = a.shape
    n = w1.shape[-1]
    lead = (None,) * len(w_idx)
    wspec = pl.BlockSpec(lead + (k, tn), lambda j, i: w_idx + (0, j))
    return pl.pallas_call(
        _mm_swiglu_kernel,
        grid=(n // tn, m // tm),
        in_specs=[
            pl.BlockSpec((tm, k), lambda j, i: (i, 0)),
            wspec, wspec,
            pl.BlockSpec((tm, 1), lambda j, i: (i, 0)),
        ],
        out_specs=pl.BlockSpec((tm, tn), lambda j, i: (i, j)),
        out_shape=jax.ShapeDtypeStruct((m, n), BF16),
        scratch_shapes=[pltpu.VMEM((k, tn), BF16), pltpu.VMEM((k, tn), BF16)],
        compiler_params=_params("arbitrary", "arbitrary"),
        name="mm_swiglu",
    )(a, w1, w3, row_scale)


def _mm_resid_kernel(a_ref, w_ref, x_ref, g_ref, o_ref, wb_ref):
    @pl.when(pl.program_id(1) == 0)
    def _():
        _cast_weight(w_ref, wb_ref)
    o_ref[...] = x_ref[...] + g_ref[...] * _dot(a_ref[...], wb_ref[...])


def _mm_resid(a, w, w_idx, x, gate, seq, tm, tn):
    m, k = a.shape
    n = w.shape[-1]
    nb = gate.shape[0]
    lead = (None,) * len(w_idx)
    return pl.pallas_call(
        _mm_resid_kernel,
        grid=(n // tn, m // tm),
        in_specs=[
            pl.BlockSpec((tm, k), lambda j, i: (i, 0)),
            pl.BlockSpec(lead + (k, tn), lambda j, i: w_idx + (0, j)),
            pl.BlockSpec((tm, tn), lambda j, i: (i, j)),
            pl.BlockSpec((None, 1, tn), lambda j, i: (i * tm // seq, 0, j)),
        ],
        out_specs=pl.BlockSpec((tm, tn), lambda j, i: (i, j)),
        out_shape=jax.ShapeDtypeStruct((m, n), F32),
        scratch_shapes=[pltpu.VMEM((k, tn), BF16)],
        compiler_params=_params("arbitrary", "arbitrary"),
        name="mm_resid",
    )(a, w, x, gate.reshape(nb, 1, n))


def _mm_branch_kernel(at_ref, ss_ref, wa_ref, ws_ref, ga_ref, gs_ref, o_ref,
                      wba_ref, wbs_ref):
    @pl.when(pl.program_id(1) == 0)
    def _():
        _cast_weight(wa_ref, wba_ref)
        _cast_weight(ws_ref, wbs_ref)
    pa = _dot(at_ref[...], wba_ref[...])
    ps = _dot(ss_ref[...], wbs_ref[...])
    ga = _sigmoid(ga_ref[...].astype(F32))
    gs = _sigmoid(gs_ref[...].astype(F32))
    o_ref[...] = (ga * pa + gs * ps).astype(o_ref.dtype)


def _mm_branch(attn, ssm, w_branch, layer, proj, gate_col, tm, tn):
    m, k = attn.shape
    n = w_branch.shape[-1]
    g0 = gate_col // tn
    g1 = (gate_col + n) // tn
    return pl.pallas_call(
        _mm_branch_kernel,
        grid=(n // tn, m // tm),
        in_specs=[
            pl.BlockSpec((tm, k), lambda j, i: (i, 0)),
            pl.BlockSpec((tm, k), lambda j, i: (i, 0)),
            pl.BlockSpec((None, k, tn), lambda j, i: (layer, 0, j)),
            pl.BlockSpec((None, k, tn), lambda j, i: (layer, 1, j)),
            pl.BlockSpec((tm, tn), lambda j, i: (i, g0 + j)),
            pl.BlockSpec((tm, tn), lambda j, i: (i, g1 + j)),
        ],
        out_specs=pl.BlockSpec((tm, tn), lambda j, i: (i, j)),
        out_shape=jax.ShapeDtypeStruct((m, n), BF16),
        scratch_shapes=[pltpu.VMEM((k, tn), BF16), pltpu.VMEM((k, tn), BF16)],
        compiler_params=_params("arbitrary", "arbitrary"),
        name="mm_branch",
    )(attn, ssm, w_branch, w_branch, proj, proj)


TOKEN_RESIDUES = 16


def _dot_nt(a, b):
    return lax.dot_general(a, b, (((1,), (1,)), ((), ())), preferred_element_type=F32)


def _attn_kernel(*refs, dilation, with_prev):
    if with_prev:
        q_ref, kp_ref, kc_ref, vp_ref, vc_ref, o_ref, lse_ref = refs
    else:
        q_ref, kc_ref, vc_ref, o_ref, lse_ref = refs
    blk = ATTN_BLOCK
    nc, rpc = q_ref.shape[0], q_ref.shape[1]
    shift = rpc.bit_length() - 1

    def offset(t):
        return nc * jnp.bitwise_and(t, rpc - 1) + jnp.right_shift(t, shift)

    lane = lax.broadcasted_iota(jnp.int32, (blk, blk), 1)
    dist_c = offset(lax.broadcasted_iota(jnp.int32, (blk, blk), 0)) - offset(lane)
    valid_c = dist_c >= 0
    pen_c = dist_c.astype(F32) * float(dilation)
    if with_prev:
        dist_p = dist_c + blk
        valid_p = jnp.logical_and(dist_p <= blk, pl.program_id(2) > 0)
        pen_p = dist_p.astype(F32) * float(dilation)
    scale = HEAD_DIM ** -0.5
    lse_tile = jnp.zeros((blk, LANES), F32)

    def head(ref, sl):
        return ref[:, :, sl].reshape(blk, HEAD_DIM).astype(BF16)

    for h in range(HEADS):
        slope = 2.0 ** (-8.0 * (h + 1) / HEADS)
        sl = slice(h * HEAD_DIM, (h + 1) * HEAD_DIM)
        q = head(q_ref, sl)
        s_c = jnp.where(valid_c, _dot_nt(q, head(kc_ref, sl)) * scale - slope * pen_c, NEG_BIG)
        m = jnp.max(s_c, axis=1, keepdims=True)
        if with_prev:
            s_p = jnp.where(valid_p, _dot_nt(q, head(kp_ref, sl)) * scale - slope * pen_p,
                            NEG_BIG)
            m = jnp.maximum(m, jnp.max(s_p, axis=1, keepdims=True))
        p_c = jnp.exp(s_c - m)
        l = jnp.sum(p_c, axis=1, keepdims=True)
        o = _dot(p_c.astype(BF16), head(vc_ref, sl))
        if with_prev:
            p_p = jnp.exp(s_p - m)
            l = l + jnp.sum(p_p, axis=1, keepdims=True)
            o = o + _dot(p_p.astype(BF16), head(vp_ref, sl))
        o_ref[:, :, sl] = (o / l).reshape(nc, rpc, HEAD_DIM).astype(o_ref.dtype)
        lse_tile = jnp.where(lane == h, m + jnp.log(l), lse_tile)
    lse_ref[...] = lse_tile.reshape(nc, rpc, LANES)


def _attention_group(qkv, qkv_cols, batch, seq, gi):
    window, dilation = ATTN_PATTERNS[gi]
    assert window // dilation == ATTN_BLOCK
    cols = qkv.shape[-1]
    per_res = seq // TOKEN_RESIDUES
    nc = TOKEN_RESIDUES // dilation
    rpc = ATTN_BLOCK // nc
    nblk = seq // dilation // ATTN_BLOCK
    with_prev = nblk > 1

    def shape5(c):
        return (batch, nc, dilation, per_res, c)

    def spec(c, col, back):
        return pl.BlockSpec(
            (None, nc, None, rpc, c),
            lambda b, r, n: (b, 0, r, jnp.maximum(n - back, 0), col))

    in_specs = [spec(ATTN_OUT, qkv_cols[0], 0)]
    for col in qkv_cols[1:]:
        if with_prev:
            in_specs.append(spec(ATTN_OUT, col, 1))
        in_specs.append(spec(ATTN_OUT, col, 0))
    view = qkv.reshape(shape5(cols))
    o, lse = pl.pallas_call(
        functools.partial(_attn_kernel, dilation=dilation, with_prev=with_prev),
        grid=(batch, dilation, nblk),
        in_specs=in_specs,
        out_specs=[spec(ATTN_OUT, 0, 0), spec(LANES, 0, 0)],
        out_shape=[
            jax.ShapeDtypeStruct(shape5(ATTN_OUT), qkv.dtype),
            jax.ShapeDtypeStruct(shape5(LANES), F32),
        ],
        compiler_params=_params("arbitrary", "arbitrary", "arbitrary"),
        name=f"attn_g{gi}",
    )(*([view] * len(in_specs)))
    return o.reshape(batch * seq, ATTN_OUT), lse.reshape(batch * seq, LANES)


COMBINE_TS = 512


def _combine_kernel(o0_ref, o1_ref, o2_ref, l0_ref, l1_ref, l2_ref, out_ref):
    for h in range(HEADS):
        sl = slice(h * HEAD_DIM, (h + 1) * HEAD_DIM)
        a0 = l0_ref[:, h:h + 1]
        a1 = l1_ref[:, h:h + 1]
        a2 = l2_ref[:, h:h + 1]
        m = jnp.maximum(jnp.maximum(a0, a1), a2)
        e0 = jnp.exp(a0 - m)
        e1 = jnp.exp(a1 - m)
        e2 = jnp.exp(a2 - m)
        inv = 1.0 / (e0 + e1 + e2)
        mix = ((e0 * inv) * o0_ref[:, sl].astype(F32)
               + (e1 * inv) * o1_ref[:, sl].astype(F32)
               + (e2 * inv) * o2_ref[:, sl].astype(F32))
        out_ref[:, sl] = mix.astype(out_ref.dtype)


def _dilation_mixture_attention(qkv0, rest, batch, seq):
    outs, lses = [], []
    for gi in range(N_GROUPS):
        if gi == 0:
            o, lse = _attention_group(qkv0, (0, 1, 2), batch, seq, gi)
        else:
            o, lse = _attention_group(rest, (gi - 1, gi + 1, gi + 3), batch, seq, gi)
        outs.append(o)
        lses.append(lse)
    m = batch * seq
    ospec = pl.BlockSpec((COMBINE_TS, ATTN_OUT), lambda i: (i, 0))
    lspec = pl.BlockSpec((COMBINE_TS, LANES), lambda i: (i, 0))
    return pl.pallas_call(
        _combine_kernel,
        grid=(m // COMBINE_TS,),
        in_specs=[ospec] * 3 + [lspec] * 3,
        out_specs=ospec,
        out_shape=jax.ShapeDtypeStruct((m, ATTN_OUT), BF16),
        compiler_params=_params("arbitrary"),
        name="attn_combine",
    )(*outs, *lses)


SSM_LT = 128
SSM_JBLK = 256
SSM_SCAN_COLS = 1024


def _gelu_tanh(y):
    return 0.5 * y * (1.0 + jnp.tanh(math.sqrt(2.0 / math.pi) * (y + 0.044715 * (y * y * y))))


def _ssm_kernel(u_ref, bblk_ref, cblk_ref, are_ref, aim_ref, d_ref, wglu_ref, bglu_ref,
                o_ref, sre_ref, sim_ref, hre_ref, him_ref, *, nbatch):
    rows, width = u_ref.shape
    njb = width // SSM_JBLK
    jstates = bblk_ref.shape[2] // 2
    nstate = njb * jstates
    pairs = rows // SUBLANES
    per_tile = SUBLANES // nbatch

    @pl.when(pl.program_id(0) == 0)
    def _():
        hre_ref[...] = jnp.zeros_like(hre_ref)
        him_ref[...] = jnp.zeros_like(him_ref)

    for j in range(njb):
        bu = _dot(u_ref[:, j * SSM_JBLK:(j + 1) * SSM_JBLK], bblk_ref[j])
        cs = slice(j * jstates, (j + 1) * jstates)
        sre_ref[:, :, cs] = bu[:, :jstates].reshape(pairs, SUBLANES, jstates)
        sim_ref[:, :, cs] = bu[:, jstates:].reshape(pairs, SUBLANES, jstates)

    for c in range(nstate // SSM_SCAN_COLS):
        cs = slice(c * SSM_SCAN_COLS, (c + 1) * SSM_SCAN_COLS)
        ar = are_ref[0:nbatch, cs]
        ai = aim_ref[0:nbatch, cs]

        def step(k, carry):
            hr, hi = carry
            for t in range(per_tile):
                rs = slice(t * nbatch, (t + 1) * nbatch)
                br = sre_ref[k, rs, cs]
                bi = sim_ref[k, rs, cs]
                nr = ar * hr - ai * hi + br
                ni = ar * hi + ai * hr + bi
                sre_ref[k, rs, cs] = nr
                sim_ref[k, rs, cs] = ni
                hr, hi = nr, ni
            return hr, hi

        hr, hi = lax.fori_loop(0, pairs, step,
                               (hre_ref[0:nbatch, cs], him_ref[0:nbatch, cs]))
        hre_ref[0:nbatch, cs] = hr
        him_ref[0:nbatch, cs] = hi

    ys = []
    for j in range(njb):
        cs = slice(j * jstates, (j + 1) * jstates)
        hr = sre_ref[:, :, cs].reshape(rows, jstates).astype(BF16)
        hi = sim_ref[:, :, cs].reshape(rows, jstates).astype(BF16)
        ys.append(_dot(hr, cblk_ref[j, 0:jstates, :]) + _dot(hi, cblk_ref[j, jstates:, :]))
    y = jnp.concatenate(ys, axis=1) + d_ref[...] * u_ref[...].astype(F32)
    z = _gelu_tanh(y).astype(BF16)
    g = _dot(z, wglu_ref[...]) + bglu_ref[...]
    o_ref[...] = (g[:, :width] * _sigmoid(g[:, width:])).astype(o_ref.dtype)


def _ssm_tables(a_re, a_im, log_dt, b_re, b_im, c_re, c_im):
    groups, nst = a_re.shape
    gpb = SSM_JBLK // SSM_GROUP_CH
    njb = groups // gpb
    lam = lax.complex(a_re.astype(F32), a_im.astype(F32))
    dt = jnp.exp(log_dt.astype(F32))[:, None]
    a_bar = jnp.exp(lam * dt)
    b_mat = lax.complex(b_re.astype(F32), b_im.astype(F32))
    b_bar = ((a_bar - 1.0) / lam)[:, :, None] * b_mat
    eye = jnp.eye(gpb, dtype=F32)

    def in_blocks(t):
        t = t.reshape(njb, gpb, nst, SSM_GROUP_CH)
        return jnp.einsum('jgnc,gh->jgchn', t, eye).reshape(njb, gpb * SSM_GROUP_CH, gpb * nst)

    def out_blocks(t):
        t = t.reshape(njb, gpb, SSM_GROUP_CH, nst)
        return jnp.einsum('jgcn,gh->jhngc', t, eye).reshape(njb, gpb * nst, gpb * SSM_GROUP_CH)

    bblk = jnp.concatenate([in_blocks(jnp.real(b_bar)), in_blocks(jnp.imag(b_bar))], axis=2)
    cblk = jnp.concatenate([out_blocks(c_re.astype(F32)), out_blocks(-c_im.astype(F32))], axis=1)
    are = jnp.broadcast_to(jnp.real(a_bar).reshape(1, groups * nst), (SUBLANES, groups * nst))
    aim = jnp.broadcast_to(jnp.imag(a_bar).reshape(1, groups * nst), (SUBLANES, groups * nst))
    return bblk.astype(BF16), cblk.astype(BF16), are, aim


def _s5_ssm(u_tb, nbatch, tables, d_skip, w_glu, b_glu):
    bblk, cblk, are, aim = tables
    total, width = u_tb.shape
    rows = SSM_LT * nbatch
    nstate = are.shape[1]
    full = lambda a: pl.BlockSpec(a.shape, lambda i: (0,) * a.ndim)
    d2 = d_skip.reshape(1, width).astype(F32)
    wg = w_glu.astype(BF16)
    bg = b_glu.reshape(1, 2 * width).astype(F32)
    return pl.pallas_call(
        functools.partial(_ssm_kernel, nbatch=nbatch),
        grid=(total // rows,),
        in_specs=[pl.BlockSpec((rows, width), lambda i: (i, 0)),
                  full(bblk), full(cblk), full(are), full(aim), full(d2), full(wg), full(bg)],
        out_specs=pl.BlockSpec((rows, width), lambda i: (i, 0)),
        out_shape=jax.ShapeDtypeStruct((total, width), BF16),
        scratch_shapes=[
            pltpu.VMEM((rows // SUBLANES, SUBLANES, nstate), F32),
            pltpu.VMEM((rows // SUBLANES, SUBLANES, nstate), F32),
            pltpu.VMEM((SUBLANES, nstate), F32),
            pltpu.VMEM((SUBLANES, nstate), F32),
        ],
        compiler_params=_params("arbitrary"),
        name="s5_ssm",
    )(u_tb, bblk, cblk, are, aim, d2, wg, bg)


ROUTER_TS = 256


def _router_kernel(x_ref, g_ref, sh_ref, sc_ref, rw_ref, rb_ref, comb_ref):
    h = _rms(x_ref[...], g_ref[...]) * (1.0 + sc_ref[...]) + sh_ref[...]
    logits = jnp.dot(h, rw_ref[...], preferred_element_type=F32,
                     precision=lax.Precision.HIGHEST) + rb_ref[...]
    ne = logits.shape[1]
    idx = lax.broadcasted_iota(jnp.int32, logits.shape, 1)
    m1 = jnp.max(logits, axis=1, keepdims=True)
    i1 = jnp.min(jnp.where(logits == m1, idx, ne), axis=1, keepdims=True)
    rest = jnp.where(idx == i1, -jnp.inf, logits)
    m2 = jnp.max(rest, axis=1, keepdims=True)
    i2 = jnp.min(jnp.where(rest == m2, idx, ne), axis=1, keepdims=True)
    e = jnp.exp(m2 - m1)
    w1 = 1.0 / (1.0 + e)
    w2 = e / (1.0 + e)
    comb_ref[...] = jnp.where(idx == i1, w1, 0.0) + jnp.where(idx == i2, w2, 0.0)


def _router(x, g, shift, scale, router_w, router_b):
    b, s, d = x.shape
    ne = router_w.shape[-1]
    vec = pl.BlockSpec((None, 1, d), lambda i, j: (i, 0, 0))
    comb = pl.pallas_call(
        _router_kernel,
        grid=(b, s // ROUTER_TS),
        in_specs=[
            pl.BlockSpec((None, ROUTER_TS, d), lambda i, j: (i, j, 0)),
            pl.BlockSpec((1, d), lambda i, j: (0, 0)),
            vec, vec,
            pl.BlockSpec((d, ne), lambda i, j: (0, 0)),
            pl.BlockSpec((1, ne), lambda i, j: (0, 0)),
        ],
        out_specs=pl.BlockSpec((None, ROUTER_TS, ne), lambda i, j: (i, j, 0)),
        out_shape=jax.ShapeDtypeStruct((b, s, ne), F32),
        compiler_params=_params("arbitrary", "arbitrary"),
        name="router",
    )(x, g.reshape(1, d), shift.reshape(b, 1, d), scale.reshape(b, 1, d),
      router_w, router_b.reshape(1, ne))
    return comb.reshape(b * s, ne)


def kernel(x, c, ada_w, ada_b, norm_mix_g, norm_ffn_g, final_norm_g, w_in, ssm_a_re, ssm_a_im, ssm_log_dt, ssm_b_re, ssm_b_im, ssm_c_re, ssm_c_im, ssm_d, w_glu, b_glu, w_branch, w_out, ffn_w1, ffn_w3, ffn_w2, router_w, router_b, moe_w1, moe_w3, moe_w2):
    batch, seq, d = x.shape
    depth = ada_w.shape[0]
    tokens = batch * seq
    ssm_width = w_glu.shape[1]
    per_res = seq // TOKEN_RESIDUES
    tn_in = ATTN_OUT
    assert ssm_width == tn_in and w_in.shape[-1] == (3 * N_GROUPS + 1) * tn_in + 2 * d

    def to_residue_major(t):
        f = t.shape[-1]
        return t.reshape(batch, per_res, TOKEN_RESIDUES, f).transpose(0, 2, 1, 3).reshape(batch, seq, f)

    def to_time_batch(t):
        f = t.shape[-1]
        return t.reshape(batch, TOKEN_RESIDUES, per_res, f).transpose(2, 1, 0, 3).reshape(tokens, f)

    def from_time_batch(t):
        f = t.shape[-1]
        return t.reshape(per_res, TOKEN_RESIDUES, batch, f).transpose(2, 1, 0, 3).reshape(tokens, f)

    mod = _ada_modulation(c, ada_w, ada_b).reshape(depth, batch, 6, d)
    ones = jnp.ones((tokens, 1), F32)
    x2 = to_residue_major(x).reshape(tokens, d)
    for layer in range(depth):
        sh1, sc1, g1, sh2, sc2, g2 = (mod[layer, :, i] for i in range(6))

        h = _norm_mod(x2.reshape(batch, seq, d), norm_mix_g[layer], sh1, sc1)
        qkv0 = _mm_plain(h, w_in, (layer,), lambda j: N_GROUPS * j, 3, F32, tm=1024, tn=tn_in)
        rest = _mm_plain(h, w_in, (layer,),
                         lambda j: j + 1 + jnp.where(j >= 2, 1, 0) + jnp.where(j >= 4, 1, 0),
                         11, BF16, tm=1024, tn=tn_in)
        attn = _dilation_mixture_attention(qkv0, rest, batch, seq)
        u_tb = to_time_batch(rest[:, 6 * tn_in:7 * tn_in])
        tables = _ssm_tables(ssm_a_re[layer], ssm_a_im[layer], ssm_log_dt[layer],
                             ssm_b_re[layer], ssm_b_im[layer], ssm_c_re[layer], ssm_c_im[layer])
        ssm = from_time_batch(
            _s5_ssm(u_tb, batch, tables, ssm_d[layer], w_glu[layer], b_glu[layer]))
        merged = _mm_branch(attn, ssm, w_branch, layer, rest, 7 * tn_in, tm=1024, tn=1024)
        x2 = _mm_resid(merged, w_out, (layer,), x2, g1, seq, tm=1024, tn=1024)

        li = layer // 2
        h = _norm_mod(x2.reshape(batch, seq, d), norm_ffn_g[layer], sh2, sc2)
        if layer % 2 == 0:
            act = _mm_swiglu(h, ffn_w1, ffn_w3, (li,), ones, tm=1024, tn=512)
            x2 = _mm_resid(act, ffn_w2, (li,), x2, g2, seq, tm=512, tn=512)
        else:
            comb = _router(x2.reshape(batch, seq, d), norm_ffn_g[layer], sh2, sc2,
                           router_w[li], router_b[li])
            for e in range(N_EXPERTS):
                act = _mm_swiglu(h, moe_w1, moe_w3, (li, e), comb[:, e:e + 1], tm=1024, tn=256)
                x2 = _mm_resid(act, moe_w2, (li, e), x2, g2, seq, tm=1024, tn=512)
    out = _final_norm(x2.reshape(batch, seq, d), final_norm_g)
    return out.reshape(batch, TOKEN_RESIDUES, per_res, d).transpose(0, 2, 1, 3).reshape(batch, seq, d)
```

```python
import functools
import math

import jax
import jax.numpy as jnp
from jax import lax
from jax.experimental import pallas as pl
from jax.experimental.pallas import tpu as pltpu

F32 = jnp.float32
BF16 = jnp.bfloat16

LANES = 128
SUBLANES = 8
VMEM_LIMIT_BYTES = 56 * 1024 * 1024

ATTN_PATTERNS = ((128, 1), (512, 4), (2048, 16))
N_GROUPS = len(ATTN_PATTERNS)
HEADS = 8
HEAD_DIM = 128
ATTN_OUT = HEADS * HEAD_DIM
QKV_WIDTH = N_GROUPS * ATTN_OUT
ATTN_BLOCK = 128
SSM_GROUP_CH = 16
SSM_STATE = 64
N_EXPERTS = 8
RMS_EPS = 1e-6
NEG_BIG = -1e30


def _params(*semantics):
    return pltpu.CompilerParams(dimension_semantics=semantics,
                                vmem_limit_bytes=VMEM_LIMIT_BYTES)


def _sigmoid(v):
    return 1.0 / (1.0 + jnp.exp(-v))


ADA_TN = 1024


def _ada_kernel(cb_ref, w_ref, b_ref, o_ref):
    nb = cb_ref.shape[0]
    tn = w_ref.shape[1]
    for b in range(nb):
        cb = cb_ref[b]
        act = cb * _sigmoid(cb)
        for j in range(tn // LANES):
            sl = slice(j * LANES, (j + 1) * LANES)
            col = jnp.sum(w_ref[:, sl] * act, axis=0, keepdims=True)
            o_ref[b:b + 1, sl] = col + b_ref[:, sl]


def _ada_modulation(c, ada_w, ada_b):
    depth, k, n = ada_w.shape
    nb = c.shape[0]
    cb = jnp.broadcast_to(c[:, :, None], (nb, k, LANES))
    return pl.pallas_call(
        _ada_kernel,
        grid=(depth, n // ADA_TN),
        in_specs=[
            pl.BlockSpec((nb, k, LANES), lambda l, j: (0, 0, 0)),
            pl.BlockSpec((None, k, ADA_TN), lambda l, j: (l, 0, j)),
            pl.BlockSpec((None, 1, ADA_TN), lambda l, j: (l, 0, j)),
        ],
        out_specs=pl.BlockSpec((None, nb, ADA_TN), lambda l, j: (l, 0, j)),
        out_shape=jax.ShapeDtypeStruct((depth, nb, n), F32),
        compiler_params=_params("arbitrary", "arbitrary"),
        name="ada_modulation",
    )(cb, ada_w, ada_b.reshape(depth, 1, n))


NORM_TS = 512


def _rms(x, g):
    return x * lax.rsqrt(jnp.mean(x * x, axis=-1, keepdims=True) + RMS_EPS) * g


def _norm_mod_kernel(x_ref, g_ref, sh_ref, sc_ref, o_ref):
    y = _rms(x_ref[...], g_ref[...])
    o_ref[...] = (y * (1.0 + sc_ref[...]) + sh_ref[...]).astype(o_ref.dtype)


def _norm_mod(x, g, shift, scale):
    b, s, d = x.shape
    vec = pl.BlockSpec((None, 1, d), lambda i, j: (i, 0, 0))
    out = pl.pallas_call(
        _norm_mod_kernel,
        grid=(b, s // NORM_TS),
        in_specs=[
            pl.BlockSpec((None, NORM_TS, d), lambda i, j: (i, j, 0)),
            pl.BlockSpec((1, d), lambda i, j: (0, 0)),
            vec, vec,
        ],
        out_specs=pl.BlockSpec((None, NORM_TS, d), lambda i, j: (i, j, 0)),
        out_shape=jax.ShapeDtypeStruct((b, s, d), BF16),
        compiler_params=_params("arbitrary", "arbitrary"),
        name="norm_mod",
    )(x, g.reshape(1, d), shift.reshape(b, 1, d), scale.reshape(b, 1, d))
    return out.reshape(b * s, d)


def _final_norm_kernel(x_ref, g_ref, o_ref):
    o_ref[...] = _rms(x_ref[...], g_ref[...])


def _final_norm(x, g):
    b, s, d = x.shape
    return pl.pallas_call(
        _final_norm_kernel,
        grid=(b, s // NORM_TS),
        in_specs=[
            pl.BlockSpec((None, NORM_TS, d), lambda i, j: (i, j, 0)),
            pl.BlockSpec((1, d), lambda i, j: (0, 0)),
        ],
        out_specs=pl.BlockSpec((None, NORM_TS, d), lambda i, j: (i, j, 0)),
        out_shape=jax.ShapeDtypeStruct((b, s, d), F32),
        compiler_params=_params("arbitrary", "arbitrary"),
        name="final_norm",
    )(x, g.reshape(1, d))


CAST_ROWS = 256


def _cast_weight(w_ref, wb_ref):
    def body(i, carry):
        r = pl.multiple_of(i * CAST_ROWS, CAST_ROWS)
        wb_ref[pl.ds(r, CAST_ROWS), :] = w_ref[pl.ds(r, CAST_ROWS), :].astype(BF16)
        return carry
    lax.fori_loop(0, w_ref.shape[0] // CAST_ROWS, body, 0)


def _dot(a, b):
    return jnp.dot(a, b, preferred_element_type=F32)


def _mm_plain_kernel(a_ref, w_ref, o_ref, wb_ref):
    @pl.when(pl.program_id(1) == 0)
    def _():
        _cast_weight(w_ref, wb_ref)
    o_ref[...] = _dot(a_ref[...], wb_ref[...]).astype(o_ref.dtype)


def _mm_plain(a, w, w_idx, col_tile, n_tiles, out_dtype, tm, tn):
    m, k = a.shape
    lead = (None,) * len(w_idx)
    return pl.pallas_call(
        _mm_plain_kernel,
        grid=(n_tiles, m // tm),
        in_specs=[
            pl.BlockSpec((tm, k), lambda j, i: (i, 0)),
            pl.BlockSpec(lead + (k, tn), lambda j, i: w_idx + (0, col_tile(j))),
        ],
        out_specs=pl.BlockSpec((tm, tn), lambda j, i: (i, j)),
        out_shape=jax.ShapeDtypeStruct((m, n_tiles * tn), out_dtype),
        scratch_shapes=[pltpu.VMEM((k, tn), BF16)],
        compiler_params=_params("arbitrary", "arbitrary"),
        name="mm_plain",
    )(a, w)


def _mm_swiglu_kernel(a_ref, w1_ref, w3_ref, rs_ref, o_ref, wb1_ref, wb3_ref):
    @pl.when(pl.program_id(1) == 0)
    def _():
        _cast_weight(w1_ref, wb1_ref)
        _cast_weight(w3_ref, wb3_ref)
    a = a_ref[...]
    u = _dot(a, wb1_ref[...])
    v = _dot(a, wb3_ref[...])
    o_ref[...] = (u * _sigmoid(u) * v * rs_ref[...]).astype(o_ref.dtype)


def _mm_swiglu(a, w1, w3, w_idx, row_scale, tm, tn):
    m, k = a.shape
    n = w1.shape[-1]
    lead = (None,) * len(w_idx)
    wspec = pl.BlockSpec(lead + (k, tn), lambda j, i: w_idx + (0, j))
    return pl.pallas_call(
        _mm_swiglu_kernel,
        grid=(n // tn, m // tm),
        in_specs=[
            pl.BlockSpec((tm, k), lambda j, i: (i, 0)),
            wspec, wspec,
            pl.BlockSpec((tm, 1), lambda j, i: (i, 0)),
        ],
        out_specs=pl.BlockSpec((tm, tn), lambda j, i: (i, j)),
        out_shape=jax.ShapeDtypeStruct((m, n), BF16),
        scratch_shapes=[pltpu.VMEM((k, tn), BF16), pltpu.VMEM((k, tn), BF16)],
        compiler_params=_params("arbitrary", "arbitrary"),
        name="mm_swiglu",
    )(a, w1, w3, row_scale)


def _mm_resid_kernel(a_ref, w_ref, x_ref, g_ref, o_ref, wb_ref):
    @pl.when(pl.program_id(1) == 0)
    def _():
        _cast_weight(w_ref, wb_ref)
    o_ref[...] = x_ref[...] + g_ref[...] * _dot(a_ref[...], wb_ref[...])


def _mm_resid(a, w, w_idx, x, gate, seq, tm, tn):
    m, k = a.shape
    n = w.shape[-1]
    nb = gate.shape[0]
    lead = (None,) * len(w_idx)
    return pl.pallas_call(
        _mm_resid_kernel,
        grid=(n // tn, m // tm),
        in_specs=[
            pl.BlockSpec((tm, k), lambda j, i: (i, 0)),
            pl.BlockSpec(lead + (k, tn), lambda j, i: w_idx + (0, j)),
            pl.BlockSpec((tm, tn), lambda j, i: (i, j)),
            pl.BlockSpec((None, 1, tn), lambda j, i: (i * tm // seq, 0, j)),
        ],
        out_specs=pl.BlockSpec((tm, tn), lambda j, i: (i, j)),
        out_shape=jax.ShapeDtypeStruct((m, n), F32),
        scratch_shapes=[pltpu.VMEM((k, tn), BF16)],
        compiler_params=_params("arbitrary", "arbitrary"),
        name="mm_resid",
    )(a, w, x, gate.reshape(nb, 1, n))


def _mm_branch_kernel(at_ref, ss_ref, wa_ref, ws_ref, ga_ref, gs_ref, o_ref,
                      wba_ref, wbs_ref):
    @pl.when(pl.program_id(1) == 0)
    def _():
        _cast_weight(wa_ref, wba_ref)
        _cast_weight(ws_ref, wbs_ref)
    pa = _dot(at_ref[...], wba_ref[...])
    ps = _dot(ss_ref[...], wbs_ref[...])
    ga = _sigmoid(ga_ref[...].astype(F32))
    gs = _sigmoid(gs_ref[...].astype(F32))
    o_ref[...] = (ga * pa + gs * ps).astype(o_ref.dtype)


def _mm_branch(attn, ssm, w_branch, layer, proj, gate_col, tm, tn):
    m, k = attn.shape
    n = w_branch.shape[-1]
    g0 = gate_col // tn
    g1 = (gate_col + n) // tn
    return pl.pallas_call(
        _mm_branch_kernel,
        grid=(n // tn, m // tm),
        in_specs=[
            pl.BlockSpec((tm, k), lambda j, i: (i, 0)),
            pl.BlockSpec((tm, k), lambda j, i: (i, 0)),
            pl.BlockSpec((None, k, tn), lambda j, i: (layer, 0, j)),
            pl.BlockSpec((None, k, tn), lambda j, i: (layer, 1, j)),
            pl.BlockSpec((tm, tn), lambda j, i: (i, g0 + j)),
            pl.BlockSpec((tm, tn), lambda j, i: (i, g1 + j)),
        ],
        out_specs=pl.BlockSpec((tm, tn), lambda j, i: (i, j)),
        out_shape=jax.ShapeDtypeStruct((m, n), BF16),
        scratch_shapes=[pltpu.VMEM((k, tn), BF16), pltpu.VMEM((k, tn), BF16)],
        compiler_params=_params("arbitrary", "arbitrary"),
        name="mm_branch",
    )(attn, ssm, w_branch, w_branch, proj, proj)


TOKEN_RESIDUES = 16
ATTN_BB = 2


def _dot_nt(a, b):
    return lax.dot_general(a, b, (((1,), (1,)), ((), ())), preferred_element_type=F32)


def _attn_kernel(*refs, dilation, with_prev):
    if with_prev:
        q_ref, kp_ref, kc_ref, vp_ref, vc_ref, o_ref, lse_ref = refs
    else:
        q_ref, kc_ref, vc_ref, o_ref, lse_ref = refs
    blk = ATTN_BLOCK
    bb, nc, rpc = q_ref.shape[0], q_ref.shape[1], q_ref.shape[2]
    shift = rpc.bit_length() - 1

    def offset(t):
        return nc * jnp.bitwise_and(t, rpc - 1) + jnp.right_shift(t, shift)

    lane = lax.broadcasted_iota(jnp.int32, (blk, blk), 1)
    dist_c = offset(lax.broadcasted_iota(jnp.int32, (blk, blk), 0)) - offset(lane)
    valid_c = dist_c >= 0
    pen_c = dist_c.astype(F32) * float(dilation)
    if with_prev:
        dist_p = dist_c + blk
        valid_p = jnp.logical_and(dist_p <= blk, pl.program_id(2) > 0)
        pen_p = dist_p.astype(F32) * float(dilation)
    scale = HEAD_DIM ** -0.5

    def head(ref, bi, sl):
        return ref[bi, :, :, sl].reshape(blk, HEAD_DIM).astype(BF16)

    for bi in range(bb):
        lse_tile = jnp.zeros((blk, LANES), F32)
        for h in range(HEADS):
            slope = 2.0 ** (-8.0 * (h + 1) / HEADS)
            sl = slice(h * HEAD_DIM, (h + 1) * HEAD_DIM)
            q = head(q_ref, bi, sl)
            s_c = jnp.where(valid_c,
                            _dot_nt(q, head(kc_ref, bi, sl)) * scale - slope * pen_c, NEG_BIG)
            m = jnp.max(s_c, axis=1, keepdims=True)
            if with_prev:
                s_p = jnp.where(valid_p,
                                _dot_nt(q, head(kp_ref, bi, sl)) * scale - slope * pen_p, NEG_BIG)
                m = jnp.maximum(m, jnp.max(s_p, axis=1, keepdims=True))
            p_c = jnp.exp(s_c - m)
            l = jnp.sum(p_c, axis=1, keepdims=True)
            o = _dot(p_c.astype(BF16), head(vc_ref, bi, sl))
            if with_prev:
                p_p = jnp.exp(s_p - m)
                l = l + jnp.sum(p_p, axis=1, keepdims=True)
                o = o + _dot(p_p.astype(BF16), head(vp_ref, bi, sl))
            o_ref[bi, :, :, sl] = (o / l).reshape(nc, rpc, HEAD_DIM).astype(o_ref.dtype)
            lse_tile = jnp.where(lane == h, m + jnp.log(l), lse_tile)
        lse_ref[bi] = lse_tile.reshape(nc, rpc, LANES)


def _attention_group(qkv, qkv_cols, batch, seq, gi):
    window, dilation = ATTN_PATTERNS[gi]
    assert window // dilation == ATTN_BLOCK
    cols = qkv.shape[-1]
    per_res = seq // TOKEN_RESIDUES
    nc = TOKEN_RESIDUES // dilation
    rpc = ATTN_BLOCK // nc
    nblk = seq // dilation // ATTN_BLOCK
    with_prev = nblk > 1

    def shape5(c):
        return (batch, nc, dilation, per_res, c)

    def spec(c, col, back):
        return pl.BlockSpec(
            (ATTN_BB, nc, None, rpc, c),
            lambda b, r, n: (b, 0, r, jnp.maximum(n - back, 0), col))

    in_specs = [spec(ATTN_OUT, qkv_cols[0], 0)]
    for col in qkv_cols[1:]:
        if with_prev:
            in_specs.append(spec(ATTN_OUT, col, 1))
        in_specs.append(spec(ATTN_OUT, col, 0))
    view = qkv.reshape(shape5(cols))
    o, lse = pl.pallas_call(
        functools.partial(_attn_kernel, dilation=dilation, with_prev=with_prev),
        grid=(batch // ATTN_BB, dilation, nblk),
        in_specs=in_specs,
        out_specs=[spec(ATTN_OUT, 0, 0), spec(LANES, 0, 0)],
        out_shape=[
            jax.ShapeDtypeStruct(shape5(ATTN_OUT), qkv.dtype),
            jax.ShapeDtypeStruct(shape5(LANES), F32),
        ],
        compiler_params=_params("arbitrary", "arbitrary", "arbitrary"),
        name=f"attn_g{gi}",
    )(*([view] * len(in_specs)))
    return o.reshape(batch * seq, ATTN_OUT), lse.reshape(batch * seq, LANES)


COMBINE_TS = 512


def _combine_kernel(o0_ref, o1_ref, o2_ref, l0_ref, l1_ref, l2_ref, out_ref):
    for h in range(HEADS):
        sl = slice(h * HEAD_DIM, (h + 1) * HEAD_DIM)
        a0 = l0_ref[:, h:h + 1]
        a1 = l1_ref[:, h:h + 1]
        a2 = l2_ref[:, h:h + 1]
        m = jnp.maximum(jnp.maximum(a0, a1), a2)
        e0 = jnp.exp(a0 - m)
        e1 = jnp.exp(a1 - m)
        e2 = jnp.exp(a2 - m)
        inv = 1.0 / (e0 + e1 + e2)
        mix = ((e0 * inv) * o0_ref[:, sl].astype(F32)
               + (e1 * inv) * o1_ref[:, sl].astype(F32)
               + (e2 * inv) * o2_ref[:, sl].astype(F32))
        out_ref[:, sl] = mix.astype(out_ref.dtype)


def _dilation_mixture_attention(qkv0, rest, batch, seq):
    outs, lses = [], []
    for gi in range(N_GROUPS):
        if gi == 0:
            o, lse = _attention_group(qkv0, (0, 1, 2), batch, seq, gi)
        else:
            o, lse = _attention_group(rest, (gi - 1, gi + 1, gi + 3), batch, seq, gi)
        outs.append(o)
        lses.append(lse)
    m = batch * seq
    ospec = pl.BlockSpec((COMBINE_TS, ATTN_OUT), lambda i: (i, 0))
    lspec = pl.BlockSpec((COMBINE_TS, LANES), lambda i: (i, 0))
    return pl.pallas_call(
        _combine_kernel,
        grid=(m // COMBINE_TS,),
        in_specs=[ospec] * 3 + [lspec] * 3,
        out_specs=ospec,
        out_shape=jax.ShapeDtypeStruct((m, ATTN_OUT), BF16),
        compiler_params=_params("arbitrary"),
        name="attn_combine",
    )(*outs, *lses)


SSM_LT = 128
SSM_JBLK = 256
SSM_SCAN_COLS = 1024


def _gelu_tanh(y):
    return 0.5 * y * (1.0 + jnp.tanh(math.sqrt(2.0 / math.pi) * (y + 0.044715 * (y * y * y))))


def _ssm_kernel(u_ref, bblk_ref, cblk_ref, are_ref, aim_ref, d_ref, wglu_ref, bglu_ref,
                o_ref, sre_ref, sim_ref, hre_ref, him_ref, *, nbatch):
    rows, width = u_ref.shape
    njb = width // SSM_JBLK
    jstates = bblk_ref.shape[2] // 2
    nstate = njb * jstates
    pairs = rows // SUBLANES
    per_tile = SUBLANES // nbatch

    @pl.when(pl.program_id(0) == 0)
    def _():
        hre_ref[...] = jnp.zeros_like(hre_ref)
        him_ref[...] = jnp.zeros_like(him_ref)

    for j in range(njb):
        bu = _dot(u_ref[:, j * SSM_JBLK:(j + 1) * SSM_JBLK], bblk_ref[j])
        cs = slice(j * jstates, (j + 1) * jstates)
        sre_ref[:, :, cs] = bu[:, :jstates].reshape(pairs, SUBLANES, jstates)
        sim_ref[:, :, cs] = bu[:, jstates:].reshape(pairs, SUBLANES, jstates)

    for c in range(nstate // SSM_SCAN_COLS):
        cs = slice(c * SSM_SCAN_COLS, (c + 1) * SSM_SCAN_COLS)
        ar = are_ref[0:nbatch, cs]
        ai = aim_ref[0:nbatch, cs]

        def step(k, carry):
            hr, hi = carry
            for t in range(per_tile):
                rs = slice(t * nbatch, (t + 1) * nbatch)
                br = sre_ref[k, rs, cs]
                bi = sim_ref[k, rs, cs]
                nr = ar * hr - ai * hi + br
                ni = ar * hi + ai * hr + bi
                sre_ref[k, rs, cs] = nr
                sim_ref[k, rs, cs] = ni
                hr, hi = nr, ni
            return hr, hi

        hr, hi = lax.fori_loop(0, pairs, step,
                               (hre_ref[0:nbatch, cs], him_ref[0:nbatch, cs]))
        hre_ref[0:nbatch, cs] = hr
        him_ref[0:nbatch, cs] = hi

    ys = []
    for j in range(njb):
        cs = slice(j * jstates, (j + 1) * jstates)
        hr = sre_ref[:, :, cs].reshape(rows, jstates).astype(BF16)
        hi = sim_ref[:, :, cs].reshape(rows, jstates).astype(BF16)
        ys.append(_dot(hr, cblk_ref[j, 0:jstates, :]) + _dot(hi, cblk_ref[j, jstates:, :]))
    y = jnp.concatenate(ys, axis=1) + d_ref[...] * u_ref[...].astype(F32)
    z = _gelu_tanh(y).astype(BF16)
    g = _dot(z, wglu_ref[...]) + bglu_ref[...]
    o_ref[...] = (g[:, :width] * _sigmoid(g[:, width:])).astype(o_ref.dtype)


def _ssm_tables(a_re, a_im, log_dt, b_re, b_im, c_re, c_im):
    groups, nst = a_re.shape
    gpb = SSM_JBLK // SSM_GROUP_CH
    njb = groups // gpb
    lam = lax.complex(a_re.astype(F32), a_im.astype(F32))
    dt = jnp.exp(log_dt.astype(F32))[:, None]
    a_bar = jnp.exp(lam * dt)
    b_mat = lax.complex(b_re.astype(F32), b_im.astype(F32))
    b_bar = ((a_bar - 1.0) / lam)[:, :, None] * b_mat
    eye = jnp.eye(gpb, dtype=F32)

    def in_blocks(t):
        t = t.reshape(njb, gpb, nst, SSM_GROUP_CH)
        return jnp.einsum('jgnc,gh->jgchn', t, eye).reshape(njb, gpb * SSM_GROUP_CH, gpb * nst)

    def out_blocks(t):
        t = t.reshape(njb, gpb, SSM_GROUP_CH, nst)
        return jnp.einsum('jgcn,gh->jhngc', t, eye).reshape(njb, gpb * nst, gpb * SSM_GROUP_CH)

    bblk = jnp.concatenate([in_blocks(jnp.real(b_bar)), in_blocks(jnp.imag(b_bar))], axis=2)
    cblk = jnp.concatenate([out_blocks(c_re.astype(F32)), out_blocks(-c_im.astype(F32))], axis=1)
    are = jnp.broadcast_to(jnp.real(a_bar).reshape(1, groups * nst), (SUBLANES, groups * nst))
    aim = jnp.broadcast_to(jnp.imag(a_bar).reshape(1, groups * nst), (SUBLANES, groups * nst))
    return bblk.astype(BF16), cblk.astype(BF16), are, aim


def _s5_ssm(u_tb, nbatch, tables, d_skip, w_glu, b_glu):
    bblk, cblk, are, aim = tables
    total, width = u_tb.shape
    rows = SSM_LT * nbatch
    nstate = are.shape[1]
    full = lambda a: pl.BlockSpec(a.shape, lambda i: (0,) * a.ndim)
    d2 = d_skip.reshape(1, width).astype(F32)
    wg = w_glu.astype(BF16)
    bg = b_glu.reshape(1, 2 * width).astype(F32)
    return pl.pallas_call(
        functools.partial(_ssm_kernel, nbatch=nbatch),
        grid=(total // rows,),
        in_specs=[pl.BlockSpec((rows, width), lambda i: (i, 0)),
                  full(bblk), full(cblk), full(are), full(aim), full(d2), full(wg), full(bg)],
        out_specs=pl.BlockSpec((rows, width), lambda i: (i, 0)),
        out_shape=jax.ShapeDtypeStruct((total, width), BF16),
        scratch_shapes=[
            pltpu.VMEM((rows // SUBLANES, SUBLANES, nstate), F32),
            pltpu.VMEM((rows // SUBLANES, SUBLANES, nstate), F32),
            pltpu.VMEM((SUBLANES, nstate), F32),
            pltpu.VMEM((SUBLANES, nstate), F32),
        ],
        compiler_params=_params("arbitrary"),
        name="s5_ssm",
    )(u_tb, bblk, cblk, are, aim, d2, wg, bg)


ROUTER_TS = 256


def _router_kernel(x_ref, g_ref, sh_ref, sc_ref, rw_ref, rb_ref,
                   sel_ref, wts_ref, cnt_ref, carry_ref):
    @pl.when(jnp.logical_and(pl.program_id(0) == 0, pl.program_id(1) == 0))
    def _():
        carry_ref[...] = jnp.zeros_like(carry_ref)

    h = _rms(x_ref[...], g_ref[...]) * (1.0 + sc_ref[...]) + sh_ref[...]
    logits = jnp.dot(h, rw_ref[...], preferred_element_type=F32,
                     precision=lax.Precision.HIGHEST) + rb_ref[...]
    ts, ne = logits.shape
    idx = lax.broadcasted_iota(jnp.int32, logits.shape, 1)
    m1 = jnp.max(logits, axis=1, keepdims=True)
    i1 = jnp.min(jnp.where(logits == m1, idx, ne), axis=1, keepdims=True)
    rest = jnp.where(idx == i1, -jnp.inf, logits)
    m2 = jnp.max(rest, axis=1, keepdims=True)
    i2 = jnp.min(jnp.where(rest == m2, idx, ne), axis=1, keepdims=True)
    e = jnp.exp(m2 - m1)
    w1 = 1.0 / (1.0 + e)
    w2 = e / (1.0 + e)

    onehot = jnp.where(idx == i1, 1.0, 0.0) + jnp.where(idx == i2, 1.0, 0.0)
    row = lax.broadcasted_iota(jnp.int32, (ts, ts), 0)
    col = lax.broadcasted_iota(jnp.int32, (ts, ts), 1)
    lower = jnp.where(col < row, 1.0, 0.0).astype(BF16)
    before = carry_ref[...] + _dot(lower, onehot.astype(BF16))
    r1 = jnp.sum(jnp.where(idx == i1, before, 0.0), axis=1, keepdims=True).astype(jnp.int32)
    r2 = jnp.sum(jnp.where(idx == i2, before, 0.0), axis=1, keepdims=True).astype(jnp.int32)
    total = carry_ref[...] + jnp.sum(onehot, axis=0, keepdims=True)
    carry_ref[...] = total
    cnt_ref[...] = total
    sel_ref[...] = jnp.where(idx == 0, i1, jnp.where(idx == 1, i2, jnp.where(
        idx == 2, r1, jnp.where(idx == 3, r2, 0))))
    wts_ref[...] = jnp.where(idx == 0, w1, jnp.where(idx == 1, w2, 0.0))


def _router(x, g, shift, scale, router_w, router_b):
    b, s, d = x.shape
    ne = router_w.shape[-1]
    vec = pl.BlockSpec((None, 1, d), lambda i, j: (i, 0, 0))
    tok = pl.BlockSpec((None, ROUTER_TS, ne), lambda i, j: (i, j, 0))
    sel, wts, cnt = pl.pallas_call(
        _router_kernel,
        grid=(b, s // ROUTER_TS),
        in_specs=[
            pl.BlockSpec((None, ROUTER_TS, d), lambda i, j: (i, j, 0)),
            pl.BlockSpec((1, d), lambda i, j: (0, 0)),
            vec, vec,
            pl.BlockSpec((d, ne), lambda i, j: (0, 0)),
            pl.BlockSpec((1, ne), lambda i, j: (0, 0)),
        ],
        out_specs=[tok, tok, pl.BlockSpec((1, ne), lambda i, j: (0, 0))],
        out_shape=[jax.ShapeDtypeStruct((b, s, ne), jnp.int32),
                   jax.ShapeDtypeStruct((b, s, ne), F32),
                   jax.ShapeDtypeStruct((1, ne), F32)],
        scratch_shapes=[pltpu.VMEM((1, ne), F32)],
        compiler_params=_params("arbitrary", "arbitrary"),
        name="router",
    )(x, g.reshape(1, d), shift.reshape(b, 1, d), scale.reshape(b, 1, d),
      router_w, router_b.reshape(1, ne))
    return sel.reshape(b * s, ne), wts.reshape(b * s, ne), cnt


MOE_TM = 256
MOE_TT = 256
MOE_UP_TN = 1408
MOE_DOWN_TN = 1024


def _moe_plan(sel, cnt, tokens):
    ne = cnt.shape[-1]
    counts = cnt[0].astype(jnp.int32)
    padded = (counts + MOE_TM - 1) // MOE_TM * MOE_TM
    ends = jnp.cumsum(padded)
    starts = ends - padded
    pos1 = starts[sel[:, 0]] + sel[:, 2]
    pos2 = starts[sel[:, 1]] + sel[:, 3]
    nt = tokens // MOE_TT
    pos = jnp.concatenate([pos1.reshape(nt, MOE_TT), pos2.reshape(nt, MOE_TT)], axis=1)
    max_tiles = 2 * tokens // MOE_TM + ne
    tile_start = jnp.arange(max_tiles, dtype=jnp.int32) * MOE_TM
    tile_expert = jnp.minimum(
        jnp.sum((tile_start[:, None] >= ends[None, :]).astype(jnp.int32), axis=1), ne - 1)
    num_tiles = (ends[-1] // MOE_TM).reshape(1)
    return pos.reshape(nt, 1, 2 * MOE_TT), tile_expert, num_tiles, max_tiles


def _row_copies(pos_ref, t, make):
    tt = pos_ref.shape[1] // 2
    return make(0, t, pos_ref[0, t]), make(1, t, pos_ref[0, tt + t])


def _issue_and_drain(pos_ref, make):
    tt = pos_ref.shape[1] // 2

    def issue(t, carry):
        for cp in _row_copies(pos_ref, t, make):
            cp.start()
        return carry

    def drain(t, carry):
        for cp in _row_copies(pos_ref, t, make):
            cp.wait()
        return carry

    lax.fori_loop(0, tt, issue, 0)
    lax.fori_loop(0, tt, drain, 0)


def _moe_dispatch_kernel(pos_ref, x_ref, g_ref, sh_ref, sc_ref, init_ref, xs_ref, hbuf_ref, sem):
    del init_ref
    hbuf_ref[...] = _rms(x_ref[...], g_ref[...]) * (1.0 + sc_ref[...]) + sh_ref[...]

    def make(k, t, p):
        return pltpu.make_async_copy(hbuf_ref.at[pl.ds(t, 1), :], xs_ref.at[pl.ds(p, 1), :], sem)

    _issue_and_drain(pos_ref, make)


def _moe_dispatch(x, g, shift, scale, pos, rows):
    b, s, d = x.shape
    per_b = s // MOE_TT
    vec = pl.BlockSpec((None, 1, d), lambda i, j: (i, 0, 0))
    return pl.pallas_call(
        _moe_dispatch_kernel,
        grid=(b, per_b),
        in_specs=[
            pl.BlockSpec((None, 1, 2 * MOE_TT), lambda i, j: (i * per_b + j, 0, 0),
                         memory_space=pltpu.SMEM),
            pl.BlockSpec((None, MOE_TT, d), lambda i, j: (i, j, 0)),
            pl.BlockSpec((1, d), lambda i, j: (0, 0)),
            vec, vec,
            pl.BlockSpec(memory_space=pl.ANY),
        ],
        out_specs=pl.BlockSpec(memory_space=pl.ANY),
        out_shape=jax.ShapeDtypeStruct((rows, d), F32),
        scratch_shapes=[pltpu.VMEM((MOE_TT, d), F32), pltpu.SemaphoreType.DMA(())],
        input_output_aliases={5: 0},
        compiler_params=_params("arbitrary", "arbitrary"),
        name="moe_dispatch",
    )(pos, x, g.reshape(1, d), shift.reshape(b, 1, d), scale.reshape(b, 1, d),
      jnp.zeros((rows, d), F32))


def _moe_mm_kernel(te_ref, nt_ref, a_ref, w_ref, *rest, mode):
    if mode == "gate":
        u_ref, o_ref, wb_ref = rest
    else:
        o_ref, wb_ref = rest
    i = pl.program_id(1)
    active = i < nt_ref[0]
    new_expert = jnp.logical_or(i == 0, te_ref[i] != te_ref[jnp.maximum(i - 1, 0)])

    @pl.when(jnp.logical_and(active, new_expert))
    def _():
        _cast_weight(w_ref, wb_ref)

    @pl.when(active)
    def _():
        p = _dot(a_ref[...].astype(BF16), wb_ref[...])
        if mode == "silu":
            o_ref[...] = (p * _sigmoid(p)).astype(o_ref.dtype)
        elif mode == "gate":
            o_ref[...] = (u_ref[...].astype(F32) * p).astype(o_ref.dtype)
        else:
            o_ref[...] = p.astype(o_ref.dtype)

    @pl.when(jnp.logical_not(active))
    def _():
        o_ref[...] = jnp.zeros_like(o_ref)


def _moe_mm(a, w, li, plan, tn, out_dtype, mode, u=None):
    _, tile_expert, num_tiles, max_tiles = plan
    rows, k = a.shape
    n = w.shape[-1]

    def tile(i, nt):
        return jnp.minimum(i, nt[0] - 1)

    in_specs = [
        pl.BlockSpec((MOE_TM, k), lambda j, i, te, nt: (tile(i, nt), 0)),
        pl.BlockSpec((None, None, k, tn), lambda j, i, te, nt: (li, te[tile(i, nt)], 0, j)),
    ]
    args = [a, w]
    if mode == "gate":
        in_specs.append(pl.BlockSpec((MOE_TM, tn), lambda j, i, te, nt: (tile(i, nt), j)))
        args.append(u)
    return pl.pallas_call(
        functools.partial(_moe_mm_kernel, mode=mode),
        grid_spec=pltpu.PrefetchScalarGridSpec(
            num_scalar_prefetch=2,
            grid=(n // tn, max_tiles),
            in_specs=in_specs,
            out_specs=pl.BlockSpec((MOE_TM, tn), lambda j, i, te, nt: (i, j)),
            scratch_shapes=[pltpu.VMEM((k, tn), BF16)],
        ),
        out_shape=jax.ShapeDtypeStruct((rows, n), out_dtype),
        compiler_params=_params("arbitrary", "arbitrary"),
        name=f"moe_mm_{mode}",
    )(tile_expert, num_tiles, *args)


def _moe_combine_kernel(pos_ref, x_ref, g_ref, wts_ref, y_ref, o_ref, ybuf_ref, sem):
    def make(k, t, p):
        return pltpu.make_async_copy(y_ref.at[pl.ds(p, 1), :], ybuf_ref.at[k, pl.ds(t, 1), :], sem)

    _issue_and_drain(pos_ref, make)
    f = wts_ref[:, 0:1] * ybuf_ref[0] + wts_ref[:, 1:2] * ybuf_ref[1]
    o_ref[...] = x_ref[...] + g_ref[...] * f


def _moe_combine(x, gate, wts, y, pos, seq):
    m, d = x.shape
    nb = gate.shape[0]
    ne = wts.shape[-1]
    return pl.pallas_call(
        _moe_combine_kernel,
        grid=(m // MOE_TT,),
        in_specs=[
            pl.BlockSpec((None, 1, 2 * MOE_TT), lambda i: (i, 0, 0), memory_space=pltpu.SMEM),
            pl.BlockSpec((MOE_TT, d), lambda i: (i, 0)),
            pl.BlockSpec((None, 1, d), lambda i: (i * MOE_TT // seq, 0, 0)),
            pl.BlockSpec((MOE_TT, ne), lambda i: (i, 0)),
            pl.BlockSpec(memory_space=pl.ANY),
        ],
        out_specs=pl.BlockSpec((MOE_TT, d), lambda i: (i, 0)),
        out_shape=jax.ShapeDtypeStruct((m, d), F32),
        scratch_shapes=[pltpu.VMEM((2, MOE_TT, d), F32), pltpu.SemaphoreType.DMA(())],
        compiler_params=_params("arbitrary"),
        name="moe_combine",
    )(pos, x, gate.reshape(nb, 1, d), wts, y)


def kernel(x, c, ada_w, ada_b, norm_mix_g, norm_ffn_g, final_norm_g, w_in, ssm_a_re, ssm_a_im, ssm_log_dt, ssm_b_re, ssm_b_im, ssm_c_re, ssm_c_im, ssm_d, w_glu, b_glu, w_branch, w_out, ffn_w1, ffn_w3, ffn_w2, router_w, router_b, moe_w1, moe_w3, moe_w2):
    batch, seq, d = x.shape
    depth = ada_w.shape[0]
    tokens = batch * seq
    ssm_width = w_glu.shape[1]
    per_res = seq // TOKEN_RESIDUES
    tn_in = ATTN_OUT
    assert ssm_width == tn_in and w_in.shape[-1] == (3 * N_GROUPS + 1) * tn_in + 2 * d

    def to_residue_major(t):
        f = t.shape[-1]
        return t.reshape(batch, per_res, TOKEN_RESIDUES, f).transpose(0, 2, 1, 3).reshape(batch, seq, f)

    def to_time_batch(t):
        f = t.shape[-1]
        return t.reshape(batch, TOKEN_RESIDUES, per_res, f).transpose(2, 1, 0, 3).reshape(tokens, f)

    def from_time_batch(t):
        f = t.shape[-1]
        return t.reshape(per_res, TOKEN_RESIDUES, batch, f).transpose(2, 1, 0, 3).reshape(tokens, f)

    mod = _ada_modulation(c, ada_w, ada_b).reshape(depth, batch, 6, d)
    ones = jnp.ones((tokens, 1), F32)
    x2 = to_residue_major(x).reshape(tokens, d)
    for layer in range(depth):
        sh1, sc1, g1, sh2, sc2, g2 = (mod[layer, :, i] for i in range(6))

        h = _norm_mod(x2.reshape(batch, seq, d), norm_mix_g[layer], sh1, sc1)
        qkv0 = _mm_plain(h, w_in, (layer,), lambda j: N_GROUPS * j, 3, F32, tm=1024, tn=tn_in)
        rest = _mm_plain(h, w_in, (layer,),
                         lambda j: j + 1 + jnp.where(j >= 2, 1, 0) + jnp.where(j >= 4, 1, 0),
                         11, BF16, tm=1024, tn=tn_in)
        attn = _dilation_mixture_attention(qkv0, rest, batch, seq)
        u_tb = to_time_batch(rest[:, 6 * tn_in:7 * tn_in])
        tables = _ssm_tables(ssm_a_re[layer], ssm_a_im[layer], ssm_log_dt[layer],
                             ssm_b_re[layer], ssm_b_im[layer], ssm_c_re[layer], ssm_c_im[layer])
        ssm = from_time_batch(
            _s5_ssm(u_tb, batch, tables, ssm_d[layer], w_glu[layer], b_glu[layer]))
        merged = _mm_branch(attn, ssm, w_branch, layer, rest, 7 * tn_in, tm=1024, tn=1024)
        x2 = _mm_resid(merged, w_out, (layer,), x2, g1, seq, tm=1024, tn=1024)

        li = layer // 2
        x3 = x2.reshape(batch, seq, d)
        if layer % 2 == 0:
            h = _norm_mod(x3, norm_ffn_g[layer], sh2, sc2)
            act = _mm_swiglu(h, ffn_w1, ffn_w3, (li,), ones, tm=1024, tn=512)
            x2 = _mm_resid(act, ffn_w2, (li,), x2, g2, seq, tm=512, tn=512)
        else:
            sel, wts, cnt = _router(x3, norm_ffn_g[layer], sh2, sc2, router_w[li], router_b[li])
            plan = _moe_plan(sel, cnt, tokens)
            xs = _moe_dispatch(x3, norm_ffn_g[layer], sh2, sc2, plan[0], plan[3] * MOE_TM)
            u = _moe_mm(xs, moe_w1, li, plan, MOE_UP_TN, BF16, "silu")
            act = _moe_mm(xs, moe_w3, li, plan, MOE_UP_TN, BF16, "gate", u)
            y = _moe_mm(act, moe_w2, li, plan, MOE_DOWN_TN, F32, "plain")
            x2 = _moe_combine(x2, g2, wts, y, plan[0], seq)
    out = _final_norm(x2.reshape(batch, seq, d), final_norm_g)
    return out.reshape(batch, TOKEN_RESIDUES, per_res, d).transpose(0, 2, 1, 3).reshape(batch, seq, d)
```

```python
import functools
import math

import jax
import jax.numpy as jnp
from jax import lax
from jax.experimental import pallas as pl
from jax.experimental.pallas import tpu as pltpu

F32 = jnp.float32
BF16 = jnp.bfloat16

LANES = 128
SUBLANES = 8
VMEM_LIMIT_BYTES = 56 * 1024 * 1024

ATTN_PATTERNS = ((128, 1), (512, 4), (2048, 16))
N_GROUPS = len(ATTN_PATTERNS)
HEADS = 8
HEAD_DIM = 128
ATTN_OUT = HEADS * HEAD_DIM
QKV_WIDTH = N_GROUPS * ATTN_OUT
ATTN_BLOCK = 128
SSM_GROUP_CH = 16
SSM_STATE = 64
N_EXPERTS = 8
RMS_EPS = 1e-6
NEG_BIG = -1e30


def _params(*semantics):
    return pltpu.CompilerParams(dimension_semantics=semantics,
                                vmem_limit_bytes=VMEM_LIMIT_BYTES)


def _sigmoid(v):
    return 1.0 / (1.0 + jnp.exp(-v))


ADA_TN = 512


def _ada_kernel(cb_ref, w_ref, b_ref, o_ref, act_ref):
    nb, k = cb_ref.shape[0], cb_ref.shape[1]
    nj = w_ref.shape[1] // LANES

    @pl.when(jnp.logical_and(pl.program_id(0) == 0, pl.program_id(1) == 0))
    def _():
        cb = cb_ref[...]
        act_ref[...] = cb * _sigmoid(cb)

    def body(i, accs):
        r = pl.multiple_of(i * SUBLANES, SUBLANES)
        w = [w_ref[pl.ds(r, SUBLANES), j * LANES:(j + 1) * LANES] for j in range(nj)]
        out = []
        for b in range(nb):
            act = act_ref[b, pl.ds(r, SUBLANES), :]
            out.extend(accs[b * nj + j] + w[j] * act for j in range(nj))
        return tuple(out)

    zero = jnp.zeros((SUBLANES, LANES), F32)
    accs = lax.fori_loop(0, k // SUBLANES, body, (zero,) * (nb * nj), unroll=2)
    for b in range(nb):
        for j in range(nj):
            sl = slice(j * LANES, (j + 1) * LANES)
            o_ref[b:b + 1, sl] = jnp.sum(accs[b * nj + j], axis=0, keepdims=True) + b_ref[:, sl]


def _ada_modulation(c, ada_w, ada_b):
    depth, k, n = ada_w.shape
    nb = c.shape[0]
    cb = jnp.broadcast_to(c[:, :, None], (nb, k, LANES))
    return pl.pallas_call(
        _ada_kernel,
        grid=(depth, n // ADA_TN),
        in_specs=[
            pl.BlockSpec((nb, k, LANES), lambda l, j: (0, 0, 0)),
            pl.BlockSpec((None, k, ADA_TN), lambda l, j: (l, 0, j)),
            pl.BlockSpec((None, 1, ADA_TN), lambda l, j: (l, 0, j)),
        ],
        out_specs=pl.BlockSpec((None, nb, ADA_TN), lambda l, j: (l, 0, j)),
        out_shape=jax.ShapeDtypeStruct((depth, nb, n), F32),
        scratch_shapes=[pltpu.VMEM((nb, k, LANES), F32)],
        compiler_params=_params("arbitrary", "arbitrary"),
        name="ada_modulation",
    )(cb, ada_w, ada_b.reshape(depth, 1, n))


NORM_TS = 512


def _rms(x, g):
    return x * lax.rsqrt(jnp.mean(x * x, axis=-1, keepdims=True) + RMS_EPS) * g


def _norm_mod_kernel(x_ref, g_ref, sh_ref, sc_ref, o_ref):
    y = _rms(x_ref[...], g_ref[...])
    o_ref[...] = (y * (1.0 + sc_ref[...]) + sh_ref[...]).astype(o_ref.dtype)


def _norm_mod(x, g, shift, scale):
    b, s, d = x.shape
    vec = pl.BlockSpec((None, 1, d), lambda i, j: (i, 0, 0))
    out = pl.pallas_call(
        _norm_mod_kernel,
        grid=(b, s // NORM_TS),
        in_specs=[
            pl.BlockSpec((None, NORM_TS, d), lambda i, j: (i, j, 0)),
            pl.BlockSpec((1, d), lambda i, j: (0, 0)),
            vec, vec,
        ],
        out_specs=pl.BlockSpec((None, NORM_TS, d), lambda i, j: (i, j, 0)),
        out_shape=jax.ShapeDtypeStruct((b, s, d), BF16),
        compiler_params=_params("arbitrary", "arbitrary"),
        name="norm_mod",
    )(x, g.reshape(1, d), shift.reshape(b, 1, d), scale.reshape(b, 1, d))
    return out.reshape(b * s, d)


def _final_norm_kernel(x_ref, g_ref, o_ref):
    o_ref[...] = _rms(x_ref[...], g_ref[...])


def _final_norm(x, g):
    b, s, d = x.shape
    return pl.pallas_call(
        _final_norm_kernel,
        grid=(b, s // NORM_TS),
        in_specs=[
            pl.BlockSpec((None, NORM_TS, d), lambda i, j: (i, j, 0)),
            pl.BlockSpec((1, d), lambda i, j: (0, 0)),
        ],
        out_specs=pl.BlockSpec((None, NORM_TS, d), lambda i, j: (i, j, 0)),
        out_shape=jax.ShapeDtypeStruct((b, s, d), F32),
        compiler_params=_params("arbitrary", "arbitrary"),
        name="final_norm",
    )(x, g.reshape(1, d))


CAST_ROWS = 256


def _cast_weight(w_ref, wb_ref):
    def body(i, carry):
        r = pl.multiple_of(i * CAST_ROWS, CAST_ROWS)
        wb_ref[pl.ds(r, CAST_ROWS), :] = w_ref[pl.ds(r, CAST_ROWS), :].astype(BF16)
        return carry
    lax.fori_loop(0, w_ref.shape[0] // CAST_ROWS, body, 0)


def _dot(a, b):
    return jnp.dot(a, b, preferred_element_type=F32)


def _mm_plain_kernel(a_ref, w_ref, o_ref, wb_ref):
    @pl.when(pl.program_id(1) == 0)
    def _():
        _cast_weight(w_ref, wb_ref)
    o_ref[...] = _dot(a_ref[...], wb_ref[...]).astype(o_ref.dtype)


def _mm_plain(a, w, w_idx, col_tile, n_tiles, out_dtype, tm, tn):
    m, k = a.shape
    lead = (None,) * len(w_idx)
    return pl.pallas_call(
        _mm_plain_kernel,
        grid=(n_tiles, m // tm),
        in_specs=[
            pl.BlockSpec((tm, k), lambda j, i: (i, 0)),
            pl.BlockSpec(lead + (k, tn), lambda j, i: w_idx + (0, col_tile(j))),
        ],
        out_specs=pl.BlockSpec((tm, tn), lambda j, i: (i, j)),
        out_shape=jax.ShapeDtypeStruct((m, n_tiles * tn), out_dtype),
        scratch_shapes=[pltpu.VMEM((k, tn), BF16)],
        compiler_params=_params("arbitrary", "arbitrary"),
        name="mm_plain",
    )(a, w)


def _mm_swiglu_kernel(a_ref, w1_ref, w3_ref, rs_ref, o_ref, wb1_ref, wb3_ref):
    @pl.when(pl.program_id(1) == 0)
    def _():
        _cast_weight(w1_ref, wb1_ref)
        _cast_weight(w3_ref, wb3_ref)
    a = a_ref[...]
    u = _dot(a, wb1_ref[...])
    v = _dot(a, wb3_ref[...])
    o_ref[...] = (u * _sigmoid(u) * v * rs_ref[...]).astype(o_ref.dtype)


def _mm_swiglu(a, w1, w3, w_idx, row_scale, tm, tn):
    m, k = a.shape
    n = w1.shape[-1]
    lead = (None,) * len(w_idx)
    wspec = pl.BlockSpec(lead + (k, tn), lambda j, i: w_idx + (0, j))
    return pl.pallas_call(
        _mm_swiglu_kernel,
        grid=(n // tn, m // tm),
        in_specs=[
            pl.BlockSpec((tm, k), lambda j, i: (i, 0)),
            wspec, wspec,
            pl.BlockSpec((tm, 1), lambda j, i: (i, 0)),
        ],
        out_specs=pl.BlockSpec((tm, tn), lambda j, i: (i, j)),
        out_shape=jax.ShapeDtypeStruct((m, n), BF16),
        scratch_shapes=[pltpu.VMEM((k, tn), BF16), pltpu.VMEM((k, tn), BF16)],
        compiler_params=_params("arbitrary", "arbitrary"),
        name="mm_swiglu",
    )(a, w1, w3, row_scale)


def _mm_resid_kernel(a_ref, w_ref, x_ref, g_ref, o_ref, wb_ref):
    @pl.when(pl.program_id(1) == 0)
    def _():
        _cast_weight(w_ref, wb_ref)
    o_ref[...] = x_ref[...] + g_ref[...] * _dot(a_ref[...], wb_ref[...])


def _mm_resid(a, w, w_idx, x, gate, seq, tm, tn):
    m, k = a.shape
    n = w.shape[-1]
    nb = gate.shape[0]
    lead = (None,) * len(w_idx)
    return pl.pallas_call(
        _mm_resid_kernel,
        grid=(n // tn, m // tm),
        in_specs=[
            pl.BlockSpec((tm, k), lambda j, i: (i, 0)),
            pl.BlockSpec(lead + (k, tn), lambda j, i: w_idx + (0, j)),
            pl.BlockSpec((tm, tn), lambda j, i: (i, j)),
            pl.BlockSpec((None, 1, tn), lambda j, i: (i * tm // seq, 0, j)),
        ],
        out_specs=pl.BlockSpec((tm, tn), lambda j, i: (i, j)),
        out_shape=jax.ShapeDtypeStruct((m, n), F32),
        scratch_shapes=[pltpu.VMEM((k, tn), BF16)],
        compiler_params=_params("arbitrary", "arbitrary"),
        name="mm_resid",
    )(a, w, x, gate.reshape(nb, 1, n))


def _mm_branch_kernel(at_ref, ss_ref, wa_ref, ws_ref, ga_ref, gs_ref, o_ref,
                      wba_ref, wbs_ref):
    @pl.when(pl.program_id(1) == 0)
    def _():
        _cast_weight(wa_ref, wba_ref)
        _cast_weight(ws_ref, wbs_ref)
    pa = _dot(at_ref[...], wba_ref[...])
    ps = _dot(ss_ref[...], wbs_ref[...])
    ga = _sigmoid(ga_ref[...].astype(F32))
    gs = _sigmoid(gs_ref[...].astype(F32))
    o_ref[...] = (ga * pa + gs * ps).astype(o_ref.dtype)


def _mm_branch(attn, ssm, w_branch, layer, proj, gate_col, tm, tn):
    m, k = attn.shape
    n = w_branch.shape[-1]
    g0 = gate_col // tn
    g1 = (gate_col + n) // tn
    return pl.pallas_call(
        _mm_branch_kernel,
        grid=(n // tn, m // tm),
        in_specs=[
            pl.BlockSpec((tm, k), lambda j, i: (i, 0)),
            pl.BlockSpec((tm, k), lambda j, i: (i, 0)),
            pl.BlockSpec((None, k, tn), lambda j, i: (layer, 0, j)),
            pl.BlockSpec((None, k, tn), lambda j, i: (layer, 1, j)),
            pl.BlockSpec((tm, tn), lambda j, i: (i, g0 + j)),
            pl.BlockSpec((tm, tn), lambda j, i: (i, g1 + j)),
        ],
        out_specs=pl.BlockSpec((tm, tn), lambda j, i: (i, j)),
        out_shape=jax.ShapeDtypeStruct((m, n), BF16),
        scratch_shapes=[pltpu.VMEM((k, tn), BF16), pltpu.VMEM((k, tn), BF16)],
        compiler_params=_params("arbitrary", "arbitrary"),
        name="mm_branch",
    )(attn, ssm, w_branch, w_branch, proj, proj)


TOKEN_RESIDUES = 16
ATTN_BB = 4


def _dot_nt(a, b):
    return lax.dot_general(a, b, (((1,), (1,)), ((), ())), preferred_element_type=F32)


def _attn_kernel(*refs, with_prev):
    if with_prev:
        q_ref, kp_ref, kc_ref, vp_ref, vc_ref, bc_ref, bp_ref, o_ref, lse_ref = refs
    else:
        q_ref, kc_ref, vc_ref, bc_ref, o_ref, lse_ref = refs
    blk = ATTN_BLOCK
    bb, nc, rpc = q_ref.shape[0], q_ref.shape[1], q_ref.shape[2]
    scale = HEAD_DIM ** -0.5
    lane = lax.broadcasted_iota(jnp.int32, (blk, LANES), 1)

    def heads(ref, bi):
        x = ref[bi].reshape(blk, ATTN_OUT)
        return jnp.stack([x[:, h * HEAD_DIM:(h + 1) * HEAD_DIM] for h in range(HEADS)]
                         ).astype(BF16)

    def qk(q, k):
        return lax.dot_general(q, k, (((2,), (2,)), ((0,), (0,))), preferred_element_type=F32)

    def pv(p, v):
        return lax.dot_general(p.astype(BF16), v, (((2,), (1,)), ((0,), (0,))),
                               preferred_element_type=F32)

    for bi in range(bb):
        q = heads(q_ref, bi)
        s_c = qk(q, heads(kc_ref, bi)) * scale + bc_ref[...]
        m = jnp.max(s_c, axis=2, keepdims=True)
        if with_prev:
            bias_p = jnp.where(pl.program_id(2) > 0, bp_ref[...], NEG_BIG)
            s_p = qk(q, heads(kp_ref, bi)) * scale + bias_p
            m = jnp.maximum(m, jnp.max(s_p, axis=2, keepdims=True))
        p_c = jnp.exp(s_c - m)
        l = jnp.sum(p_c, axis=2, keepdims=True)
        o = pv(p_c, heads(vc_ref, bi))
        if with_prev:
            p_p = jnp.exp(s_p - m)
            l = l + jnp.sum(p_p, axis=2, keepdims=True)
            o = o + pv(p_p, heads(vp_ref, bi))
        o = o / l
        lse = m + jnp.log(l)
        lse_tile = jnp.zeros((blk, LANES), F32)
        for h in range(HEADS):
            sl = slice(h * HEAD_DIM, (h + 1) * HEAD_DIM)
            o_ref[bi, :, :, sl] = o[h].reshape(nc, rpc, HEAD_DIM).astype(o_ref.dtype)
            lse_tile = jnp.where(lane == h, lse[h], lse_tile)
        lse_ref[bi] = lse_tile.reshape(nc, rpc, LANES)


def _attn_bias(dilation, nc, back):
    rpc = ATTN_BLOCK // nc
    i = jnp.arange(ATTN_BLOCK, dtype=jnp.int32)
    off = nc * (i % rpc) + i // rpc
    dist = off[:, None] - off[None, :] + back * ATTN_BLOCK
    valid = jnp.logical_and(dist >= 0, dist <= ATTN_BLOCK)
    slopes = 2.0 ** (-8.0 * (jnp.arange(HEADS, dtype=F32) + 1.0) / HEADS)
    pen = slopes[:, None, None] * (dist * dilation).astype(F32)[None]
    return jnp.where(valid[None], -pen, NEG_BIG)


def _attention_group(qkv, qkv_cols, batch, seq, gi):
    window, dilation = ATTN_PATTERNS[gi]
    assert window // dilation == ATTN_BLOCK
    cols = qkv.shape[-1]
    per_res = seq // TOKEN_RESIDUES
    nc = TOKEN_RESIDUES // dilation
    rpc = ATTN_BLOCK // nc
    nblk = seq // dilation // ATTN_BLOCK
    with_prev = nblk > 1

    def shape5(c):
        return (batch, nc, dilation, per_res, c)

    def spec(c, col, back):
        return pl.BlockSpec(
            (ATTN_BB, nc, None, rpc, c),
            lambda b, r, n: (b, 0, r, jnp.maximum(n - back, 0), col))

    bias_spec = pl.BlockSpec((HEADS, ATTN_BLOCK, ATTN_BLOCK), lambda b, r, n: (0, 0, 0))
    in_specs = [spec(ATTN_OUT, qkv_cols[0], 0)]
    for col in qkv_cols[1:]:
        if with_prev:
            in_specs.append(spec(ATTN_OUT, col, 1))
        in_specs.append(spec(ATTN_OUT, col, 0))
    view = qkv.reshape(shape5(cols))
    args = [view] * len(in_specs) + [_attn_bias(dilation, nc, 0)]
    in_specs.append(bias_spec)
    if with_prev:
        args.append(_attn_bias(dilation, nc, 1))
        in_specs.append(bias_spec)
    o, lse = pl.pallas_call(
        functools.partial(_attn_kernel, with_prev=with_prev),
        grid=(batch // ATTN_BB, dilation, nblk),
        in_specs=in_specs,
        out_specs=[spec(ATTN_OUT, 0, 0), spec(LANES, 0, 0)],
        out_shape=[
            jax.ShapeDtypeStruct(shape5(ATTN_OUT), qkv.dtype),
            jax.ShapeDtypeStruct(shape5(LANES), F32),
        ],
        compiler_params=_params("arbitrary", "arbitrary", "arbitrary"),
        name=f"attn_g{gi}",
    )(*args)
    return o.reshape(batch * seq, ATTN_OUT), lse.reshape(batch * seq, LANES)


COMBINE_TS = 512


def _combine_kernel(o0_ref, o1_ref, o2_ref, l0_ref, l1_ref, l2_ref, out_ref):
    for h in range(HEADS):
        sl = slice(h * HEAD_DIM, (h + 1) * HEAD_DIM)
        a0 = l0_ref[:, h:h + 1]
        a1 = l1_ref[:, h:h + 1]
        a2 = l2_ref[:, h:h + 1]
        m = jnp.maximum(jnp.maximum(a0, a1), a2)
        e0 = jnp.exp(a0 - m)
        e1 = jnp.exp(a1 - m)
        e2 = jnp.exp(a2 - m)
        inv = 1.0 / (e0 + e1 + e2)
        mix = ((e0 * inv) * o0_ref[:, sl].astype(F32)
               + (e1 * inv) * o1_ref[:, sl].astype(F32)
               + (e2 * inv) * o2_ref[:, sl].astype(F32))
        out_ref[:, sl] = mix.astype(out_ref.dtype)


def _dilation_mixture_attention(qkv0, rest, batch, seq):
    outs, lses = [], []
    for gi in range(N_GROUPS):
        if gi == 0:
            o, lse = _attention_group(qkv0, (0, 1, 2), batch, seq, gi)
        else:
            o, lse = _attention_group(rest, (gi - 1, gi + 1, gi + 3), batch, seq, gi)
        outs.append(o)
        lses.append(lse)
    m = batch * seq
    ospec = pl.BlockSpec((COMBINE_TS, ATTN_OUT), lambda i: (i, 0))
    lspec = pl.BlockSpec((COMBINE_TS, LANES), lambda i: (i, 0))
    return pl.pallas_call(
        _combine_kernel,
        grid=(m // COMBINE_TS,),
        in_specs=[ospec] * 3 + [lspec] * 3,
        out_specs=ospec,
        out_shape=jax.ShapeDtypeStruct((m, ATTN_OUT), BF16),
        compiler_params=_params("arbitrary"),
        name="attn_combine",
    )(*outs, *lses)


SSM_LT = 128
SSM_JBLK = 256
SSM_SCAN_COLS = 1024


def _gelu_tanh(y):
    return 0.5 * y * (1.0 + jnp.tanh(math.sqrt(2.0 / math.pi) * (y + 0.044715 * (y * y * y))))


def _ssm_kernel(u_ref, bblk_ref, cblk_ref, are_ref, aim_ref, d_ref, wglu_ref, bglu_ref,
                o_ref, sre_ref, sim_ref, hre_ref, him_ref, *, nbatch):
    rows, width = u_ref.shape
    njb = width // SSM_JBLK
    jstates = bblk_ref.shape[2] // 2
    nstate = njb * jstates
    pairs = rows // SUBLANES
    per_tile = SUBLANES // nbatch

    @pl.when(pl.program_id(0) == 0)
    def _():
        hre_ref[...] = jnp.zeros_like(hre_ref)
        him_ref[...] = jnp.zeros_like(him_ref)

    for j in range(njb):
        bu = _dot(u_ref[:, j * SSM_JBLK:(j + 1) * SSM_JBLK], bblk_ref[j])
        cs = slice(j * jstates, (j + 1) * jstates)
        sre_ref[:, :, cs] = bu[:, :jstates].reshape(pairs, SUBLANES, jstates)
        sim_ref[:, :, cs] = bu[:, jstates:].reshape(pairs, SUBLANES, jstates)

    for c in range(nstate // SSM_SCAN_COLS):
        cs = slice(c * SSM_SCAN_COLS, (c + 1) * SSM_SCAN_COLS)
        ar = are_ref[0:nbatch, cs]
        ai = aim_ref[0:nbatch, cs]

        def step(k, carry):
            hr, hi = carry
            for t in range(per_tile):
                rs = slice(t * nbatch, (t + 1) * nbatch)
                br = sre_ref[k, rs, cs]
                bi = sim_ref[k, rs, cs]
                nr = ar * hr - ai * hi + br
                ni = ar * hi + ai * hr + bi
                sre_ref[k, rs, cs] = nr
                sim_ref[k, rs, cs] = ni
                hr, hi = nr, ni
            return hr, hi

        hr, hi = lax.fori_loop(0, pairs, step,
                               (hre_ref[0:nbatch, cs], him_ref[0:nbatch, cs]))
        hre_ref[0:nbatch, cs] = hr
        him_ref[0:nbatch, cs] = hi

    ys = []
    for j in range(njb):
        cs = slice(j * jstates, (j + 1) * jstates)
        hr = sre_ref[:, :, cs].reshape(rows, jstates).astype(BF16)
        hi = sim_ref[:, :, cs].reshape(rows, jstates).astype(BF16)
        ys.append(_dot(hr, cblk_ref[j, 0:jstates, :]) + _dot(hi, cblk_ref[j, jstates:, :]))
    y = jnp.concatenate(ys, axis=1) + d_ref[...] * u_ref[...].astype(F32)
    z = _gelu_tanh(y).astype(BF16)
    g = _dot(z, wglu_ref[...]) + bglu_ref[...]
    o_ref[...] = (g[:, :width] * _sigmoid(g[:, width:])).astype(o_ref.dtype)


def _ssm_tables(a_re, a_im, log_dt, b_re, b_im, c_re, c_im):
    groups, nst = a_re.shape
    gpb = SSM_JBLK // SSM_GROUP_CH
    njb = groups // gpb
    lam = lax.complex(a_re.astype(F32), a_im.astype(F32))
    dt = jnp.exp(log_dt.astype(F32))[:, None]
    a_bar = jnp.exp(lam * dt)
    b_mat = lax.complex(b_re.astype(F32), b_im.astype(F32))
    b_bar = ((a_bar - 1.0) / lam)[:, :, None] * b_mat
    eye = jnp.eye(gpb, dtype=F32)

    def in_blocks(t):
        t = t.reshape(njb, gpb, nst, SSM_GROUP_CH)
        return jnp.einsum('jgnc,gh->jgchn', t, eye).reshape(njb, gpb * SSM_GROUP_CH, gpb * nst)

    def out_blocks(t):
        t = t.reshape(njb, gpb, SSM_GROUP_CH, nst)
        return jnp.einsum('jgcn,gh->jhngc', t, eye).reshape(njb, gpb * nst, gpb * SSM_GROUP_CH)

    bblk = jnp.concatenate([in_blocks(jnp.real(b_bar)), in_blocks(jnp.imag(b_bar))], axis=2)
    cblk = jnp.concatenate([out_blocks(c_re.astype(F32)), out_blocks(-c_im.astype(F32))], axis=1)
    are = jnp.broadcast_to(jnp.real(a_bar).reshape(1, groups * nst), (SUBLANES, groups * nst))
    aim = jnp.broadcast_to(jnp.imag(a_bar).reshape(1, groups * nst), (SUBLANES, groups * nst))
    return bblk.astype(BF16), cblk.astype(BF16), are, aim


def _s5_ssm(u_tb, nbatch, tables, d_skip, w_glu, b_glu):
    bblk, cblk, are, aim = tables
    total, width = u_tb.shape
    rows = SSM_LT * nbatch
    nstate = are.shape[1]
    full = lambda a: pl.BlockSpec(a.shape, lambda i: (0,) * a.ndim)
    d2 = d_skip.reshape(1, width).astype(F32)
    wg = w_glu.astype(BF16)
    bg = b_glu.reshape(1, 2 * width).astype(F32)
    return pl.pallas_call(
        functools.partial(_ssm_kernel, nbatch=nbatch),
        grid=(total // rows,),
        in_specs=[pl.BlockSpec((rows, width), lambda i: (i, 0)),
                  full(bblk), full(cblk), full(are), full(aim), full(d2), full(wg), full(bg)],
        out_specs=pl.BlockSpec((rows, width), lambda i: (i, 0)),
        out_shape=jax.ShapeDtypeStruct((total, width), BF16),
        scratch_shapes=[
            pltpu.VMEM((rows // SUBLANES, SUBLANES, nstate), F32),
            pltpu.VMEM((rows // SUBLANES, SUBLANES, nstate), F32),
            pltpu.VMEM((SUBLANES, nstate), F32),
            pltpu.VMEM((SUBLANES, nstate), F32),
        ],
        compiler_params=_params("arbitrary"),
        name="s5_ssm",
    )(u_tb, bblk, cblk, are, aim, d2, wg, bg)


ROUTER_TS = 256


def _router_kernel(x_ref, g_ref, sh_ref, sc_ref, rw_ref, rb_ref,
                   sel_ref, wts_ref, cnt_ref, carry_ref):
    @pl.when(jnp.logical_and(pl.program_id(0) == 0, pl.program_id(1) == 0))
    def _():
        carry_ref[...] = jnp.zeros_like(carry_ref)

    h = _rms(x_ref[...], g_ref[...]) * (1.0 + sc_ref[...]) + sh_ref[...]
    logits = jnp.dot(h, rw_ref[...], preferred_element_type=F32,
                     precision=lax.Precision.HIGHEST) + rb_ref[...]
    ts, ne = logits.shape
    idx = lax.broadcasted_iota(jnp.int32, logits.shape, 1)
    m1 = jnp.max(logits, axis=1, keepdims=True)
    i1 = jnp.min(jnp.where(logits == m1, idx, ne), axis=1, keepdims=True)
    rest = jnp.where(idx == i1, -jnp.inf, logits)
    m2 = jnp.max(rest, axis=1, keepdims=True)
    i2 = jnp.min(jnp.where(rest == m2, idx, ne), axis=1, keepdims=True)
    e = jnp.exp(m2 - m1)
    w1 = 1.0 / (1.0 + e)
    w2 = e / (1.0 + e)

    onehot = jnp.where(idx == i1, 1.0, 0.0) + jnp.where(idx == i2, 1.0, 0.0)
    row = lax.broadcasted_iota(jnp.int32, (ts, ts), 0)
    col = lax.broadcasted_iota(jnp.int32, (ts, ts), 1)
    lower = jnp.where(col < row, 1.0, 0.0).astype(BF16)
    before = carry_ref[...] + _dot(lower, onehot.astype(BF16))
    r1 = jnp.sum(jnp.where(idx == i1, before, 0.0), axis=1, keepdims=True).astype(jnp.int32)
    r2 = jnp.sum(jnp.where(idx == i2, before, 0.0), axis=1, keepdims=True).astype(jnp.int32)
    total = carry_ref[...] + jnp.sum(onehot, axis=0, keepdims=True)
    carry_ref[...] = total
    cnt_ref[...] = total
    sel_ref[...] = jnp.where(idx == 0, i1, jnp.where(idx == 1, i2, jnp.where(
        idx == 2, r1, jnp.where(idx == 3, r2, 0))))
    wts_ref[...] = jnp.where(idx == 0, w1, jnp.where(idx == 1, w2, 0.0))


def _router(x, g, shift, scale, router_w, router_b):
    b, s, d = x.shape
    ne = router_w.shape[-1]
    vec = pl.BlockSpec((None, 1, d), lambda i, j: (i, 0, 0))
    tok = pl.BlockSpec((None, ROUTER_TS, ne), lambda i, j: (i, j, 0))
    sel, wts, cnt = pl.pallas_call(
        _router_kernel,
        grid=(b, s // ROUTER_TS),
        in_specs=[
            pl.BlockSpec((None, ROUTER_TS, d), lambda i, j: (i, j, 0)),
            pl.BlockSpec((1, d), lambda i, j: (0, 0)),
            vec, vec,
            pl.BlockSpec((d, ne), lambda i, j: (0, 0)),
            pl.BlockSpec((1, ne), lambda i, j: (0, 0)),
        ],
        out_specs=[tok, tok, pl.BlockSpec((1, ne), lambda i, j: (0, 0))],
        out_shape=[jax.ShapeDtypeStruct((b, s, ne), jnp.int32),
                   jax.ShapeDtypeStruct((b, s, ne), F32),
                   jax.ShapeDtypeStruct((1, ne), F32)],
        scratch_shapes=[pltpu.VMEM((1, ne), F32)],
        compiler_params=_params("arbitrary", "arbitrary"),
        name="router",
    )(x, g.reshape(1, d), shift.reshape(b, 1, d), scale.reshape(b, 1, d),
      router_w, router_b.reshape(1, ne))
    return sel.reshape(b * s, ne), wts.reshape(b * s, ne), cnt


MOE_TM = 256
MOE_TT = 256
MOE_UP_TN = 1408
MOE_DOWN_TN = 1024


def _moe_plan(sel, cnt, tokens):
    ne = cnt.shape[-1]
    counts = cnt[0].astype(jnp.int32)
    padded = (counts + MOE_TM - 1) // MOE_TM * MOE_TM
    ends = jnp.cumsum(padded)
    starts = ends - padded
    pos1 = starts[sel[:, 0]] + sel[:, 2]
    pos2 = starts[sel[:, 1]] + sel[:, 3]
    nt = tokens // MOE_TT
    pos = jnp.concatenate([pos1.reshape(nt, MOE_TT), pos2.reshape(nt, MOE_TT)], axis=1)
    max_tiles = 2 * tokens // MOE_TM + ne
    tile_start = jnp.arange(max_tiles, dtype=jnp.int32) * MOE_TM
    tile_expert = jnp.minimum(
        jnp.sum((tile_start[:, None] >= ends[None, :]).astype(jnp.int32), axis=1), ne - 1)
    num_tiles = (ends[-1] // MOE_TM).reshape(1)
    zero_tiles = jnp.concatenate([
        jnp.maximum(ends // MOE_TM - 1, 0),
        jnp.minimum(num_tiles[0] + jnp.arange(ne, dtype=jnp.int32), max_tiles - 1)])
    return pos.reshape(nt, 1, 2 * MOE_TT), tile_expert, num_tiles, max_tiles, zero_tiles


def _row_copies(pos_ref, t, make):
    tt = pos_ref.shape[1] // 2
    return make(0, t, pos_ref[0, t]), make(1, t, pos_ref[0, tt + t])


def _issue_and_drain(pos_ref, make):
    tt = pos_ref.shape[1] // 2

    def issue(t, carry):
        for cp in _row_copies(pos_ref, t, make):
            cp.start()
        return carry

    def drain(t, carry):
        for cp in _row_copies(pos_ref, t, make):
            cp.wait()
        return carry

    lax.fori_loop(0, tt, issue, 0, unroll=4)
    lax.fori_loop(0, tt, drain, 0, unroll=4)


def _moe_dispatch_kernel(pos_ref, zt_ref, x_ref, g_ref, sh_ref, sc_ref, xs_ref, hbuf_ref, sem):
    @pl.when(jnp.logical_and(pl.program_id(0) == 0, pl.program_id(1) == 0))
    def _():
        hbuf_ref[...] = jnp.zeros_like(hbuf_ref)
        for k in range(zt_ref.shape[0]):
            cp = pltpu.make_async_copy(
                hbuf_ref, xs_ref.at[pl.ds(pl.multiple_of(zt_ref[k] * MOE_TM, MOE_TM), MOE_TM), :],
                sem)
            cp.start()
            cp.wait()

    hbuf_ref[...] = _rms(x_ref[...], g_ref[...]) * (1.0 + sc_ref[...]) + sh_ref[...]

    def make(k, t, p):
        return pltpu.make_async_copy(hbuf_ref.at[pl.ds(t, 1), :], xs_ref.at[pl.ds(p, 1), :], sem)

    _issue_and_drain(pos_ref, make)


def _moe_dispatch(x, g, shift, scale, plan):
    pos, _, _, max_tiles, zero_tiles = plan
    b, s, d = x.shape
    per_b = s // MOE_TT
    assert MOE_TT == MOE_TM
    vec = pl.BlockSpec((None, 1, d), lambda i, j: (i, 0, 0))
    return pl.pallas_call(
        _moe_dispatch_kernel,
        grid=(b, per_b),
        in_specs=[
            pl.BlockSpec((None, 1, 2 * MOE_TT), lambda i, j: (i * per_b + j, 0, 0),
                         memory_space=pltpu.SMEM),
            pl.BlockSpec(memory_space=pltpu.SMEM),
            pl.BlockSpec((None, MOE_TT, d), lambda i, j: (i, j, 0)),
            pl.BlockSpec((1, d), lambda i, j: (0, 0)),
            vec, vec,
        ],
        out_specs=pl.BlockSpec(memory_space=pl.ANY),
        out_shape=jax.ShapeDtypeStruct((max_tiles * MOE_TM, d), F32),
        scratch_shapes=[pltpu.VMEM((MOE_TT, d), F32), pltpu.SemaphoreType.DMA(())],
        compiler_params=_params("arbitrary", "arbitrary"),
        name="moe_dispatch",
    )(pos, zero_tiles, x, g.reshape(1, d), shift.reshape(b, 1, d), scale.reshape(b, 1, d))


def _moe_mm_kernel(te_ref, nt_ref, a_ref, w_ref, *rest, mode):
    if mode == "gate":
        u_ref, o_ref, wb_ref = rest
    else:
        o_ref, wb_ref = rest
    i = pl.program_id(1)
    active = i < nt_ref[0]
    new_expert = jnp.logical_or(i == 0, te_ref[i] != te_ref[jnp.maximum(i - 1, 0)])

    @pl.when(jnp.logical_and(active, new_expert))
    def _():
        _cast_weight(w_ref, wb_ref)

    @pl.when(active)
    def _():
        p = _dot(a_ref[...].astype(BF16), wb_ref[...])
        if mode == "silu":
            o_ref[...] = (p * _sigmoid(p)).astype(o_ref.dtype)
        elif mode == "gate":
            o_ref[...] = (u_ref[...].astype(F32) * p).astype(o_ref.dtype)
        else:
            o_ref[...] = p.astype(o_ref.dtype)

    @pl.when(jnp.logical_not(active))
    def _():
        o_ref[...] = jnp.zeros_like(o_ref)


def _moe_mm(a, w, li, plan, tn, out_dtype, mode, u=None):
    _, tile_expert, num_tiles, max_tiles, _ = plan
    rows, k = a.shape
    n = w.shape[-1]

    def tile(i, nt):
        return jnp.minimum(i, nt[0] - 1)

    in_specs = [
        pl.BlockSpec((MOE_TM, k), lambda j, i, te, nt: (tile(i, nt), 0)),
        pl.BlockSpec((None, None, k, tn), lambda j, i, te, nt: (li, te[tile(i, nt)], 0, j)),
    ]
    args = [a, w]
    if mode == "gate":
        in_specs.append(pl.BlockSpec((MOE_TM, tn), lambda j, i, te, nt: (tile(i, nt), j)))
        args.append(u)
    return pl.pallas_call(
        functools.partial(_moe_mm_kernel, mode=mode),
        grid_spec=pltpu.PrefetchScalarGridSpec(
            num_scalar_prefetch=2,
            grid=(n // tn, max_tiles),
            in_specs=in_specs,
            out_specs=pl.BlockSpec((MOE_TM, tn), lambda j, i, te, nt: (i, j)),
            scratch_shapes=[pltpu.VMEM((k, tn), BF16)],
        ),
        out_shape=jax.ShapeDtypeStruct((rows, n), out_dtype),
        compiler_params=_params("arbitrary", "arbitrary"),
        name=f"moe_mm_{mode}",
    )(tile_expert, num_tiles, *args)


def _moe_combine_kernel(pos_ref, x_ref, g_ref, wts_ref, y_ref, o_ref, ybuf_ref, sem):
    def make(k, t, p):
        return pltpu.make_async_copy(y_ref.at[pl.ds(p, 1), :], ybuf_ref.at[k, pl.ds(t, 1), :], sem)

    _issue_and_drain(pos_ref, make)
    f = wts_ref[:, 0:1] * ybuf_ref[0] + wts_ref[:, 1:2] * ybuf_ref[1]
    o_ref[...] = x_ref[...] + g_ref[...] * f


def _moe_combine(x, gate, wts, y, pos, seq):
    m, d = x.shape
    nb = gate.shape[0]
    ne = wts.shape[-1]
    return pl.pallas_call(
        _moe_combine_kernel,
        grid=(m // MOE_TT,),
        in_specs=[
            pl.BlockSpec((None, 1, 2 * MOE_TT), lambda i: (i, 0, 0), memory_space=pltpu.SMEM),
            pl.BlockSpec((MOE_TT, d), lambda i: (i, 0)),
            pl.BlockSpec((None, 1, d), lambda i: (i * MOE_TT // seq, 0, 0)),
            pl.BlockSpec((MOE_TT, ne), lambda i: (i, 0)),
            pl.BlockSpec(memory_space=pl.ANY),
        ],
        out_specs=pl.BlockSpec((MOE_TT, d), lambda i: (i, 0)),
        out_shape=jax.ShapeDtypeStruct((m, d), F32),
        scratch_shapes=[pltpu.VMEM((2, MOE_TT, d), F32), pltpu.SemaphoreType.DMA(())],
        compiler_params=_params("arbitrary"),
        name="moe_combine",
    )(pos, x, gate.reshape(nb, 1, d), wts, y)


def kernel(x, c, ada_w, ada_b, norm_mix_g, norm_ffn_g, final_norm_g, w_in, ssm_a_re, ssm_a_im, ssm_log_dt, ssm_b_re, ssm_b_im, ssm_c_re, ssm_c_im, ssm_d, w_glu, b_glu, w_branch, w_out, ffn_w1, ffn_w3, ffn_w2, router_w, router_b, moe_w1, moe_w3, moe_w2):
    batch, seq, d = x.shape
    depth = ada_w.shape[0]
    tokens = batch * seq
    ssm_width = w_glu.shape[1]
    per_res = seq // TOKEN_RESIDUES
    tn_in = ATTN_OUT
    assert ssm_width == tn_in and w_in.shape[-1] == (3 * N_GROUPS + 1) * tn_in + 2 * d

    def to_residue_major(t):
        f = t.shape[-1]
        return t.reshape(batch, per_res, TOKEN_RESIDUES, f).transpose(0, 2, 1, 3).reshape(batch, seq, f)

    def to_time_batch(t):
        f = t.shape[-1]
        return t.reshape(batch, TOKEN_RESIDUES, per_res, f).transpose(2, 1, 0, 3).reshape(tokens, f)

    def from_time_batch(t):
        f = t.shape[-1]
        return t.reshape(per_res, TOKEN_RESIDUES, batch, f).transpose(2, 1, 0, 3).reshape(tokens, f)

    mod = _ada_modulation(c, ada_w, ada_b).reshape(depth, batch, 6, d)
    ones = jnp.ones((tokens, 1), F32)
    x2 = to_residue_major(x).reshape(tokens, d)
    for layer in range(depth):
        sh1, sc1, g1, sh2, sc2, g2 = (mod[layer, :, i] for i in range(6))

        h = _norm_mod(x2.reshape(batch, seq, d), norm_mix_g[layer], sh1, sc1)
        qkv0 = _mm_plain(h, w_in, (layer,), lambda j: N_GROUPS * j, 3, F32, tm=1024, tn=tn_in)
        rest = _mm_plain(h, w_in, (layer,),
                         lambda j: j + 1 + jnp.where(j >= 2, 1, 0) + jnp.where(j >= 4, 1, 0),
                         11, BF16, tm=1024, tn=tn_in)
        attn = _dilation_mixture_attention(qkv0, rest, batch, seq)
        u_tb = to_time_batch(rest[:, 6 * tn_in:7 * tn_in])
        tables = _ssm_tables(ssm_a_re[layer], ssm_a_im[layer], ssm_log_dt[layer],
                             ssm_b_re[layer], ssm_b_im[layer], ssm_c_re[layer], ssm_c_im[layer])
        ssm = from_time_batch(
            _s5_ssm(u_tb, batch, tables, ssm_d[layer], w_glu[layer], b_glu[layer]))
        merged = _mm_branch(attn, ssm, w_branch, layer, rest, 7 * tn_in, tm=1024, tn=1024)
        x2 = _mm_resid(merged, w_out, (layer,), x2, g1, seq, tm=1024, tn=1024)

        li = layer // 2
        x3 = x2.reshape(batch, seq, d)
        if layer % 2 == 0:
            h = _norm_mod(x3, norm_ffn_g[layer], sh2, sc2)
            act = _mm_swiglu(h, ffn_w1, ffn_w3, (li,), ones, tm=1024, tn=512)
            x2 = _mm_resid(act, ffn_w2, (li,), x2, g2, seq, tm=512, tn=512)
        else:
            sel, wts, cnt = _router(x3, norm_ffn_g[layer], sh2, sc2, router_w[li], router_b[li])
            plan = _moe_plan(sel, cnt, tokens)
            xs = _moe_dispatch(x3, norm_ffn_g[layer], sh2, sc2, plan)
            u = _moe_mm(xs, moe_w1, li, plan, MOE_UP_TN, BF16, "silu")
            act = _moe_mm(xs, moe_w3, li, plan, MOE_UP_TN, BF16, "gate", u)
            y = _moe_mm(act, moe_w2, li, plan, MOE_DOWN_TN, F32, "plain")
            x2 = _moe_combine(x2, g2, wts, y, plan[0], seq)
    out = _final_norm(x2.reshape(batch, seq, d), final_norm_g)
    return out.reshape(batch, TOKEN_RESIDUES, per_res, d).transpose(0, 2, 1, 3).reshape(batch, seq, d)
```

```python
import functools
import math

import jax
import jax.numpy as jnp
from jax import lax
from jax.experimental import pallas as pl
from jax.experimental.pallas import tpu as pltpu

F32 = jnp.float32
BF16 = jnp.bfloat16

LANES = 128
SUBLANES = 8
VMEM_LIMIT_BYTES = 56 * 1024 * 1024

ATTN_PATTERNS = ((128, 1), (512, 4), (2048, 16))
N_GROUPS = len(ATTN_PATTERNS)
HEADS = 8
HEAD_DIM = 128
ATTN_OUT = HEADS * HEAD_DIM
QKV_WIDTH = N_GROUPS * ATTN_OUT
ATTN_BLOCK = 128
SSM_GROUP_CH = 16
SSM_STATE = 64
N_EXPERTS = 8
RMS_EPS = 1e-6
NEG_BIG = -1e30


def _params(*semantics):
    return pltpu.CompilerParams(dimension_semantics=semantics,
                                vmem_limit_bytes=VMEM_LIMIT_BYTES)


def _sigmoid(v):
    return 1.0 / (1.0 + jnp.exp(-v))


ADA_TN = 512


def _ada_kernel(cb_ref, w_ref, b_ref, o_ref, act_ref):
    nb, k = cb_ref.shape[0], cb_ref.shape[1]
    nj = w_ref.shape[1] // LANES

    @pl.when(jnp.logical_and(pl.program_id(0) == 0, pl.program_id(1) == 0))
    def _():
        cb = cb_ref[...]
        act_ref[...] = cb * _sigmoid(cb)

    def body(i, accs):
        r = pl.multiple_of(i * SUBLANES, SUBLANES)
        w = [w_ref[pl.ds(r, SUBLANES), j * LANES:(j + 1) * LANES] for j in range(nj)]
        out = []
        for b in range(nb):
            act = act_ref[b, pl.ds(r, SUBLANES), :]
            out.extend(accs[b * nj + j] + w[j] * act for j in range(nj))
        return tuple(out)

    zero = jnp.zeros((SUBLANES, LANES), F32)
    accs = lax.fori_loop(0, k // SUBLANES, body, (zero,) * (nb * nj), unroll=2)
    for b in range(nb):
        for j in range(nj):
            sl = slice(j * LANES, (j + 1) * LANES)
            o_ref[b:b + 1, sl] = jnp.sum(accs[b * nj + j], axis=0, keepdims=True) + b_ref[:, sl]


def _ada_modulation(c, ada_w, ada_b):
    depth, k, n = ada_w.shape
    nb = c.shape[0]
    cb = jnp.broadcast_to(c[:, :, None], (nb, k, LANES))
    return pl.pallas_call(
        _ada_kernel,
        grid=(depth, n // ADA_TN),
        in_specs=[
            pl.BlockSpec((nb, k, LANES), lambda l, j: (0, 0, 0)),
            pl.BlockSpec((None, k, ADA_TN), lambda l, j: (l, 0, j)),
            pl.BlockSpec((None, 1, ADA_TN), lambda l, j: (l, 0, j)),
        ],
        out_specs=pl.BlockSpec((None, nb, ADA_TN), lambda l, j: (l, 0, j)),
        out_shape=jax.ShapeDtypeStruct((depth, nb, n), F32),
        scratch_shapes=[pltpu.VMEM((nb, k, LANES), F32)],
        compiler_params=_params("arbitrary", "arbitrary"),
        name="ada_modulation",
    )(cb, ada_w, ada_b.reshape(depth, 1, n))


NORM_TS = 512


def _rms(x, g):
    return x * lax.rsqrt(jnp.mean(x * x, axis=-1, keepdims=True) + RMS_EPS) * g


def _norm_mod_kernel(x_ref, g_ref, sh_ref, sc_ref, o_ref):
    y = _rms(x_ref[...], g_ref[...])
    o_ref[...] = (y * (1.0 + sc_ref[...]) + sh_ref[...]).astype(o_ref.dtype)


def _norm_mod(x, g, shift, scale):
    b, s, d = x.shape
    vec = pl.BlockSpec((None, 1, d), lambda i, j: (i, 0, 0))
    out = pl.pallas_call(
        _norm_mod_kernel,
        grid=(b, s // NORM_TS),
        in_specs=[
            pl.BlockSpec((None, NORM_TS, d), lambda i, j: (i, j, 0)),
            pl.BlockSpec((1, d), lambda i, j: (0, 0)),
            vec, vec,
        ],
        out_specs=pl.BlockSpec((None, NORM_TS, d), lambda i, j: (i, j, 0)),
        out_shape=jax.ShapeDtypeStruct((b, s, d), BF16),
        compiler_params=_params("arbitrary", "arbitrary"),
        name="norm_mod",
    )(x, g.reshape(1, d), shift.reshape(b, 1, d), scale.reshape(b, 1, d))
    return out.reshape(b * s, d)


def _final_norm_kernel(x_ref, g_ref, o_ref):
    o_ref[...] = _rms(x_ref[...], g_ref[...])


def _final_norm(x, g):
    b, s, d = x.shape
    return pl.pallas_call(
        _final_norm_kernel,
        grid=(b, s // NORM_TS),
        in_specs=[
            pl.BlockSpec((None, NORM_TS, d), lambda i, j: (i, j, 0)),
            pl.BlockSpec((1, d), lambda i, j: (0, 0)),
        ],
        out_specs=pl.BlockSpec((None, NORM_TS, d), lambda i, j: (i, j, 0)),
        out_shape=jax.ShapeDtypeStruct((b, s, d), F32),
        compiler_params=_params("arbitrary", "arbitrary"),
        name="final_norm",
    )(x, g.reshape(1, d))


CAST_ROWS = 256


def _cast_weight(w_ref, wb_ref):
    def body(i, carry):
        r = pl.multiple_of(i * CAST_ROWS, CAST_ROWS)
        wb_ref[pl.ds(r, CAST_ROWS), :] = w_ref[pl.ds(r, CAST_ROWS), :].astype(BF16)
        return carry
    lax.fori_loop(0, w_ref.shape[0] // CAST_ROWS, body, 0)


def _dot(a, b):
    return jnp.dot(a, b, preferred_element_type=F32)


def _mm_plain_kernel(a_ref, w_ref, o_ref, wb_ref):
    @pl.when(pl.program_id(1) == 0)
    def _():
        _cast_weight(w_ref, wb_ref)
    o_ref[...] = _dot(a_ref[...], wb_ref[...]).astype(o_ref.dtype)


def _mm_plain(a, w, w_idx, col_tile, n_tiles, out_dtype, tm, tn):
    m, k = a.shape
    lead = (None,) * len(w_idx)
    return pl.pallas_call(
        _mm_plain_kernel,
        grid=(n_tiles, m // tm),
        in_specs=[
            pl.BlockSpec((tm, k), lambda j, i: (i, 0)),
            pl.BlockSpec(lead + (k, tn), lambda j, i: w_idx + (0, col_tile(j))),
        ],
        out_specs=pl.BlockSpec((tm, tn), lambda j, i: (i, j)),
        out_shape=jax.ShapeDtypeStruct((m, n_tiles * tn), out_dtype),
        scratch_shapes=[pltpu.VMEM((k, tn), BF16)],
        compiler_params=_params("arbitrary", "arbitrary"),
        name="mm_plain",
    )(a, w)


def _mm_swiglu_kernel(a_ref, w1_ref, w3_ref, rs_ref, o_ref, wb1_ref, wb3_ref):
    @pl.when(pl.program_id(1) == 0)
    def _():
        _cast_weight(w1_ref, wb1_ref)
        _cast_weight(w3_ref, wb3_ref)
    a = a_ref[...]
    u = _dot(a, wb1_ref[...])
    v = _dot(a, wb3_ref[...])
    o_ref[...] = (u * _sigmoid(u) * v * rs_ref[...]).astype(o_ref.dtype)


def _mm_swiglu(a, w1, w3, w_idx, row_scale, tm, tn):
    m, k = a.shape
    n = w1.shape[-1]
    lead = (None,) * len(w_idx)
    wspec = pl.BlockSpec(lead + (k, tn), lambda j, i: w_idx + (0, j))
    return pl.pallas_call(
        _mm_swiglu_kernel,
        grid=(n // tn, m // tm),
        in_specs=[
            pl.BlockSpec((tm, k), lambda j, i: (i, 0)),
            wspec, wspec,
            pl.BlockSpec((tm, 1), lambda j, i: (i, 0)),
        ],
        out_specs=pl.BlockSpec((tm, tn), lambda j, i: (i, j)),
        out_shape=jax.ShapeDtypeStruct((m, n), BF16),
        scratch_shapes=[pltpu.VMEM((k, tn), BF16), pltpu.VMEM((k, tn), BF16)],
        compiler_params=_params("arbitrary", "arbitrary"),
        name="mm_swiglu",
    )(a, w1, w3, row_scale)


def _mm_resid_kernel(a_ref, w_ref, x_ref, g_ref, o_ref, wb_ref):
    @pl.when(pl.program_id(1) == 0)
    def _():
        _cast_weight(w_ref, wb_ref)
    o_ref[...] = x_ref[...] + g_ref[...] * _dot(a_ref[...], wb_ref[...])


def _mm_resid(a, w, w_idx, x, gate, seq, tm, tn):
    m, k = a.shape
    n = w.shape[-1]
    nb = gate.shape[0]
    lead = (None,) * len(w_idx)
    return pl.pallas_call(
        _mm_resid_kernel,
        grid=(n // tn, m // tm),
        in_specs=[
            pl.BlockSpec((tm, k), lambda j, i: (i, 0)),
            pl.BlockSpec(lead + (k, tn), lambda j, i: w_idx + (0, j)),
            pl.BlockSpec((tm, tn), lambda j, i: (i, j)),
            pl.BlockSpec((None, 1, tn), lambda j, i: (i * tm // seq, 0, j)),
        ],
        out_specs=pl.BlockSpec((tm, tn), lambda j, i: (i, j)),
        out_shape=jax.ShapeDtypeStruct((m, n), F32),
        scratch_shapes=[pltpu.VMEM((k, tn), BF16)],
        compiler_params=_params("arbitrary", "arbitrary"),
        name="mm_resid",
    )(a, w, x, gate.reshape(nb, 1, n))


def _mm_branch_kernel(at_ref, ss_ref, wa_ref, ws_ref, ga_ref, gs_ref, o_ref,
                      wba_ref, wbs_ref):
    @pl.when(pl.program_id(1) == 0)
    def _():
        _cast_weight(wa_ref, wba_ref)
        _cast_weight(ws_ref, wbs_ref)
    pa = _dot(at_ref[...], wba_ref[...])
    ps = _dot(ss_ref[...], wbs_ref[...])
    ga = _sigmoid(ga_ref[...].astype(F32))
    gs = _sigmoid(gs_ref[...].astype(F32))
    o_ref[...] = (ga * pa + gs * ps).astype(o_ref.dtype)


def _mm_branch(attn, ssm, w_branch, layer, proj, gate_col, tm, tn):
    m, k = attn.shape
    n = w_branch.shape[-1]
    g0 = gate_col // tn
    g1 = (gate_col + n) // tn
    return pl.pallas_call(
        _mm_branch_kernel,
        grid=(n // tn, m // tm),
        in_specs=[
            pl.BlockSpec((tm, k), lambda j, i: (i, 0)),
            pl.BlockSpec((tm, k), lambda j, i: (i, 0)),
            pl.BlockSpec((None, k, tn), lambda j, i: (layer, 0, j)),
            pl.BlockSpec((None, k, tn), lambda j, i: (layer, 1, j)),
            pl.BlockSpec((tm, tn), lambda j, i: (i, g0 + j)),
            pl.BlockSpec((tm, tn), lambda j, i: (i, g1 + j)),
        ],
        out_specs=pl.BlockSpec((tm, tn), lambda j, i: (i, j)),
        out_shape=jax.ShapeDtypeStruct((m, n), BF16),
        scratch_shapes=[pltpu.VMEM((k, tn), BF16), pltpu.VMEM((k, tn), BF16)],
        compiler_params=_params("arbitrary", "arbitrary"),
        name="mm_branch",
    )(attn, ssm, w_branch, w_branch, proj, proj)


TOKEN_RESIDUES = 16
ATTN_BB = 4


def _dot_nt(a, b):
    return lax.dot_general(a, b, (((1,), (1,)), ((), ())), preferred_element_type=F32)


def _attn_kernel(*refs, with_prev):
    if with_prev:
        q_ref, kp_ref, kc_ref, vp_ref, vc_ref, bc_ref, bp_ref, o_ref, lse_ref = refs
    else:
        q_ref, kc_ref, vc_ref, bc_ref, o_ref, lse_ref = refs
    blk = ATTN_BLOCK
    bb, nc, rpc = q_ref.shape[0], q_ref.shape[1], q_ref.shape[2]
    scale = HEAD_DIM ** -0.5
    lane = lax.broadcasted_iota(jnp.int32, (blk, LANES), 1)

    def heads(ref, bi):
        x = ref[bi].reshape(blk, ATTN_OUT)
        return jnp.stack([x[:, h * HEAD_DIM:(h + 1) * HEAD_DIM] for h in range(HEADS)]
                         ).astype(BF16)

    def qk(q, k):
        return lax.dot_general(q, k, (((2,), (2,)), ((0,), (0,))), preferred_element_type=F32)

    def pv(p, v):
        return lax.dot_general(p.astype(BF16), v, (((2,), (1,)), ((0,), (0,))),
                               preferred_element_type=F32)

    for bi in range(bb):
        q = heads(q_ref, bi)
        s_c = qk(q, heads(kc_ref, bi)) * scale + bc_ref[...]
        m = jnp.max(s_c, axis=2, keepdims=True)
        if with_prev:
            bias_p = jnp.where(pl.program_id(2) > 0, bp_ref[...], NEG_BIG)
            s_p = qk(q, heads(kp_ref, bi)) * scale + bias_p
            m = jnp.maximum(m, jnp.max(s_p, axis=2, keepdims=True))
        p_c = jnp.exp(s_c - m)
        l = jnp.sum(p_c, axis=2, keepdims=True)
        o = pv(p_c, heads(vc_ref, bi))
        if with_prev:
            p_p = jnp.exp(s_p - m)
            l = l + jnp.sum(p_p, axis=2, keepdims=True)
            o = o + pv(p_p, heads(vp_ref, bi))
        o = o / l
        lse = m + jnp.log(l)
        lse_tile = jnp.zeros((blk, LANES), F32)
        for h in range(HEADS):
            sl = slice(h * HEAD_DIM, (h + 1) * HEAD_DIM)
            o_ref[bi, :, :, sl] = o[h].reshape(nc, rpc, HEAD_DIM).astype(o_ref.dtype)
            lse_tile = jnp.where(lane == h, lse[h], lse_tile)
        lse_ref[bi] = lse_tile.reshape(nc, rpc, LANES)


def _attn_bias(dilation, nc, back):
    rpc = ATTN_BLOCK // nc
    i = jnp.arange(ATTN_BLOCK, dtype=jnp.int32)
    off = nc * (i % rpc) + i // rpc
    dist = off[:, None] - off[None, :] + back * ATTN_BLOCK
    valid = jnp.logical_and(dist >= 0, dist <= ATTN_BLOCK)
    slopes = 2.0 ** (-8.0 * (jnp.arange(HEADS, dtype=F32) + 1.0) / HEADS)
    pen = slopes[:, None, None] * (dist * dilation).astype(F32)[None]
    return jnp.where(valid[None], -pen, NEG_BIG)


def _attention_group(qkv, qkv_cols, batch, seq, gi):
    window, dilation = ATTN_PATTERNS[gi]
    assert window // dilation == ATTN_BLOCK
    cols = qkv.shape[-1]
    per_res = seq // TOKEN_RESIDUES
    nc = TOKEN_RESIDUES // dilation
    rpc = ATTN_BLOCK // nc
    nblk = seq // dilation // ATTN_BLOCK
    with_prev = nblk > 1

    def shape5(c):
        return (batch, nc, dilation, per_res, c)

    def spec(c, col, back):
        return pl.BlockSpec(
            (ATTN_BB, nc, None, rpc, c),
            lambda b, r, n: (b, 0, r, jnp.maximum(n - back, 0), col))

    bias_spec = pl.BlockSpec((HEADS, ATTN_BLOCK, ATTN_BLOCK), lambda b, r, n: (0, 0, 0))
    in_specs = [spec(ATTN_OUT, qkv_cols[0], 0)]
    for col in qkv_cols[1:]:
        if with_prev:
            in_specs.append(spec(ATTN_OUT, col, 1))
        in_specs.append(spec(ATTN_OUT, col, 0))
    view = qkv.reshape(shape5(cols))
    args = [view] * len(in_specs) + [_attn_bias(dilation, nc, 0)]
    in_specs.append(bias_spec)
    if with_prev:
        args.append(_attn_bias(dilation, nc, 1))
        in_specs.append(bias_spec)
    o, lse = pl.pallas_call(
        functools.partial(_attn_kernel, with_prev=with_prev),
        grid=(batch // ATTN_BB, dilation, nblk),
        in_specs=in_specs,
        out_specs=[spec(ATTN_OUT, 0, 0), spec(LANES, 0, 0)],
        out_shape=[
            jax.ShapeDtypeStruct(shape5(ATTN_OUT), qkv.dtype),
            jax.ShapeDtypeStruct(shape5(LANES), F32),
        ],
        compiler_params=_params("arbitrary", "arbitrary", "arbitrary"),
        name=f"attn_g{gi}",
    )(*args)
    return o.reshape(batch * seq, ATTN_OUT), lse.reshape(batch * seq, LANES)


COMBINE_TS = 512


def _combine_kernel(o0_ref, o1_ref, o2_ref, l0_ref, l1_ref, l2_ref, out_ref):
    for h in range(HEADS):
        sl = slice(h * HEAD_DIM, (h + 1) * HEAD_DIM)
        a0 = l0_ref[:, h:h + 1]
        a1 = l1_ref[:, h:h + 1]
        a2 = l2_ref[:, h:h + 1]
        m = jnp.maximum(jnp.maximum(a0, a1), a2)
        e0 = jnp.exp(a0 - m)
        e1 = jnp.exp(a1 - m)
        e2 = jnp.exp(a2 - m)
        inv = 1.0 / (e0 + e1 + e2)
        mix = ((e0 * inv) * o0_ref[:, sl].astype(F32)
               + (e1 * inv) * o1_ref[:, sl].astype(F32)
               + (e2 * inv) * o2_ref[:, sl].astype(F32))
        out_ref[:, sl] = mix.astype(out_ref.dtype)


def _dilation_mixture_attention(qkv0, rest, batch, seq):
    outs, lses = [], []
    for gi in range(N_GROUPS):
        if gi == 0:
            o, lse = _attention_group(qkv0, (0, 1, 2), batch, seq, gi)
        else:
            o, lse = _attention_group(rest, (gi - 1, gi + 1, gi + 3), batch, seq, gi)
        outs.append(o)
        lses.append(lse)
    m = batch * seq
    ospec = pl.BlockSpec((COMBINE_TS, ATTN_OUT), lambda i: (i, 0))
    lspec = pl.BlockSpec((COMBINE_TS, LANES), lambda i: (i, 0))
    return pl.pallas_call(
        _combine_kernel,
        grid=(m // COMBINE_TS,),
        in_specs=[ospec] * 3 + [lspec] * 3,
        out_specs=ospec,
        out_shape=jax.ShapeDtypeStruct((m, ATTN_OUT), BF16),
        compiler_params=_params("arbitrary"),
        name="attn_combine",
    )(*outs, *lses)


SSM_LT = 128
SSM_JBLK = 256
SSM_SCAN_COLS = 1024


def _gelu_tanh(y):
    return 0.5 * y * (1.0 + jnp.tanh(math.sqrt(2.0 / math.pi) * (y + 0.044715 * (y * y * y))))


def _ssm_kernel(u_ref, bblk_ref, cblk_ref, are_ref, aim_ref, d_ref, wglu_ref, bglu_ref,
                o_ref, sre_ref, sim_ref, hre_ref, him_ref, *, nbatch):
    rows, width = u_ref.shape
    njb = width // SSM_JBLK
    jstates = bblk_ref.shape[2] // 2
    nstate = njb * jstates
    pairs = rows // SUBLANES
    per_tile = SUBLANES // nbatch

    @pl.when(pl.program_id(0) == 0)
    def _():
        hre_ref[...] = jnp.zeros_like(hre_ref)
        him_ref[...] = jnp.zeros_like(him_ref)

    for j in range(njb):
        bu = _dot(u_ref[:, j * SSM_JBLK:(j + 1) * SSM_JBLK], bblk_ref[j])
        cs = slice(j * jstates, (j + 1) * jstates)
        sre_ref[:, :, cs] = bu[:, :jstates].reshape(pairs, SUBLANES, jstates)
        sim_ref[:, :, cs] = bu[:, jstates:].reshape(pairs, SUBLANES, jstates)

    for c in range(nstate // SSM_SCAN_COLS):
        cs = slice(c * SSM_SCAN_COLS, (c + 1) * SSM_SCAN_COLS)
        ar = are_ref[0:nbatch, cs]
        ai = aim_ref[0:nbatch, cs]

        def step(k, carry):
            hr, hi = carry
            for t in range(per_tile):
                rs = slice(t * nbatch, (t + 1) * nbatch)
                br = sre_ref[k, rs, cs]
                bi = sim_ref[k, rs, cs]
                nr = ar * hr - ai * hi + br
                ni = ar * hi + ai * hr + bi
                sre_ref[k, rs, cs] = nr
                sim_ref[k, rs, cs] = ni
                hr, hi = nr, ni
            return hr, hi

        hr, hi = lax.fori_loop(0, pairs, step,
                               (hre_ref[0:nbatch, cs], him_ref[0:nbatch, cs]))
        hre_ref[0:nbatch, cs] = hr
        him_ref[0:nbatch, cs] = hi

    ys = []
    for j in range(njb):
        cs = slice(j * jstates, (j + 1) * jstates)
        hr = sre_ref[:, :, cs].reshape(rows, jstates).astype(BF16)
        hi = sim_ref[:, :, cs].reshape(rows, jstates).astype(BF16)
        ys.append(_dot(hr, cblk_ref[j, 0:jstates, :]) + _dot(hi, cblk_ref[j, jstates:, :]))
    y = jnp.concatenate(ys, axis=1) + d_ref[...] * u_ref[...].astype(F32)
    z = _gelu_tanh(y).astype(BF16)
    g = _dot(z, wglu_ref[...]) + bglu_ref[...]
    o_ref[...] = (g[:, :width] * _sigmoid(g[:, width:])).astype(o_ref.dtype)


def _ssm_tables(a_re, a_im, log_dt, b_re, b_im, c_re, c_im):
    groups, nst = a_re.shape
    gpb = SSM_JBLK // SSM_GROUP_CH
    njb = groups // gpb
    lam = lax.complex(a_re.astype(F32), a_im.astype(F32))
    dt = jnp.exp(log_dt.astype(F32))[:, None]
    a_bar = jnp.exp(lam * dt)
    b_mat = lax.complex(b_re.astype(F32), b_im.astype(F32))
    b_bar = ((a_bar - 1.0) / lam)[:, :, None] * b_mat
    eye = jnp.eye(gpb, dtype=F32)

    def in_blocks(t):
        t = t.reshape(njb, gpb, nst, SSM_GROUP_CH)
        return jnp.einsum('jgnc,gh->jgchn', t, eye).reshape(njb, gpb * SSM_GROUP_CH, gpb * nst)

    def out_blocks(t):
        t = t.reshape(njb, gpb, SSM_GROUP_CH, nst)
        return jnp.einsum('jgcn,gh->jhngc', t, eye).reshape(njb, gpb * nst, gpb * SSM_GROUP_CH)

    bblk = jnp.concatenate([in_blocks(jnp.real(b_bar)), in_blocks(jnp.imag(b_bar))], axis=2)
    cblk = jnp.concatenate([out_blocks(c_re.astype(F32)), out_blocks(-c_im.astype(F32))], axis=1)
    are = jnp.broadcast_to(jnp.real(a_bar).reshape(1, groups * nst), (SUBLANES, groups * nst))
    aim = jnp.broadcast_to(jnp.imag(a_bar).reshape(1, groups * nst), (SUBLANES, groups * nst))
    return bblk.astype(BF16), cblk.astype(BF16), are, aim


def _s5_ssm(u_tb, nbatch, tables, d_skip, w_glu, b_glu):
    bblk, cblk, are, aim = tables
    total, width = u_tb.shape
    rows = SSM_LT * nbatch
    nstate = are.shape[1]
    full = lambda a: pl.BlockSpec(a.shape, lambda i: (0,) * a.ndim)
    d2 = d_skip.reshape(1, width).astype(F32)
    wg = w_glu.astype(BF16)
    bg = b_glu.reshape(1, 2 * width).astype(F32)
    return pl.pallas_call(
        functools.partial(_ssm_kernel, nbatch=nbatch),
        grid=(total // rows,),
        in_specs=[pl.BlockSpec((rows, width), lambda i: (i, 0)),
                  full(bblk), full(cblk), full(are), full(aim), full(d2), full(wg), full(bg)],
        out_specs=pl.BlockSpec((rows, width), lambda i: (i, 0)),
        out_shape=jax.ShapeDtypeStruct((total, width), BF16),
        scratch_shapes=[
            pltpu.VMEM((rows // SUBLANES, SUBLANES, nstate), F32),
            pltpu.VMEM((rows // SUBLANES, SUBLANES, nstate), F32),
            pltpu.VMEM((SUBLANES, nstate), F32),
            pltpu.VMEM((SUBLANES, nstate), F32),
        ],
        compiler_params=_params("arbitrary"),
        name="s5_ssm",
    )(u_tb, bblk, cblk, are, aim, d2, wg, bg)


ROUTER_TS = 256


def _router_kernel(x_ref, g_ref, sh_ref, sc_ref, rw_ref, rb_ref,
                   sel_ref, wts_ref, cnt_ref, carry_ref):
    @pl.when(jnp.logical_and(pl.program_id(0) == 0, pl.program_id(1) == 0))
    def _():
        carry_ref[...] = jnp.zeros_like(carry_ref)

    h = _rms(x_ref[...], g_ref[...]) * (1.0 + sc_ref[...]) + sh_ref[...]
    logits = jnp.dot(h, rw_ref[...], preferred_element_type=F32,
                     precision=lax.Precision.HIGHEST) + rb_ref[...]
    ts, ne = logits.shape
    idx = lax.broadcasted_iota(jnp.int32, logits.shape, 1)
    m1 = jnp.max(logits, axis=1, keepdims=True)
    i1 = jnp.min(jnp.where(logits == m1, idx, ne), axis=1, keepdims=True)
    rest = jnp.where(idx == i1, -jnp.inf, logits)
    m2 = jnp.max(rest, axis=1, keepdims=True)
    i2 = jnp.min(jnp.where(rest == m2, idx, ne), axis=1, keepdims=True)
    e = jnp.exp(m2 - m1)
    w1 = 1.0 / (1.0 + e)
    w2 = e / (1.0 + e)

    onehot = jnp.where(idx == i1, 1.0, 0.0) + jnp.where(idx == i2, 1.0, 0.0)
    row = lax.broadcasted_iota(jnp.int32, (ts, ts), 0)
    col = lax.broadcasted_iota(jnp.int32, (ts, ts), 1)
    lower = jnp.where(col < row, 1.0, 0.0).astype(BF16)
    before = carry_ref[...] + _dot(lower, onehot.astype(BF16))
    r1 = jnp.sum(jnp.where(idx == i1, before, 0.0), axis=1, keepdims=True).astype(jnp.int32)
    r2 = jnp.sum(jnp.where(idx == i2, before, 0.0), axis=1, keepdims=True).astype(jnp.int32)
    total = carry_ref[...] + jnp.sum(onehot, axis=0, keepdims=True)
    carry_ref[...] = total
    cnt_ref[...] = total
    sel_ref[...] = jnp.where(idx == 0, i1, jnp.where(idx == 1, i2, jnp.where(
        idx == 2, r1, jnp.where(idx == 3, r2, 0))))
    wts_ref[...] = jnp.where(idx == 0, w1, jnp.where(idx == 1, w2, 0.0))


def _router(x, g, shift, scale, router_w, router_b):
    b, s, d = x.shape
    ne = router_w.shape[-1]
    vec = pl.BlockSpec((None, 1, d), lambda i, j: (i, 0, 0))
    tok = pl.BlockSpec((None, ROUTER_TS, ne), lambda i, j: (i, j, 0))
    sel, wts, cnt = pl.pallas_call(
        _router_kernel,
        grid=(b, s // ROUTER_TS),
        in_specs=[
            pl.BlockSpec((None, ROUTER_TS, d), lambda i, j: (i, j, 0)),
            pl.BlockSpec((1, d), lambda i, j: (0, 0)),
            vec, vec,
            pl.BlockSpec((d, ne), lambda i, j: (0, 0)),
            pl.BlockSpec((1, ne), lambda i, j: (0, 0)),
        ],
        out_specs=[tok, tok, pl.BlockSpec((1, ne), lambda i, j: (0, 0))],
        out_shape=[jax.ShapeDtypeStruct((b, s, ne), jnp.int32),
                   jax.ShapeDtypeStruct((b, s, ne), F32),
                   jax.ShapeDtypeStruct((1, ne), F32)],
        scratch_shapes=[pltpu.VMEM((1, ne), F32)],
        compiler_params=_params("arbitrary", "arbitrary"),
        name="router",
    )(x, g.reshape(1, d), shift.reshape(b, 1, d), scale.reshape(b, 1, d),
      router_w, router_b.reshape(1, ne))
    return sel.reshape(b * s, ne), wts.reshape(b * s, ne), cnt


MOE_TM = 256
MOE_TT = 256
MOE_UP_TN = 1408
MOE_DOWN_TN = 1024


def _moe_plan(sel, cnt, tokens):
    ne = cnt.shape[-1]
    counts = cnt[0].astype(jnp.int32)
    padded = (counts + MOE_TM - 1) // MOE_TM * MOE_TM
    ends = jnp.cumsum(padded)
    starts = ends - padded
    pos1 = starts[sel[:, 0]] + sel[:, 2]
    pos2 = starts[sel[:, 1]] + sel[:, 3]
    nt = tokens // MOE_TT
    pos = jnp.concatenate([pos1.reshape(nt, MOE_TT), pos2.reshape(nt, MOE_TT)], axis=1)
    max_tiles = 2 * tokens // MOE_TM + ne
    tile_start = jnp.arange(max_tiles, dtype=jnp.int32) * MOE_TM
    tile_expert = jnp.minimum(
        jnp.sum((tile_start[:, None] >= ends[None, :]).astype(jnp.int32), axis=1), ne - 1)
    num_tiles = (ends[-1] // MOE_TM).reshape(1)
    zero_tiles = jnp.concatenate([
        jnp.maximum(ends // MOE_TM - 1, 0),
        jnp.minimum(num_tiles[0] + jnp.arange(ne, dtype=jnp.int32), max_tiles - 1)])
    return pos.reshape(nt, 1, 2 * MOE_TT), tile_expert, num_tiles, max_tiles, zero_tiles


def _row_copies(pos_ref, t, make):
    tt = pos_ref.shape[1] // 2
    return make(0, t, pos_ref[0, t]), make(1, t, pos_ref[0, tt + t])


def _issue_and_drain(pos_ref, make):
    tt = pos_ref.shape[1] // 2

    def issue(t, carry):
        for cp in _row_copies(pos_ref, t, make):
            cp.start()
        return carry

    def drain(t, carry):
        for cp in _row_copies(pos_ref, t, make):
            cp.wait()
        return carry

    lax.fori_loop(0, tt, issue, 0, unroll=4)
    lax.fori_loop(0, tt, drain, 0, unroll=4)


def _moe_dispatch_kernel(pos_ref, zt_ref, x_ref, g_ref, sh_ref, sc_ref, xs_ref, hbuf_ref, sem):
    @pl.when(jnp.logical_and(pl.program_id(0) == 0, pl.program_id(1) == 0))
    def _():
        hbuf_ref[...] = jnp.zeros_like(hbuf_ref)
        for k in range(zt_ref.shape[0]):
            cp = pltpu.make_async_copy(
                hbuf_ref, xs_ref.at[pl.ds(pl.multiple_of(zt_ref[k] * MOE_TM, MOE_TM), MOE_TM), :],
                sem)
            cp.start()
            cp.wait()

    hbuf_ref[...] = _rms(x_ref[...], g_ref[...]) * (1.0 + sc_ref[...]) + sh_ref[...]

    def make(k, t, p):
        return pltpu.make_async_copy(hbuf_ref.at[pl.ds(t, 1), :], xs_ref.at[pl.ds(p, 1), :], sem)

    _issue_and_drain(pos_ref, make)


def _moe_dispatch(x, g, shift, scale, plan):
    pos, _, _, max_tiles, zero_tiles = plan
    b, s, d = x.shape
    per_b = s // MOE_TT
    assert MOE_TT == MOE_TM
    vec = pl.BlockSpec((None, 1, d), lambda i, j: (i, 0, 0))
    return pl.pallas_call(
        _moe_dispatch_kernel,
        grid=(b, per_b),
        in_specs=[
            pl.BlockSpec((None, 1, 2 * MOE_TT), lambda i, j: (i * per_b + j, 0, 0),
                         memory_space=pltpu.SMEM),
            pl.BlockSpec(memory_space=pltpu.SMEM),
            pl.BlockSpec((None, MOE_TT, d), lambda i, j: (i, j, 0)),
            pl.BlockSpec((1, d), lambda i, j: (0, 0)),
            vec, vec,
        ],
        out_specs=pl.BlockSpec(memory_space=pl.ANY),
        out_shape=jax.ShapeDtypeStruct((max_tiles * MOE_TM, d), F32),
        scratch_shapes=[pltpu.VMEM((MOE_TT, d), F32), pltpu.SemaphoreType.DMA(())],
        compiler_params=_params("arbitrary", "arbitrary"),
        name="moe_dispatch",
    )(pos, zero_tiles, x, g.reshape(1, d), shift.reshape(b, 1, d), scale.reshape(b, 1, d))


def _moe_mm_kernel(te_ref, nt_ref, a_ref, w_ref, *rest, mode):
    if mode == "gate":
        u_ref, o_ref, wb_ref = rest
    else:
        o_ref, wb_ref = rest
    i = pl.program_id(1)
    active = i < nt_ref[0]
    new_expert = jnp.logical_or(i == 0, te_ref[i] != te_ref[jnp.maximum(i - 1, 0)])

    @pl.when(jnp.logical_and(active, new_expert))
    def _():
        _cast_weight(w_ref, wb_ref)

    @pl.when(active)
    def _():
        if mode == "down":
            p = lax.dot_general(wb_ref[...], a_ref[...], (((0,), (0,)), ((), ())),
                                preferred_element_type=F32)
            o_ref[...] = p.T
        else:
            p = lax.dot_general(wb_ref[...], a_ref[...].astype(BF16), (((0,), (1,)), ((), ())),
                                preferred_element_type=F32)
            if mode == "silu":
                o_ref[...] = (p * _sigmoid(p)).astype(o_ref.dtype)
            else:
                o_ref[...] = (u_ref[...].astype(F32) * p).astype(o_ref.dtype)

    @pl.when(jnp.logical_not(active))
    def _():
        o_ref[...] = jnp.zeros_like(o_ref)


def _moe_mm(a, w, li, plan, tn, mode, u=None):
    _, tile_expert, num_tiles, max_tiles, _ = plan
    k, n = w.shape[-2], w.shape[-1]
    rows = max_tiles * MOE_TM

    def tile(i, nt):
        return jnp.minimum(i, nt[0] - 1)

    wspec = pl.BlockSpec((None, None, k, tn), lambda j, i, te, nt: (li, te[tile(i, nt)], 0, j))
    if mode == "down":
        in_specs = [pl.BlockSpec((k, MOE_TM), lambda j, i, te, nt: (0, tile(i, nt))), wspec]
        out_spec = pl.BlockSpec((MOE_TM, tn), lambda j, i, te, nt: (i, j))
        out_shape = jax.ShapeDtypeStruct((rows, n), F32)
    else:
        in_specs = [pl.BlockSpec((MOE_TM, k), lambda j, i, te, nt: (tile(i, nt), 0)), wspec]
        out_spec = pl.BlockSpec((tn, MOE_TM), lambda j, i, te, nt: (j, i))
        out_shape = jax.ShapeDtypeStruct((n, rows), BF16)
    args = [a, w]
    if mode == "gate":
        in_specs.append(pl.BlockSpec((tn, MOE_TM), lambda j, i, te, nt: (j, tile(i, nt))))
        args.append(u)
    return pl.pallas_call(
        functools.partial(_moe_mm_kernel, mode=mode),
        grid_spec=pltpu.PrefetchScalarGridSpec(
            num_scalar_prefetch=2,
            grid=(n // tn, max_tiles),
            in_specs=in_specs,
            out_specs=out_spec,
            scratch_shapes=[pltpu.VMEM((k, tn), BF16)],
        ),
        out_shape=out_shape,
        compiler_params=_params("arbitrary", "arbitrary"),
        name=f"moe_mm_{mode}",
    )(tile_expert, num_tiles, *args)


def _moe_combine_kernel(pos_ref, x_ref, g_ref, wts_ref, y_ref, o_ref, ybuf_ref, sem):
    def make(k, t, p):
        return pltpu.make_async_copy(y_ref.at[pl.ds(p, 1), :], ybuf_ref.at[k, pl.ds(t, 1), :], sem)

    _issue_and_drain(pos_ref, make)
    f = wts_ref[:, 0:1] * ybuf_ref[0] + wts_ref[:, 1:2] * ybuf_ref[1]
    o_ref[...] = x_ref[...] + g_ref[...] * f


def _moe_combine(x, gate, wts, y, pos, seq):
    m, d = x.shape
    nb = gate.shape[0]
    ne = wts.shape[-1]
    return pl.pallas_call(
        _moe_combine_kernel,
        grid=(m // MOE_TT,),
        in_specs=[
            pl.BlockSpec((None, 1, 2 * MOE_TT), lambda i: (i, 0, 0), memory_space=pltpu.SMEM),
            pl.BlockSpec((MOE_TT, d), lambda i: (i, 0)),
            pl.BlockSpec((None, 1, d), lambda i: (i * MOE_TT // seq, 0, 0)),
            pl.BlockSpec((MOE_TT, ne), lambda i: (i, 0)),
            pl.BlockSpec(memory_space=pl.ANY),
        ],
        out_specs=pl.BlockSpec((MOE_TT, d), lambda i: (i, 0)),
        out_shape=jax.ShapeDtypeStruct((m, d), F32),
        scratch_shapes=[pltpu.VMEM((2, MOE_TT, d), F32), pltpu.SemaphoreType.DMA(())],
        compiler_params=_params("arbitrary"),
        name="moe_combine",
    )(pos, x, gate.reshape(nb, 1, d), wts, y)


def kernel(x, c, ada_w, ada_b, norm_mix_g, norm_ffn_g, final_norm_g, w_in, ssm_a_re, ssm_a_im, ssm_log_dt, ssm_b_re, ssm_b_im, ssm_c_re, ssm_c_im, ssm_d, w_glu, b_glu, w_branch, w_out, ffn_w1, ffn_w3, ffn_w2, router_w, router_b, moe_w1, moe_w3, moe_w2):
    batch, seq, d = x.shape
    depth = ada_w.shape[0]
    tokens = batch * seq
    ssm_width = w_glu.shape[1]
    per_res = seq // TOKEN_RESIDUES
    tn_in = ATTN_OUT
    assert ssm_width == tn_in and w_in.shape[-1] == (3 * N_GROUPS + 1) * tn_in + 2 * d

    def to_residue_major(t):
        f = t.shape[-1]
        return t.reshape(batch, per_res, TOKEN_RESIDUES, f).transpose(0, 2, 1, 3).reshape(batch, seq, f)

    def to_time_batch(t):
        f = t.shape[-1]
        return t.reshape(batch, TOKEN_RESIDUES, per_res, f).transpose(2, 1, 0, 3).reshape(tokens, f)

    def from_time_batch(t):
        f = t.shape[-1]
        return t.reshape(per_res, TOKEN_RESIDUES, batch, f).transpose(2, 1, 0, 3).reshape(tokens, f)

    mod = _ada_modulation(c, ada_w, ada_b).reshape(depth, batch, 6, d)
    ones = jnp.ones((tokens, 1), F32)
    x2 = to_residue_major(x).reshape(tokens, d)
    for layer in range(depth):
        sh1, sc1, g1, sh2, sc2, g2 = (mod[layer, :, i] for i in range(6))

        h = _norm_mod(x2.reshape(batch, seq, d), norm_mix_g[layer], sh1, sc1)
        qkv0 = _mm_plain(h, w_in, (layer,), lambda j: N_GROUPS * j, 3, F32, tm=1024, tn=tn_in)
        rest = _mm_plain(h, w_in, (layer,),
                         lambda j: j + 1 + jnp.where(j >= 2, 1, 0) + jnp.where(j >= 4, 1, 0),
                         11, BF16, tm=1024, tn=tn_in)
        attn = _dilation_mixture_attention(qkv0, rest, batch, seq)
        u_tb = to_time_batch(rest[:, 6 * tn_in:7 * tn_in])
        tables = _ssm_tables(ssm_a_re[layer], ssm_a_im[layer], ssm_log_dt[layer],
                             ssm_b_re[layer], ssm_b_im[layer], ssm_c_re[layer], ssm_c_im[layer])
        ssm = from_time_batch(
            _s5_ssm(u_tb, batch, tables, ssm_d[layer], w_glu[layer], b_glu[layer]))
        merged = _mm_branch(attn, ssm, w_branch, layer, rest, 7 * tn_in, tm=1024, tn=1024)
        x2 = _mm_resid(merged, w_out, (layer,), x2, g1, seq, tm=1024, tn=1024)

        li = layer // 2
        x3 = x2.reshape(batch, seq, d)
        if layer % 2 == 0:
            h = _norm_mod(x3, norm_ffn_g[layer], sh2, sc2)
            act = _mm_swiglu(h, ffn_w1, ffn_w3, (li,), ones, tm=1024, tn=512)
            x2 = _mm_resid(act, ffn_w2, (li,), x2, g2, seq, tm=512, tn=512)
        else:
            sel, wts, cnt = _router(x3, norm_ffn_g[layer], sh2, sc2, router_w[li], router_b[li])
            plan = _moe_plan(sel, cnt, tokens)
            xs = _moe_dispatch(x3, norm_ffn_g[layer], sh2, sc2, plan)
            u = _moe_mm(xs, moe_w1, li, plan, MOE_UP_TN, "silu")
            act = _moe_mm(xs, moe_w3, li, plan, MOE_UP_TN, "gate", u)
            y = _moe_mm(act, moe_w2, li, plan, MOE_DOWN_TN, "down")
            x2 = _moe_combine(x2, g2, wts, y, plan[0], seq)
    out = _final_norm(x2.reshape(batch, seq, d), final_norm_g)
    return out.reshape(batch, TOKEN_RESIDUES, per_res, d).transpose(0, 2, 1, 3).reshape(batch, seq, d)
```

```python
import functools
import math

import jax
import jax.numpy as jnp
from jax import lax
from jax.experimental import pallas as pl
from jax.experimental.pallas import tpu as pltpu

F32 = jnp.float32
BF16 = jnp.bfloat16

LANES = 128
SUBLANES = 8
VMEM_LIMIT_BYTES = 56 * 1024 * 1024

ATTN_PATTERNS = ((128, 1), (512, 4), (2048, 16))
N_GROUPS = len(ATTN_PATTERNS)
HEADS = 8
HEAD_DIM = 128
ATTN_OUT = HEADS * HEAD_DIM
QKV_WIDTH = N_GROUPS * ATTN_OUT
ATTN_BLOCK = 128
SSM_GROUP_CH = 16
SSM_STATE = 64
N_EXPERTS = 8
RMS_EPS = 1e-6
NEG_BIG = -1e30


def _params(*semantics):
    return pltpu.CompilerParams(dimension_semantics=semantics,
                                vmem_limit_bytes=VMEM_LIMIT_BYTES)


def _sigmoid(v):
    return 1.0 / (1.0 + jnp.exp(-v))


ADA_TN = 512


def _ada_kernel(cb_ref, w_ref, b_ref, o_ref, act_ref):
    nb, k = cb_ref.shape[0], cb_ref.shape[1]
    nj = w_ref.shape[1] // LANES

    @pl.when(jnp.logical_and(pl.program_id(0) == 0, pl.program_id(1) == 0))
    def _():
        cb = cb_ref[...]
        act_ref[...] = cb * _sigmoid(cb)

    def body(i, accs):
        r = pl.multiple_of(i * SUBLANES, SUBLANES)
        w = [w_ref[pl.ds(r, SUBLANES), j * LANES:(j + 1) * LANES] for j in range(nj)]
        out = []
        for b in range(nb):
            act = act_ref[b, pl.ds(r, SUBLANES), :]
            out.extend(accs[b * nj + j] + w[j] * act for j in range(nj))
        return tuple(out)

    zero = jnp.zeros((SUBLANES, LANES), F32)
    accs = lax.fori_loop(0, k // SUBLANES, body, (zero,) * (nb * nj), unroll=2)
    for b in range(nb):
        for j in range(nj):
            sl = slice(j * LANES, (j + 1) * LANES)
            o_ref[b:b + 1, sl] = jnp.sum(accs[b * nj + j], axis=0, keepdims=True) + b_ref[:, sl]


def _ada_modulation(c, ada_w, ada_b):
    depth, k, n = ada_w.shape
    nb = c.shape[0]
    cb = jnp.broadcast_to(c[:, :, None], (nb, k, LANES))
    return pl.pallas_call(
        _ada_kernel,
        grid=(depth, n // ADA_TN),
        in_specs=[
            pl.BlockSpec((nb, k, LANES), lambda l, j: (0, 0, 0)),
            pl.BlockSpec((None, k, ADA_TN), lambda l, j: (l, 0, j)),
            pl.BlockSpec((None, 1, ADA_TN), lambda l, j: (l, 0, j)),
        ],
        out_specs=pl.BlockSpec((None, nb, ADA_TN), lambda l, j: (l, 0, j)),
        out_shape=jax.ShapeDtypeStruct((depth, nb, n), F32),
        scratch_shapes=[pltpu.VMEM((nb, k, LANES), F32)],
        compiler_params=_params("arbitrary", "arbitrary"),
        name="ada_modulation",
    )(cb, ada_w, ada_b.reshape(depth, 1, n))


NORM_TS = 512


def _rms(x, g):
    return x * lax.rsqrt(jnp.mean(x * x, axis=-1, keepdims=True) + RMS_EPS) * g


def _norm_mod_kernel(x_ref, g_ref, sh_ref, sc_ref, o_ref):
    y = _rms(x_ref[...], g_ref[...])
    o_ref[...] = (y * (1.0 + sc_ref[...]) + sh_ref[...]).astype(o_ref.dtype)


def _norm_mod(x, g, shift, scale):
    b, s, d = x.shape
    vec = pl.BlockSpec((None, 1, d), lambda i, j: (i, 0, 0))
    out = pl.pallas_call(
        _norm_mod_kernel,
        grid=(b, s // NORM_TS),
        in_specs=[
            pl.BlockSpec((None, NORM_TS, d), lambda i, j: (i, j, 0)),
            pl.BlockSpec((1, d), lambda i, j: (0, 0)),
            vec, vec,
        ],
        out_specs=pl.BlockSpec((None, NORM_TS, d), lambda i, j: (i, j, 0)),
        out_shape=jax.ShapeDtypeStruct((b, s, d), BF16),
        compiler_params=_params("arbitrary", "arbitrary"),
        name="norm_mod",
    )(x, g.reshape(1, d), shift.reshape(b, 1, d), scale.reshape(b, 1, d))
    return out.reshape(b * s, d)


def _final_norm_kernel(x_ref, g_ref, o_ref):
    o_ref[...] = _rms(x_ref[...], g_ref[...])


def _final_norm(x, g):
    b, s, d = x.shape
    return pl.pallas_call(
        _final_norm_kernel,
        grid=(b, s // NORM_TS),
        in_specs=[
            pl.BlockSpec((None, NORM_TS, d), lambda i, j: (i, j, 0)),
            pl.BlockSpec((1, d), lambda i, j: (0, 0)),
        ],
        out_specs=pl.BlockSpec((None, NORM_TS, d), lambda i, j: (i, j, 0)),
        out_shape=jax.ShapeDtypeStruct((b, s, d), F32),
        compiler_params=_params("arbitrary", "arbitrary"),
        name="final_norm",
    )(x, g.reshape(1, d))


CAST_ROWS = 256


def _cast_weight(w_ref, wb_ref):
    def body(i, carry):
        r = pl.multiple_of(i * CAST_ROWS, CAST_ROWS)
        wb_ref[pl.ds(r, CAST_ROWS), :] = w_ref[pl.ds(r, CAST_ROWS), :].astype(BF16)
        return carry
    lax.fori_loop(0, w_ref.shape[0] // CAST_ROWS, body, 0)


def _dot(a, b):
    return jnp.dot(a, b, preferred_element_type=F32)


def _mm_plain_kernel(a_ref, w_ref, o_ref, wb_ref):
    @pl.when(pl.program_id(1) == 0)
    def _():
        _cast_weight(w_ref, wb_ref)
    o_ref[...] = _dot(a_ref[...], wb_ref[...]).astype(o_ref.dtype)


def _mm_plain(a, w, w_idx, col_tile, n_tiles, out_dtype, tm, tn):
    m, k = a.shape
    lead = (None,) * len(w_idx)
    return pl.pallas_call(
        _mm_plain_kernel,
        grid=(n_tiles, m // tm),
        in_specs=[
            pl.BlockSpec((tm, k), lambda j, i: (i, 0)),
            pl.BlockSpec(lead + (k, tn), lambda j, i: w_idx + (0, col_tile(j))),
        ],
        out_specs=pl.BlockSpec((tm, tn), lambda j, i: (i, j)),
        out_shape=jax.ShapeDtypeStruct((m, n_tiles * tn), out_dtype),
        scratch_shapes=[pltpu.VMEM((k, tn), BF16)],
        compiler_params=_params("arbitrary", "arbitrary"),
        name="mm_plain",
    )(a, w)


def _mm_swiglu_kernel(a_ref, w1_ref, w3_ref, o_ref, wb1_ref, wb3_ref):
    @pl.when(pl.program_id(1) == 0)
    def _():
        _cast_weight(w1_ref, wb1_ref)
        _cast_weight(w3_ref, wb3_ref)
    a = a_ref[...]
    u = _dot(a, wb1_ref[...])
    v = _dot(a, wb3_ref[...])
    o_ref[...] = (u * _sigmoid(u) * v).astype(o_ref.dtype)


def _mm_swiglu(a, w1, w3, w_idx, tm, tn):
    m, k = a.shape
    n = w1.shape[-1]
    lead = (None,) * len(w_idx)
    wspec = pl.BlockSpec(lead + (k, tn), lambda j, i: w_idx + (0, j))
    return pl.pallas_call(
        _mm_swiglu_kernel,
        grid=(n // tn, m // tm),
        in_specs=[pl.BlockSpec((tm, k), lambda j, i: (i, 0)), wspec, wspec],
        out_specs=pl.BlockSpec((tm, tn), lambda j, i: (i, j)),
        out_shape=jax.ShapeDtypeStruct((m, n), BF16),
        scratch_shapes=[pltpu.VMEM((k, tn), BF16), pltpu.VMEM((k, tn), BF16)],
        compiler_params=_params("arbitrary", "arbitrary"),
        name="mm_swiglu",
    )(a, w1, w3)


def _mm_resid_kernel(a_ref, w_ref, x_ref, g_ref, o_ref, wb_ref):
    @pl.when(pl.program_id(1) == 0)
    def _():
        _cast_weight(w_ref, wb_ref)
    o_ref[...] = x_ref[...] + g_ref[...] * _dot(a_ref[...], wb_ref[...])


def _mm_resid(a, w, w_idx, x, gate, seq, tm, tn):
    m, k = a.shape
    n = w.shape[-1]
    nb = gate.shape[0]
    lead = (None,) * len(w_idx)
    return pl.pallas_call(
        _mm_resid_kernel,
        grid=(n // tn, m // tm),
        in_specs=[
            pl.BlockSpec((tm, k), lambda j, i: (i, 0)),
            pl.BlockSpec(lead + (k, tn), lambda j, i: w_idx + (0, j)),
            pl.BlockSpec((tm, tn), lambda j, i: (i, j)),
            pl.BlockSpec((None, 1, tn), lambda j, i: (i * tm // seq, 0, j)),
        ],
        out_specs=pl.BlockSpec((tm, tn), lambda j, i: (i, j)),
        out_shape=jax.ShapeDtypeStruct((m, n), F32),
        scratch_shapes=[pltpu.VMEM((k, tn), BF16)],
        compiler_params=_params("arbitrary", "arbitrary"),
        name="mm_resid",
    )(a, w, x, gate.reshape(nb, 1, n))


def _mm_resid_norm_kernel(a_ref, w_ref, x_ref, g_ref, ng_ref, sh_ref, sc_ref,
                          xo_ref, ho_ref, wb_ref):
    @pl.when(pl.program_id(0) == 0)
    def _():
        _cast_weight(w_ref, wb_ref)
    xn = x_ref[...] + g_ref[...] * _dot(a_ref[...], wb_ref[...])
    xo_ref[...] = xn
    yn = _rms(xn, ng_ref[...])
    ho_ref[...] = (yn * (1.0 + sc_ref[...]) + sh_ref[...]).astype(ho_ref.dtype)


def _mm_resid_norm(a, w, layer, x, gate, norm_g, shift, scale, seq, tm):
    m, k = a.shape
    n = w.shape[-1]
    nb = gate.shape[0]
    vec = pl.BlockSpec((None, 1, n), lambda i: (i * tm // seq, 0, 0))
    row = pl.BlockSpec((tm, n), lambda i: (i, 0))
    return pl.pallas_call(
        _mm_resid_norm_kernel,
        grid=(m // tm,),
        in_specs=[
            pl.BlockSpec((tm, k), lambda i: (i, 0)),
            pl.BlockSpec((None, k, n), lambda i: (layer, 0, 0), pipeline_mode=pl.Buffered(1)),
            row, vec,
            pl.BlockSpec((1, n), lambda i: (0, 0)),
            vec, vec,
        ],
        out_specs=[row, row],
        out_shape=[jax.ShapeDtypeStruct((m, n), F32), jax.ShapeDtypeStruct((m, n), BF16)],
        scratch_shapes=[pltpu.VMEM((k, n), BF16)],
        compiler_params=_params("arbitrary"),
        name="mm_resid_norm",
    )(a, w, x, gate.reshape(nb, 1, n), norm_g.reshape(1, n),
      shift.reshape(nb, 1, n), scale.reshape(nb, 1, n))


def _mm_branch_kernel(at_ref, ss_ref, wa_ref, ws_ref, ga_ref, gs_ref, o_ref,
                      wba_ref, wbs_ref):
    @pl.when(pl.program_id(1) == 0)
    def _():
        _cast_weight(wa_ref, wba_ref)
        _cast_weight(ws_ref, wbs_ref)
    pa = _dot(at_ref[...], wba_ref[...])
    ps = _dot(ss_ref[...], wbs_ref[...])
    ga = _sigmoid(ga_ref[...].astype(F32))
    gs = _sigmoid(gs_ref[...].astype(F32))
    o_ref[...] = (ga * pa + gs * ps).astype(o_ref.dtype)


def _mm_branch(attn, ssm, w_branch, layer, proj, gate_col, tm, tn):
    m, k = attn.shape
    n = w_branch.shape[-1]
    g0 = gate_col // tn
    g1 = (gate_col + n) // tn
    return pl.pallas_call(
        _mm_branch_kernel,
        grid=(n // tn, m // tm),
        in_specs=[
            pl.BlockSpec((tm, k), lambda j, i: (i, 0)),
            pl.BlockSpec((tm, k), lambda j, i: (i, 0)),
            pl.BlockSpec((None, k, tn), lambda j, i: (layer, 0, j)),
            pl.BlockSpec((None, k, tn), lambda j, i: (layer, 1, j)),
            pl.BlockSpec((tm, tn), lambda j, i: (i, g0 + j)),
            pl.BlockSpec((tm, tn), lambda j, i: (i, g1 + j)),
        ],
        out_specs=pl.BlockSpec((tm, tn), lambda j, i: (i, j)),
        out_shape=jax.ShapeDtypeStruct((m, n), BF16),
        scratch_shapes=[pltpu.VMEM((k, tn), BF16), pltpu.VMEM((k, tn), BF16)],
        compiler_params=_params("arbitrary", "arbitrary"),
        name="mm_branch",
    )(attn, ssm, w_branch, w_branch, proj, proj)


TOKEN_RESIDUES = 16
ATTN_BB = 4


def _dot_nt(a, b):
    return lax.dot_general(a, b, (((1,), (1,)), ((), ())), preferred_element_type=F32)


def _attn_kernel(*refs, with_prev):
    if with_prev:
        q_ref, kp_ref, kc_ref, vp_ref, vc_ref, bc_ref, bp_ref, o_ref, lse_ref = refs
    else:
        q_ref, kc_ref, vc_ref, bc_ref, o_ref, lse_ref = refs
    blk = ATTN_BLOCK
    bb, nc, rpc = q_ref.shape[0], q_ref.shape[1], q_ref.shape[2]
    scale = HEAD_DIM ** -0.5
    lane = lax.broadcasted_iota(jnp.int32, (blk, LANES), 1)

    def heads(ref, bi):
        x = ref[bi].reshape(blk, ATTN_OUT)
        return jnp.stack([x[:, h * HEAD_DIM:(h + 1) * HEAD_DIM] for h in range(HEADS)]
                         ).astype(BF16)

    def qk(q, k):
        return lax.dot_general(q, k, (((2,), (2,)), ((0,), (0,))), preferred_element_type=F32)

    def pv(p, v):
        return lax.dot_general(p.astype(BF16), v, (((2,), (1,)), ((0,), (0,))),
                               preferred_element_type=F32)

    for bi in range(bb):
        q = heads(q_ref, bi)
        s_c = qk(q, heads(kc_ref, bi)) * scale + bc_ref[...]
        m = jnp.max(s_c, axis=2, keepdims=True)
        if with_prev:
            bias_p = jnp.where(pl.program_id(2) > 0, bp_ref[...], NEG_BIG)
            s_p = qk(q, heads(kp_ref, bi)) * scale + bias_p
            m = jnp.maximum(m, jnp.max(s_p, axis=2, keepdims=True))
        p_c = jnp.exp(s_c - m)
        l = jnp.sum(p_c, axis=2, keepdims=True)
        o = pv(p_c, heads(vc_ref, bi))
        if with_prev:
            p_p = jnp.exp(s_p - m)
            l = l + jnp.sum(p_p, axis=2, keepdims=True)
            o = o + pv(p_p, heads(vp_ref, bi))
        o = o / l
        lse = m + jnp.log(l)
        lse_tile = jnp.zeros((blk, LANES), F32)
        for h in range(HEADS):
            sl = slice(h * HEAD_DIM, (h + 1) * HEAD_DIM)
            o_ref[bi, :, :, sl] = o[h].reshape(nc, rpc, HEAD_DIM).astype(o_ref.dtype)
            lse_tile = jnp.where(lane == h, lse[h], lse_tile)
        lse_ref[bi] = lse_tile.reshape(nc, rpc, LANES)


def _attn_bias(dilation, nc, back):
    rpc = ATTN_BLOCK // nc
    i = jnp.arange(ATTN_BLOCK, dtype=jnp.int32)
    off = nc * (i % rpc) + i // rpc
    dist = off[:, None] - off[None, :] + back * ATTN_BLOCK
    valid = jnp.logical_and(dist >= 0, dist <= ATTN_BLOCK)
    slopes = 2.0 ** (-8.0 * (jnp.arange(HEADS, dtype=F32) + 1.0) / HEADS)
    pen = slopes[:, None, None] * (dist * dilation).astype(F32)[None]
    return jnp.where(valid[None], -pen, NEG_BIG)


def _attention_group(qkv, qkv_cols, batch, seq, gi):
    window, dilation = ATTN_PATTERNS[gi]
    assert window // dilation == ATTN_BLOCK
    cols = qkv.shape[-1]
    per_res = seq // TOKEN_RESIDUES
    nc = TOKEN_RESIDUES // dilation
    rpc = ATTN_BLOCK // nc
    nblk = seq // dilation // ATTN_BLOCK
    with_prev = nblk > 1

    def shape5(c):
        return (batch, nc, dilation, per_res, c)

    def spec(c, col, back):
        return pl.BlockSpec(
            (ATTN_BB, nc, None, rpc, c),
            lambda b, r, n: (b, 0, r, jnp.maximum(n - back, 0), col))

    bias_spec = pl.BlockSpec((HEADS, ATTN_BLOCK, ATTN_BLOCK), lambda b, r, n: (0, 0, 0))
    in_specs = [spec(ATTN_OUT, qkv_cols[0], 0)]
    for col in qkv_cols[1:]:
        if with_prev:
            in_specs.append(spec(ATTN_OUT, col, 1))
        in_specs.append(spec(ATTN_OUT, col, 0))
    view = qkv.reshape(shape5(cols))
    args = [view] * len(in_specs) + [_attn_bias(dilation, nc, 0)]
    in_specs.append(bias_spec)
    if with_prev:
        args.append(_attn_bias(dilation, nc, 1))
        in_specs.append(bias_spec)
    o, lse = pl.pallas_call(
        functools.partial(_attn_kernel, with_prev=with_prev),
        grid=(batch // ATTN_BB, dilation, nblk),
        in_specs=in_specs,
        out_specs=[spec(ATTN_OUT, 0, 0), spec(LANES, 0, 0)],
        out_shape=[
            jax.ShapeDtypeStruct(shape5(ATTN_OUT), qkv.dtype),
            jax.ShapeDtypeStruct(shape5(LANES), F32),
        ],
        compiler_params=_params("arbitrary", "arbitrary", "arbitrary"),
        name=f"attn_g{gi}",
    )(*args)
    return o.reshape(batch * seq, ATTN_OUT), lse.reshape(batch * seq, LANES)


COMBINE_TS = 512


def _combine_kernel(o0_ref, o1_ref, o2_ref, l0_ref, l1_ref, l2_ref, out_ref):
    a0, a1, a2 = l0_ref[...], l1_ref[...], l2_ref[...]
    m = jnp.maximum(jnp.maximum(a0, a1), a2)
    e0 = jnp.exp(a0 - m)
    e1 = jnp.exp(a1 - m)
    e2 = jnp.exp(a2 - m)
    inv = 1.0 / (e0 + e1 + e2)
    w0, w1, w2 = e0 * inv, e1 * inv, e2 * inv
    for h in range(HEADS):
        sl = slice(h * HEAD_DIM, (h + 1) * HEAD_DIM)
        mix = (w0[:, h:h + 1] * o0_ref[:, sl].astype(F32)
               + w1[:, h:h + 1] * o1_ref[:, sl].astype(F32)
               + w2[:, h:h + 1] * o2_ref[:, sl].astype(F32))
        out_ref[:, sl] = mix.astype(out_ref.dtype)


def _dilation_mixture_attention(qkv0, rest, batch, seq):
    outs, lses = [], []
    for gi in range(N_GROUPS):
        if gi == 0:
            o, lse = _attention_group(qkv0, (0, 1, 2), batch, seq, gi)
        else:
            o, lse = _attention_group(rest, (gi - 1, gi + 1, gi + 3), batch, seq, gi)
        outs.append(o)
        lses.append(lse)
    m = batch * seq
    ospec = pl.BlockSpec((COMBINE_TS, ATTN_OUT), lambda i: (i, 0))
    lspec = pl.BlockSpec((COMBINE_TS, LANES), lambda i: (i, 0))
    return pl.pallas_call(
        _combine_kernel,
        grid=(m // COMBINE_TS,),
        in_specs=[ospec] * 3 + [lspec] * 3,
        out_specs=ospec,
        out_shape=jax.ShapeDtypeStruct((m, ATTN_OUT), BF16),
        compiler_params=_params("arbitrary"),
        name="attn_combine",
    )(*outs, *lses)


SSM_LT = 128
SSM_JBLK = 256
SSM_SCAN_COLS = 1024


def _gelu_tanh(y):
    return 0.5 * y * (1.0 + jnp.tanh(math.sqrt(2.0 / math.pi) * (y + 0.044715 * (y * y * y))))


def _ssm_kernel(u_ref, bblk_ref, cblk_ref, are_ref, aim_ref, d_ref, wglu_ref, bglu_ref,
                o_ref, sre_ref, sim_ref, hre_ref, him_ref, *, nbatch):
    rows, width = u_ref.shape
    njb = width // SSM_JBLK
    jstates = bblk_ref.shape[2] // 2
    nstate = njb * jstates
    pairs = rows // SUBLANES
    per_tile = SUBLANES // nbatch

    @pl.when(pl.program_id(0) == 0)
    def _():
        hre_ref[...] = jnp.zeros_like(hre_ref)
        him_ref[...] = jnp.zeros_like(him_ref)

    for j in range(njb):
        bu = _dot(u_ref[:, j * SSM_JBLK:(j + 1) * SSM_JBLK], bblk_ref[j])
        cs = slice(j * jstates, (j + 1) * jstates)
        sre_ref[:, :, cs] = bu[:, :jstates].reshape(pairs, SUBLANES, jstates)
        sim_ref[:, :, cs] = bu[:, jstates:].reshape(pairs, SUBLANES, jstates)

    for c in range(nstate // SSM_SCAN_COLS):
        cs = slice(c * SSM_SCAN_COLS, (c + 1) * SSM_SCAN_COLS)
        ar = are_ref[0:nbatch, cs]
        ai = aim_ref[0:nbatch, cs]

        def step(k, carry):
            hr, hi = carry
            for t in range(per_tile):
                rs = slice(t * nbatch, (t + 1) * nbatch)
                br = sre_ref[k, rs, cs]
                bi = sim_ref[k, rs, cs]
                nr = ar * hr - ai * hi + br
                ni = ar * hi + ai * hr + bi
                sre_ref[k, rs, cs] = nr
                sim_ref[k, rs, cs] = ni
                hr, hi = nr, ni
            return hr, hi

        hr, hi = lax.fori_loop(0, pairs, step,
                               (hre_ref[0:nbatch, cs], him_ref[0:nbatch, cs]))
        hre_ref[0:nbatch, cs] = hr
        him_ref[0:nbatch, cs] = hi

    ys = []
    for j in range(njb):
        cs = slice(j * jstates, (j + 1) * jstates)
        hr = sre_ref[:, :, cs].reshape(rows, jstates).astype(BF16)
        hi = sim_ref[:, :, cs].reshape(rows, jstates).astype(BF16)
        ys.append(_dot(hr, cblk_ref[j, 0:jstates, :]) + _dot(hi, cblk_ref[j, jstates:, :]))
    y = jnp.concatenate(ys, axis=1) + d_ref[...] * u_ref[...].astype(F32)
    z = _gelu_tanh(y).astype(BF16)
    g = _dot(z, wglu_ref[...]) + bglu_ref[...]
    o_ref[...] = (g[:, :width] * _sigmoid(g[:, width:])).astype(o_ref.dtype)


def _ssm_tables(a_re, a_im, log_dt, b_re, b_im, c_re, c_im):
    groups, nst = a_re.shape
    gpb = SSM_JBLK // SSM_GROUP_CH
    njb = groups // gpb
    lam = lax.complex(a_re.astype(F32), a_im.astype(F32))
    dt = jnp.exp(log_dt.astype(F32))[:, None]
    a_bar = jnp.exp(lam * dt)
    b_mat = lax.complex(b_re.astype(F32), b_im.astype(F32))
    b_bar = ((a_bar - 1.0) / lam)[:, :, None] * b_mat
    eye = jnp.eye(gpb, dtype=F32)

    def in_blocks(t):
        t = t.reshape(njb, gpb, nst, SSM_GROUP_CH)
        return jnp.einsum('jgnc,gh->jgchn', t, eye).reshape(njb, gpb * SSM_GROUP_CH, gpb * nst)

    def out_blocks(t):
        t = t.reshape(njb, gpb, SSM_GROUP_CH, nst)
        return jnp.einsum('jgcn,gh->jhngc', t, eye).reshape(njb, gpb * nst, gpb * SSM_GROUP_CH)

    bblk = jnp.concatenate([in_blocks(jnp.real(b_bar)), in_blocks(jnp.imag(b_bar))], axis=2)
    cblk = jnp.concatenate([out_blocks(c_re.astype(F32)), out_blocks(-c_im.astype(F32))], axis=1)
    are = jnp.broadcast_to(jnp.real(a_bar).reshape(1, groups * nst), (SUBLANES, groups * nst))
    aim = jnp.broadcast_to(jnp.imag(a_bar).reshape(1, groups * nst), (SUBLANES, groups * nst))
    return bblk.astype(BF16), cblk.astype(BF16), are, aim


def _s5_ssm(u_tb, nbatch, tables, d_skip, w_glu, b_glu):
    bblk, cblk, are, aim = tables
    total, width = u_tb.shape
    rows = SSM_LT * nbatch
    nstate = are.shape[1]
    full = lambda a: pl.BlockSpec(a.shape, lambda i: (0,) * a.ndim)
    d2 = d_skip.reshape(1, width).astype(F32)
    wg = w_glu.astype(BF16)
    bg = b_glu.reshape(1, 2 * width).astype(F32)
    return pl.pallas_call(
        functools.partial(_ssm_kernel, nbatch=nbatch),
        grid=(total // rows,),
        in_specs=[pl.BlockSpec((rows, width), lambda i: (i, 0)),
                  full(bblk), full(cblk), full(are), full(aim), full(d2), full(wg), full(bg)],
        out_specs=pl.BlockSpec((rows, width), lambda i: (i, 0)),
        out_shape=jax.ShapeDtypeStruct((total, width), BF16),
        scratch_shapes=[
            pltpu.VMEM((rows // SUBLANES, SUBLANES, nstate), F32),
            pltpu.VMEM((rows // SUBLANES, SUBLANES, nstate), F32),
            pltpu.VMEM((SUBLANES, nstate), F32),
            pltpu.VMEM((SUBLANES, nstate), F32),
        ],
        compiler_params=_params("arbitrary"),
        name="s5_ssm",
    )(u_tb, bblk, cblk, are, aim, d2, wg, bg)


ROUTER_TS = 256


def _router_kernel(x_ref, g_ref, sh_ref, sc_ref, rw_ref, rb_ref,
                   sel_ref, wts_ref, cnt_ref, carry_ref):
    @pl.when(jnp.logical_and(pl.program_id(0) == 0, pl.program_id(1) == 0))
    def _():
        carry_ref[...] = jnp.zeros_like(carry_ref)

    h = _rms(x_ref[...], g_ref[...]) * (1.0 + sc_ref[...]) + sh_ref[...]
    logits = jnp.dot(h, rw_ref[...], preferred_element_type=F32,
                     precision=lax.Precision.HIGHEST) + rb_ref[...]
    ts, ne = logits.shape
    idx = lax.broadcasted_iota(jnp.int32, logits.shape, 1)
    m1 = jnp.max(logits, axis=1, keepdims=True)
    i1 = jnp.min(jnp.where(logits == m1, idx, ne), axis=1, keepdims=True)
    rest = jnp.where(idx == i1, -jnp.inf, logits)
    m2 = jnp.max(rest, axis=1, keepdims=True)
    i2 = jnp.min(jnp.where(rest == m2, idx, ne), axis=1, keepdims=True)
    e = jnp.exp(m2 - m1)
    w1 = 1.0 / (1.0 + e)
    w2 = e / (1.0 + e)

    onehot = jnp.where(idx == i1, 1.0, 0.0) + jnp.where(idx == i2, 1.0, 0.0)
    row = lax.broadcasted_iota(jnp.int32, (ts, ts), 0)
    col = lax.broadcasted_iota(jnp.int32, (ts, ts), 1)
    lower = jnp.where(col < row, 1.0, 0.0).astype(BF16)
    before = carry_ref[...] + _dot(lower, onehot.astype(BF16))
    r1 = jnp.sum(jnp.where(idx == i1, before, 0.0), axis=1, keepdims=True).astype(jnp.int32)
    r2 = jnp.sum(jnp.where(idx == i2, before, 0.0), axis=1, keepdims=True).astype(jnp.int32)
    total = carry_ref[...] + jnp.sum(onehot, axis=0, keepdims=True)
    carry_ref[...] = total
    cnt_ref[...] = total
    sel_ref[...] = jnp.where(idx == 0, i1, jnp.where(idx == 1, i2, jnp.where(
        idx == 2, r1, jnp.where(idx == 3, r2, 0))))
    wts_ref[...] = jnp.where(idx == 0, w1, jnp.where(idx == 1, w2, 0.0))


def _router(x, g, shift, scale, router_w, router_b):
    b, s, d = x.shape
    ne = router_w.shape[-1]
    vec = pl.BlockSpec((None, 1, d), lambda i, j: (i, 0, 0))
    tok = pl.BlockSpec((None, ROUTER_TS, ne), lambda i, j: (i, j, 0))
    sel, wts, cnt = pl.pallas_call(
        _router_kernel,
        grid=(b, s // ROUTER_TS),
        in_specs=[
            pl.BlockSpec((None, ROUTER_TS, d), lambda i, j: (i, j, 0)),
            pl.BlockSpec((1, d), lambda i, j: (0, 0)),
            vec, vec,
            pl.BlockSpec((d, ne), lambda i, j: (0, 0)),
            pl.BlockSpec((1, ne), lambda i, j: (0, 0)),
        ],
        out_specs=[tok, tok, pl.BlockSpec((1, ne), lambda i, j: (0, 0))],
        out_shape=[jax.ShapeDtypeStruct((b, s, ne), jnp.int32),
                   jax.ShapeDtypeStruct((b, s, ne), F32),
                   jax.ShapeDtypeStruct((1, ne), F32)],
        scratch_shapes=[pltpu.VMEM((1, ne), F32)],
        compiler_params=_params("arbitrary", "arbitrary"),
        name="router",
    )(x, g.reshape(1, d), shift.reshape(b, 1, d), scale.reshape(b, 1, d),
      router_w, router_b.reshape(1, ne))
    return sel.reshape(b * s, ne), wts.reshape(b * s, ne), cnt


MOE_TM = 256
MOE_TT = 256
MOE_UP_TN = 1408
MOE_DOWN_TN = 1024


def _moe_plan(sel, cnt, tokens):
    ne = cnt.shape[-1]
    counts = cnt[0].astype(jnp.int32)
    padded = (counts + MOE_TM - 1) // MOE_TM * MOE_TM
    ends = jnp.cumsum(padded)
    starts = ends - padded
    pos1 = starts[sel[:, 0]] + sel[:, 2]
    pos2 = starts[sel[:, 1]] + sel[:, 3]
    nt = tokens // MOE_TT
    pos = jnp.concatenate([pos1.reshape(nt, MOE_TT), pos2.reshape(nt, MOE_TT)], axis=1)
    max_tiles = 2 * tokens // MOE_TM + ne
    tile_start = jnp.arange(max_tiles, dtype=jnp.int32) * MOE_TM
    tile_expert = jnp.minimum(
        jnp.sum((tile_start[:, None] >= ends[None, :]).astype(jnp.int32), axis=1), ne - 1)
    num_tiles = (ends[-1] // MOE_TM).reshape(1)
    zero_tiles = jnp.concatenate([
        jnp.maximum(ends // MOE_TM - 1, 0),
        jnp.minimum(num_tiles[0] + jnp.arange(ne, dtype=jnp.int32), max_tiles - 1)])
    return pos.reshape(nt, 1, 2 * MOE_TT), tile_expert, num_tiles, max_tiles, zero_tiles


def _row_copies(pos_ref, t, make):
    tt = pos_ref.shape[1] // 2
    return make(0, t, pos_ref[0, t]), make(1, t, pos_ref[0, tt + t])


def _issue_and_drain(pos_ref, make):
    tt = pos_ref.shape[1] // 2

    def issue(t, carry):
        for cp in _row_copies(pos_ref, t, make):
            cp.start()
        return carry

    def drain(t, carry):
        for cp in _row_copies(pos_ref, t, make):
            cp.wait()
        return carry

    lax.fori_loop(0, tt, issue, 0, unroll=4)
    lax.fori_loop(0, tt, drain, 0, unroll=4)


def _moe_dispatch_kernel(pos_ref, zt_ref, x_ref, g_ref, sh_ref, sc_ref, xs_ref, hbuf_ref, sem):
    @pl.when(jnp.logical_and(pl.program_id(0) == 0, pl.program_id(1) == 0))
    def _():
        hbuf_ref[...] = jnp.zeros_like(hbuf_ref)
        for k in range(zt_ref.shape[0]):
            cp = pltpu.make_async_copy(
                hbuf_ref, xs_ref.at[pl.ds(pl.multiple_of(zt_ref[k] * MOE_TM, MOE_TM), MOE_TM), :],
                sem)
            cp.start()
            cp.wait()

    hbuf_ref[...] = _rms(x_ref[...], g_ref[...]) * (1.0 + sc_ref[...]) + sh_ref[...]

    def make(k, t, p):
        return pltpu.make_async_copy(hbuf_ref.at[pl.ds(t, 1), :], xs_ref.at[pl.ds(p, 1), :], sem)

    _issue_and_drain(pos_ref, make)


def _moe_dispatch(x, g, shift, scale, plan):
    pos, _, _, max_tiles, zero_tiles = plan
    b, s, d = x.shape
    per_b = s // MOE_TT
    assert MOE_TT == MOE_TM
    vec = pl.BlockSpec((None, 1, d), lambda i, j: (i, 0, 0))
    return pl.pallas_call(
        _moe_dispatch_kernel,
        grid=(b, per_b),
        in_specs=[
            pl.BlockSpec((None, 1, 2 * MOE_TT), lambda i, j: (i * per_b + j, 0, 0),
                         memory_space=pltpu.SMEM),
            pl.BlockSpec(memory_space=pltpu.SMEM),
            pl.BlockSpec((None, MOE_TT, d), lambda i, j: (i, j, 0)),
            pl.BlockSpec((1, d), lambda i, j: (0, 0)),
            vec, vec,
        ],
        out_specs=pl.BlockSpec(memory_space=pl.ANY),
        out_shape=jax.ShapeDtypeStruct((max_tiles * MOE_TM, d), F32),
        scratch_shapes=[pltpu.VMEM((MOE_TT, d), F32), pltpu.SemaphoreType.DMA(())],
        compiler_params=_params("arbitrary", "arbitrary"),
        name="moe_dispatch",
    )(pos, zero_tiles, x, g.reshape(1, d), shift.reshape(b, 1, d), scale.reshape(b, 1, d))


def _moe_mm_kernel(te_ref, nt_ref, a_ref, w_ref, *rest, mode):
    if mode == "gate":
        u_ref, o_ref, wb_ref = rest
    else:
        o_ref, wb_ref = rest
    i = pl.program_id(1)
    active = i < nt_ref[0]
    new_expert = jnp.logical_or(i == 0, te_ref[i] != te_ref[jnp.maximum(i - 1, 0)])

    @pl.when(jnp.logical_and(active, new_expert))
    def _():
        _cast_weight(w_ref, wb_ref)

    @pl.when(active)
    def _():
        if mode == "down":
            p = lax.dot_general(wb_ref[...], a_ref[...], (((0,), (0,)), ((), ())),
                                preferred_element_type=F32)
            o_ref[...] = p.T
        else:
            p = lax.dot_general(wb_ref[...], a_ref[...].astype(BF16), (((0,), (1,)), ((), ())),
                                preferred_element_type=F32)
            if mode == "silu":
                o_ref[...] = (p * _sigmoid(p)).astype(o_ref.dtype)
            else:
                o_ref[...] = (u_ref[...].astype(F32) * p).astype(o_ref.dtype)

    @pl.when(jnp.logical_not(active))
    def _():
        o_ref[...] = jnp.zeros_like(o_ref)


def _moe_mm(a, w, li, plan, tn, mode, u=None):
    _, tile_expert, num_tiles, max_tiles, _ = plan
    k, n = w.shape[-2], w.shape[-1]
    rows = max_tiles * MOE_TM

    def tile(i, nt):
        return jnp.minimum(i, nt[0] - 1)

    wspec = pl.BlockSpec((None, None, k, tn), lambda j, i, te, nt: (li, te[tile(i, nt)], 0, j))
    if mode == "down":
        in_specs = [pl.BlockSpec((k, MOE_TM), lambda j, i, te, nt: (0, tile(i, nt))), wspec]
        out_spec = pl.BlockSpec((MOE_TM, tn), lambda j, i, te, nt: (i, j))
        out_shape = jax.ShapeDtypeStruct((rows, n), F32)
    else:
        in_specs = [pl.BlockSpec((MOE_TM, k), lambda j, i, te, nt: (tile(i, nt), 0)), wspec]
        out_spec = pl.BlockSpec((tn, MOE_TM), lambda j, i, te, nt: (j, i))
        out_shape = jax.ShapeDtypeStruct((n, rows), BF16)
    args = [a, w]
    if mode == "gate":
        in_specs.append(pl.BlockSpec((tn, MOE_TM), lambda j, i, te, nt: (j, tile(i, nt))))
        args.append(u)
    return pl.pallas_call(
        functools.partial(_moe_mm_kernel, mode=mode),
        grid_spec=pltpu.PrefetchScalarGridSpec(
            num_scalar_prefetch=2,
            grid=(n // tn, max_tiles),
            in_specs=in_specs,
            out_specs=out_spec,
            scratch_shapes=[pltpu.VMEM((k, tn), BF16)],
        ),
        out_shape=out_shape,
        compiler_params=_params("arbitrary", "arbitrary"),
        name=f"moe_mm_{mode}",
    )(tile_expert, num_tiles, *args)


def _moe_combine_kernel(pos_ref, x_ref, g_ref, wts_ref, y_ref, *rest, final):
    if final:
        ng_ref, o_ref, ybuf_ref, sem = rest
    else:
        ng_ref, sh_ref, sc_ref, o_ref, h_ref, ybuf_ref, sem = rest

    def make(k, t, p):
        return pltpu.make_async_copy(y_ref.at[pl.ds(p, 1), :], ybuf_ref.at[k, pl.ds(t, 1), :], sem)

    _issue_and_drain(pos_ref, make)
    f = wts_ref[:, 0:1] * ybuf_ref[0] + wts_ref[:, 1:2] * ybuf_ref[1]
    xn = x_ref[...] + g_ref[...] * f
    yn = _rms(xn, ng_ref[...])
    if final:
        o_ref[...] = yn
    else:
        o_ref[...] = xn
        h_ref[...] = (yn * (1.0 + sc_ref[...]) + sh_ref[...]).astype(h_ref.dtype)


def _moe_combine(x, gate, wts, y, pos, seq, norm_g, shift=None, scale=None):
    m, d = x.shape
    nb = gate.shape[0]
    ne = wts.shape[-1]
    final = shift is None
    vec = pl.BlockSpec((None, 1, d), lambda i: (i * MOE_TT // seq, 0, 0))
    row = pl.BlockSpec((MOE_TT, d), lambda i: (i, 0))
    in_specs = [
        pl.BlockSpec((None, 1, 2 * MOE_TT), lambda i: (i, 0, 0), memory_space=pltpu.SMEM),
        row, vec,
        pl.BlockSpec((MOE_TT, ne), lambda i: (i, 0)),
        pl.BlockSpec(memory_space=pl.ANY),
        pl.BlockSpec((1, d), lambda i: (0, 0)),
    ]
    args = [pos, x, gate.reshape(nb, 1, d), wts, y, norm_g.reshape(1, d)]
    if final:
        out_specs, out_shape = row, jax.ShapeDtypeStruct((m, d), F32)
    else:
        in_specs += [vec, vec]
        args += [shift.reshape(nb, 1, d), scale.reshape(nb, 1, d)]
        out_specs = [row, row]
        out_shape = [jax.ShapeDtypeStruct((m, d), F32), jax.ShapeDtypeStruct((m, d), BF16)]
    return pl.pallas_call(
        functools.partial(_moe_combine_kernel, final=final),
        grid=(m // MOE_TT,),
        in_specs=in_specs,
        out_specs=out_specs,
        out_shape=out_shape,
        scratch_shapes=[pltpu.VMEM((2, MOE_TT, d), F32), pltpu.SemaphoreType.DMA(())],
        compiler_params=_params("arbitrary"),
        name="moe_combine",
    )(*args)


def kernel(x, c, ada_w, ada_b, norm_mix_g, norm_ffn_g, final_norm_g, w_in, ssm_a_re, ssm_a_im, ssm_log_dt, ssm_b_re, ssm_b_im, ssm_c_re, ssm_c_im, ssm_d, w_glu, b_glu, w_branch, w_out, ffn_w1, ffn_w3, ffn_w2, router_w, router_b, moe_w1, moe_w3, moe_w2):
    batch, seq, d = x.shape
    depth = ada_w.shape[0]
    tokens = batch * seq
    ssm_width = w_glu.shape[1]
    per_res = seq // TOKEN_RESIDUES
    tn_in = ATTN_OUT
    assert ssm_width == tn_in and w_in.shape[-1] == (3 * N_GROUPS + 1) * tn_in + 2 * d

    def to_residue_major(t):
        f = t.shape[-1]
        return t.reshape(batch, per_res, TOKEN_RESIDUES, f).transpose(0, 2, 1, 3).reshape(batch, seq, f)

    def to_time_batch(t):
        f = t.shape[-1]
        return t.reshape(batch, TOKEN_RESIDUES, per_res, f).transpose(2, 1, 0, 3).reshape(tokens, f)

    def from_time_batch(t):
        f = t.shape[-1]
        return t.reshape(per_res, TOKEN_RESIDUES, batch, f).transpose(2, 1, 0, 3).reshape(tokens, f)

    mod = _ada_modulation(c, ada_w, ada_b).reshape(depth, batch, 6, d)
    x2 = to_residue_major(x).reshape(tokens, d)
    h = None
    out = None
    for layer in range(depth):
        sh1, sc1, g1, sh2, sc2, g2 = (mod[layer, :, i] for i in range(6))

        if h is None:
            h = _norm_mod(x2.reshape(batch, seq, d), norm_mix_g[layer], sh1, sc1)
        qkv0 = _mm_plain(h, w_in, (layer,), lambda j: N_GROUPS * j, 3, F32, tm=1024, tn=tn_in)
        rest = _mm_plain(h, w_in, (layer,),
                         lambda j: j + 1 + jnp.where(j >= 2, 1, 0) + jnp.where(j >= 4, 1, 0),
                         11, BF16, tm=2048, tn=tn_in)
        attn = _dilation_mixture_attention(qkv0, rest, batch, seq)
        u_tb = to_time_batch(rest[:, 6 * tn_in:7 * tn_in])
        tables = _ssm_tables(ssm_a_re[layer], ssm_a_im[layer], ssm_log_dt[layer],
                             ssm_b_re[layer], ssm_b_im[layer], ssm_c_re[layer], ssm_c_im[layer])
        ssm = from_time_batch(
            _s5_ssm(u_tb, batch, tables, ssm_d[layer], w_glu[layer], b_glu[layer]))
        merged = _mm_branch(attn, ssm, w_branch, layer, rest, 7 * tn_in, tm=1024, tn=1024)
        h = None

        li = layer // 2
        if layer % 2 == 0:
            x2, hf = _mm_resid_norm(merged, w_out, layer, x2, g1, norm_ffn_g[layer], sh2, sc2,
                                    seq, tm=256)
            act = _mm_swiglu(hf, ffn_w1, ffn_w3, (li,), tm=2048, tn=512)
            x2 = _mm_resid(act, ffn_w2, (li,), x2, g2, seq, tm=512, tn=512)
        else:
            x2 = _mm_resid(merged, w_out, (layer,), x2, g1, seq, tm=1024, tn=1024)
            x3 = x2.reshape(batch, seq, d)
            sel, wts, cnt = _router(x3, norm_ffn_g[layer], sh2, sc2, router_w[li], router_b[li])
            plan = _moe_plan(sel, cnt, tokens)
            xs = _moe_dispatch(x3, norm_ffn_g[layer], sh2, sc2, plan)
            u = _moe_mm(xs, moe_w1, li, plan, MOE_UP_TN, "silu")
            act = _moe_mm(xs, moe_w3, li, plan, MOE_UP_TN, "gate", u)
            y = _moe_mm(act, moe_w2, li, plan, MOE_DOWN_TN, "down")
            if layer + 1 < depth:
                x2, h = _moe_combine(x2, g2, wts, y, plan[0], seq, norm_mix_g[layer + 1],
                                     mod[layer + 1, :, 0], mod[layer + 1, :, 1])
            else:
                out = _moe_combine(x2, g2, wts, y, plan[0], seq, final_norm_g)
    if out is None:
        out = _final_norm(x2.reshape(batch, seq, d), final_norm_g)
    return out.reshape(batch, TOKEN_RESIDUES, per_res, d).transpose(0, 2, 1, 3).reshape(batch, seq, d)
```

```python
import functools
import math

import jax
import jax.numpy as jnp
from jax import lax
from jax.experimental import pallas as pl
from jax.experimental.pallas import tpu as pltpu

F32 = jnp.float32
BF16 = jnp.bfloat16

LANES = 128
SUBLANES = 8
VMEM_LIMIT_BYTES = 56 * 1024 * 1024

ATTN_PATTERNS = ((128, 1), (512, 4), (2048, 16))
N_GROUPS = len(ATTN_PATTERNS)
HEADS = 8
HEAD_DIM = 128
ATTN_OUT = HEADS * HEAD_DIM
QKV_WIDTH = N_GROUPS * ATTN_OUT
ATTN_BLOCK = 128
SSM_GROUP_CH = 16
SSM_STATE = 64
N_EXPERTS = 8
RMS_EPS = 1e-6
NEG_BIG = -1e30


def _params(*semantics):
    return pltpu.CompilerParams(dimension_semantics=semantics,
                                vmem_limit_bytes=VMEM_LIMIT_BYTES)


def _sigmoid(v):
    return 1.0 / (1.0 + jnp.exp(-v))


ADA_KR = 256
ADA_CG = 512


def _ada_kernel(cb_ref, w_ref, b_ref, o_ref, act_ref):
    nb = cb_ref.shape[0]
    kr, n = w_ref.shape
    nj = ADA_CG // LANES
    kstep = pl.program_id(1)

    @pl.when(jnp.logical_and(pl.program_id(0) == 0, kstep == 0))
    def _():
        cb = cb_ref[...]
        act_ref[...] = cb * _sigmoid(cb)

    @pl.when(kstep == 0)
    def _():
        o_ref[...] = jnp.broadcast_to(b_ref[...], o_ref.shape)

    base = kstep * kr
    zero = jnp.zeros((SUBLANES, LANES), F32)
    for cg in range(n // ADA_CG):
        c0 = cg * ADA_CG

        def body(i, accs):
            r = pl.multiple_of(i * SUBLANES, SUBLANES)
            w = [w_ref[pl.ds(r, SUBLANES), c0 + j * LANES:c0 + (j + 1) * LANES] for j in range(nj)]
            out = []
            for b in range(nb):
                act = act_ref[b, pl.ds(pl.multiple_of(base + r, SUBLANES), SUBLANES), :]
                out.extend(accs[b * nj + j] + w[j] * act for j in range(nj))
            return tuple(out)

        accs = lax.fori_loop(0, kr // SUBLANES, body, (zero,) * (nb * nj), unroll=2)
        for b in range(nb):
            for j in range(nj):
                sl = slice(c0 + j * LANES, c0 + (j + 1) * LANES)
                o_ref[b:b + 1, sl] += jnp.sum(accs[b * nj + j], axis=0, keepdims=True)


def _ada_modulation(c, ada_w, ada_b):
    depth, k, n = ada_w.shape
    nb = c.shape[0]
    cb = jnp.broadcast_to(c[:, :, None], (nb, k, LANES))
    return pl.pallas_call(
        _ada_kernel,
        grid=(depth, k // ADA_KR),
        in_specs=[
            pl.BlockSpec((nb, k, LANES), lambda l, i: (0, 0, 0)),
            pl.BlockSpec((None, ADA_KR, n), lambda l, i: (l, i, 0)),
            pl.BlockSpec((None, 1, n), lambda l, i: (l, 0, 0)),
        ],
        out_specs=pl.BlockSpec((None, nb, n), lambda l, i: (l, 0, 0)),
        out_shape=jax.ShapeDtypeStruct((depth, nb, n), F32),
        scratch_shapes=[pltpu.VMEM((nb, k, LANES), F32)],
        compiler_params=_params("arbitrary", "arbitrary"),
        name="ada_modulation",
    )(cb, ada_w, ada_b.reshape(depth, 1, n))


NORM_TS = 512


def _rms(x, g):
    return x * lax.rsqrt(jnp.mean(x * x, axis=-1, keepdims=True) + RMS_EPS) * g


def _norm_mod_kernel(x_ref, g_ref, sh_ref, sc_ref, o_ref):
    y = _rms(x_ref[...], g_ref[...])
    o_ref[...] = (y * (1.0 + sc_ref[...]) + sh_ref[...]).astype(o_ref.dtype)


def _norm_mod(x, g, shift, scale):
    b, s, d = x.shape
    vec = pl.BlockSpec((None, 1, d), lambda i, j: (i, 0, 0))
    out = pl.pallas_call(
        _norm_mod_kernel,
        grid=(b, s // NORM_TS),
        in_specs=[
            pl.BlockSpec((None, NORM_TS, d), lambda i, j: (i, j, 0)),
            pl.BlockSpec((1, d), lambda i, j: (0, 0)),
            vec, vec,
        ],
        out_specs=pl.BlockSpec((None, NORM_TS, d), lambda i, j: (i, j, 0)),
        out_shape=jax.ShapeDtypeStruct((b, s, d), BF16),
        compiler_params=_params("arbitrary", "arbitrary"),
        name="norm_mod",
    )(x, g.reshape(1, d), shift.reshape(b, 1, d), scale.reshape(b, 1, d))
    return out.reshape(b * s, d)


def _final_norm_kernel(x_ref, g_ref, o_ref):
    o_ref[...] = _rms(x_ref[...], g_ref[...])


def _final_norm(x, g):
    b, s, d = x.shape
    return pl.pallas_call(
        _final_norm_kernel,
        grid=(b, s // NORM_TS),
        in_specs=[
            pl.BlockSpec((None, NORM_TS, d), lambda i, j: (i, j, 0)),
            pl.BlockSpec((1, d), lambda i, j: (0, 0)),
        ],
        out_specs=pl.BlockSpec((None, NORM_TS, d), lambda i, j: (i, j, 0)),
        out_shape=jax.ShapeDtypeStruct((b, s, d), F32),
        compiler_params=_params("arbitrary", "arbitrary"),
        name="final_norm",
    )(x, g.reshape(1, d))


CAST_ROWS = 256


def _cast_weight(w_ref, wb_ref):
    def body(i, carry):
        r = pl.multiple_of(i * CAST_ROWS, CAST_ROWS)
        wb_ref[pl.ds(r, CAST_ROWS), :] = w_ref[pl.ds(r, CAST_ROWS), :].astype(BF16)
        return carry
    lax.fori_loop(0, w_ref.shape[0] // CAST_ROWS, body, 0)


def _dot(a, b):
    return jnp.dot(a, b, preferred_element_type=F32)


def _mm_plain_kernel(a_ref, w_ref, o_ref, wb_ref):
    @pl.when(pl.program_id(1) == 0)
    def _():
        _cast_weight(w_ref, wb_ref)
    o_ref[...] = _dot(a_ref[...], wb_ref[...]).astype(o_ref.dtype)


def _mm_plain(a, w, w_idx, col_tile, n_tiles, out_dtype, tm, tn):
    m, k = a.shape
    lead = (None,) * len(w_idx)
    return pl.pallas_call(
        _mm_plain_kernel,
        grid=(n_tiles, m // tm),
        in_specs=[
            pl.BlockSpec((tm, k), lambda j, i: (i, 0)),
            pl.BlockSpec(lead + (k, tn), lambda j, i: w_idx + (0, col_tile(j))),
        ],
        out_specs=pl.BlockSpec((tm, tn), lambda j, i: (i, j)),
        out_shape=jax.ShapeDtypeStruct((m, n_tiles * tn), out_dtype),
        scratch_shapes=[pltpu.VMEM((k, tn), BF16)],
        compiler_params=_params("arbitrary", "arbitrary"),
        name="mm_plain",
    )(a, w)


def _mm_swiglu_kernel(a_ref, w1_ref, w3_ref, o_ref, wb1_ref, wb3_ref):
    @pl.when(pl.program_id(1) == 0)
    def _():
        _cast_weight(w1_ref, wb1_ref)
        _cast_weight(w3_ref, wb3_ref)
    a = a_ref[...]
    u = _dot(a, wb1_ref[...])
    v = _dot(a, wb3_ref[...])
    o_ref[...] = (u * _sigmoid(u) * v).astype(o_ref.dtype)


def _mm_swiglu(a, w1, w3, w_idx, tm, tn):
    m, k = a.shape
    n = w1.shape[-1]
    lead = (None,) * len(w_idx)
    wspec = pl.BlockSpec(lead + (k, tn), lambda j, i: w_idx + (0, j))
    return pl.pallas_call(
        _mm_swiglu_kernel,
        grid=(n // tn, m // tm),
        in_specs=[pl.BlockSpec((tm, k), lambda j, i: (i, 0)), wspec, wspec],
        out_specs=pl.BlockSpec((tm, tn), lambda j, i: (i, j)),
        out_shape=jax.ShapeDtypeStruct((m, n), BF16),
        scratch_shapes=[pltpu.VMEM((k, tn), BF16), pltpu.VMEM((k, tn), BF16)],
        compiler_params=_params("arbitrary", "arbitrary"),
        name="mm_swiglu",
    )(a, w1, w3)


def _mm_resid_kernel(a_ref, w_ref, x_ref, g_ref, o_ref, wb_ref):
    @pl.when(pl.program_id(1) == 0)
    def _():
        _cast_weight(w_ref, wb_ref)
    o_ref[...] = x_ref[...] + g_ref[...] * _dot(a_ref[...], wb_ref[...])


def _mm_resid(a, w, w_idx, x, gate, seq, tm, tn):
    m, k = a.shape
    n = w.shape[-1]
    nb = gate.shape[0]
    lead = (None,) * len(w_idx)
    return pl.pallas_call(
        _mm_resid_kernel,
        grid=(n // tn, m // tm),
        in_specs=[
            pl.BlockSpec((tm, k), lambda j, i: (i, 0)),
            pl.BlockSpec(lead + (k, tn), lambda j, i: w_idx + (0, j)),
            pl.BlockSpec((tm, tn), lambda j, i: (i, j)),
            pl.BlockSpec((None, 1, tn), lambda j, i: (i * tm // seq, 0, j)),
        ],
        out_specs=pl.BlockSpec((tm, tn), lambda j, i: (i, j)),
        out_shape=jax.ShapeDtypeStruct((m, n), F32),
        scratch_shapes=[pltpu.VMEM((k, tn), BF16)],
        compiler_params=_params("arbitrary", "arbitrary"),
        name="mm_resid",
    )(a, w, x, gate.reshape(nb, 1, n))


def _mm_resid_norm_kernel(a_ref, w_ref, x_ref, g_ref, ng_ref, sh_ref, sc_ref,
                          xo_ref, ho_ref, wb_ref):
    @pl.when(pl.program_id(0) == 0)
    def _():
        _cast_weight(w_ref, wb_ref)
    xn = x_ref[...] + g_ref[...] * _dot(a_ref[...], wb_ref[...])
    xo_ref[...] = xn
    yn = _rms(xn, ng_ref[...])
    ho_ref[...] = (yn * (1.0 + sc_ref[...]) + sh_ref[...]).astype(ho_ref.dtype)


def _mm_resid_norm(a, w, layer, x, gate, norm_g, shift, scale, seq, tm):
    m, k = a.shape
    n = w.shape[-1]
    nb = gate.shape[0]
    vec = pl.BlockSpec((None, 1, n), lambda i: (i * tm // seq, 0, 0))
    row = pl.BlockSpec((tm, n), lambda i: (i, 0))
    return pl.pallas_call(
        _mm_resid_norm_kernel,
        grid=(m // tm,),
        in_specs=[
            pl.BlockSpec((tm, k), lambda i: (i, 0)),
            pl.BlockSpec((None, k, n), lambda i: (layer, 0, 0), pipeline_mode=pl.Buffered(1)),
            row, vec,
            pl.BlockSpec((1, n), lambda i: (0, 0)),
            vec, vec,
        ],
        out_specs=[row, row],
        out_shape=[jax.ShapeDtypeStruct((m, n), F32), jax.ShapeDtypeStruct((m, n), BF16)],
        scratch_shapes=[pltpu.VMEM((k, n), BF16)],
        compiler_params=_params("arbitrary"),
        name="mm_resid_norm",
    )(a, w, x, gate.reshape(nb, 1, n), norm_g.reshape(1, n),
      shift.reshape(nb, 1, n), scale.reshape(nb, 1, n))


def _mm_branch_kernel(at_ref, ss_ref, wa_ref, ws_ref, ga_ref, gs_ref, o_ref,
                      wba_ref, wbs_ref):
    @pl.when(pl.program_id(1) == 0)
    def _():
        _cast_weight(wa_ref, wba_ref)
        _cast_weight(ws_ref, wbs_ref)
    pa = _dot(at_ref[...], wba_ref[...])
    ps = _dot(ss_ref[...], wbs_ref[...])
    ga = _sigmoid(ga_ref[...].astype(F32))
    gs = _sigmoid(gs_ref[...].astype(F32))
    o_ref[...] = (ga * pa + gs * ps).astype(o_ref.dtype)


def _mm_branch(attn, ssm, w_branch, layer, proj, gate_col, tm, tn):
    m, k = attn.shape
    n = w_branch.shape[-1]
    g0 = gate_col // tn
    g1 = (gate_col + n) // tn
    return pl.pallas_call(
        _mm_branch_kernel,
        grid=(n // tn, m // tm),
        in_specs=[
            pl.BlockSpec((tm, k), lambda j, i: (i, 0)),
            pl.BlockSpec((tm, k), lambda j, i: (i, 0)),
            pl.BlockSpec((None, k, tn), lambda j, i: (layer, 0, j)),
            pl.BlockSpec((None, k, tn), lambda j, i: (layer, 1, j)),
            pl.BlockSpec((tm, tn), lambda j, i: (i, g0 + j)),
            pl.BlockSpec((tm, tn), lambda j, i: (i, g1 + j)),
        ],
        out_specs=pl.BlockSpec((tm, tn), lambda j, i: (i, j)),
        out_shape=jax.ShapeDtypeStruct((m, n), BF16),
        scratch_shapes=[pltpu.VMEM((k, tn), BF16), pltpu.VMEM((k, tn), BF16)],
        compiler_params=_params("arbitrary", "arbitrary"),
        name="mm_branch",
    )(attn, ssm, w_branch, w_branch, proj, proj)


TOKEN_RESIDUES = 16
ATTN_BB = 4


def _dot_nt(a, b):
    return lax.dot_general(a, b, (((1,), (1,)), ((), ())), preferred_element_type=F32)


def _attn_kernel(*refs, with_prev):
    if with_prev:
        q_ref, kp_ref, kc_ref, vp_ref, vc_ref, bc_ref, bp_ref, o_ref, lse_ref = refs
    else:
        q_ref, kc_ref, vc_ref, bc_ref, o_ref, lse_ref = refs
    blk = ATTN_BLOCK
    bb, nc, rpc = q_ref.shape[0], q_ref.shape[1], q_ref.shape[2]
    scale = HEAD_DIM ** -0.5
    lane = lax.broadcasted_iota(jnp.int32, (blk, LANES), 1)

    def heads(ref, bi):
        x = ref[bi].reshape(blk, ATTN_OUT)
        return jnp.stack([x[:, h * HEAD_DIM:(h + 1) * HEAD_DIM] for h in range(HEADS)]
                         ).astype(BF16)

    def qk(q, k):
        return lax.dot_general(q, k, (((2,), (2,)), ((0,), (0,))), preferred_element_type=F32)

    def pv(p, v):
        return lax.dot_general(p.astype(BF16), v, (((2,), (1,)), ((0,), (0,))),
                               preferred_element_type=F32)

    for bi in range(bb):
        q = heads(q_ref, bi)
        s_c = qk(q, heads(kc_ref, bi)) * scale + bc_ref[...]
        m = jnp.max(s_c, axis=2, keepdims=True)
        if with_prev:
            bias_p = jnp.where(pl.program_id(2) > 0, bp_ref[...], NEG_BIG)
            s_p = qk(q, heads(kp_ref, bi)) * scale + bias_p
            m = jnp.maximum(m, jnp.max(s_p, axis=2, keepdims=True))
        p_c = jnp.exp(s_c - m)
        l = jnp.sum(p_c, axis=2, keepdims=True)
        o = pv(p_c, heads(vc_ref, bi))
        if with_prev:
            p_p = jnp.exp(s_p - m)
            l = l + jnp.sum(p_p, axis=2, keepdims=True)
            o = o + pv(p_p, heads(vp_ref, bi))
        o = o / l
        lse = m + jnp.log(l)
        lse_tile = jnp.zeros((blk, LANES), F32)
        for h in range(HEADS):
            sl = slice(h * HEAD_DIM, (h + 1) * HEAD_DIM)
            o_ref[bi, :, :, sl] = o[h].reshape(nc, rpc, HEAD_DIM).astype(o_ref.dtype)
            lse_tile = jnp.where(lane == h, lse[h], lse_tile)
        lse_ref[bi] = lse_tile.reshape(nc, rpc, LANES)


def _attn_bias(dilation, nc, back):
    rpc = ATTN_BLOCK // nc
    i = jnp.arange(ATTN_BLOCK, dtype=jnp.int32)
    off = nc * (i % rpc) + i // rpc
    dist = off[:, None] - off[None, :] + back * ATTN_BLOCK
    valid = jnp.logical_and(dist >= 0, dist <= ATTN_BLOCK)
    slopes = 2.0 ** (-8.0 * (jnp.arange(HEADS, dtype=F32) + 1.0) / HEADS)
    pen = slopes[:, None, None] * (dist * dilation).astype(F32)[None]
    return jnp.where(valid[None], -pen, NEG_BIG)


def _attention_group(qkv, qkv_cols, batch, seq, gi):
    window, dilation = ATTN_PATTERNS[gi]
    assert window // dilation == ATTN_BLOCK
    cols = qkv.shape[-1]
    per_res = seq // TOKEN_RESIDUES
    nc = TOKEN_RESIDUES // dilation
    rpc = ATTN_BLOCK // nc
    nblk = seq // dilation // ATTN_BLOCK
    with_prev = nblk > 1

    def shape5(c):
        return (batch, nc, dilation, per_res, c)

    def spec(c, col, back):
        return pl.BlockSpec(
            (ATTN_BB, nc, None, rpc, c),
            lambda b, r, n: (b, 0, r, jnp.maximum(n - back, 0), col))

    bias_spec = pl.BlockSpec((HEADS, ATTN_BLOCK, ATTN_BLOCK), lambda b, r, n: (0, 0, 0))
    in_specs = [spec(ATTN_OUT, qkv_cols[0], 0)]
    for col in qkv_cols[1:]:
        if with_prev:
            in_specs.append(spec(ATTN_OUT, col, 1))
        in_specs.append(spec(ATTN_OUT, col, 0))
    view = qkv.reshape(shape5(cols))
    args = [view] * len(in_specs) + [_attn_bias(dilation, nc, 0)]
    in_specs.append(bias_spec)
    if with_prev:
        args.append(_attn_bias(dilation, nc, 1))
        in_specs.append(bias_spec)
    o, lse = pl.pallas_call(
        functools.partial(_attn_kernel, with_prev=with_prev),
        grid=(batch // ATTN_BB, dilation, nblk),
        in_specs=in_specs,
        out_specs=[spec(ATTN_OUT, 0, 0), spec(LANES, 0, 0)],
        out_shape=[
            jax.ShapeDtypeStruct(shape5(ATTN_OUT), qkv.dtype),
            jax.ShapeDtypeStruct(shape5(LANES), F32),
        ],
        compiler_params=_params("arbitrary", "arbitrary", "arbitrary"),
        name=f"attn_g{gi}",
    )(*args)
    return o.reshape(batch * seq, ATTN_OUT), lse.reshape(batch * seq, LANES)


COMBINE_TS = 512


def _combine_kernel(o0_ref, o1_ref, o2_ref, l0_ref, l1_ref, l2_ref, out_ref):
    a0, a1, a2 = l0_ref[...], l1_ref[...], l2_ref[...]
    m = jnp.maximum(jnp.maximum(a0, a1), a2)
    e0 = jnp.exp(a0 - m)
    e1 = jnp.exp(a1 - m)
    e2 = jnp.exp(a2 - m)
    inv = 1.0 / (e0 + e1 + e2)
    w0, w1, w2 = e0 * inv, e1 * inv, e2 * inv
    for h in range(HEADS):
        sl = slice(h * HEAD_DIM, (h + 1) * HEAD_DIM)
        mix = (w0[:, h:h + 1] * o0_ref[:, sl].astype(F32)
               + w1[:, h:h + 1] * o1_ref[:, sl].astype(F32)
               + w2[:, h:h + 1] * o2_ref[:, sl].astype(F32))
        out_ref[:, sl] = mix.astype(out_ref.dtype)


def _dilation_mixture_attention(qkv0, rest, batch, seq):
    outs, lses = [], []
    for gi in range(N_GROUPS):
        if gi == 0:
            o, lse = _attention_group(qkv0, (0, 1, 2), batch, seq, gi)
        else:
            o, lse = _attention_group(rest, (gi - 1, gi + 1, gi + 3), batch, seq, gi)
        outs.append(o)
        lses.append(lse)
    m = batch * seq
    ospec = pl.BlockSpec((COMBINE_TS, ATTN_OUT), lambda i: (i, 0))
    lspec = pl.BlockSpec((COMBINE_TS, LANES), lambda i: (i, 0))
    return pl.pallas_call(
        _combine_kernel,
        grid=(m // COMBINE_TS,),
        in_specs=[ospec] * 3 + [lspec] * 3,
        out_specs=ospec,
        out_shape=jax.ShapeDtypeStruct((m, ATTN_OUT), BF16),
        compiler_params=_params("arbitrary"),
        name="attn_combine",
    )(*outs, *lses)


SSM_LT = 128
SSM_PASSES = 2
SSM_JBLK = 256


def _gelu_tanh(y):
    return 0.5 * y * (1.0 + jnp.tanh(math.sqrt(2.0 / math.pi) * (y + 0.044715 * (y * y * y))))


def _ssm_kernel(u_ref, pin_ref, pout_ref, bblk_ref, cblk_ref, are_ref, aim_ref, d_ref,
                wglu_ref, bglu_ref, o_ref, sre_ref, sim_ref, hre_ref, him_ref, *, nbatch):
    width = u_ref.shape[-1]
    rows = pin_ref.shape[1]
    njb = width // SSM_JBLK
    jstates = bblk_ref.shape[2] // 2
    pairs = rows // SUBLANES
    per_tile = SUBLANES // nbatch

    @pl.when(pl.program_id(0) == 0)
    def _():
        hre_ref[...] = jnp.zeros_like(hre_ref)
        him_ref[...] = jnp.zeros_like(him_ref)

    u_rm = u_ref[...].reshape(pout_ref.shape[1], width)
    out_rm = None
    for p in range(pin_ref.shape[0]):
        u_f32 = _dot(pin_ref[p], u_rm)
        u_tb = u_f32.astype(BF16)
        ys = []
        for j in range(njb):
            bu = _dot(u_tb[:, j * SSM_JBLK:(j + 1) * SSM_JBLK], bblk_ref[j])
            sre_ref[...] = bu[:, :jstates].reshape(pairs, SUBLANES, jstates)
            sim_ref[...] = bu[:, jstates:].reshape(pairs, SUBLANES, jstates)
            cs = slice(j * jstates, (j + 1) * jstates)
            ar = are_ref[0:nbatch, cs]
            ai = aim_ref[0:nbatch, cs]

            def step(k, carry):
                hr, hi = carry
                for t in range(per_tile):
                    rs = slice(t * nbatch, (t + 1) * nbatch)
                    nr = ar * hr - ai * hi + sre_ref[k, rs, :]
                    ni = ar * hi + ai * hr + sim_ref[k, rs, :]
                    sre_ref[k, rs, :] = nr
                    sim_ref[k, rs, :] = ni
                    hr, hi = nr, ni
                return hr, hi

            hr, hi = lax.fori_loop(0, pairs, step,
                                   (hre_ref[0:nbatch, cs], him_ref[0:nbatch, cs]))
            hre_ref[0:nbatch, cs] = hr
            him_ref[0:nbatch, cs] = hi

            h_re = sre_ref[...].reshape(rows, jstates).astype(BF16)
            h_im = sim_ref[...].reshape(rows, jstates).astype(BF16)
            ys.append(_dot(h_re, cblk_ref[j, 0:jstates, :]) + _dot(h_im, cblk_ref[j, jstates:, :]))

        y = jnp.concatenate(ys, axis=1) + d_ref[...] * u_f32
        z = _gelu_tanh(y).astype(BF16)
        g = _dot(z, wglu_ref[...]) + bglu_ref[...]
        o_tb = (g[:, :width] * _sigmoid(g[:, width:])).astype(BF16)
        back = _dot(pout_ref[p], o_tb)
        out_rm = back if out_rm is None else out_rm + back
    o_ref[...] = out_rm.astype(o_ref.dtype).reshape(o_ref.shape)


def _ssm_tables(a_re, a_im, log_dt, b_re, b_im, c_re, c_im):
    groups, nst = a_re.shape
    gpb = SSM_JBLK // SSM_GROUP_CH
    njb = groups // gpb
    lam = lax.complex(a_re.astype(F32), a_im.astype(F32))
    dt = jnp.exp(log_dt.astype(F32))[:, None]
    a_bar = jnp.exp(lam * dt)
    b_mat = lax.complex(b_re.astype(F32), b_im.astype(F32))
    b_bar = ((a_bar - 1.0) / lam)[:, :, None] * b_mat
    eye = jnp.eye(gpb, dtype=F32)

    def in_blocks(t):
        t = t.reshape(njb, gpb, nst, SSM_GROUP_CH)
        return jnp.einsum('jgnc,gh->jgchn', t, eye).reshape(njb, gpb * SSM_GROUP_CH, gpb * nst)

    def out_blocks(t):
        t = t.reshape(njb, gpb, SSM_GROUP_CH, nst)
        return jnp.einsum('jgcn,gh->jhngc', t, eye).reshape(njb, gpb * nst, gpb * SSM_GROUP_CH)

    bblk = jnp.concatenate([in_blocks(jnp.real(b_bar)), in_blocks(jnp.imag(b_bar))], axis=2)
    cblk = jnp.concatenate([out_blocks(c_re.astype(F32)), out_blocks(-c_im.astype(F32))], axis=1)
    are = jnp.broadcast_to(jnp.real(a_bar).reshape(1, groups * nst), (SUBLANES, groups * nst))
    aim = jnp.broadcast_to(jnp.imag(a_bar).reshape(1, groups * nst), (SUBLANES, groups * nst))
    return bblk.astype(BF16), cblk.astype(BF16), are, aim


def _ssm_row_perms(nbatch):
    per_res = SSM_PASSES * SSM_LT // TOKEN_RESIDUES
    col = jnp.arange(nbatch * TOKEN_RESIDUES * per_res, dtype=jnp.int32)
    b = col // (TOKEN_RESIDUES * per_res)
    t_local = TOKEN_RESIDUES * (col % per_res) + (col // per_res) % TOKEN_RESIDUES
    row = (t_local % SSM_LT) * nbatch + b
    pin = jnp.logical_and(
        (t_local // SSM_LT)[None, None, :] == jnp.arange(SSM_PASSES, dtype=jnp.int32)[:, None, None],
        row[None, None, :] == jnp.arange(SSM_LT * nbatch, dtype=jnp.int32)[None, :, None])
    pin = pin.astype(BF16)
    return pin, pin.transpose(0, 2, 1)


def _s5_ssm(src, u_col, nbatch, seq, tables, d_skip, w_glu, b_glu):
    bblk, cblk, are, aim = tables
    width = w_glu.shape[0]
    nstate = are.shape[1]
    jstates = bblk.shape[2] // 2
    rows = SSM_LT * nbatch
    step_rows = SSM_PASSES * SSM_LT // TOKEN_RESIDUES
    per_res = seq // TOKEN_RESIDUES
    pin, pout = _ssm_row_perms(nbatch)
    d2 = d_skip.reshape(1, width).astype(F32)
    wg = w_glu.astype(BF16)
    bg = b_glu.reshape(1, 2 * width).astype(F32)

    def const(a):
        return pl.BlockSpec(a.shape, lambda i: (0,) * a.ndim, pipeline_mode=pl.Buffered(1))

    blk = (nbatch, TOKEN_RESIDUES, step_rows, width)
    out = pl.pallas_call(
        functools.partial(_ssm_kernel, nbatch=nbatch),
        grid=(per_res // step_rows,),
        in_specs=[pl.BlockSpec(blk, lambda i: (0, 0, i, u_col)),
                  const(pin), const(pout), const(bblk), const(cblk), const(are), const(aim),
                  const(d2), const(wg), const(bg)],
        out_specs=pl.BlockSpec(blk, lambda i: (0, 0, i, 0)),
        out_shape=jax.ShapeDtypeStruct((nbatch, TOKEN_RESIDUES, per_res, width), BF16),
        scratch_shapes=[
            pltpu.VMEM((rows // SUBLANES, SUBLANES, jstates), F32),
            pltpu.VMEM((rows // SUBLANES, SUBLANES, jstates), F32),
            pltpu.VMEM((SUBLANES, nstate), F32),
            pltpu.VMEM((SUBLANES, nstate), F32),
        ],
        compiler_params=_params("arbitrary"),
        name="s5_ssm",
    )(src.reshape(nbatch, TOKEN_RESIDUES, per_res, src.shape[-1]),
      pin, pout, bblk, cblk, are, aim, d2, wg, bg)
    return out.reshape(nbatch * seq, width)


ROUTER_TS = 256


def _router_kernel(x_ref, g_ref, sh_ref, sc_ref, rw_ref, rb_ref,
                   sel_ref, wts_ref, cnt_ref, carry_ref):
    @pl.when(jnp.logical_and(pl.program_id(0) == 0, pl.program_id(1) == 0))
    def _():
        carry_ref[...] = jnp.zeros_like(carry_ref)

    h = _rms(x_ref[...], g_ref[...]) * (1.0 + sc_ref[...]) + sh_ref[...]
    logits = jnp.dot(h, rw_ref[...], preferred_element_type=F32,
                     precision=lax.Precision.HIGHEST) + rb_ref[...]
    ts, ne = logits.shape
    idx = lax.broadcasted_iota(jnp.int32, logits.shape, 1)
    m1 = jnp.max(logits, axis=1, keepdims=True)
    i1 = jnp.min(jnp.where(logits == m1, idx, ne), axis=1, keepdims=True)
    rest = jnp.where(idx == i1, -jnp.inf, logits)
    m2 = jnp.max(rest, axis=1, keepdims=True)
    i2 = jnp.min(jnp.where(rest == m2, idx, ne), axis=1, keepdims=True)
    e = jnp.exp(m2 - m1)
    w1 = 1.0 / (1.0 + e)
    w2 = e / (1.0 + e)

    onehot = jnp.where(idx == i1, 1.0, 0.0) + jnp.where(idx == i2, 1.0, 0.0)
    row = lax.broadcasted_iota(jnp.int32, (ts, ts), 0)
    col = lax.broadcasted_iota(jnp.int32, (ts, ts), 1)
    lower = jnp.where(col < row, 1.0, 0.0).astype(BF16)
    before = carry_ref[...] + _dot(lower, onehot.astype(BF16))
    r1 = jnp.sum(jnp.where(idx == i1, before, 0.0), axis=1, keepdims=True).astype(jnp.int32)
    r2 = jnp.sum(jnp.where(idx == i2, before, 0.0), axis=1, keepdims=True).astype(jnp.int32)
    total = carry_ref[...] + jnp.sum(onehot, axis=0, keepdims=True)
    carry_ref[...] = total
    cnt_ref[...] = total
    sel_ref[...] = jnp.where(idx == 0, i1, jnp.where(idx == 1, i2, jnp.where(
        idx == 2, r1, jnp.where(idx == 3, r2, 0))))
    wts_ref[...] = jnp.where(idx == 0, w1, jnp.where(idx == 1, w2, 0.0))


def _router(x, g, shift, scale, router_w, router_b):
    b, s, d = x.shape
    ne = router_w.shape[-1]
    vec = pl.BlockSpec((None, 1, d), lambda i, j: (i, 0, 0))
    tok = pl.BlockSpec((None, ROUTER_TS, ne), lambda i, j: (i, j, 0))
    sel, wts, cnt = pl.pallas_call(
        _router_kernel,
        grid=(b, s // ROUTER_TS),
        in_specs=[
            pl.BlockSpec((None, ROUTER_TS, d), lambda i, j: (i, j, 0)),
            pl.BlockSpec((1, d), lambda i, j: (0, 0)),
            vec, vec,
            pl.BlockSpec((d, ne), lambda i, j: (0, 0)),
            pl.BlockSpec((1, ne), lambda i, j: (0, 0)),
        ],
        out_specs=[tok, tok, pl.BlockSpec((1, ne), lambda i, j: (0, 0))],
        out_shape=[jax.ShapeDtypeStruct((b, s, ne), jnp.int32),
                   jax.ShapeDtypeStruct((b, s, ne), F32),
                   jax.ShapeDtypeStruct((1, ne), F32)],
        scratch_shapes=[pltpu.VMEM((1, ne), F32)],
        compiler_params=_params("arbitrary", "arbitrary"),
        name="router",
    )(x, g.reshape(1, d), shift.reshape(b, 1, d), scale.reshape(b, 1, d),
      router_w, router_b.reshape(1, ne))
    return sel.reshape(b * s, ne), wts.reshape(b * s, ne), cnt


MOE_TM = 256
MOE_TT = 256
MOE_UP_TN = 1408
MOE_DOWN_TN = 1024


def _moe_plan(sel, cnt, tokens):
    ne = cnt.shape[-1]
    counts = cnt[0].astype(jnp.int32)
    padded = (counts + MOE_TM - 1) // MOE_TM * MOE_TM
    ends = jnp.cumsum(padded)
    starts = ends - padded
    pos1 = starts[sel[:, 0]] + sel[:, 2]
    pos2 = starts[sel[:, 1]] + sel[:, 3]
    nt = tokens // MOE_TT
    pos = jnp.concatenate([pos1.reshape(nt, MOE_TT), pos2.reshape(nt, MOE_TT)], axis=1)
    max_tiles = 2 * tokens // MOE_TM + ne
    tile_start = jnp.arange(max_tiles, dtype=jnp.int32) * MOE_TM
    tile_expert = jnp.minimum(
        jnp.sum((tile_start[:, None] >= ends[None, :]).astype(jnp.int32), axis=1), ne - 1)
    num_tiles = (ends[-1] // MOE_TM).reshape(1)
    zero_tiles = jnp.concatenate([
        jnp.maximum(ends // MOE_TM - 1, 0),
        jnp.minimum(num_tiles[0] + jnp.arange(ne, dtype=jnp.int32), max_tiles - 1)])
    return pos.reshape(nt, 1, 2 * MOE_TT), tile_expert, num_tiles, max_tiles, zero_tiles


def _row_copies(pos_ref, t, make):
    tt = pos_ref.shape[1] // 2
    return make(0, t, pos_ref[0, t]), make(1, t, pos_ref[0, tt + t])


def _issue_and_drain(pos_ref, make):
    tt = pos_ref.shape[1] // 2

    def issue(t, carry):
        for cp in _row_copies(pos_ref, t, make):
            cp.start()
        return carry

    def drain(t, carry):
        for cp in _row_copies(pos_ref, t, make):
            cp.wait()
        return carry

    lax.fori_loop(0, tt, issue, 0, unroll=4)
    lax.fori_loop(0, tt, drain, 0, unroll=4)


def _moe_dispatch_kernel(pos_ref, zt_ref, x_ref, g_ref, sh_ref, sc_ref, xs_ref, hbuf_ref, sem):
    @pl.when(jnp.logical_and(pl.program_id(0) == 0, pl.program_id(1) == 0))
    def _():
        hbuf_ref[...] = jnp.zeros_like(hbuf_ref)
        for k in range(zt_ref.shape[0]):
            cp = pltpu.make_async_copy(
                hbuf_ref, xs_ref.at[pl.ds(pl.multiple_of(zt_ref[k] * MOE_TM, MOE_TM), MOE_TM), :],
                sem)
            cp.start()
            cp.wait()

    hbuf_ref[...] = _rms(x_ref[...], g_ref[...]) * (1.0 + sc_ref[...]) + sh_ref[...]

    def make(k, t, p):
        return pltpu.make_async_copy(hbuf_ref.at[pl.ds(t, 1), :], xs_ref.at[pl.ds(p, 1), :], sem)

    _issue_and_drain(pos_ref, make)


def _moe_dispatch(x, g, shift, scale, plan):
    pos, _, _, max_tiles, zero_tiles = plan
    b, s, d = x.shape
    per_b = s // MOE_TT
    assert MOE_TT == MOE_TM
    vec = pl.BlockSpec((None, 1, d), lambda i, j: (i, 0, 0))
    return pl.pallas_call(
        _moe_dispatch_kernel,
        grid=(b, per_b),
        in_specs=[
            pl.BlockSpec((None, 1, 2 * MOE_TT), lambda i, j: (i * per_b + j, 0, 0),
                         memory_space=pltpu.SMEM),
            pl.BlockSpec(memory_space=pltpu.SMEM),
            pl.BlockSpec((None, MOE_TT, d), lambda i, j: (i, j, 0)),
            pl.BlockSpec((1, d), lambda i, j: (0, 0)),
            vec, vec,
        ],
        out_specs=pl.BlockSpec(memory_space=pl.ANY),
        out_shape=jax.ShapeDtypeStruct((max_tiles * MOE_TM, d), F32),
        scratch_shapes=[pltpu.VMEM((MOE_TT, d), F32), pltpu.SemaphoreType.DMA(())],
        compiler_params=_params("arbitrary", "arbitrary"),
        name="moe_dispatch",
    )(pos, zero_tiles, x, g.reshape(1, d), shift.reshape(b, 1, d), scale.reshape(b, 1, d))


def _moe_mm_kernel(te_ref, nt_ref, a_ref, w_ref, *rest, mode):
    if mode == "gate":
        u_ref, o_ref, wb_ref = rest
    else:
        o_ref, wb_ref = rest
    i = pl.program_id(1)
    active = i < nt_ref[0]
    new_expert = jnp.logical_or(i == 0, te_ref[i] != te_ref[jnp.maximum(i - 1, 0)])

    @pl.when(jnp.logical_and(active, new_expert))
    def _():
        _cast_weight(w_ref, wb_ref)

    @pl.when(active)
    def _():
        if mode == "down":
            p = lax.dot_general(wb_ref[...], a_ref[...], (((0,), (0,)), ((), ())),
                                preferred_element_type=F32)
            o_ref[...] = p.T
        else:
            p = lax.dot_general(wb_ref[...], a_ref[...].astype(BF16), (((0,), (1,)), ((), ())),
                                preferred_element_type=F32)
            if mode == "silu":
                o_ref[...] = (p * _sigmoid(p)).astype(o_ref.dtype)
            else:
                o_ref[...] = (u_ref[...].astype(F32) * p).astype(o_ref.dtype)

    @pl.when(jnp.logical_not(active))
    def _():
        o_ref[...] = jnp.zeros_like(o_ref)


def _moe_mm(a, w, li, plan, tn, mode, u=None):
    _, tile_expert, num_tiles, max_tiles, _ = plan
    k, n = w.shape[-2], w.shape[-1]
    rows = max_tiles * MOE_TM

    def tile(i, nt):
        return jnp.minimum(i, nt[0] - 1)

    wspec = pl.BlockSpec((None, None, k, tn), lambda j, i, te, nt: (li, te[tile(i, nt)], 0, j))
    if mode == "down":
        in_specs = [pl.BlockSpec((k, MOE_TM), lambda j, i, te, nt: (0, tile(i, nt))), wspec]
        out_spec = pl.BlockSpec((MOE_TM, tn), lambda j, i, te, nt: (i, j))
        out_shape = jax.ShapeDtypeStruct((rows, n), F32)
    else:
        in_specs = [pl.BlockSpec((MOE_TM, k), lambda j, i, te, nt: (tile(i, nt), 0)), wspec]
        out_spec = pl.BlockSpec((tn, MOE_TM), lambda j, i, te, nt: (j, i))
        out_shape = jax.ShapeDtypeStruct((n, rows), BF16)
    args = [a, w]
    if mode == "gate":
        in_specs.append(pl.BlockSpec((tn, MOE_TM), lambda j, i, te, nt: (j, tile(i, nt))))
        args.append(u)
    return pl.pallas_call(
        functools.partial(_moe_mm_kernel, mode=mode),
        grid_spec=pltpu.PrefetchScalarGridSpec(
            num_scalar_prefetch=2,
            grid=(n // tn, max_tiles),
            in_specs=in_specs,
            out_specs=out_spec,
            scratch_shapes=[pltpu.VMEM((k, tn), BF16)],
        ),
        out_shape=out_shape,
        compiler_params=_params("arbitrary", "arbitrary"),
        name=f"moe_mm_{mode}",
    )(tile_expert, num_tiles, *args)


def _moe_combine_kernel(pos_ref, x_ref, g_ref, wts_ref, y_ref, *rest, final):
    if final:
        ng_ref, o_ref, ybuf_ref, sem = rest
    else:
        ng_ref, sh_ref, sc_ref, o_ref, h_ref, ybuf_ref, sem = rest

    def make(k, t, p):
        return pltpu.make_async_copy(y_ref.at[pl.ds(p, 1), :], ybuf_ref.at[k, pl.ds(t, 1), :], sem)

    _issue_and_drain(pos_ref, make)
    f = wts_ref[:, 0:1] * ybuf_ref[0] + wts_ref[:, 1:2] * ybuf_ref[1]
    xn = x_ref[...] + g_ref[...] * f
    yn = _rms(xn, ng_ref[...])
    if final:
        o_ref[...] = yn
    else:
        o_ref[...] = xn
        h_ref[...] = (yn * (1.0 + sc_ref[...]) + sh_ref[...]).astype(h_ref.dtype)


def _moe_combine(x, gate, wts, y, pos, seq, norm_g, shift=None, scale=None):
    m, d = x.shape
    nb = gate.shape[0]
    ne = wts.shape[-1]
    final = shift is None
    vec = pl.BlockSpec((None, 1, d), lambda i: (i * MOE_TT // seq, 0, 0))
    row = pl.BlockSpec((MOE_TT, d), lambda i: (i, 0))
    in_specs = [
        pl.BlockSpec((None, 1, 2 * MOE_TT), lambda i: (i, 0, 0), memory_space=pltpu.SMEM),
        row, vec,
        pl.BlockSpec((MOE_TT, ne), lambda i: (i, 0)),
        pl.BlockSpec(memory_space=pl.ANY),
        pl.BlockSpec((1, d), lambda i: (0, 0)),
    ]
    args = [pos, x, gate.reshape(nb, 1, d), wts, y, norm_g.reshape(1, d)]
    if final:
        out_specs, out_shape = row, jax.ShapeDtypeStruct((m, d), F32)
    else:
        in_specs += [vec, vec]
        args += [shift.reshape(nb, 1, d), scale.reshape(nb, 1, d)]
        out_specs = [row, row]
        out_shape = [jax.ShapeDtypeStruct((m, d), F32), jax.ShapeDtypeStruct((m, d), BF16)]
    return pl.pallas_call(
        functools.partial(_moe_combine_kernel, final=final),
        grid=(m // MOE_TT,),
        in_specs=in_specs,
        out_specs=out_specs,
        out_shape=out_shape,
        scratch_shapes=[pltpu.VMEM((2, MOE_TT, d), F32), pltpu.SemaphoreType.DMA(())],
        compiler_params=_params("arbitrary"),
        name="moe_combine",
    )(*args)


def kernel(x, c, ada_w, ada_b, norm_mix_g, norm_ffn_g, final_norm_g, w_in, ssm_a_re, ssm_a_im, ssm_log_dt, ssm_b_re, ssm_b_im, ssm_c_re, ssm_c_im, ssm_d, w_glu, b_glu, w_branch, w_out, ffn_w1, ffn_w3, ffn_w2, router_w, router_b, moe_w1, moe_w3, moe_w2):
    batch, seq, d = x.shape
    depth = ada_w.shape[0]
    tokens = batch * seq
    ssm_width = w_glu.shape[1]
    per_res = seq // TOKEN_RESIDUES
    tn_in = ATTN_OUT
    assert ssm_width == tn_in and w_in.shape[-1] == (3 * N_GROUPS + 1) * tn_in + 2 * d

    def to_residue_major(t):
        f = t.shape[-1]
        return t.reshape(batch, per_res, TOKEN_RESIDUES, f).transpose(0, 2, 1, 3).reshape(batch, seq, f)

    mod = _ada_modulation(c, ada_w, ada_b).reshape(depth, batch, 6, d)
    x2 = to_residue_major(x).reshape(tokens, d)
    h = None
    out = None
    for layer in range(depth):
        sh1, sc1, g1, sh2, sc2, g2 = (mod[layer, :, i] for i in range(6))

        if h is None:
            h = _norm_mod(x2.reshape(batch, seq, d), norm_mix_g[layer], sh1, sc1)
        qkv0 = _mm_plain(h, w_in, (layer,), lambda j: N_GROUPS * j, 3, F32, tm=1024, tn=tn_in)
        rest = _mm_plain(h, w_in, (layer,),
                         lambda j: j + 1 + jnp.where(j >= 2, 1, 0) + jnp.where(j >= 4, 1, 0),
                         11, BF16, tm=2048, tn=tn_in)
        attn = _dilation_mixture_attention(qkv0, rest, batch, seq)
        tables = _ssm_tables(ssm_a_re[layer], ssm_a_im[layer], ssm_log_dt[layer],
                             ssm_b_re[layer], ssm_b_im[layer], ssm_c_re[layer], ssm_c_im[layer])
        ssm = _s5_ssm(rest, 6, batch, seq, tables, ssm_d[layer], w_glu[layer], b_glu[layer])
        merged = _mm_branch(attn, ssm, w_branch, layer, rest, 7 * tn_in, tm=1024, tn=1024)
        h = None

        li = layer // 2
        if layer % 2 == 0:
            x2, hf = _mm_resid_norm(merged, w_out, layer, x2, g1, norm_ffn_g[layer], sh2, sc2,
                                    seq, tm=256)
            act = _mm_swiglu(hf, ffn_w1, ffn_w3, (li,), tm=2048, tn=512)
            x2 = _mm_resid(act, ffn_w2, (li,), x2, g2, seq, tm=512, tn=512)
        else:
            x2 = _mm_resid(merged, w_out, (layer,), x2, g1, seq, tm=1024, tn=1024)
            x3 = x2.reshape(batch, seq, d)
            sel, wts, cnt = _router(x3, norm_ffn_g[layer], sh2, sc2, router_w[li], router_b[li])
            plan = _moe_plan(sel, cnt, tokens)
            xs = _moe_dispatch(x3, norm_ffn_g[layer], sh2, sc2, plan)
            u = _moe_mm(xs, moe_w1, li, plan, MOE_UP_TN, "silu")
            act = _moe_mm(xs, moe_w3, li, plan, MOE_UP_TN, "gate", u)
            y = _moe_mm(act, moe_w2, li, plan, MOE_DOWN_TN, "down")
            if layer + 1 < depth:
                x2, h = _moe_combine(x2, g2, wts, y, plan[0], seq, norm_mix_g[layer + 1],
                                     mod[layer + 1, :, 0], mod[layer + 1, :, 1])
            else:
                out = _moe_combine(x2, g2, wts, y, plan[0], seq, final_norm_g)
    if out is None:
        out = _final_norm(x2.reshape(batch, seq, d), final_norm_g)
    return out.reshape(batch, TOKEN_RESIDUES, per_res, d).transpose(0, 2, 1, 3).reshape(batch, seq, d)
```

```python
import functools
import math

import jax
import jax.numpy as jnp
from jax import lax
from jax.experimental import pallas as pl
from jax.experimental.pallas import tpu as pltpu

F32 = jnp.float32
BF16 = jnp.bfloat16

LANES = 128
SUBLANES = 8
VMEM_LIMIT_BYTES = 56 * 1024 * 1024

ATTN_PATTERNS = ((128, 1), (512, 4), (2048, 16))
N_GROUPS = len(ATTN_PATTERNS)
HEADS = 8
HEAD_DIM = 128
ATTN_OUT = HEADS * HEAD_DIM
QKV_WIDTH = N_GROUPS * ATTN_OUT
ATTN_BLOCK = 128
SSM_GROUP_CH = 16
SSM_STATE = 64
N_EXPERTS = 8
RMS_EPS = 1e-6
NEG_BIG = -1e30


def _params(*semantics):
    return pltpu.CompilerParams(dimension_semantics=semantics,
                                vmem_limit_bytes=VMEM_LIMIT_BYTES)


def _sigmoid(v):
    return 1.0 / (1.0 + jnp.exp(-v))


ADA_KR = 256
ADA_CG = 512


def _ada_kernel(cb_ref, w_ref, b_ref, o_ref, act_ref):
    nb = cb_ref.shape[0]
    kr, n = w_ref.shape
    nj = ADA_CG // LANES
    kstep = pl.program_id(1)

    @pl.when(jnp.logical_and(pl.program_id(0) == 0, kstep == 0))
    def _():
        cb = cb_ref[...]
        act_ref[...] = cb * _sigmoid(cb)

    @pl.when(kstep == 0)
    def _():
        o_ref[...] = jnp.broadcast_to(b_ref[...], o_ref.shape)

    base = kstep * kr
    zero = jnp.zeros((SUBLANES, LANES), F32)
    for cg in range(n // ADA_CG):
        c0 = cg * ADA_CG

        def body(i, accs):
            r = pl.multiple_of(i * SUBLANES, SUBLANES)
            w = [w_ref[pl.ds(r, SUBLANES), c0 + j * LANES:c0 + (j + 1) * LANES] for j in range(nj)]
            out = []
            for b in range(nb):
                act = act_ref[b, pl.ds(pl.multiple_of(base + r, SUBLANES), SUBLANES), :]
                out.extend(accs[b * nj + j] + w[j] * act for j in range(nj))
            return tuple(out)

        accs = lax.fori_loop(0, kr // SUBLANES, body, (zero,) * (nb * nj), unroll=2)
        for b in range(nb):
            for j in range(nj):
                sl = slice(c0 + j * LANES, c0 + (j + 1) * LANES)
                o_ref[b:b + 1, sl] += jnp.sum(accs[b * nj + j], axis=0, keepdims=True)


def _ada_modulation(c, ada_w, ada_b):
    depth, k, n = ada_w.shape
    nb = c.shape[0]
    cb = jnp.broadcast_to(c[:, :, None], (nb, k, LANES))
    return pl.pallas_call(
        _ada_kernel,
        grid=(depth, k // ADA_KR),
        in_specs=[
            pl.BlockSpec((nb, k, LANES), lambda l, i: (0, 0, 0)),
            pl.BlockSpec((None, ADA_KR, n), lambda l, i: (l, i, 0)),
            pl.BlockSpec((None, 1, n), lambda l, i: (l, 0, 0)),
        ],
        out_specs=pl.BlockSpec((None, nb, n), lambda l, i: (l, 0, 0)),
        out_shape=jax.ShapeDtypeStruct((depth, nb, n), F32),
        scratch_shapes=[pltpu.VMEM((nb, k, LANES), F32)],
        compiler_params=_params("arbitrary", "arbitrary"),
        name="ada_modulation",
    )(cb, ada_w, ada_b.reshape(depth, 1, n))


NORM_TS = 512


def _rms(x, g):
    return x * lax.rsqrt(jnp.mean(x * x, axis=-1, keepdims=True) + RMS_EPS) * g


def _norm_mod_kernel(x_ref, g_ref, sh_ref, sc_ref, o_ref):
    y = _rms(x_ref[...], g_ref[...])
    o_ref[...] = (y * (1.0 + sc_ref[...]) + sh_ref[...]).astype(o_ref.dtype)


def _norm_mod(x, g, shift, scale):
    b, s, d = x.shape
    vec = pl.BlockSpec((None, 1, d), lambda i, j: (i, 0, 0))
    out = pl.pallas_call(
        _norm_mod_kernel,
        grid=(b, s // NORM_TS),
        in_specs=[
            pl.BlockSpec((None, NORM_TS, d), lambda i, j: (i, j, 0)),
            pl.BlockSpec((1, d), lambda i, j: (0, 0)),
            vec, vec,
        ],
        out_specs=pl.BlockSpec((None, NORM_TS, d), lambda i, j: (i, j, 0)),
        out_shape=jax.ShapeDtypeStruct((b, s, d), BF16),
        compiler_params=_params("arbitrary", "arbitrary"),
        name="norm_mod",
    )(x, g.reshape(1, d), shift.reshape(b, 1, d), scale.reshape(b, 1, d))
    return out.reshape(b * s, d)


def _final_norm_kernel(x_ref, g_ref, o_ref):
    o_ref[...] = _rms(x_ref[...], g_ref[...])


def _final_norm(x, g):
    b, s, d = x.shape
    return pl.pallas_call(
        _final_norm_kernel,
        grid=(b, s // NORM_TS),
        in_specs=[
            pl.BlockSpec((None, NORM_TS, d), lambda i, j: (i, j, 0)),
            pl.BlockSpec((1, d), lambda i, j: (0, 0)),
        ],
        out_specs=pl.BlockSpec((None, NORM_TS, d), lambda i, j: (i, j, 0)),
        out_shape=jax.ShapeDtypeStruct((b, s, d), F32),
        compiler_params=_params("arbitrary", "arbitrary"),
        name="final_norm",
    )(x, g.reshape(1, d))


CAST_ROWS = 256


def _cast_weight(w_ref, wb_ref):
    def body(i, carry):
        r = pl.multiple_of(i * CAST_ROWS, CAST_ROWS)
        wb_ref[pl.ds(r, CAST_ROWS), :] = w_ref[pl.ds(r, CAST_ROWS), :].astype(BF16)
        return carry
    lax.fori_loop(0, w_ref.shape[0] // CAST_ROWS, body, 0)


def _dot(a, b):
    return jnp.dot(a, b, preferred_element_type=F32)


def _mm_plain_kernel(a_ref, w_ref, o_ref, wb_ref):
    @pl.when(pl.program_id(1) == 0)
    def _():
        _cast_weight(w_ref, wb_ref)
    o_ref[...] = _dot(a_ref[...], wb_ref[...]).astype(o_ref.dtype)


def _mm_plain(a, w, w_idx, col_tile, n_tiles, out_dtype, tm, tn):
    m, k = a.shape
    lead = (None,) * len(w_idx)
    return pl.pallas_call(
        _mm_plain_kernel,
        grid=(n_tiles, m // tm),
        in_specs=[
            pl.BlockSpec((tm, k), lambda j, i: (i, 0)),
            pl.BlockSpec(lead + (k, tn), lambda j, i: w_idx + (0, col_tile(j))),
        ],
        out_specs=pl.BlockSpec((tm, tn), lambda j, i: (i, j)),
        out_shape=jax.ShapeDtypeStruct((m, n_tiles * tn), out_dtype),
        scratch_shapes=[pltpu.VMEM((k, tn), BF16)],
        compiler_params=_params("arbitrary", "arbitrary"),
        name="mm_plain",
    )(a, w)


def _mm_swiglu_kernel(a_ref, w1_ref, w3_ref, o_ref, wb1_ref, wb3_ref):
    @pl.when(pl.program_id(1) == 0)
    def _():
        _cast_weight(w1_ref, wb1_ref)
        _cast_weight(w3_ref, wb3_ref)
    a = a_ref[...]
    u = _dot(a, wb1_ref[...])
    v = _dot(a, wb3_ref[...])
    o_ref[...] = (u * _sigmoid(u) * v).astype(o_ref.dtype)


def _mm_swiglu(a, w1, w3, w_idx, tm, tn):
    m, k = a.shape
    n = w1.shape[-1]
    lead = (None,) * len(w_idx)
    wspec = pl.BlockSpec(lead + (k, tn), lambda j, i: w_idx + (0, j))
    return pl.pallas_call(
        _mm_swiglu_kernel,
        grid=(n // tn, m // tm),
        in_specs=[pl.BlockSpec((tm, k), lambda j, i: (i, 0)), wspec, wspec],
        out_specs=pl.BlockSpec((tm, tn), lambda j, i: (i, j)),
        out_shape=jax.ShapeDtypeStruct((m, n), BF16),
        scratch_shapes=[pltpu.VMEM((k, tn), BF16), pltpu.VMEM((k, tn), BF16)],
        compiler_params=_params("arbitrary", "arbitrary"),
        name="mm_swiglu",
    )(a, w1, w3)


def _mm_resid_kernel(a_ref, w_ref, x_ref, g_ref, o_ref, wb_ref):
    @pl.when(pl.program_id(1) == 0)
    def _():
        _cast_weight(w_ref, wb_ref)
    o_ref[...] = x_ref[...] + g_ref[...] * _dot(a_ref[...], wb_ref[...])


def _mm_resid(a, w, w_idx, x, gate, seq, tm, tn):
    m, k = a.shape
    n = w.shape[-1]
    nb = gate.shape[0]
    lead = (None,) * len(w_idx)
    return pl.pallas_call(
        _mm_resid_kernel,
        grid=(n // tn, m // tm),
        in_specs=[
            pl.BlockSpec((tm, k), lambda j, i: (i, 0)),
            pl.BlockSpec(lead + (k, tn), lambda j, i: w_idx + (0, j)),
            pl.BlockSpec((tm, tn), lambda j, i: (i, j)),
            pl.BlockSpec((None, 1, tn), lambda j, i: (i * tm // seq, 0, j)),
        ],
        out_specs=pl.BlockSpec((tm, tn), lambda j, i: (i, j)),
        out_shape=jax.ShapeDtypeStruct((m, n), F32),
        scratch_shapes=[pltpu.VMEM((k, tn), BF16)],
        compiler_params=_params("arbitrary", "arbitrary"),
        name="mm_resid",
    )(a, w, x, gate.reshape(nb, 1, n))


def _mm_resid_norm_kernel(a_ref, w_ref, x_ref, g_ref, ng_ref, sh_ref, sc_ref,
                          xo_ref, ho_ref, wb_ref):
    @pl.when(pl.program_id(0) == 0)
    def _():
        _cast_weight(w_ref, wb_ref)
    xn = x_ref[...] + g_ref[...] * _dot(a_ref[...], wb_ref[...])
    xo_ref[...] = xn
    yn = _rms(xn, ng_ref[...])
    ho_ref[...] = (yn * (1.0 + sc_ref[...]) + sh_ref[...]).astype(ho_ref.dtype)


def _mm_resid_norm(a, w, layer, x, gate, norm_g, shift, scale, seq, tm):
    m, k = a.shape
    n = w.shape[-1]
    nb = gate.shape[0]
    vec = pl.BlockSpec((None, 1, n), lambda i: (i * tm // seq, 0, 0))
    row = pl.BlockSpec((tm, n), lambda i: (i, 0))
    return pl.pallas_call(
        _mm_resid_norm_kernel,
        grid=(m // tm,),
        in_specs=[
            pl.BlockSpec((tm, k), lambda i: (i, 0)),
            pl.BlockSpec((None, k, n), lambda i: (layer, 0, 0), pipeline_mode=pl.Buffered(1)),
            row, vec,
            pl.BlockSpec((1, n), lambda i: (0, 0)),
            vec, vec,
        ],
        out_specs=[row, row],
        out_shape=[jax.ShapeDtypeStruct((m, n), F32), jax.ShapeDtypeStruct((m, n), BF16)],
        scratch_shapes=[pltpu.VMEM((k, n), BF16)],
        compiler_params=_params("arbitrary"),
        name="mm_resid_norm",
    )(a, w, x, gate.reshape(nb, 1, n), norm_g.reshape(1, n),
      shift.reshape(nb, 1, n), scale.reshape(nb, 1, n))


def _mm_branch_kernel(at_ref, ss_ref, wa_ref, ws_ref, ga_ref, gs_ref, o_ref,
                      wba_ref, wbs_ref):
    @pl.when(pl.program_id(1) == 0)
    def _():
        _cast_weight(wa_ref, wba_ref)
        _cast_weight(ws_ref, wbs_ref)
    pa = _dot(at_ref[...], wba_ref[...])
    ps = _dot(ss_ref[...], wbs_ref[...])
    ga = _sigmoid(ga_ref[...].astype(F32))
    gs = _sigmoid(gs_ref[...].astype(F32))
    o_ref[...] = (ga * pa + gs * ps).astype(o_ref.dtype)


def _mm_branch(attn, ssm, w_branch, layer, proj, gate_col, tm, tn):
    m, k = attn.shape
    n = w_branch.shape[-1]
    g0 = gate_col // tn
    g1 = (gate_col + n) // tn
    return pl.pallas_call(
        _mm_branch_kernel,
        grid=(n // tn, m // tm),
        in_specs=[
            pl.BlockSpec((tm, k), lambda j, i: (i, 0)),
            pl.BlockSpec((tm, k), lambda j, i: (i, 0)),
            pl.BlockSpec((None, k, tn), lambda j, i: (layer, 0, j)),
            pl.BlockSpec((None, k, tn), lambda j, i: (layer, 1, j)),
            pl.BlockSpec((tm, tn), lambda j, i: (i, g0 + j)),
            pl.BlockSpec((tm, tn), lambda j, i: (i, g1 + j)),
        ],
        out_specs=pl.BlockSpec((tm, tn), lambda j, i: (i, j)),
        out_shape=jax.ShapeDtypeStruct((m, n), BF16),
        scratch_shapes=[pltpu.VMEM((k, tn), BF16), pltpu.VMEM((k, tn), BF16)],
        compiler_params=_params("arbitrary", "arbitrary"),
        name="mm_branch",
    )(attn, ssm, w_branch, w_branch, proj, proj)


TOKEN_RESIDUES = 16
ATTN_BB = 4


def _dot_nt(a, b):
    return lax.dot_general(a, b, (((1,), (1,)), ((), ())), preferred_element_type=F32)


def _attn_kernel(*refs, with_prev):
    if with_prev:
        q_ref, kp_ref, kc_ref, vp_ref, vc_ref, bc_ref, bp_ref, o_ref, lse_ref = refs
    else:
        q_ref, kc_ref, vc_ref, bc_ref, o_ref, lse_ref = refs
    blk = ATTN_BLOCK
    bb, nc, rpc = q_ref.shape[0], q_ref.shape[1], q_ref.shape[2]
    scale = HEAD_DIM ** -0.5
    lane = lax.broadcasted_iota(jnp.int32, (blk, LANES), 1)

    def heads(ref, bi):
        x = ref[bi].reshape(blk, ATTN_OUT)
        return jnp.stack([x[:, h * HEAD_DIM:(h + 1) * HEAD_DIM] for h in range(HEADS)]
                         ).astype(BF16)

    def qk(q, k):
        return lax.dot_general(q, k, (((2,), (2,)), ((0,), (0,))), preferred_element_type=F32)

    def pv(p, v):
        return lax.dot_general(p.astype(BF16), v, (((2,), (1,)), ((0,), (0,))),
                               preferred_element_type=F32)

    for bi in range(bb):
        q = heads(q_ref, bi)
        s_c = qk(q, heads(kc_ref, bi)) * scale + bc_ref[...]
        m = jnp.max(s_c, axis=2, keepdims=True)
        if with_prev:
            bias_p = jnp.where(pl.program_id(2) > 0, bp_ref[...], NEG_BIG)
            s_p = qk(q, heads(kp_ref, bi)) * scale + bias_p
            m = jnp.maximum(m, jnp.max(s_p, axis=2, keepdims=True))
        p_c = jnp.exp(s_c - m)
        l = jnp.sum(p_c, axis=2, keepdims=True)
        o = pv(p_c, heads(vc_ref, bi))
        if with_prev:
            p_p = jnp.exp(s_p - m)
            l = l + jnp.sum(p_p, axis=2, keepdims=True)
            o = o + pv(p_p, heads(vp_ref, bi))
        o = o / l
        lse = m + jnp.log(l)
        lse_tile = jnp.zeros((blk, LANES), F32)
        for h in range(HEADS):
            sl = slice(h * HEAD_DIM, (h + 1) * HEAD_DIM)
            o_ref[bi, :, :, sl] = o[h].reshape(nc, rpc, HEAD_DIM).astype(o_ref.dtype)
            lse_tile = jnp.where(lane == h, lse[h], lse_tile)
        lse_ref[bi] = lse_tile.reshape(nc, rpc, LANES)


def _attn_bias(dilation, nc, back):
    rpc = ATTN_BLOCK // nc
    i = jnp.arange(ATTN_BLOCK, dtype=jnp.int32)
    off = nc * (i % rpc) + i // rpc
    dist = off[:, None] - off[None, :] + back * ATTN_BLOCK
    valid = jnp.logical_and(dist >= 0, dist <= ATTN_BLOCK)
    slopes = 2.0 ** (-8.0 * (jnp.arange(HEADS, dtype=F32) + 1.0) / HEADS)
    pen = slopes[:, None, None] * (dist * dilation).astype(F32)[None]
    return jnp.where(valid[None], -pen, NEG_BIG)


def _attention_group(qkv, qkv_cols, batch, seq, gi):
    window, dilation = ATTN_PATTERNS[gi]
    assert window // dilation == ATTN_BLOCK
    cols = qkv.shape[-1]
    per_res = seq // TOKEN_RESIDUES
    nc = TOKEN_RESIDUES // dilation
    rpc = ATTN_BLOCK // nc
    nblk = seq // dilation // ATTN_BLOCK
    with_prev = nblk > 1

    def shape5(c):
        return (batch, nc, dilation, per_res, c)

    def spec(c, col, back):
        return pl.BlockSpec(
            (ATTN_BB, nc, None, rpc, c),
            lambda b, r, n: (b, 0, r, jnp.maximum(n - back, 0), col))

    bias_spec = pl.BlockSpec((HEADS, ATTN_BLOCK, ATTN_BLOCK), lambda b, r, n: (0, 0, 0))
    in_specs = [spec(ATTN_OUT, qkv_cols[0], 0)]
    for col in qkv_cols[1:]:
        if with_prev:
            in_specs.append(spec(ATTN_OUT, col, 1))
        in_specs.append(spec(ATTN_OUT, col, 0))
    view = qkv.reshape(shape5(cols))
    args = [view] * len(in_specs) + [_attn_bias(dilation, nc, 0)]
    in_specs.append(bias_spec)
    if with_prev:
        args.append(_attn_bias(dilation, nc, 1))
        in_specs.append(bias_spec)
    o, lse = pl.pallas_call(
        functools.partial(_attn_kernel, with_prev=with_prev),
        grid=(batch // ATTN_BB, dilation, nblk),
        in_specs=in_specs,
        out_specs=[spec(ATTN_OUT, 0, 0), spec(LANES, 0, 0)],
        out_shape=[
            jax.ShapeDtypeStruct(shape5(ATTN_OUT), qkv.dtype),
            jax.ShapeDtypeStruct(shape5(LANES), F32),
        ],
        compiler_params=_params("arbitrary", "arbitrary", "arbitrary"),
        name=f"attn_g{gi}",
    )(*args)
    return o.reshape(batch * seq, ATTN_OUT), lse.reshape(batch * seq, LANES)


COMBINE_TS = 512


def _combine_kernel(o0_ref, o1_ref, o2_ref, l0_ref, l1_ref, l2_ref, out_ref):
    a0, a1, a2 = l0_ref[...], l1_ref[...], l2_ref[...]
    m = jnp.maximum(jnp.maximum(a0, a1), a2)
    e0 = jnp.exp(a0 - m)
    e1 = jnp.exp(a1 - m)
    e2 = jnp.exp(a2 - m)
    inv = 1.0 / (e0 + e1 + e2)
    w0, w1, w2 = e0 * inv, e1 * inv, e2 * inv
    for h in range(HEADS):
        sl = slice(h * HEAD_DIM, (h + 1) * HEAD_DIM)
        mix = (w0[:, h:h + 1] * o0_ref[:, sl].astype(F32)
               + w1[:, h:h + 1] * o1_ref[:, sl].astype(F32)
               + w2[:, h:h + 1] * o2_ref[:, sl].astype(F32))
        out_ref[:, sl] = mix.astype(out_ref.dtype)


def _dilation_mixture_attention(qkv0, rest, batch, seq):
    outs, lses = [], []
    for gi in range(N_GROUPS):
        if gi == 0:
            o, lse = _attention_group(qkv0, (0, 1, 2), batch, seq, gi)
        else:
            o, lse = _attention_group(rest, (gi - 1, gi + 1, gi + 3), batch, seq, gi)
        outs.append(o)
        lses.append(lse)
    m = batch * seq
    ospec = pl.BlockSpec((COMBINE_TS, ATTN_OUT), lambda i: (i, 0))
    lspec = pl.BlockSpec((COMBINE_TS, LANES), lambda i: (i, 0))
    return pl.pallas_call(
        _combine_kernel,
        grid=(m // COMBINE_TS,),
        in_specs=[ospec] * 3 + [lspec] * 3,
        out_specs=ospec,
        out_shape=jax.ShapeDtypeStruct((m, ATTN_OUT), BF16),
        compiler_params=_params("arbitrary"),
        name="attn_combine",
    )(*outs, *lses)


SSM_LT = 128
SSM_PASSES = 2
SSM_JBLK = 256


def _gelu_tanh(y):
    return 0.5 * y * (1.0 + jnp.tanh(math.sqrt(2.0 / math.pi) * (y + 0.044715 * (y * y * y))))


def _ssm_kernel(u_ref, pin_ref, pout_ref, bblk_ref, cblk_ref, are_ref, aim_ref, d_ref,
                wglu_ref, bglu_ref, o_ref, sre_ref, sim_ref, hre_ref, him_ref, *, nbatch):
    width = u_ref.shape[-1]
    rows = pin_ref.shape[1]
    njb = width // SSM_JBLK
    jstates = bblk_ref.shape[2] // 2
    pairs = rows // SUBLANES
    per_tile = SUBLANES // nbatch

    @pl.when(pl.program_id(0) == 0)
    def _():
        hre_ref[...] = jnp.zeros_like(hre_ref)
        him_ref[...] = jnp.zeros_like(him_ref)

    u_rm = u_ref[...].reshape(pout_ref.shape[1], width)
    out_rm = None
    for p in range(pin_ref.shape[0]):
        u_f32 = _dot(pin_ref[p], u_rm)
        u_tb = u_f32.astype(BF16)
        ys = []
        for j in range(njb):
            bu = _dot(u_tb[:, j * SSM_JBLK:(j + 1) * SSM_JBLK], bblk_ref[j])
            sre_ref[...] = bu[:, :jstates].reshape(pairs, SUBLANES, jstates)
            sim_ref[...] = bu[:, jstates:].reshape(pairs, SUBLANES, jstates)
            cs = slice(j * jstates, (j + 1) * jstates)
            ar = are_ref[0:nbatch, cs]
            ai = aim_ref[0:nbatch, cs]

            def step(k, carry):
                hr, hi = carry
                for t in range(per_tile):
                    rs = slice(t * nbatch, (t + 1) * nbatch)
                    nr = ar * hr - ai * hi + sre_ref[k, rs, :]
                    ni = ar * hi + ai * hr + sim_ref[k, rs, :]
                    sre_ref[k, rs, :] = nr
                    sim_ref[k, rs, :] = ni
                    hr, hi = nr, ni
                return hr, hi

            hr, hi = lax.fori_loop(0, pairs, step,
                                   (hre_ref[0:nbatch, cs], him_ref[0:nbatch, cs]), unroll=2)
            hre_ref[0:nbatch, cs] = hr
            him_ref[0:nbatch, cs] = hi

            h_re = sre_ref[...].reshape(rows, jstates).astype(BF16)
            h_im = sim_ref[...].reshape(rows, jstates).astype(BF16)
            ys.append(_dot(h_re, cblk_ref[j, 0:jstates, :]) + _dot(h_im, cblk_ref[j, jstates:, :]))

        y = jnp.concatenate(ys, axis=1) + d_ref[...] * u_f32
        z = _gelu_tanh(y).astype(BF16)
        g = _dot(z, wglu_ref[...]) + bglu_ref[...]
        o_tb = (g[:, :width] * _sigmoid(g[:, width:])).astype(BF16)
        back = _dot(pout_ref[p], o_tb)
        out_rm = back if out_rm is None else out_rm + back
    o_ref[...] = out_rm.astype(o_ref.dtype).reshape(o_ref.shape)


def _ssm_tables(a_re, a_im, log_dt, b_re, b_im, c_re, c_im):
    groups, nst = a_re.shape
    gpb = SSM_JBLK // SSM_GROUP_CH
    njb = groups // gpb
    lam = lax.complex(a_re.astype(F32), a_im.astype(F32))
    dt = jnp.exp(log_dt.astype(F32))[:, None]
    a_bar = jnp.exp(lam * dt)
    b_mat = lax.complex(b_re.astype(F32), b_im.astype(F32))
    b_bar = ((a_bar - 1.0) / lam)[:, :, None] * b_mat
    eye = jnp.eye(gpb, dtype=F32)

    def in_blocks(t):
        t = t.reshape(njb, gpb, nst, SSM_GROUP_CH)
        return jnp.einsum('jgnc,gh->jgchn', t, eye).reshape(njb, gpb * SSM_GROUP_CH, gpb * nst)

    def out_blocks(t):
        t = t.reshape(njb, gpb, SSM_GROUP_CH, nst)
        return jnp.einsum('jgcn,gh->jhngc', t, eye).reshape(njb, gpb * nst, gpb * SSM_GROUP_CH)

    bblk = jnp.concatenate([in_blocks(jnp.real(b_bar)), in_blocks(jnp.imag(b_bar))], axis=2)
    cblk = jnp.concatenate([out_blocks(c_re.astype(F32)), out_blocks(-c_im.astype(F32))], axis=1)
    are = jnp.broadcast_to(jnp.real(a_bar).reshape(1, groups * nst), (SUBLANES, groups * nst))
    aim = jnp.broadcast_to(jnp.imag(a_bar).reshape(1, groups * nst), (SUBLANES, groups * nst))
    return bblk.astype(BF16), cblk.astype(BF16), are, aim


def _ssm_row_perms(nbatch):
    per_res = SSM_PASSES * SSM_LT // TOKEN_RESIDUES
    col = jnp.arange(nbatch * TOKEN_RESIDUES * per_res, dtype=jnp.int32)
    b = col // (TOKEN_RESIDUES * per_res)
    t_local = TOKEN_RESIDUES * (col % per_res) + (col // per_res) % TOKEN_RESIDUES
    row = (t_local % SSM_LT) * nbatch + b
    pin = jnp.logical_and(
        (t_local // SSM_LT)[None, None, :] == jnp.arange(SSM_PASSES, dtype=jnp.int32)[:, None, None],
        row[None, None, :] == jnp.arange(SSM_LT * nbatch, dtype=jnp.int32)[None, :, None])
    pin = pin.astype(BF16)
    return pin, pin.transpose(0, 2, 1)


def _s5_ssm(src, u_col, nbatch, seq, tables, d_skip, w_glu, b_glu):
    bblk, cblk, are, aim = tables
    width = w_glu.shape[0]
    nstate = are.shape[1]
    jstates = bblk.shape[2] // 2
    rows = SSM_LT * nbatch
    step_rows = SSM_PASSES * SSM_LT // TOKEN_RESIDUES
    per_res = seq // TOKEN_RESIDUES
    pin, pout = _ssm_row_perms(nbatch)
    d2 = d_skip.reshape(1, width).astype(F32)
    wg = w_glu.astype(BF16)
    bg = b_glu.reshape(1, 2 * width).astype(F32)

    def const(a):
        return pl.BlockSpec(a.shape, lambda i: (0,) * a.ndim, pipeline_mode=pl.Buffered(1))

    blk = (nbatch, TOKEN_RESIDUES, step_rows, width)
    out = pl.pallas_call(
        functools.partial(_ssm_kernel, nbatch=nbatch),
        grid=(per_res // step_rows,),
        in_specs=[pl.BlockSpec(blk, lambda i: (0, 0, i, u_col)),
                  const(pin), const(pout), const(bblk), const(cblk), const(are), const(aim),
                  const(d2), const(wg), const(bg)],
        out_specs=pl.BlockSpec(blk, lambda i: (0, 0, i, 0)),
        out_shape=jax.ShapeDtypeStruct((nbatch, TOKEN_RESIDUES, per_res, width), BF16),
        scratch_shapes=[
            pltpu.VMEM((rows // SUBLANES, SUBLANES, jstates), F32),
            pltpu.VMEM((rows // SUBLANES, SUBLANES, jstates), F32),
            pltpu.VMEM((SUBLANES, nstate), F32),
            pltpu.VMEM((SUBLANES, nstate), F32),
        ],
        compiler_params=_params("arbitrary"),
        name="s5_ssm",
    )(src.reshape(nbatch, TOKEN_RESIDUES, per_res, src.shape[-1]),
      pin, pout, bblk, cblk, are, aim, d2, wg, bg)
    return out.reshape(nbatch * seq, width)


ROUTER_TS = 256


def _router_kernel(x_ref, g_ref, sh_ref, sc_ref, rw_ref, rb_ref,
                   sel_ref, wts_ref, cnt_ref, carry_ref):
    @pl.when(jnp.logical_and(pl.program_id(0) == 0, pl.program_id(1) == 0))
    def _():
        carry_ref[...] = jnp.zeros_like(carry_ref)

    h = _rms(x_ref[...], g_ref[...]) * (1.0 + sc_ref[...]) + sh_ref[...]
    logits = jnp.dot(h, rw_ref[...], preferred_element_type=F32,
                     precision=lax.Precision.HIGHEST) + rb_ref[...]
    ts, ne = logits.shape
    idx = lax.broadcasted_iota(jnp.int32, logits.shape, 1)
    m1 = jnp.max(logits, axis=1, keepdims=True)
    i1 = jnp.min(jnp.where(logits == m1, idx, ne), axis=1, keepdims=True)
    rest = jnp.where(idx == i1, -jnp.inf, logits)
    m2 = jnp.max(rest, axis=1, keepdims=True)
    i2 = jnp.min(jnp.where(rest == m2, idx, ne), axis=1, keepdims=True)
    e = jnp.exp(m2 - m1)
    w1 = 1.0 / (1.0 + e)
    w2 = e / (1.0 + e)

    onehot = jnp.where(idx == i1, 1.0, 0.0) + jnp.where(idx == i2, 1.0, 0.0)
    row = lax.broadcasted_iota(jnp.int32, (ts, ts), 0)
    col = lax.broadcasted_iota(jnp.int32, (ts, ts), 1)
    lower = jnp.where(col < row, 1.0, 0.0).astype(BF16)
    before = carry_ref[...] + _dot(lower, onehot.astype(BF16))
    r1 = jnp.sum(jnp.where(idx == i1, before, 0.0), axis=1, keepdims=True).astype(jnp.int32)
    r2 = jnp.sum(jnp.where(idx == i2, before, 0.0), axis=1, keepdims=True).astype(jnp.int32)
    total = carry_ref[...] + jnp.sum(onehot, axis=0, keepdims=True)
    carry_ref[...] = total
    cnt_ref[...] = total
    sel_ref[...] = jnp.where(idx == 0, i1, jnp.where(idx == 1, i2, jnp.where(
        idx == 2, r1, jnp.where(idx == 3, r2, 0))))
    wts_ref[...] = jnp.where(idx == 0, w1, jnp.where(idx == 1, w2, 0.0))


def _router(x, g, shift, scale, router_w, router_b):
    b, s, d = x.shape
    ne = router_w.shape[-1]
    vec = pl.BlockSpec((None, 1, d), lambda i, j: (i, 0, 0))
    tok = pl.BlockSpec((None, ROUTER_TS, ne), lambda i, j: (i, j, 0))
    sel, wts, cnt = pl.pallas_call(
        _router_kernel,
        grid=(b, s // ROUTER_TS),
        in_specs=[
            pl.BlockSpec((None, ROUTER_TS, d), lambda i, j: (i, j, 0)),
            pl.BlockSpec((1, d), lambda i, j: (0, 0)),
            vec, vec,
            pl.BlockSpec((d, ne), lambda i, j: (0, 0)),
            pl.BlockSpec((1, ne), lambda i, j: (0, 0)),
        ],
        out_specs=[tok, tok, pl.BlockSpec((1, ne), lambda i, j: (0, 0))],
        out_shape=[jax.ShapeDtypeStruct((b, s, ne), jnp.int32),
                   jax.ShapeDtypeStruct((b, s, ne), F32),
                   jax.ShapeDtypeStruct((1, ne), F32)],
        scratch_shapes=[pltpu.VMEM((1, ne), F32)],
        compiler_params=_params("arbitrary", "arbitrary"),
        name="router",
    )(x, g.reshape(1, d), shift.reshape(b, 1, d), scale.reshape(b, 1, d),
      router_w, router_b.reshape(1, ne))
    return sel.reshape(b * s, ne), wts.reshape(b * s, ne), cnt


MOE_TM = 256
MOE_TT = 256
MOE_UP_TN = 1408
MOE_DOWN_TN = 1024


def _moe_plan(sel, cnt, tokens):
    ne = cnt.shape[-1]
    counts = cnt[0].astype(jnp.int32)
    padded = (counts + MOE_TM - 1) // MOE_TM * MOE_TM
    ends = jnp.cumsum(padded)
    starts = ends - padded
    pos1 = starts[sel[:, 0]] + sel[:, 2]
    pos2 = starts[sel[:, 1]] + sel[:, 3]
    nt = tokens // MOE_TT
    pos = jnp.concatenate([pos1.reshape(nt, MOE_TT), pos2.reshape(nt, MOE_TT)], axis=1)
    max_tiles = 2 * tokens // MOE_TM + ne
    tile_start = jnp.arange(max_tiles, dtype=jnp.int32) * MOE_TM
    tile_expert = jnp.minimum(
        jnp.sum((tile_start[:, None] >= ends[None, :]).astype(jnp.int32), axis=1), ne - 1)
    num_tiles = (ends[-1] // MOE_TM).reshape(1)
    zero_tiles = jnp.concatenate([
        jnp.maximum(ends // MOE_TM - 1, 0),
        jnp.minimum(num_tiles[0] + jnp.arange(ne, dtype=jnp.int32), max_tiles - 1)])
    return pos.reshape(nt, 1, 2 * MOE_TT), tile_expert, num_tiles, max_tiles, zero_tiles


def _row_copies(pos_ref, t, make):
    tt = pos_ref.shape[1] // 2
    return make(0, t, pos_ref[0, t]), make(1, t, pos_ref[0, tt + t])


def _issue_and_drain(pos_ref, make):
    tt = pos_ref.shape[1] // 2

    def issue(t, carry):
        for cp in _row_copies(pos_ref, t, make):
            cp.start()
        return carry

    def drain(t, carry):
        for cp in _row_copies(pos_ref, t, make):
            cp.wait()
        return carry

    lax.fori_loop(0, tt, issue, 0, unroll=4)
    lax.fori_loop(0, tt, drain, 0, unroll=4)


def _pack_pairs(lo, hi):
    lo_bits = pltpu.bitcast(lo.astype(BF16).astype(F32), jnp.uint32)
    hi_bits = pltpu.bitcast(hi.astype(BF16).astype(F32), jnp.uint32)
    return jnp.bitwise_or(jnp.bitwise_and(hi_bits, jnp.uint32(0xFFFF0000)),
                          jnp.right_shift(lo_bits, jnp.uint32(16)))


def _unpack_pairs(words):
    lo = pltpu.bitcast(jnp.left_shift(words, jnp.uint32(16)), F32)
    hi = pltpu.bitcast(jnp.bitwise_and(words, jnp.uint32(0xFFFF0000)), F32)
    return lo, hi


def _moe_dispatch_kernel(pos_ref, zt_ref, x_ref, g_ref, sh_ref, sc_ref, xs_ref, hbuf_ref, sem):
    @pl.when(jnp.logical_and(pl.program_id(0) == 0, pl.program_id(1) == 0))
    def _():
        hbuf_ref[...] = jnp.zeros_like(hbuf_ref)
        for k in range(zt_ref.shape[0]):
            cp = pltpu.make_async_copy(
                hbuf_ref, xs_ref.at[pl.ds(pl.multiple_of(zt_ref[k] * MOE_TM, MOE_TM), MOE_TM), :],
                sem)
            cp.start()
            cp.wait()

    h = _rms(x_ref[...], g_ref[...]) * (1.0 + sc_ref[...]) + sh_ref[...]
    half = h.shape[1] // 2
    hbuf_ref[...] = _pack_pairs(h[:, :half], h[:, half:])

    def make(k, t, p):
        return pltpu.make_async_copy(hbuf_ref.at[pl.ds(t, 1), :], xs_ref.at[pl.ds(p, 1), :], sem)

    _issue_and_drain(pos_ref, make)


def _moe_dispatch(x, g, shift, scale, plan):
    pos, _, _, max_tiles, zero_tiles = plan
    b, s, d = x.shape
    per_b = s // MOE_TT
    assert MOE_TT == MOE_TM
    vec = pl.BlockSpec((None, 1, d), lambda i, j: (i, 0, 0))
    return pl.pallas_call(
        _moe_dispatch_kernel,
        grid=(b, per_b),
        in_specs=[
            pl.BlockSpec((None, 1, 2 * MOE_TT), lambda i, j: (i * per_b + j, 0, 0),
                         memory_space=pltpu.SMEM),
            pl.BlockSpec(memory_space=pltpu.SMEM),
            pl.BlockSpec((None, MOE_TT, d), lambda i, j: (i, j, 0)),
            pl.BlockSpec((1, d), lambda i, j: (0, 0)),
            vec, vec,
        ],
        out_specs=pl.BlockSpec(memory_space=pl.ANY),
        out_shape=jax.ShapeDtypeStruct((max_tiles * MOE_TM, d // 2), jnp.uint32),
        scratch_shapes=[pltpu.VMEM((MOE_TT, d // 2), jnp.uint32), pltpu.SemaphoreType.DMA(())],
        compiler_params=_params("arbitrary", "arbitrary"),
        name="moe_dispatch",
    )(pos, zero_tiles, x, g.reshape(1, d), shift.reshape(b, 1, d), scale.reshape(b, 1, d))


def _moe_mm_kernel(te_ref, nt_ref, a_ref, w_ref, *rest, mode):
    if mode == "gate":
        u_ref, o_ref, wb_ref = rest
    else:
        o_ref, wb_ref = rest
    i = pl.program_id(1)
    active = i < nt_ref[0]
    new_expert = jnp.logical_or(i == 0, te_ref[i] != te_ref[jnp.maximum(i - 1, 0)])

    @pl.when(jnp.logical_and(active, new_expert))
    def _():
        _cast_weight(w_ref, wb_ref)

    @pl.when(active)
    def _():
        if mode == "down":
            p = lax.dot_general(wb_ref[...], a_ref[...], (((0,), (0,)), ((), ())),
                                preferred_element_type=F32).T
            half = p.shape[1] // 2
            o_ref[...] = _pack_pairs(p[:, :half], p[:, half:])
        else:
            lo, hi = _unpack_pairs(a_ref[...])
            half = lo.shape[1]
            nt_dims = (((0,), (1,)), ((), ()))
            p = (lax.dot_general(wb_ref[0:half, :], lo.astype(BF16), nt_dims,
                                 preferred_element_type=F32)
                 + lax.dot_general(wb_ref[half:, :], hi.astype(BF16), nt_dims,
                                   preferred_element_type=F32))
            if mode == "silu":
                o_ref[...] = (p * _sigmoid(p)).astype(o_ref.dtype)
            else:
                o_ref[...] = (u_ref[...].astype(F32) * p).astype(o_ref.dtype)

    @pl.when(jnp.logical_not(active))
    def _():
        o_ref[...] = jnp.zeros_like(o_ref)


def _moe_mm(a, w, li, plan, tn, mode, u=None):
    _, tile_expert, num_tiles, max_tiles, _ = plan
    k, n = w.shape[-2], w.shape[-1]
    rows = max_tiles * MOE_TM

    def tile(i, nt):
        return jnp.minimum(i, nt[0] - 1)

    wspec = pl.BlockSpec((None, None, k, tn), lambda j, i, te, nt: (li, te[tile(i, nt)], 0, j))
    if mode == "down":
        in_specs = [pl.BlockSpec((k, MOE_TM), lambda j, i, te, nt: (0, tile(i, nt))), wspec]
        out_spec = pl.BlockSpec((MOE_TM, tn // 2), lambda j, i, te, nt: (i, j))
        out_shape = jax.ShapeDtypeStruct((rows, n // 2), jnp.uint32)
    else:
        in_specs = [pl.BlockSpec((MOE_TM, k // 2), lambda j, i, te, nt: (tile(i, nt), 0)), wspec]
        out_spec = pl.BlockSpec((tn, MOE_TM), lambda j, i, te, nt: (j, i))
        out_shape = jax.ShapeDtypeStruct((n, rows), BF16)
    args = [a, w]
    if mode == "gate":
        in_specs.append(pl.BlockSpec((tn, MOE_TM), lambda j, i, te, nt: (j, tile(i, nt))))
        args.append(u)
    return pl.pallas_call(
        functools.partial(_moe_mm_kernel, mode=mode),
        grid_spec=pltpu.PrefetchScalarGridSpec(
            num_scalar_prefetch=2,
            grid=(n // tn, max_tiles),
            in_specs=in_specs,
            out_specs=out_spec,
            scratch_shapes=[pltpu.VMEM((k, tn), BF16)],
        ),
        out_shape=out_shape,
        compiler_params=_params("arbitrary", "arbitrary"),
        name=f"moe_mm_{mode}",
    )(tile_expert, num_tiles, *args)


def _moe_combine_kernel(pos_ref, x_ref, g_ref, wts_ref, y_ref, *rest, final):
    if final:
        ng_ref, o_ref, ybuf_ref, sem = rest
    else:
        ng_ref, sh_ref, sc_ref, o_ref, h_ref, ybuf_ref, sem = rest

    def make(k, t, p):
        return pltpu.make_async_copy(y_ref.at[pl.ds(p, 1), :], ybuf_ref.at[k, pl.ds(t, 1), :], sem)

    _issue_and_drain(pos_ref, make)

    def rows(k):
        lo, hi = _unpack_pairs(ybuf_ref[k])
        hw = MOE_DOWN_TN // 2
        parts = []
        for c in range(lo.shape[1] // hw):
            parts += [lo[:, c * hw:(c + 1) * hw], hi[:, c * hw:(c + 1) * hw]]
        return jnp.concatenate(parts, axis=1)

    f = wts_ref[:, 0:1] * rows(0) + wts_ref[:, 1:2] * rows(1)
    xn = x_ref[...] + g_ref[...] * f
    yn = _rms(xn, ng_ref[...])
    if final:
        o_ref[...] = yn
    else:
        o_ref[...] = xn
        h_ref[...] = (yn * (1.0 + sc_ref[...]) + sh_ref[...]).astype(h_ref.dtype)


def _moe_combine(x, gate, wts, y, pos, seq, norm_g, shift=None, scale=None):
    m, d = x.shape
    nb = gate.shape[0]
    ne = wts.shape[-1]
    final = shift is None
    vec = pl.BlockSpec((None, 1, d), lambda i: (i * MOE_TT // seq, 0, 0))
    row = pl.BlockSpec((MOE_TT, d), lambda i: (i, 0))
    in_specs = [
        pl.BlockSpec((None, 1, 2 * MOE_TT), lambda i: (i, 0, 0), memory_space=pltpu.SMEM),
        row, vec,
        pl.BlockSpec((MOE_TT, ne), lambda i: (i, 0)),
        pl.BlockSpec(memory_space=pl.ANY),
        pl.BlockSpec((1, d), lambda i: (0, 0)),
    ]
    args = [pos, x, gate.reshape(nb, 1, d), wts, y, norm_g.reshape(1, d)]
    if final:
        out_specs, out_shape = row, jax.ShapeDtypeStruct((m, d), F32)
    else:
        in_specs += [vec, vec]
        args += [shift.reshape(nb, 1, d), scale.reshape(nb, 1, d)]
        out_specs = [row, row]
        out_shape = [jax.ShapeDtypeStruct((m, d), F32), jax.ShapeDtypeStruct((m, d), BF16)]
    return pl.pallas_call(
        functools.partial(_moe_combine_kernel, final=final),
        grid=(m // MOE_TT,),
        in_specs=in_specs,
        out_specs=out_specs,
        out_shape=out_shape,
        scratch_shapes=[pltpu.VMEM((2, MOE_TT, d // 2), jnp.uint32), pltpu.SemaphoreType.DMA(())],
        compiler_params=_params("arbitrary"),
        name="moe_combine",
    )(*args)


def kernel(x, c, ada_w, ada_b, norm_mix_g, norm_ffn_g, final_norm_g, w_in, ssm_a_re, ssm_a_im, ssm_log_dt, ssm_b_re, ssm_b_im, ssm_c_re, ssm_c_im, ssm_d, w_glu, b_glu, w_branch, w_out, ffn_w1, ffn_w3, ffn_w2, router_w, router_b, moe_w1, moe_w3, moe_w2):
    batch, seq, d = x.shape
    depth = ada_w.shape[0]
    tokens = batch * seq
    ssm_width = w_glu.shape[1]
    per_res = seq // TOKEN_RESIDUES
    tn_in = ATTN_OUT
    assert ssm_width == tn_in and w_in.shape[-1] == (3 * N_GROUPS + 1) * tn_in + 2 * d

    def to_residue_major(t):
        f = t.shape[-1]
        return t.reshape(batch, per_res, TOKEN_RESIDUES, f).transpose(0, 2, 1, 3).reshape(batch, seq, f)

    mod = _ada_modulation(c, ada_w, ada_b).reshape(depth, batch, 6, d)
    x2 = to_residue_major(x).reshape(tokens, d)
    h = None
    out = None
    for layer in range(depth):
        sh1, sc1, g1, sh2, sc2, g2 = (mod[layer, :, i] for i in range(6))

        if h is None:
            h = _norm_mod(x2.reshape(batch, seq, d), norm_mix_g[layer], sh1, sc1)
        qkv0 = _mm_plain(h, w_in, (layer,), lambda j: N_GROUPS * j, 3, F32, tm=1024, tn=tn_in)
        rest = _mm_plain(h, w_in, (layer,),
                         lambda j: j + 1 + jnp.where(j >= 2, 1, 0) + jnp.where(j >= 4, 1, 0),
                         11, BF16, tm=2048, tn=tn_in)
        attn = _dilation_mixture_attention(qkv0, rest, batch, seq)
        tables = _ssm_tables(ssm_a_re[layer], ssm_a_im[layer], ssm_log_dt[layer],
                             ssm_b_re[layer], ssm_b_im[layer], ssm_c_re[layer], ssm_c_im[layer])
        ssm = _s5_ssm(rest, 6, batch, seq, tables, ssm_d[layer], w_glu[layer], b_glu[layer])
        merged = _mm_branch(attn, ssm, w_branch, layer, rest, 7 * tn_in, tm=1024, tn=1024)
        h = None

        li = layer // 2
        if layer % 2 == 0:
            x2, hf = _mm_resid_norm(merged, w_out, layer, x2, g1, norm_ffn_g[layer], sh2, sc2,
                                    seq, tm=256)
            act = _mm_swiglu(hf, ffn_w1, ffn_w3, (li,), tm=2048, tn=512)
            x2 = _mm_resid(act, ffn_w2, (li,), x2, g2, seq, tm=512, tn=512)
        else:
            x2 = _mm_resid(merged, w_out, (layer,), x2, g1, seq, tm=1024, tn=1024)
            x3 = x2.reshape(batch, seq, d)
            sel, wts, cnt = _router(x3, norm_ffn_g[layer], sh2, sc2, router_w[li], router_b[li])
            plan = _moe_plan(sel, cnt, tokens)
            xs = _moe_dispatch(x3, norm_ffn_g[layer], sh2, sc2, plan)
            u = _moe_mm(xs, moe_w1, li, plan, MOE_UP_TN, "silu")
            act = _moe_mm(xs, moe_w3, li, plan, MOE_UP_TN, "gate", u)
            y = _moe_mm(act, moe_w2, li, plan, MOE_DOWN_TN, "down")
            if layer + 1 < depth:
                x2, h = _moe_combine(x2, g2, wts, y, plan[0], seq, norm_mix_g[layer + 1],
                                     mod[layer + 1, :, 0], mod[layer + 1, :, 1])
            else:
                out = _moe_combine(x2, g2, wts, y, plan[0], seq, final_norm_g)
    if out is None:
        out = _final_norm(x2.reshape(batch, seq, d), final_norm_g)
    return out.reshape(batch, TOKEN_RESIDUES, per_res, d).transpose(0, 2, 1, 3).reshape(batch, seq, d)
```

```python
import functools
import math

import jax
import jax.numpy as jnp
from jax import lax
from jax.experimental import pallas as pl
from jax.experimental.pallas import tpu as pltpu

F32 = jnp.float32
BF16 = jnp.bfloat16

LANES = 128
SUBLANES = 8
VMEM_LIMIT_BYTES = 56 * 1024 * 1024

ATTN_PATTERNS = ((128, 1), (512, 4), (2048, 16))
N_GROUPS = len(ATTN_PATTERNS)
HEADS = 8
HEAD_DIM = 128
ATTN_OUT = HEADS * HEAD_DIM
QKV_WIDTH = N_GROUPS * ATTN_OUT
ATTN_BLOCK = 128
SSM_GROUP_CH = 16
SSM_STATE = 64
N_EXPERTS = 8
RMS_EPS = 1e-6
NEG_BIG = -1e30


def _params(*semantics):
    return pltpu.CompilerParams(dimension_semantics=semantics,
                                vmem_limit_bytes=VMEM_LIMIT_BYTES)


def _sigmoid(v):
    return 1.0 / (1.0 + jnp.exp(-v))


ADA_KR = 256
ADA_CG = 512


def _ada_kernel(cb_ref, w_ref, b_ref, o_ref, act_ref):
    nb = cb_ref.shape[0]
    kr, n = w_ref.shape
    nj = ADA_CG // LANES
    kstep = pl.program_id(1)

    @pl.when(jnp.logical_and(pl.program_id(0) == 0, kstep == 0))
    def _():
        cb = cb_ref[...]
        act_ref[...] = cb * _sigmoid(cb)

    @pl.when(kstep == 0)
    def _():
        o_ref[...] = jnp.broadcast_to(b_ref[...], o_ref.shape)

    base = kstep * kr
    zero = jnp.zeros((SUBLANES, LANES), F32)
    for cg in range(n // ADA_CG):
        c0 = cg * ADA_CG

        def body(i, accs):
            r = pl.multiple_of(i * SUBLANES, SUBLANES)
            w = [w_ref[pl.ds(r, SUBLANES), c0 + j * LANES:c0 + (j + 1) * LANES] for j in range(nj)]
            out = []
            for b in range(nb):
                act = act_ref[b, pl.ds(pl.multiple_of(base + r, SUBLANES), SUBLANES), :]
                out.extend(accs[b * nj + j] + w[j] * act for j in range(nj))
            return tuple(out)

        accs = lax.fori_loop(0, kr // SUBLANES, body, (zero,) * (nb * nj), unroll=2)
        for b in range(nb):
            for j in range(nj):
                sl = slice(c0 + j * LANES, c0 + (j + 1) * LANES)
                o_ref[b:b + 1, sl] += jnp.sum(accs[b * nj + j], axis=0, keepdims=True)


def _ada_modulation(c, ada_w, ada_b):
    depth, k, n = ada_w.shape
    nb = c.shape[0]
    cb = jnp.broadcast_to(c[:, :, None], (nb, k, LANES))
    return pl.pallas_call(
        _ada_kernel,
        grid=(depth, k // ADA_KR),
        in_specs=[
            pl.BlockSpec((nb, k, LANES), lambda l, i: (0, 0, 0)),
            pl.BlockSpec((None, ADA_KR, n), lambda l, i: (l, i, 0)),
            pl.BlockSpec((None, 1, n), lambda l, i: (l, 0, 0)),
        ],
        out_specs=pl.BlockSpec((None, nb, n), lambda l, i: (l, 0, 0)),
        out_shape=jax.ShapeDtypeStruct((depth, nb, n), F32),
        scratch_shapes=[pltpu.VMEM((nb, k, LANES), F32)],
        compiler_params=_params("arbitrary", "arbitrary"),
        name="ada_modulation",
    )(cb, ada_w, ada_b.reshape(depth, 1, n))


NORM_TS = 512


def _rms(x, g):
    return x * lax.rsqrt(jnp.mean(x * x, axis=-1, keepdims=True) + RMS_EPS) * g


def _norm_mod_kernel(x_ref, g_ref, sh_ref, sc_ref, o_ref):
    y = _rms(x_ref[...], g_ref[...])
    o_ref[...] = (y * (1.0 + sc_ref[...]) + sh_ref[...]).astype(o_ref.dtype)


def _norm_mod(x, g, shift, scale):
    b, s, d = x.shape
    vec = pl.BlockSpec((None, 1, d), lambda i, j: (i, 0, 0))
    out = pl.pallas_call(
        _norm_mod_kernel,
        grid=(b, s // NORM_TS),
        in_specs=[
            pl.BlockSpec((None, NORM_TS, d), lambda i, j: (i, j, 0)),
            pl.BlockSpec((1, d), lambda i, j: (0, 0)),
            vec, vec,
        ],
        out_specs=pl.BlockSpec((None, NORM_TS, d), lambda i, j: (i, j, 0)),
        out_shape=jax.ShapeDtypeStruct((b, s, d), BF16),
        compiler_params=_params("arbitrary", "arbitrary"),
        name="norm_mod",
    )(x, g.reshape(1, d), shift.reshape(b, 1, d), scale.reshape(b, 1, d))
    return out.reshape(b * s, d)


def _final_norm_kernel(x_ref, g_ref, o_ref):
    o_ref[...] = _rms(x_ref[...], g_ref[...])


def _final_norm(x, g):
    b, s, d = x.shape
    return pl.pallas_call(
        _final_norm_kernel,
        grid=(b, s // NORM_TS),
        in_specs=[
            pl.BlockSpec((None, NORM_TS, d), lambda i, j: (i, j, 0)),
            pl.BlockSpec((1, d), lambda i, j: (0, 0)),
        ],
        out_specs=pl.BlockSpec((None, NORM_TS, d), lambda i, j: (i, j, 0)),
        out_shape=jax.ShapeDtypeStruct((b, s, d), F32),
        compiler_params=_params("arbitrary", "arbitrary"),
        name="final_norm",
    )(x, g.reshape(1, d))


CAST_ROWS = 256


def _cast_weight(w_ref, wb_ref):
    def body(i, carry):
        r = pl.multiple_of(i * CAST_ROWS, CAST_ROWS)
        wb_ref[pl.ds(r, CAST_ROWS), :] = w_ref[pl.ds(r, CAST_ROWS), :].astype(BF16)
        return carry
    lax.fori_loop(0, w_ref.shape[0] // CAST_ROWS, body, 0)


def _dot(a, b):
    return jnp.dot(a, b, preferred_element_type=F32)


def _mm_plain_kernel(a_ref, w_ref, o_ref, wb_ref):
    @pl.when(pl.program_id(1) == 0)
    def _():
        _cast_weight(w_ref, wb_ref)
    o_ref[...] = _dot(a_ref[...], wb_ref[...]).astype(o_ref.dtype)


def _mm_plain(a, w, w_idx, col_tile, n_tiles, out_dtype, tm, tn):
    m, k = a.shape
    lead = (None,) * len(w_idx)
    return pl.pallas_call(
        _mm_plain_kernel,
        grid=(n_tiles, m // tm),
        in_specs=[
            pl.BlockSpec((tm, k), lambda j, i: (i, 0)),
            pl.BlockSpec(lead + (k, tn), lambda j, i: w_idx + (0, col_tile(j))),
        ],
        out_specs=pl.BlockSpec((tm, tn), lambda j, i: (i, j)),
        out_shape=jax.ShapeDtypeStruct((m, n_tiles * tn), out_dtype),
        scratch_shapes=[pltpu.VMEM((k, tn), BF16)],
        compiler_params=_params("arbitrary", "arbitrary"),
        name="mm_plain",
    )(a, w)


def _mm_swiglu_kernel(a_ref, w1_ref, w3_ref, o_ref, wb1_ref, wb3_ref):
    @pl.when(pl.program_id(1) == 0)
    def _():
        _cast_weight(w1_ref, wb1_ref)
        _cast_weight(w3_ref, wb3_ref)
    a = a_ref[...]
    u = _dot(a, wb1_ref[...])
    v = _dot(a, wb3_ref[...])
    o_ref[...] = (u * _sigmoid(u) * v).astype(o_ref.dtype)


def _mm_swiglu(a, w1, w3, w_idx, tm, tn):
    m, k = a.shape
    n = w1.shape[-1]
    lead = (None,) * len(w_idx)
    wspec = pl.BlockSpec(lead + (k, tn), lambda j, i: w_idx + (0, j))
    return pl.pallas_call(
        _mm_swiglu_kernel,
        grid=(n // tn, m // tm),
        in_specs=[pl.BlockSpec((tm, k), lambda j, i: (i, 0)), wspec, wspec],
        out_specs=pl.BlockSpec((tm, tn), lambda j, i: (i, j)),
        out_shape=jax.ShapeDtypeStruct((m, n), BF16),
        scratch_shapes=[pltpu.VMEM((k, tn), BF16), pltpu.VMEM((k, tn), BF16)],
        compiler_params=_params("arbitrary", "arbitrary"),
        name="mm_swiglu",
    )(a, w1, w3)


def _mm_resid_kernel(a_ref, w_ref, x_ref, g_ref, o_ref, wb_ref):
    @pl.when(pl.program_id(1) == 0)
    def _():
        _cast_weight(w_ref, wb_ref)
    o_ref[...] = x_ref[...] + g_ref[...] * _dot(a_ref[...], wb_ref[...])


def _mm_resid(a, w, w_idx, x, gate, seq, tm, tn):
    m, k = a.shape
    n = w.shape[-1]
    nb = gate.shape[0]
    lead = (None,) * len(w_idx)
    return pl.pallas_call(
        _mm_resid_kernel,
        grid=(n // tn, m // tm),
        in_specs=[
            pl.BlockSpec((tm, k), lambda j, i: (i, 0)),
            pl.BlockSpec(lead + (k, tn), lambda j, i: w_idx + (0, j)),
            pl.BlockSpec((tm, tn), lambda j, i: (i, j)),
            pl.BlockSpec((None, 1, tn), lambda j, i: (i * tm // seq, 0, j)),
        ],
        out_specs=pl.BlockSpec((tm, tn), lambda j, i: (i, j)),
        out_shape=jax.ShapeDtypeStruct((m, n), F32),
        scratch_shapes=[pltpu.VMEM((k, tn), BF16)],
        compiler_params=_params("arbitrary", "arbitrary"),
        name="mm_resid",
    )(a, w, x, gate.reshape(nb, 1, n))


def _mm_resid_norm_kernel(a_ref, w_ref, x_ref, g_ref, ng_ref, sh_ref, sc_ref,
                          xo_ref, ho_ref, wb_ref):
    @pl.when(pl.program_id(0) == 0)
    def _():
        _cast_weight(w_ref, wb_ref)
    xn = x_ref[...] + g_ref[...] * _dot(a_ref[...], wb_ref[...])
    xo_ref[...] = xn
    yn = _rms(xn, ng_ref[...])
    ho_ref[...] = (yn * (1.0 + sc_ref[...]) + sh_ref[...]).astype(ho_ref.dtype)


def _mm_resid_norm(a, w, layer, x, gate, norm_g, shift, scale, seq, tm):
    m, k = a.shape
    n = w.shape[-1]
    nb = gate.shape[0]
    vec = pl.BlockSpec((None, 1, n), lambda i: (i * tm // seq, 0, 0))
    row = pl.BlockSpec((tm, n), lambda i: (i, 0))
    return pl.pallas_call(
        _mm_resid_norm_kernel,
        grid=(m // tm,),
        in_specs=[
            pl.BlockSpec((tm, k), lambda i: (i, 0)),
            pl.BlockSpec((None, k, n), lambda i: (layer, 0, 0), pipeline_mode=pl.Buffered(1)),
            row, vec,
            pl.BlockSpec((1, n), lambda i: (0, 0)),
            vec, vec,
        ],
        out_specs=[row, row],
        out_shape=[jax.ShapeDtypeStruct((m, n), F32), jax.ShapeDtypeStruct((m, n), BF16)],
        scratch_shapes=[pltpu.VMEM((k, n), BF16)],
        compiler_params=_params("arbitrary"),
        name="mm_resid_norm",
    )(a, w, x, gate.reshape(nb, 1, n), norm_g.reshape(1, n),
      shift.reshape(nb, 1, n), scale.reshape(nb, 1, n))


def _mm_branch_kernel(at_ref, ss_ref, wa_ref, ws_ref, ga_ref, gs_ref, o_ref,
                      wba_ref, wbs_ref):
    @pl.when(pl.program_id(1) == 0)
    def _():
        _cast_weight(wa_ref, wba_ref)
        _cast_weight(ws_ref, wbs_ref)
    pa = _dot(at_ref[...], wba_ref[...])
    ps = _dot(ss_ref[...], wbs_ref[...])
    ga = _sigmoid(ga_ref[...].astype(F32))
    gs = _sigmoid(gs_ref[...].astype(F32))
    o_ref[...] = (ga * pa + gs * ps).astype(o_ref.dtype)


def _mm_branch(attn, ssm, w_branch, layer, proj, gate_col, tm, tn):
    m, k = attn.shape
    n = w_branch.shape[-1]
    g0 = gate_col // tn
    g1 = (gate_col + n) // tn
    return pl.pallas_call(
        _mm_branch_kernel,
        grid=(n // tn, m // tm),
        in_specs=[
            pl.BlockSpec((tm, k), lambda j, i: (i, 0)),
            pl.BlockSpec((tm, k), lambda j, i: (i, 0)),
            pl.BlockSpec((None, k, tn), lambda j, i: (layer, 0, j)),
            pl.BlockSpec((None, k, tn), lambda j, i: (layer, 1, j)),
            pl.BlockSpec((tm, tn), lambda j, i: (i, g0 + j)),
            pl.BlockSpec((tm, tn), lambda j, i: (i, g1 + j)),
        ],
        out_specs=pl.BlockSpec((tm, tn), lambda j, i: (i, j)),
        out_shape=jax.ShapeDtypeStruct((m, n), BF16),
        scratch_shapes=[pltpu.VMEM((k, tn), BF16), pltpu.VMEM((k, tn), BF16)],
        compiler_params=_params("arbitrary", "arbitrary"),
        name="mm_branch",
    )(attn, ssm, w_branch, w_branch, proj, proj)


TOKEN_RESIDUES = 16
ATTN_BB = 4


def _dot_nt(a, b):
    return lax.dot_general(a, b, (((1,), (1,)), ((), ())), preferred_element_type=F32)


def _attn_kernel(*refs, with_prev):
    if with_prev:
        q_ref, kp_ref, kc_ref, vp_ref, vc_ref, bc_ref, bp_ref, o_ref, lse_ref = refs
    else:
        q_ref, kc_ref, vc_ref, bc_ref, o_ref, lse_ref = refs
    blk = ATTN_BLOCK
    bb, nc, rpc = q_ref.shape[0], q_ref.shape[1], q_ref.shape[2]
    scale = HEAD_DIM ** -0.5
    lane = lax.broadcasted_iota(jnp.int32, (blk, LANES), 1)

    def heads(ref, bi):
        x = ref[bi].reshape(blk, ATTN_OUT)
        return jnp.stack([x[:, h * HEAD_DIM:(h + 1) * HEAD_DIM] for h in range(HEADS)]
                         ).astype(BF16)

    def qk(q, k):
        return lax.dot_general(q, k, (((2,), (2,)), ((0,), (0,))), preferred_element_type=F32)

    def pv(p, v):
        return lax.dot_general(p.astype(BF16), v, (((2,), (1,)), ((0,), (0,))),
                               preferred_element_type=F32)

    for bi in range(bb):
        q = heads(q_ref, bi)
        s_c = qk(q, heads(kc_ref, bi)) * scale + bc_ref[...]
        m = jnp.max(s_c, axis=2, keepdims=True)
        if with_prev:
            bias_p = jnp.where(pl.program_id(2) > 0, bp_ref[...], NEG_BIG)
            s_p = qk(q, heads(kp_ref, bi)) * scale + bias_p
            m = jnp.maximum(m, jnp.max(s_p, axis=2, keepdims=True))
        p_c = jnp.exp(s_c - m)
        l = jnp.sum(p_c, axis=2, keepdims=True)
        o = pv(p_c, heads(vc_ref, bi))
        if with_prev:
            p_p = jnp.exp(s_p - m)
            l = l + jnp.sum(p_p, axis=2, keepdims=True)
            o = o + pv(p_p, heads(vp_ref, bi))
        o = o / l
        lse = m + jnp.log(l)
        lse_tile = jnp.zeros((blk, LANES), F32)
        for h in range(HEADS):
            sl = slice(h * HEAD_DIM, (h + 1) * HEAD_DIM)
            o_ref[bi, :, :, sl] = o[h].reshape(nc, rpc, HEAD_DIM).astype(o_ref.dtype)
            lse_tile = jnp.where(lane == h, lse[h], lse_tile)
        lse_ref[bi] = lse_tile.reshape(nc, rpc, LANES)


def _attn_bias(dilation, nc, back):
    rpc = ATTN_BLOCK // nc
    i = jnp.arange(ATTN_BLOCK, dtype=jnp.int32)
    off = nc * (i % rpc) + i // rpc
    dist = off[:, None] - off[None, :] + back * ATTN_BLOCK
    valid = jnp.logical_and(dist >= 0, dist <= ATTN_BLOCK)
    slopes = 2.0 ** (-8.0 * (jnp.arange(HEADS, dtype=F32) + 1.0) / HEADS)
    pen = slopes[:, None, None] * (dist * dilation).astype(F32)[None]
    return jnp.where(valid[None], -pen, NEG_BIG)


def _attention_group(qkv, qkv_cols, batch, seq, gi):
    window, dilation = ATTN_PATTERNS[gi]
    assert window // dilation == ATTN_BLOCK
    cols = qkv.shape[-1]
    per_res = seq // TOKEN_RESIDUES
    nc = TOKEN_RESIDUES // dilation
    rpc = ATTN_BLOCK // nc
    nblk = seq // dilation // ATTN_BLOCK
    with_prev = nblk > 1

    def shape5(c):
        return (batch, nc, dilation, per_res, c)

    def spec(c, col, back):
        return pl.BlockSpec(
            (ATTN_BB, nc, None, rpc, c),
            lambda b, r, n: (b, 0, r, jnp.maximum(n - back, 0), col))

    bias_spec = pl.BlockSpec((HEADS, ATTN_BLOCK, ATTN_BLOCK), lambda b, r, n: (0, 0, 0))
    in_specs = [spec(ATTN_OUT, qkv_cols[0], 0)]
    for col in qkv_cols[1:]:
        if with_prev:
            in_specs.append(spec(ATTN_OUT, col, 1))
        in_specs.append(spec(ATTN_OUT, col, 0))
    view = qkv.reshape(shape5(cols))
    args = [view] * len(in_specs) + [_attn_bias(dilation, nc, 0)]
    in_specs.append(bias_spec)
    if with_prev:
        args.append(_attn_bias(dilation, nc, 1))
        in_specs.append(bias_spec)
    o, lse = pl.pallas_call(
        functools.partial(_attn_kernel, with_prev=with_prev),
        grid=(batch // ATTN_BB, dilation, nblk),
        in_specs=in_specs,
        out_specs=[spec(ATTN_OUT, 0, 0), spec(LANES, 0, 0)],
        out_shape=[
            jax.ShapeDtypeStruct(shape5(ATTN_OUT), qkv.dtype),
            jax.ShapeDtypeStruct(shape5(LANES), F32),
        ],
        compiler_params=_params("arbitrary", "arbitrary", "arbitrary"),
        name=f"attn_g{gi}",
    )(*args)
    return o.reshape(batch * seq, ATTN_OUT), lse.reshape(batch * seq, LANES)


COMBINE_TS = 512


def _combine_kernel(o0_ref, o1_ref, o2_ref, l0_ref, l1_ref, l2_ref, out_ref):
    a0, a1, a2 = l0_ref[...], l1_ref[...], l2_ref[...]
    m = jnp.maximum(jnp.maximum(a0, a1), a2)
    e0 = jnp.exp(a0 - m)
    e1 = jnp.exp(a1 - m)
    e2 = jnp.exp(a2 - m)
    inv = 1.0 / (e0 + e1 + e2)
    w0, w1, w2 = e0 * inv, e1 * inv, e2 * inv
    for h in range(HEADS):
        sl = slice(h * HEAD_DIM, (h + 1) * HEAD_DIM)
        mix = (w0[:, h:h + 1] * o0_ref[:, sl].astype(F32)
               + w1[:, h:h + 1] * o1_ref[:, sl].astype(F32)
               + w2[:, h:h + 1] * o2_ref[:, sl].astype(F32))
        out_ref[:, sl] = mix.astype(out_ref.dtype)


def _dilation_mixture_attention(qkv0, rest, batch, seq):
    outs, lses = [], []
    for gi in range(N_GROUPS):
        if gi == 0:
            o, lse = _attention_group(qkv0, (0, 1, 2), batch, seq, gi)
        else:
            o, lse = _attention_group(rest, (gi - 1, gi + 1, gi + 3), batch, seq, gi)
        outs.append(o)
        lses.append(lse)
    m = batch * seq
    ospec = pl.BlockSpec((COMBINE_TS, ATTN_OUT), lambda i: (i, 0))
    lspec = pl.BlockSpec((COMBINE_TS, LANES), lambda i: (i, 0))
    return pl.pallas_call(
        _combine_kernel,
        grid=(m // COMBINE_TS,),
        in_specs=[ospec] * 3 + [lspec] * 3,
        out_specs=ospec,
        out_shape=jax.ShapeDtypeStruct((m, ATTN_OUT), BF16),
        compiler_params=_params("arbitrary"),
        name="attn_combine",
    )(*outs, *lses)


SSM_LT = 128
SSM_PASSES = 2
SSM_JBLK = 256


def _gelu_tanh(y):
    return 0.5 * y * (1.0 + jnp.tanh(math.sqrt(2.0 / math.pi) * (y + 0.044715 * (y * y * y))))


def _ssm_kernel(u_ref, pin_ref, pout_ref, bblk_ref, cblk_ref, are_ref, aim_ref, d_ref,
                wglu_ref, bglu_ref, o_ref, sre_ref, sim_ref, hre_ref, him_ref, *, nbatch):
    width = u_ref.shape[-1]
    rows = pin_ref.shape[1]
    njb = width // SSM_JBLK
    jstates = bblk_ref.shape[2] // 2
    pairs = rows // SUBLANES
    per_tile = SUBLANES // nbatch

    @pl.when(pl.program_id(0) == 0)
    def _():
        hre_ref[...] = jnp.zeros_like(hre_ref)
        him_ref[...] = jnp.zeros_like(him_ref)

    u_rm = u_ref[...].reshape(pout_ref.shape[1], width)
    out_rm = None
    for p in range(pin_ref.shape[0]):
        u_f32 = _dot(pin_ref[p], u_rm)
        u_tb = u_f32.astype(BF16)
        ys = []
        for j in range(njb):
            bu = _dot(u_tb[:, j * SSM_JBLK:(j + 1) * SSM_JBLK], bblk_ref[j])
            sre_ref[...] = bu[:, :jstates].reshape(pairs, SUBLANES, jstates)
            sim_ref[...] = bu[:, jstates:].reshape(pairs, SUBLANES, jstates)
            cs = slice(j * jstates, (j + 1) * jstates)
            ar = are_ref[0:nbatch, cs]
            ai = aim_ref[0:nbatch, cs]

            def step(k, carry):
                hr, hi = carry
                for t in range(per_tile):
                    rs = slice(t * nbatch, (t + 1) * nbatch)
                    nr = ar * hr - ai * hi + sre_ref[k, rs, :]
                    ni = ar * hi + ai * hr + sim_ref[k, rs, :]
                    sre_ref[k, rs, :] = nr
                    sim_ref[k, rs, :] = ni
                    hr, hi = nr, ni
                return hr, hi

            hr, hi = lax.fori_loop(0, pairs, step,
                                   (hre_ref[0:nbatch, cs], him_ref[0:nbatch, cs]), unroll=2)
            hre_ref[0:nbatch, cs] = hr
            him_ref[0:nbatch, cs] = hi

            h_re = sre_ref[...].reshape(rows, jstates).astype(BF16)
            h_im = sim_ref[...].reshape(rows, jstates).astype(BF16)
            ys.append(_dot(h_re, cblk_ref[j, 0:jstates, :]) + _dot(h_im, cblk_ref[j, jstates:, :]))

        y = jnp.concatenate(ys, axis=1) + d_ref[...] * u_f32
        z = _gelu_tanh(y).astype(BF16)
        g = _dot(z, wglu_ref[...]) + bglu_ref[...]
        o_tb = (g[:, :width] * _sigmoid(g[:, width:])).astype(BF16)
        back = _dot(pout_ref[p], o_tb)
        out_rm = back if out_rm is None else out_rm + back
    o_ref[...] = out_rm.astype(o_ref.dtype).reshape(o_ref.shape)


def _ssm_tables(a_re, a_im, log_dt, b_re, b_im, c_re, c_im):
    groups, nst = a_re.shape
    gpb = SSM_JBLK // SSM_GROUP_CH
    njb = groups // gpb
    lam = lax.complex(a_re.astype(F32), a_im.astype(F32))
    dt = jnp.exp(log_dt.astype(F32))[:, None]
    a_bar = jnp.exp(lam * dt)
    b_mat = lax.complex(b_re.astype(F32), b_im.astype(F32))
    b_bar = ((a_bar - 1.0) / lam)[:, :, None] * b_mat
    eye = jnp.eye(gpb, dtype=F32)

    def in_blocks(t):
        t = t.reshape(njb, gpb, nst, SSM_GROUP_CH)
        return jnp.einsum('jgnc,gh->jgchn', t, eye).reshape(njb, gpb * SSM_GROUP_CH, gpb * nst)

    def out_blocks(t):
        t = t.reshape(njb, gpb, SSM_GROUP_CH, nst)
        return jnp.einsum('jgcn,gh->jhngc', t, eye).reshape(njb, gpb * nst, gpb * SSM_GROUP_CH)

    bblk = jnp.concatenate([in_blocks(jnp.real(b_bar)), in_blocks(jnp.imag(b_bar))], axis=2)
    cblk = jnp.concatenate([out_blocks(c_re.astype(F32)), out_blocks(-c_im.astype(F32))], axis=1)
    are = jnp.broadcast_to(jnp.real(a_bar).reshape(1, groups * nst), (SUBLANES, groups * nst))
    aim = jnp.broadcast_to(jnp.imag(a_bar).reshape(1, groups * nst), (SUBLANES, groups * nst))
    return bblk.astype(BF16), cblk.astype(BF16), are, aim


def _ssm_row_perms(nbatch):
    per_res = SSM_PASSES * SSM_LT // TOKEN_RESIDUES
    col = jnp.arange(nbatch * TOKEN_RESIDUES * per_res, dtype=jnp.int32)
    b = col // (TOKEN_RESIDUES * per_res)
    t_local = TOKEN_RESIDUES * (col % per_res) + (col // per_res) % TOKEN_RESIDUES
    row = (t_local % SSM_LT) * nbatch + b
    pin = jnp.logical_and(
        (t_local // SSM_LT)[None, None, :] == jnp.arange(SSM_PASSES, dtype=jnp.int32)[:, None, None],
        row[None, None, :] == jnp.arange(SSM_LT * nbatch, dtype=jnp.int32)[None, :, None])
    pin = pin.astype(BF16)
    return pin, pin.transpose(0, 2, 1)


def _s5_ssm(src, u_col, nbatch, seq, tables, d_skip, w_glu, b_glu):
    bblk, cblk, are, aim = tables
    width = w_glu.shape[0]
    nstate = are.shape[1]
    jstates = bblk.shape[2] // 2
    rows = SSM_LT * nbatch
    step_rows = SSM_PASSES * SSM_LT // TOKEN_RESIDUES
    per_res = seq // TOKEN_RESIDUES
    pin, pout = _ssm_row_perms(nbatch)
    d2 = d_skip.reshape(1, width).astype(F32)
    wg = w_glu.astype(BF16)
    bg = b_glu.reshape(1, 2 * width).astype(F32)

    def const(a):
        return pl.BlockSpec(a.shape, lambda i: (0,) * a.ndim, pipeline_mode=pl.Buffered(1))

    blk = (nbatch, TOKEN_RESIDUES, step_rows, width)
    out = pl.pallas_call(
        functools.partial(_ssm_kernel, nbatch=nbatch),
        grid=(per_res // step_rows,),
        in_specs=[pl.BlockSpec(blk, lambda i: (0, 0, i, u_col)),
                  const(pin), const(pout), const(bblk), const(cblk), const(are), const(aim),
                  const(d2), const(wg), const(bg)],
        out_specs=pl.BlockSpec(blk, lambda i: (0, 0, i, 0)),
        out_shape=jax.ShapeDtypeStruct((nbatch, TOKEN_RESIDUES, per_res, width), BF16),
        scratch_shapes=[
            pltpu.VMEM((rows // SUBLANES, SUBLANES, jstates), F32),
            pltpu.VMEM((rows // SUBLANES, SUBLANES, jstates), F32),
            pltpu.VMEM((SUBLANES, nstate), F32),
            pltpu.VMEM((SUBLANES, nstate), F32),
        ],
        compiler_params=_params("arbitrary"),
        name="s5_ssm",
    )(src.reshape(nbatch, TOKEN_RESIDUES, per_res, src.shape[-1]),
      pin, pout, bblk, cblk, are, aim, d2, wg, bg)
    return out.reshape(nbatch * seq, width)


ROUTER_TS = 256


def _router_kernel(x_ref, g_ref, sh_ref, sc_ref, rw_ref, rb_ref,
                   sel_ref, wts_ref, cnt_ref, carry_ref):
    @pl.when(jnp.logical_and(pl.program_id(0) == 0, pl.program_id(1) == 0))
    def _():
        carry_ref[...] = jnp.zeros_like(carry_ref)

    h = _rms(x_ref[...], g_ref[...]) * (1.0 + sc_ref[...]) + sh_ref[...]
    ts, ne = h.shape[0], rw_ref.shape[0]
    idx = lax.broadcasted_iota(jnp.int32, (ts, ne), 1)
    logits = jnp.broadcast_to(rb_ref[...], (ts, ne))
    for ex in range(ne):
        col = jnp.sum(h * rw_ref[ex:ex + 1, :], axis=1, keepdims=True)
        logits = logits + jnp.where(idx == ex, col, 0.0)
    m1 = jnp.max(logits, axis=1, keepdims=True)
    i1 = jnp.min(jnp.where(logits == m1, idx, ne), axis=1, keepdims=True)
    rest = jnp.where(idx == i1, -jnp.inf, logits)
    m2 = jnp.max(rest, axis=1, keepdims=True)
    i2 = jnp.min(jnp.where(rest == m2, idx, ne), axis=1, keepdims=True)
    e = jnp.exp(m2 - m1)
    w1 = 1.0 / (1.0 + e)
    w2 = e / (1.0 + e)

    onehot = jnp.where(idx == i1, 1.0, 0.0) + jnp.where(idx == i2, 1.0, 0.0)
    row = lax.broadcasted_iota(jnp.int32, (ts, ts), 0)
    col = lax.broadcasted_iota(jnp.int32, (ts, ts), 1)
    lower = jnp.where(col < row, 1.0, 0.0).astype(BF16)
    before = carry_ref[...] + _dot(lower, onehot.astype(BF16))
    r1 = jnp.sum(jnp.where(idx == i1, before, 0.0), axis=1, keepdims=True).astype(jnp.int32)
    r2 = jnp.sum(jnp.where(idx == i2, before, 0.0), axis=1, keepdims=True).astype(jnp.int32)
    total = carry_ref[...] + jnp.sum(onehot, axis=0, keepdims=True)
    carry_ref[...] = total
    cnt_ref[...] = total
    sel_ref[...] = jnp.where(idx == 0, i1, jnp.where(idx == 1, i2, jnp.where(
        idx == 2, r1, jnp.where(idx == 3, r2, 0))))
    wts_ref[...] = jnp.where(idx == 0, w1, jnp.where(idx == 1, w2, 0.0))


def _router(x, g, shift, scale, router_w, router_b):
    b, s, d = x.shape
    ne = router_w.shape[-1]
    vec = pl.BlockSpec((None, 1, d), lambda i, j: (i, 0, 0))
    tok = pl.BlockSpec((None, ROUTER_TS, ne), lambda i, j: (i, j, 0))
    sel, wts, cnt = pl.pallas_call(
        _router_kernel,
        grid=(b, s // ROUTER_TS),
        in_specs=[
            pl.BlockSpec((None, ROUTER_TS, d), lambda i, j: (i, j, 0)),
            pl.BlockSpec((1, d), lambda i, j: (0, 0)),
            vec, vec,
            pl.BlockSpec((ne, d), lambda i, j: (0, 0)),
            pl.BlockSpec((1, ne), lambda i, j: (0, 0)),
        ],
        out_specs=[tok, tok, pl.BlockSpec((1, ne), lambda i, j: (0, 0))],
        out_shape=[jax.ShapeDtypeStruct((b, s, ne), jnp.int32),
                   jax.ShapeDtypeStruct((b, s, ne), F32),
                   jax.ShapeDtypeStruct((1, ne), F32)],
        scratch_shapes=[pltpu.VMEM((1, ne), F32)],
        compiler_params=_params("arbitrary", "arbitrary"),
        name="router",
    )(x, g.reshape(1, d), shift.reshape(b, 1, d), scale.reshape(b, 1, d),
      router_w.T, router_b.reshape(1, ne))
    return sel.reshape(b * s, ne), wts.reshape(b * s, ne), cnt


MOE_TM = 256
MOE_TT = 256
MOE_UP_TN = 1408
MOE_DOWN_TN = 1024


def _moe_plan(sel, cnt, tokens):
    ne = cnt.shape[-1]
    counts = cnt[0].astype(jnp.int32)
    padded = (counts + MOE_TM - 1) // MOE_TM * MOE_TM
    ends = jnp.cumsum(padded)
    starts = ends - padded
    pos1 = starts[sel[:, 0]] + sel[:, 2]
    pos2 = starts[sel[:, 1]] + sel[:, 3]
    nt = tokens // MOE_TT
    pos = jnp.concatenate([pos1.reshape(nt, MOE_TT), pos2.reshape(nt, MOE_TT)], axis=1)
    max_tiles = 2 * tokens // MOE_TM + ne
    tile_start = jnp.arange(max_tiles, dtype=jnp.int32) * MOE_TM
    tile_expert = jnp.minimum(
        jnp.sum((tile_start[:, None] >= ends[None, :]).astype(jnp.int32), axis=1), ne - 1)
    num_tiles = (ends[-1] // MOE_TM).reshape(1)
    zero_tiles = jnp.concatenate([
        jnp.maximum(ends // MOE_TM - 1, 0),
        jnp.minimum(num_tiles[0] + jnp.arange(ne, dtype=jnp.int32), max_tiles - 1)])
    nonempty = counts > 0
    ar = jnp.arange(ne, dtype=jnp.int32)
    order = (ar[:, None] + 1 + ar[None, :]) % ne
    nxt = order[ar, jnp.argmax(nonempty[order], axis=1)].astype(jnp.int32)
    gidx = jnp.cumsum(nonempty.astype(jnp.int32)) - 1
    groups = jnp.concatenate([nxt, gidx, jnp.sum(nonempty.astype(jnp.int32)).reshape(1)])
    return (pos.reshape(nt, 1, 2 * MOE_TT), tile_expert, num_tiles, max_tiles, zero_tiles,
            groups)


def _row_copies(pos_ref, t, make):
    tt = pos_ref.shape[1] // 2
    return make(0, t, pos_ref[0, t]), make(1, t, pos_ref[0, tt + t])


def _issue_and_drain(pos_ref, make):
    tt = pos_ref.shape[1] // 2

    def issue(t, carry):
        for k, cp in enumerate(_row_copies(pos_ref, t, make)):
            cp.start(priority=k)
        return carry

    def drain(t, carry):
        for cp in _row_copies(pos_ref, t, make):
            cp.wait()
        return carry

    lax.fori_loop(0, tt, issue, 0, unroll=4)
    lax.fori_loop(0, tt, drain, 0, unroll=4)


def _pack_pairs(lo, hi):
    lo_bits = pltpu.bitcast(lo.astype(BF16).astype(F32), jnp.uint32)
    hi_bits = pltpu.bitcast(hi.astype(BF16).astype(F32), jnp.uint32)
    return jnp.bitwise_or(jnp.bitwise_and(hi_bits, jnp.uint32(0xFFFF0000)),
                          jnp.right_shift(lo_bits, jnp.uint32(16)))


def _unpack_pairs(words):
    lo = pltpu.bitcast(jnp.left_shift(words, jnp.uint32(16)), F32)
    hi = pltpu.bitcast(jnp.bitwise_and(words, jnp.uint32(0xFFFF0000)), F32)
    return lo, hi


def _moe_dispatch_kernel(pos_ref, zt_ref, x_ref, g_ref, sh_ref, sc_ref, xs_ref, hbuf_ref, sem):
    @pl.when(jnp.logical_and(pl.program_id(0) == 0, pl.program_id(1) == 0))
    def _():
        hbuf_ref[...] = jnp.zeros_like(hbuf_ref)
        for k in range(zt_ref.shape[0]):
            cp = pltpu.make_async_copy(
                hbuf_ref, xs_ref.at[pl.ds(pl.multiple_of(zt_ref[k] * MOE_TM, MOE_TM), MOE_TM), :],
                sem)
            cp.start()
            cp.wait()

    h = _rms(x_ref[...], g_ref[...]) * (1.0 + sc_ref[...]) + sh_ref[...]
    half = h.shape[1] // 2
    hbuf_ref[...] = _pack_pairs(h[:, :half], h[:, half:])

    def make(k, t, p):
        return pltpu.make_async_copy(hbuf_ref.at[pl.ds(t, 1), :], xs_ref.at[pl.ds(p, 1), :], sem)

    _issue_and_drain(pos_ref, make)


def _moe_dispatch(x, g, shift, scale, plan):
    pos, _, _, max_tiles, zero_tiles, _ = plan
    b, s, d = x.shape
    per_b = s // MOE_TT
    assert MOE_TT == MOE_TM
    vec = pl.BlockSpec((None, 1, d), lambda i, j: (i, 0, 0))
    return pl.pallas_call(
        _moe_dispatch_kernel,
        grid=(b, per_b),
        in_specs=[
            pl.BlockSpec((None, 1, 2 * MOE_TT), lambda i, j: (i * per_b + j, 0, 0),
                         memory_space=pltpu.SMEM),
            pl.BlockSpec(memory_space=pltpu.SMEM),
            pl.BlockSpec((None, MOE_TT, d), lambda i, j: (i, j, 0)),
            pl.BlockSpec((1, d), lambda i, j: (0, 0)),
            vec, vec,
        ],
        out_specs=pl.BlockSpec(memory_space=pl.ANY),
        out_shape=jax.ShapeDtypeStruct((max_tiles * MOE_TM, d // 2), jnp.uint32),
        scratch_shapes=[pltpu.VMEM((MOE_TT, d // 2), jnp.uint32), pltpu.SemaphoreType.DMA(())],
        compiler_params=_params("arbitrary", "arbitrary"),
        name="moe_dispatch",
    )(pos, zero_tiles, x, g.reshape(1, d), shift.reshape(b, 1, d), scale.reshape(b, 1, d))


def _moe_mm_kernel(te_ref, nt_ref, grp_ref, a_ref, w_hbm, *rest, mode, li, tn):
    if mode == "gate":
        u_ref, o_ref, wb_ref, wbuf_ref, wsem = rest
    else:
        o_ref, wb_ref, wbuf_ref, wsem = rest
    j = pl.program_id(0)
    i = pl.program_id(1)
    ne = (grp_ref.shape[0] - 1) // 2
    expert = te_ref[i]
    active = i < nt_ref[0]
    new_expert = jnp.logical_or(i == 0, expert != te_ref[jnp.maximum(i - 1, 0)])

    def weight_copy(ex, jx, slot):
        cols = pl.ds(pl.multiple_of(jx * tn, LANES), tn)
        return pltpu.make_async_copy(w_hbm.at[li, ex, :, cols], wbuf_ref.at[slot], wsem.at[slot])

    @pl.when(jnp.logical_and(active, new_expert))
    def _():
        group = j * grp_ref[2 * ne] + grp_ref[ne + expert]
        slot = lax.rem(group, 2)

        @pl.when(group == 0)
        def _():
            weight_copy(expert, j, 0).start()

        weight_copy(expert, j, slot).wait()
        _cast_weight(wbuf_ref.at[slot], wb_ref)
        nxt = grp_ref[expert]
        nxt_j = j + jnp.where(nxt <= expert, 1, 0)

        @pl.when(nxt_j < pl.num_programs(0))
        def _():
            weight_copy(nxt, nxt_j, 1 - slot).start()

    @pl.when(active)
    def _():
        if mode == "down":
            p = lax.dot_general(wb_ref[...], a_ref[...], (((0,), (0,)), ((), ())),
                                preferred_element_type=F32).T
            half = p.shape[1] // 2
            o_ref[...] = _pack_pairs(p[:, :half], p[:, half:])
        else:
            lo, hi = _unpack_pairs(a_ref[...])
            half = lo.shape[1]
            nt_dims = (((0,), (1,)), ((), ()))
            p = (lax.dot_general(wb_ref[0:half, :], lo.astype(BF16), nt_dims,
                                 preferred_element_type=F32)
                 + lax.dot_general(wb_ref[half:, :], hi.astype(BF16), nt_dims,
                                   preferred_element_type=F32))
            if mode == "silu":
                o_ref[...] = (p * _sigmoid(p)).astype(o_ref.dtype)
            else:
                o_ref[...] = (u_ref[...].astype(F32) * p).astype(o_ref.dtype)

    @pl.when(jnp.logical_not(active))
    def _():
        o_ref[...] = jnp.zeros_like(o_ref)


def _moe_mm(a, w, li, plan, tn, mode, u=None):
    _, tile_expert, num_tiles, max_tiles, _, groups = plan
    k, n = w.shape[-2], w.shape[-1]
    rows = max_tiles * MOE_TM

    def tile(i, nt):
        return jnp.minimum(i, nt[0] - 1)

    wspec = pl.BlockSpec(memory_space=pl.ANY)
    if mode == "down":
        in_specs = [pl.BlockSpec((k, MOE_TM), lambda j, i, te, nt, gr: (0, tile(i, nt))), wspec]
        out_spec = pl.BlockSpec((MOE_TM, tn // 2), lambda j, i, te, nt, gr: (i, j))
        out_shape = jax.ShapeDtypeStruct((rows, n // 2), jnp.uint32)
    else:
        in_specs = [pl.BlockSpec((MOE_TM, k // 2), lambda j, i, te, nt, gr: (tile(i, nt), 0)),
                    wspec]
        out_spec = pl.BlockSpec((tn, MOE_TM), lambda j, i, te, nt, gr: (j, i))
        out_shape = jax.ShapeDtypeStruct((n, rows), BF16)
    args = [a, w]
    if mode == "gate":
        in_specs.append(pl.BlockSpec((tn, MOE_TM), lambda j, i, te, nt, gr: (j, tile(i, nt))))
        args.append(u)
    return pl.pallas_call(
        functools.partial(_moe_mm_kernel, mode=mode, li=li, tn=tn),
        grid_spec=pltpu.PrefetchScalarGridSpec(
            num_scalar_prefetch=3,
            grid=(n // tn, max_tiles),
            in_specs=in_specs,
            out_specs=out_spec,
            scratch_shapes=[pltpu.VMEM((k, tn), BF16), pltpu.VMEM((2, k, tn), F32),
                            pltpu.SemaphoreType.DMA((2,))],
        ),
        out_shape=out_shape,
        compiler_params=_params("arbitrary", "arbitrary"),
        name=f"moe_mm_{mode}",
    )(tile_expert, num_tiles, groups, *args)


def _moe_combine_kernel(pos_ref, x_ref, g_ref, wts_ref, y_ref, *rest, final):
    if final:
        ng_ref, o_ref, ybuf_ref, sem = rest
    else:
        ng_ref, sh_ref, sc_ref, o_ref, h_ref, ybuf_ref, sem = rest

    def make(k, t, p):
        return pltpu.make_async_copy(y_ref.at[pl.ds(p, 1), :], ybuf_ref.at[k, pl.ds(t, 1), :], sem)

    _issue_and_drain(pos_ref, make)

    def rows(k):
        lo, hi = _unpack_pairs(ybuf_ref[k])
        hw = MOE_DOWN_TN // 2
        parts = []
        for c in range(lo.shape[1] // hw):
            parts += [lo[:, c * hw:(c + 1) * hw], hi[:, c * hw:(c + 1) * hw]]
        return jnp.concatenate(parts, axis=1)

    f = wts_ref[:, 0:1] * rows(0) + wts_ref[:, 1:2] * rows(1)
    xn = x_ref[...] + g_ref[...] * f
    yn = _rms(xn, ng_ref[...])
    if final:
        o_ref[...] = yn
    else:
        o_ref[...] = xn
        h_ref[...] = (yn * (1.0 + sc_ref[...]) + sh_ref[...]).astype(h_ref.dtype)


def _moe_combine(x, gate, wts, y, pos, seq, norm_g, shift=None, scale=None):
    m, d = x.shape
    nb = gate.shape[0]
    ne = wts.shape[-1]
    final = shift is None
    vec = pl.BlockSpec((None, 1, d), lambda i: (i * MOE_TT // seq, 0, 0))
    row = pl.BlockSpec((MOE_TT, d), lambda i: (i, 0))
    in_specs = [
        pl.BlockSpec((None, 1, 2 * MOE_TT), lambda i: (i, 0, 0), memory_space=pltpu.SMEM),
        row, vec,
        pl.BlockSpec((MOE_TT, ne), lambda i: (i, 0)),
        pl.BlockSpec(memory_space=pl.ANY),
        pl.BlockSpec((1, d), lambda i: (0, 0)),
    ]
    args = [pos, x, gate.reshape(nb, 1, d), wts, y, norm_g.reshape(1, d)]
    if final:
        out_specs, out_shape = row, jax.ShapeDtypeStruct((m, d), F32)
    else:
        in_specs += [vec, vec]
        args += [shift.reshape(nb, 1, d), scale.reshape(nb, 1, d)]
        out_specs = [row, row]
        out_shape = [jax.ShapeDtypeStruct((m, d), F32), jax.ShapeDtypeStruct((m, d), BF16)]
    return pl.pallas_call(
        functools.partial(_moe_combine_kernel, final=final),
        grid=(m // MOE_TT,),
        in_specs=in_specs,
        out_specs=out_specs,
        out_shape=out_shape,
        scratch_shapes=[pltpu.VMEM((2, MOE_TT, d // 2), jnp.uint32), pltpu.SemaphoreType.DMA(())],
        compiler_params=_params("arbitrary"),
        name="moe_combine",
    )(*args)


def kernel(x, c, ada_w, ada_b, norm_mix_g, norm_ffn_g, final_norm_g, w_in, ssm_a_re, ssm_a_im, ssm_log_dt, ssm_b_re, ssm_b_im, ssm_c_re, ssm_c_im, ssm_d, w_glu, b_glu, w_branch, w_out, ffn_w1, ffn_w3, ffn_w2, router_w, router_b, moe_w1, moe_w3, moe_w2):
    batch, seq, d = x.shape
    depth = ada_w.shape[0]
    tokens = batch * seq
    ssm_width = w_glu.shape[1]
    per_res = seq // TOKEN_RESIDUES
    tn_in = ATTN_OUT
    assert ssm_width == tn_in and w_in.shape[-1] == (3 * N_GROUPS + 1) * tn_in + 2 * d

    def to_residue_major(t):
        f = t.shape[-1]
        return t.reshape(batch, per_res, TOKEN_RESIDUES, f).transpose(0, 2, 1, 3).reshape(batch, seq, f)

    mod = _ada_modulation(c, ada_w, ada_b).reshape(depth, batch, 6, d)
    x2 = to_residue_major(x).reshape(tokens, d)
    h = None
    out = None
    for layer in range(depth):
        sh1, sc1, g1, sh2, sc2, g2 = (mod[layer, :, i] for i in range(6))

        if h is None:
            h = _norm_mod(x2.reshape(batch, seq, d), norm_mix_g[layer], sh1, sc1)
        qkv0 = _mm_plain(h, w_in, (layer,), lambda j: N_GROUPS * j, 3, F32, tm=1024, tn=tn_in)
        rest = _mm_plain(h, w_in, (layer,),
                         lambda j: j + 1 + jnp.where(j >= 2, 1, 0) + jnp.where(j >= 4, 1, 0),
                         11, BF16, tm=2048, tn=tn_in)
        attn = _dilation_mixture_attention(qkv0, rest, batch, seq)
        tables = _ssm_tables(ssm_a_re[layer], ssm_a_im[layer], ssm_log_dt[layer],
                             ssm_b_re[layer], ssm_b_im[layer], ssm_c_re[layer], ssm_c_im[layer])
        ssm = _s5_ssm(rest, 6, batch, seq, tables, ssm_d[layer], w_glu[layer], b_glu[layer])
        merged = _mm_branch(attn, ssm, w_branch, layer, rest, 7 * tn_in, tm=1024, tn=1024)
        h = None

        li = layer // 2
        if layer % 2 == 0:
            x2, hf = _mm_resid_norm(merged, w_out, layer, x2, g1, norm_ffn_g[layer], sh2, sc2,
                                    seq, tm=256)
            act = _mm_swiglu(hf, ffn_w1, ffn_w3, (li,), tm=2048, tn=512)
            x2 = _mm_resid(act, ffn_w2, (li,), x2, g2, seq, tm=512, tn=512)
        else:
            x2 = _mm_resid(merged, w_out, (layer,), x2, g1, seq, tm=1024, tn=1024)
            x3 = x2.reshape(batch, seq, d)
            sel, wts, cnt = _router(x3, norm_ffn_g[layer], sh2, sc2, router_w[li], router_b[li])
            plan = _moe_plan(sel, cnt, tokens)
            xs = _moe_dispatch(x3, norm_ffn_g[layer], sh2, sc2, plan)
            u = _moe_mm(xs, moe_w1, li, plan, MOE_UP_TN, "silu")
            act = _moe_mm(xs, moe_w3, li, plan, MOE_UP_TN, "gate", u)
            y = _moe_mm(act, moe_w2, li, plan, MOE_DOWN_TN, "down")
            if layer + 1 < depth:
                x2, h = _moe_combine(x2, g2, wts, y, plan[0], seq, norm_mix_g[layer + 1],
                                     mod[layer + 1, :, 0], mod[layer + 1, :, 1])
            else:
                out = _moe_combine(x2, g2, wts, y, plan[0], seq, final_norm_g)
    if out is None:
        out = _final_norm(x2.reshape(batch, seq, d), final_norm_g)
    return out.reshape(batch, TOKEN_RESIDUES, per_res, d).transpose(0, 2, 1, 3).reshape(batch, seq, d)
```

```python
import functools
import math

import jax
import jax.numpy as jnp
from jax import lax
from jax.experimental import pallas as pl
from jax.experimental.pallas import tpu as pltpu

F32 = jnp.float32
BF16 = jnp.bfloat16

LANES = 128
SUBLANES = 8
VMEM_LIMIT_BYTES = 56 * 1024 * 1024

ATTN_PATTERNS = ((128, 1), (512, 4), (2048, 16))
N_GROUPS = len(ATTN_PATTERNS)
HEADS = 8
HEAD_DIM = 128
ATTN_OUT = HEADS * HEAD_DIM
QKV_WIDTH = N_GROUPS * ATTN_OUT
ATTN_BLOCK = 128
SSM_GROUP_CH = 16
SSM_STATE = 64
N_EXPERTS = 8
RMS_EPS = 1e-6
NEG_BIG = -1e30


def _params(*semantics):
    return pltpu.CompilerParams(dimension_semantics=semantics,
                                vmem_limit_bytes=VMEM_LIMIT_BYTES)


def _sigmoid(v):
    return 1.0 / (1.0 + jnp.exp(-v))


ADA_KR = 256
ADA_CG = 512


def _ada_kernel(cb_ref, w_ref, b_ref, o_ref, act_ref):
    nb = cb_ref.shape[0]
    kr, n = w_ref.shape
    nj = ADA_CG // LANES
    kstep = pl.program_id(1)

    @pl.when(jnp.logical_and(pl.program_id(0) == 0, kstep == 0))
    def _():
        cb = cb_ref[...]
        act_ref[...] = cb * _sigmoid(cb)

    @pl.when(kstep == 0)
    def _():
        o_ref[...] = jnp.broadcast_to(b_ref[...], o_ref.shape)

    base = kstep * kr
    zero = jnp.zeros((SUBLANES, LANES), F32)
    for cg in range(n // ADA_CG):
        c0 = cg * ADA_CG

        def body(i, accs):
            r = pl.multiple_of(i * SUBLANES, SUBLANES)
            w = [w_ref[pl.ds(r, SUBLANES), c0 + j * LANES:c0 + (j + 1) * LANES] for j in range(nj)]
            out = []
            for b in range(nb):
                act = act_ref[b, pl.ds(pl.multiple_of(base + r, SUBLANES), SUBLANES), :]
                out.extend(accs[b * nj + j] + w[j] * act for j in range(nj))
            return tuple(out)

        accs = lax.fori_loop(0, kr // SUBLANES, body, (zero,) * (nb * nj), unroll=2)
        for b in range(nb):
            for j in range(nj):
                sl = slice(c0 + j * LANES, c0 + (j + 1) * LANES)
                o_ref[b:b + 1, sl] += jnp.sum(accs[b * nj + j], axis=0, keepdims=True)


def _ada_modulation(c, ada_w, ada_b):
    depth, k, n = ada_w.shape
    nb = c.shape[0]
    cb = jnp.broadcast_to(c[:, :, None], (nb, k, LANES))
    return pl.pallas_call(
        _ada_kernel,
        grid=(depth, k // ADA_KR),
        in_specs=[
            pl.BlockSpec((nb, k, LANES), lambda l, i: (0, 0, 0)),
            pl.BlockSpec((None, ADA_KR, n), lambda l, i: (l, i, 0)),
            pl.BlockSpec((None, 1, n), lambda l, i: (l, 0, 0)),
        ],
        out_specs=pl.BlockSpec((None, nb, n), lambda l, i: (l, 0, 0)),
        out_shape=jax.ShapeDtypeStruct((depth, nb, n), F32),
        scratch_shapes=[pltpu.VMEM((nb, k, LANES), F32)],
        compiler_params=_params("arbitrary", "arbitrary"),
        name="ada_modulation",
    )(cb, ada_w, ada_b.reshape(depth, 1, n))


NORM_TS = 512


def _rms(x, g):
    return x * lax.rsqrt(jnp.mean(x * x, axis=-1, keepdims=True) + RMS_EPS) * g


def _norm_mod_kernel(x_ref, g_ref, sh_ref, sc_ref, o_ref):
    y = _rms(x_ref[...], g_ref[...])
    o_ref[...] = (y * (1.0 + sc_ref[...]) + sh_ref[...]).astype(o_ref.dtype)


def _norm_mod(x, g, shift, scale):
    b, s, d = x.shape
    vec = pl.BlockSpec((None, 1, d), lambda i, j: (i, 0, 0))
    out = pl.pallas_call(
        _norm_mod_kernel,
        grid=(b, s // NORM_TS),
        in_specs=[
            pl.BlockSpec((None, NORM_TS, d), lambda i, j: (i, j, 0)),
            pl.BlockSpec((1, d), lambda i, j: (0, 0)),
            vec, vec,
        ],
        out_specs=pl.BlockSpec((None, NORM_TS, d), lambda i, j: (i, j, 0)),
        out_shape=jax.ShapeDtypeStruct((b, s, d), BF16),
        compiler_params=_params("arbitrary", "arbitrary"),
        name="norm_mod",
    )(x, g.reshape(1, d), shift.reshape(b, 1, d), scale.reshape(b, 1, d))
    return out.reshape(b * s, d)


def _final_norm_kernel(x_ref, g_ref, o_ref):
    o_ref[...] = _rms(x_ref[...], g_ref[...])


def _final_norm(x, g):
    b, s, d = x.shape
    return pl.pallas_call(
        _final_norm_kernel,
        grid=(b, s // NORM_TS),
        in_specs=[
            pl.BlockSpec((None, NORM_TS, d), lambda i, j: (i, j, 0)),
            pl.BlockSpec((1, d), lambda i, j: (0, 0)),
        ],
        out_specs=pl.BlockSpec((None, NORM_TS, d), lambda i, j: (i, j, 0)),
        out_shape=jax.ShapeDtypeStruct((b, s, d), F32),
        compiler_params=_params("arbitrary", "arbitrary"),
        name="final_norm",
    )(x, g.reshape(1, d))


CAST_ROWS = 256


def _cast_weight(w_ref, wb_ref):
    def body(i, carry):
        r = pl.multiple_of(i * CAST_ROWS, CAST_ROWS)
        wb_ref[pl.ds(r, CAST_ROWS), :] = w_ref[pl.ds(r, CAST_ROWS), :].astype(BF16)
        return carry
    lax.fori_loop(0, w_ref.shape[0] // CAST_ROWS, body, 0)


def _dot(a, b):
    return jnp.dot(a, b, preferred_element_type=F32)


def _mm_plain_kernel(a_ref, w_ref, o_ref, wb_ref):
    @pl.when(pl.program_id(1) == 0)
    def _():
        _cast_weight(w_ref, wb_ref)
    o_ref[...] = _dot(a_ref[...], wb_ref[...]).astype(o_ref.dtype)


def _mm_plain(a, w, w_idx, col_tile, n_tiles, out_dtype, tm, tn):
    m, k = a.shape
    lead = (None,) * len(w_idx)
    return pl.pallas_call(
        _mm_plain_kernel,
        grid=(n_tiles, m // tm),
        in_specs=[
            pl.BlockSpec((tm, k), lambda j, i: (i, 0)),
            pl.BlockSpec(lead + (k, tn), lambda j, i: w_idx + (0, col_tile(j))),
        ],
        out_specs=pl.BlockSpec((tm, tn), lambda j, i: (i, j)),
        out_shape=jax.ShapeDtypeStruct((m, n_tiles * tn), out_dtype),
        scratch_shapes=[pltpu.VMEM((k, tn), BF16)],
        compiler_params=_params("arbitrary", "arbitrary"),
        name="mm_plain",
    )(a, w)


def _mm_swiglu_kernel(a_ref, w1_ref, w3_ref, o_ref, wb1_ref, wb3_ref):
    @pl.when(pl.program_id(1) == 0)
    def _():
        _cast_weight(w1_ref, wb1_ref)
        _cast_weight(w3_ref, wb3_ref)
    a = a_ref[...]
    u = _dot(a, wb1_ref[...])
    v = _dot(a, wb3_ref[...])
    o_ref[...] = (u * _sigmoid(u) * v).astype(o_ref.dtype)


def _mm_swiglu(a, w1, w3, w_idx, tm, tn):
    m, k = a.shape
    n = w1.shape[-1]
    lead = (None,) * len(w_idx)
    wspec = pl.BlockSpec(lead + (k, tn), lambda j, i: w_idx + (0, j))
    return pl.pallas_call(
        _mm_swiglu_kernel,
        grid=(n // tn, m // tm),
        in_specs=[pl.BlockSpec((tm, k), lambda j, i: (i, 0)), wspec, wspec],
        out_specs=pl.BlockSpec((tm, tn), lambda j, i: (i, j)),
        out_shape=jax.ShapeDtypeStruct((m, n), BF16),
        scratch_shapes=[pltpu.VMEM((k, tn), BF16), pltpu.VMEM((k, tn), BF16)],
        compiler_params=_params("arbitrary", "arbitrary"),
        name="mm_swiglu",
    )(a, w1, w3)


def _mm_resid_kernel(a_ref, w_ref, x_ref, g_ref, o_ref, wb_ref):
    @pl.when(pl.program_id(1) == 0)
    def _():
        _cast_weight(w_ref, wb_ref)
    o_ref[...] = x_ref[...] + g_ref[...] * _dot(a_ref[...], wb_ref[...])


def _mm_resid(a, w, w_idx, x, gate, seq, tm, tn):
    m, k = a.shape
    n = w.shape[-1]
    nb = gate.shape[0]
    lead = (None,) * len(w_idx)
    return pl.pallas_call(
        _mm_resid_kernel,
        grid=(n // tn, m // tm),
        in_specs=[
            pl.BlockSpec((tm, k), lambda j, i: (i, 0)),
            pl.BlockSpec(lead + (k, tn), lambda j, i: w_idx + (0, j)),
            pl.BlockSpec((tm, tn), lambda j, i: (i, j)),
            pl.BlockSpec((None, 1, tn), lambda j, i: (i * tm // seq, 0, j)),
        ],
        out_specs=pl.BlockSpec((tm, tn), lambda j, i: (i, j)),
        out_shape=jax.ShapeDtypeStruct((m, n), F32),
        scratch_shapes=[pltpu.VMEM((k, tn), BF16)],
        compiler_params=_params("arbitrary", "arbitrary"),
        name="mm_resid",
    )(a, w, x, gate.reshape(nb, 1, n))


def _mm_resid_norm_kernel(a_ref, w_ref, x_ref, g_ref, ng_ref, sh_ref, sc_ref,
                          xo_ref, ho_ref, wb_ref):
    @pl.when(pl.program_id(0) == 0)
    def _():
        _cast_weight(w_ref, wb_ref)
    xn = x_ref[...] + g_ref[...] * _dot(a_ref[...], wb_ref[...])
    xo_ref[...] = xn
    yn = _rms(xn, ng_ref[...])
    ho_ref[...] = (yn * (1.0 + sc_ref[...]) + sh_ref[...]).astype(ho_ref.dtype)


def _mm_resid_norm(a, w, layer, x, gate, norm_g, shift, scale, seq, tm):
    m, k = a.shape
    n = w.shape[-1]
    nb = gate.shape[0]
    vec = pl.BlockSpec((None, 1, n), lambda i: (i * tm // seq, 0, 0))
    row = pl.BlockSpec((tm, n), lambda i: (i, 0))
    return pl.pallas_call(
        _mm_resid_norm_kernel,
        grid=(m // tm,),
        in_specs=[
            pl.BlockSpec((tm, k), lambda i: (i, 0)),
            pl.BlockSpec((None, k, n), lambda i: (layer, 0, 0), pipeline_mode=pl.Buffered(1)),
            row, vec,
            pl.BlockSpec((1, n), lambda i: (0, 0)),
            vec, vec,
        ],
        out_specs=[row, row],
        out_shape=[jax.ShapeDtypeStruct((m, n), F32), jax.ShapeDtypeStruct((m, n), BF16)],
        scratch_shapes=[pltpu.VMEM((k, n), BF16)],
        compiler_params=_params("arbitrary"),
        name="mm_resid_norm",
    )(a, w, x, gate.reshape(nb, 1, n), norm_g.reshape(1, n),
      shift.reshape(nb, 1, n), scale.reshape(nb, 1, n))


def _mm_branch_kernel(at_ref, ss_ref, wa_ref, ws_ref, ga_ref, gs_ref, o_ref,
                      wba_ref, wbs_ref):
    @pl.when(pl.program_id(1) == 0)
    def _():
        _cast_weight(wa_ref, wba_ref)
        _cast_weight(ws_ref, wbs_ref)
    pa = _dot(at_ref[...], wba_ref[...])
    ps = _dot(ss_ref[...], wbs_ref[...])
    ga = _sigmoid(ga_ref[...].astype(F32))
    gs = _sigmoid(gs_ref[...].astype(F32))
    o_ref[...] = (ga * pa + gs * ps).astype(o_ref.dtype)


def _mm_branch(attn, ssm, w_branch, layer, proj, gate_col, tm, tn):
    m, k = attn.shape
    n = w_branch.shape[-1]
    g0 = gate_col // tn
    g1 = (gate_col + n) // tn
    return pl.pallas_call(
        _mm_branch_kernel,
        grid=(n // tn, m // tm),
        in_specs=[
            pl.BlockSpec((tm, k), lambda j, i: (i, 0)),
            pl.BlockSpec((tm, k), lambda j, i: (i, 0)),
            pl.BlockSpec((None, k, tn), lambda j, i: (layer, 0, j)),
            pl.BlockSpec((None, k, tn), lambda j, i: (layer, 1, j)),
            pl.BlockSpec((tm, tn), lambda j, i: (i, g0 + j)),
            pl.BlockSpec((tm, tn), lambda j, i: (i, g1 + j)),
        ],
        out_specs=pl.BlockSpec((tm, tn), lambda j, i: (i, j)),
        out_shape=jax.ShapeDtypeStruct((m, n), BF16),
        scratch_shapes=[pltpu.VMEM((k, tn), BF16), pltpu.VMEM((k, tn), BF16)],
        compiler_params=_params("arbitrary", "arbitrary"),
        name="mm_branch",
    )(attn, ssm, w_branch, w_branch, proj, proj)


TOKEN_RESIDUES = 16
ATTN_BB = 4


def _dot_nt(a, b):
    return lax.dot_general(a, b, (((1,), (1,)), ((), ())), preferred_element_type=F32)


def _attn_kernel(*refs, with_prev):
    if with_prev:
        q_ref, kp_ref, kc_ref, vp_ref, vc_ref, bc_ref, bp_ref, o_ref, lse_ref = refs
    else:
        q_ref, kc_ref, vc_ref, bc_ref, o_ref, lse_ref = refs
    blk = ATTN_BLOCK
    bb, nc, rpc = q_ref.shape[0], q_ref.shape[1], q_ref.shape[2]
    scale = HEAD_DIM ** -0.5
    lane = lax.broadcasted_iota(jnp.int32, (blk, LANES), 1)

    def heads(ref, bi):
        x = ref[bi].reshape(blk, ATTN_OUT)
        return jnp.stack([x[:, h * HEAD_DIM:(h + 1) * HEAD_DIM] for h in range(HEADS)]
                         ).astype(BF16)

    def qk(q, k):
        return lax.dot_general(q, k, (((2,), (2,)), ((0,), (0,))), preferred_element_type=F32)

    def pv(p, v):
        return lax.dot_general(p.astype(BF16), v, (((2,), (1,)), ((0,), (0,))),
                               preferred_element_type=F32)

    for bi in range(bb):
        q = heads(q_ref, bi)
        s_c = qk(q, heads(kc_ref, bi)) * scale + bc_ref[...]
        m = jnp.max(s_c, axis=2, keepdims=True)
        if with_prev:
            bias_p = jnp.where(pl.program_id(2) > 0, bp_ref[...], NEG_BIG)
            s_p = qk(q, heads(kp_ref, bi)) * scale + bias_p
            m = jnp.maximum(m, jnp.max(s_p, axis=2, keepdims=True))
        p_c = jnp.exp(s_c - m)
        l = jnp.sum(p_c, axis=2, keepdims=True)
        o = pv(p_c, heads(vc_ref, bi))
        if with_prev:
            p_p = jnp.exp(s_p - m)
            l = l + jnp.sum(p_p, axis=2, keepdims=True)
            o = o + pv(p_p, heads(vp_ref, bi))
        o = o / l
        lse = m + jnp.log(l)
        lse_tile = jnp.zeros((blk, LANES), F32)
        for h in range(HEADS):
            sl = slice(h * HEAD_DIM, (h + 1) * HEAD_DIM)
            o_ref[bi, :, :, sl] = o[h].reshape(nc, rpc, HEAD_DIM).astype(o_ref.dtype)
            lse_tile = jnp.where(lane == h, lse[h], lse_tile)
        lse_ref[bi] = lse_tile.reshape(nc, rpc, LANES)


def _attn_bias(dilation, nc, back):
    rpc = ATTN_BLOCK // nc
    i = jnp.arange(ATTN_BLOCK, dtype=jnp.int32)
    off = nc * (i % rpc) + i // rpc
    dist = off[:, None] - off[None, :] + back * ATTN_BLOCK
    valid = jnp.logical_and(dist >= 0, dist <= ATTN_BLOCK)
    slopes = 2.0 ** (-8.0 * (jnp.arange(HEADS, dtype=F32) + 1.0) / HEADS)
    pen = slopes[:, None, None] * (dist * dilation).astype(F32)[None]
    return jnp.where(valid[None], -pen, NEG_BIG)


def _attention_group(qkv, qkv_cols, batch, seq, gi):
    window, dilation = ATTN_PATTERNS[gi]
    assert window // dilation == ATTN_BLOCK
    cols = qkv.shape[-1]
    per_res = seq // TOKEN_RESIDUES
    nc = TOKEN_RESIDUES // dilation
    rpc = ATTN_BLOCK // nc
    nblk = seq // dilation // ATTN_BLOCK
    with_prev = nblk > 1

    def shape5(c):
        return (batch, nc, dilation, per_res, c)

    def spec(c, col, back):
        return pl.BlockSpec(
            (ATTN_BB, nc, None, rpc, c),
            lambda b, r, n: (b, 0, r, jnp.maximum(n - back, 0), col))

    bias_spec = pl.BlockSpec((HEADS, ATTN_BLOCK, ATTN_BLOCK), lambda b, r, n: (0, 0, 0))
    in_specs = [spec(ATTN_OUT, qkv_cols[0], 0)]
    for col in qkv_cols[1:]:
        if with_prev:
            in_specs.append(spec(ATTN_OUT, col, 1))
        in_specs.append(spec(ATTN_OUT, col, 0))
    view = qkv.reshape(shape5(cols))
    args = [view] * len(in_specs) + [_attn_bias(dilation, nc, 0)]
    in_specs.append(bias_spec)
    if with_prev:
        args.append(_attn_bias(dilation, nc, 1))
        in_specs.append(bias_spec)
    o, lse = pl.pallas_call(
        functools.partial(_attn_kernel, with_prev=with_prev),
        grid=(batch // ATTN_BB, dilation, nblk),
        in_specs=in_specs,
        out_specs=[spec(ATTN_OUT, 0, 0), spec(LANES, 0, 0)],
        out_shape=[
            jax.ShapeDtypeStruct(shape5(ATTN_OUT), qkv.dtype),
            jax.ShapeDtypeStruct(shape5(LANES), F32),
        ],
        compiler_params=_params("arbitrary", "arbitrary", "arbitrary"),
        name=f"attn_g{gi}",
    )(*args)
    return o.reshape(batch * seq, ATTN_OUT), lse.reshape(batch * seq, LANES)


COMBINE_TS = 512


def _combine_kernel(o0_ref, o1_ref, o2_ref, l0_ref, l1_ref, l2_ref, out_ref):
    a0, a1, a2 = l0_ref[...], l1_ref[...], l2_ref[...]
    m = jnp.maximum(jnp.maximum(a0, a1), a2)
    e0 = jnp.exp(a0 - m)
    e1 = jnp.exp(a1 - m)
    e2 = jnp.exp(a2 - m)
    inv = 1.0 / (e0 + e1 + e2)
    w0, w1, w2 = e0 * inv, e1 * inv, e2 * inv
    for h in range(HEADS):
        sl = slice(h * HEAD_DIM, (h + 1) * HEAD_DIM)
        mix = (w0[:, h:h + 1] * o0_ref[:, sl].astype(F32)
               + w1[:, h:h + 1] * o1_ref[:, sl].astype(F32)
               + w2[:, h:h + 1] * o2_ref[:, sl].astype(F32))
        out_ref[:, sl] = mix.astype(out_ref.dtype)


def _dilation_mixture_attention(qkv0, rest, batch, seq):
    outs, lses = [], []
    for gi in range(N_GROUPS):
        if gi == 0:
            o, lse = _attention_group(qkv0, (0, 1, 2), batch, seq, gi)
        else:
            o, lse = _attention_group(rest, (gi - 1, gi + 1, gi + 3), batch, seq, gi)
        outs.append(o)
        lses.append(lse)
    m = batch * seq
    ospec = pl.BlockSpec((COMBINE_TS, ATTN_OUT), lambda i: (i, 0))
    lspec = pl.BlockSpec((COMBINE_TS, LANES), lambda i: (i, 0))
    return pl.pallas_call(
        _combine_kernel,
        grid=(m // COMBINE_TS,),
        in_specs=[ospec] * 3 + [lspec] * 3,
        out_specs=ospec,
        out_shape=jax.ShapeDtypeStruct((m, ATTN_OUT), BF16),
        compiler_params=_params("arbitrary"),
        name="attn_combine",
    )(*outs, *lses)


SSM_LT = 128
SSM_PASSES = 2
SSM_JBLK = 256


def _gelu_tanh(y):
    return 0.5 * y * (1.0 + jnp.tanh(math.sqrt(2.0 / math.pi) * (y + 0.044715 * (y * y * y))))


def _ssm_kernel(u_ref, pin_ref, pout_ref, bblk_ref, cblk_ref, are_ref, aim_ref, d_ref,
                wglu_ref, bglu_ref, o_ref, sre_ref, sim_ref, hre_ref, him_ref, *, nbatch):
    width = u_ref.shape[-1]
    rows = pin_ref.shape[1]
    njb = width // SSM_JBLK
    jstates = bblk_ref.shape[2] // 2
    pairs = rows // SUBLANES
    per_tile = SUBLANES // nbatch

    @pl.when(pl.program_id(0) == 0)
    def _():
        hre_ref[...] = jnp.zeros_like(hre_ref)
        him_ref[...] = jnp.zeros_like(him_ref)

    u_rm = u_ref[...].reshape(pout_ref.shape[1], width)
    out_rm = None
    for p in range(pin_ref.shape[0]):
        u_f32 = _dot(pin_ref[p], u_rm)
        u_tb = u_f32.astype(BF16)
        ys = []
        for j in range(njb):
            bu = _dot(u_tb[:, j * SSM_JBLK:(j + 1) * SSM_JBLK], bblk_ref[j])
            sre_ref[...] = bu[:, :jstates].reshape(pairs, SUBLANES, jstates)
            sim_ref[...] = bu[:, jstates:].reshape(pairs, SUBLANES, jstates)
            cs = slice(j * jstates, (j + 1) * jstates)
            ar = are_ref[0:nbatch, cs]
            ai = aim_ref[0:nbatch, cs]

            def step(k, carry):
                hr, hi = carry
                for t in range(per_tile):
                    rs = slice(t * nbatch, (t + 1) * nbatch)
                    nr = ar * hr - ai * hi + sre_ref[k, rs, :]
                    ni = ar * hi + ai * hr + sim_ref[k, rs, :]
                    sre_ref[k, rs, :] = nr
                    sim_ref[k, rs, :] = ni
                    hr, hi = nr, ni
                return hr, hi

            hr, hi = lax.fori_loop(0, pairs, step,
                                   (hre_ref[0:nbatch, cs], him_ref[0:nbatch, cs]), unroll=2)
            hre_ref[0:nbatch, cs] = hr
            him_ref[0:nbatch, cs] = hi

            h_re = sre_ref[...].reshape(rows, jstates).astype(BF16)
            h_im = sim_ref[...].reshape(rows, jstates).astype(BF16)
            ys.append(_dot(h_re, cblk_ref[j, 0:jstates, :]) + _dot(h_im, cblk_ref[j, jstates:, :]))

        y = jnp.concatenate(ys, axis=1) + d_ref[...] * u_f32
        z = _gelu_tanh(y).astype(BF16)
        g = _dot(z, wglu_ref[...]) + bglu_ref[...]
        o_tb = (g[:, :width] * _sigmoid(g[:, width:])).astype(BF16)
        back = _dot(pout_ref[p], o_tb)
        out_rm = back if out_rm is None else out_rm + back
    o_ref[...] = out_rm.astype(o_ref.dtype).reshape(o_ref.shape)


def _ssm_tables(a_re, a_im, log_dt, b_re, b_im, c_re, c_im):
    groups, nst = a_re.shape
    gpb = SSM_JBLK // SSM_GROUP_CH
    njb = groups // gpb
    lam = lax.complex(a_re.astype(F32), a_im.astype(F32))
    dt = jnp.exp(log_dt.astype(F32))[:, None]
    a_bar = jnp.exp(lam * dt)
    b_mat = lax.complex(b_re.astype(F32), b_im.astype(F32))
    b_bar = ((a_bar - 1.0) / lam)[:, :, None] * b_mat
    eye = jnp.eye(gpb, dtype=F32)

    def in_blocks(t):
        t = t.reshape(njb, gpb, nst, SSM_GROUP_CH)
        return jnp.einsum('jgnc,gh->jgchn', t, eye).reshape(njb, gpb * SSM_GROUP_CH, gpb * nst)

    def out_blocks(t):
        t = t.reshape(njb, gpb, SSM_GROUP_CH, nst)
        return jnp.einsum('jgcn,gh->jhngc', t, eye).reshape(njb, gpb * nst, gpb * SSM_GROUP_CH)

    bblk = jnp.concatenate([in_blocks(jnp.real(b_bar)), in_blocks(jnp.imag(b_bar))], axis=2)
    cblk = jnp.concatenate([out_blocks(c_re.astype(F32)), out_blocks(-c_im.astype(F32))], axis=1)
    are = jnp.broadcast_to(jnp.real(a_bar).reshape(1, groups * nst), (SUBLANES, groups * nst))
    aim = jnp.broadcast_to(jnp.imag(a_bar).reshape(1, groups * nst), (SUBLANES, groups * nst))
    return bblk.astype(BF16), cblk.astype(BF16), are, aim


def _ssm_row_perms(nbatch):
    per_res = SSM_PASSES * SSM_LT // TOKEN_RESIDUES
    col = jnp.arange(nbatch * TOKEN_RESIDUES * per_res, dtype=jnp.int32)
    b = col // (TOKEN_RESIDUES * per_res)
    t_local = TOKEN_RESIDUES * (col % per_res) + (col // per_res) % TOKEN_RESIDUES
    row = (t_local % SSM_LT) * nbatch + b
    pin = jnp.logical_and(
        (t_local // SSM_LT)[None, None, :] == jnp.arange(SSM_PASSES, dtype=jnp.int32)[:, None, None],
        row[None, None, :] == jnp.arange(SSM_LT * nbatch, dtype=jnp.int32)[None, :, None])
    pin = pin.astype(BF16)
    return pin, pin.transpose(0, 2, 1)


def _s5_ssm(src, u_col, nbatch, seq, tables, d_skip, w_glu, b_glu):
    bblk, cblk, are, aim = tables
    width = w_glu.shape[0]
    nstate = are.shape[1]
    jstates = bblk.shape[2] // 2
    rows = SSM_LT * nbatch
    step_rows = SSM_PASSES * SSM_LT // TOKEN_RESIDUES
    per_res = seq // TOKEN_RESIDUES
    pin, pout = _ssm_row_perms(nbatch)
    d2 = d_skip.reshape(1, width).astype(F32)
    wg = w_glu.astype(BF16)
    bg = b_glu.reshape(1, 2 * width).astype(F32)

    def const(a):
        return pl.BlockSpec(a.shape, lambda i: (0,) * a.ndim, pipeline_mode=pl.Buffered(1))

    blk = (nbatch, TOKEN_RESIDUES, step_rows, width)
    out = pl.pallas_call(
        functools.partial(_ssm_kernel, nbatch=nbatch),
        grid=(per_res // step_rows,),
        in_specs=[pl.BlockSpec(blk, lambda i: (0, 0, i, u_col)),
                  const(pin), const(pout), const(bblk), const(cblk), const(are), const(aim),
                  const(d2), const(wg), const(bg)],
        out_specs=pl.BlockSpec(blk, lambda i: (0, 0, i, 0)),
        out_shape=jax.ShapeDtypeStruct((nbatch, TOKEN_RESIDUES, per_res, width), BF16),
        scratch_shapes=[
            pltpu.VMEM((rows // SUBLANES, SUBLANES, jstates), F32),
            pltpu.VMEM((rows // SUBLANES, SUBLANES, jstates), F32),
            pltpu.VMEM((SUBLANES, nstate), F32),
            pltpu.VMEM((SUBLANES, nstate), F32),
        ],
        compiler_params=_params("arbitrary"),
        name="s5_ssm",
    )(src.reshape(nbatch, TOKEN_RESIDUES, per_res, src.shape[-1]),
      pin, pout, bblk, cblk, are, aim, d2, wg, bg)
    return out.reshape(nbatch * seq, width)


ROUTER_TS = 256


def _router_kernel(x_ref, g_ref, sh_ref, sc_ref, rw_ref, rb_ref,
                   sel_ref, wts_ref, cnt_ref, carry_ref):
    @pl.when(jnp.logical_and(pl.program_id(0) == 0, pl.program_id(1) == 0))
    def _():
        carry_ref[...] = jnp.zeros_like(carry_ref)

    h = _rms(x_ref[...], g_ref[...]) * (1.0 + sc_ref[...]) + sh_ref[...]
    ts, ne = h.shape[0], rw_ref.shape[0]
    idx = lax.broadcasted_iota(jnp.int32, (ts, ne), 1)
    logits = jnp.broadcast_to(rb_ref[...], (ts, ne))
    for ex in range(ne):
        col = jnp.sum(h * rw_ref[ex:ex + 1, :], axis=1, keepdims=True)
        logits = logits + jnp.where(idx == ex, col, 0.0)
    m1 = jnp.max(logits, axis=1, keepdims=True)
    i1 = jnp.min(jnp.where(logits == m1, idx, ne), axis=1, keepdims=True)
    rest = jnp.where(idx == i1, -jnp.inf, logits)
    m2 = jnp.max(rest, axis=1, keepdims=True)
    i2 = jnp.min(jnp.where(rest == m2, idx, ne), axis=1, keepdims=True)
    e = jnp.exp(m2 - m1)
    w1 = 1.0 / (1.0 + e)
    w2 = e / (1.0 + e)

    onehot = jnp.where(idx == i1, 1.0, 0.0) + jnp.where(idx == i2, 1.0, 0.0)
    row = lax.broadcasted_iota(jnp.int32, (ts, ts), 0)
    col = lax.broadcasted_iota(jnp.int32, (ts, ts), 1)
    lower = jnp.where(col < row, 1.0, 0.0).astype(BF16)
    before = carry_ref[...] + _dot(lower, onehot.astype(BF16))
    r1 = jnp.sum(jnp.where(idx == i1, before, 0.0), axis=1, keepdims=True).astype(jnp.int32)
    r2 = jnp.sum(jnp.where(idx == i2, before, 0.0), axis=1, keepdims=True).astype(jnp.int32)
    total = carry_ref[...] + jnp.sum(onehot, axis=0, keepdims=True)
    carry_ref[...] = total
    cnt_ref[...] = total
    sel_ref[...] = jnp.where(idx == 0, i1, jnp.where(idx == 1, i2, jnp.where(
        idx == 2, r1, jnp.where(idx == 3, r2, 0))))
    wts_ref[...] = jnp.where(idx == 0, w1, jnp.where(idx == 1, w2, 0.0))


def _router(x, g, shift, scale, router_w, router_b):
    b, s, d = x.shape
    ne = router_w.shape[-1]
    vec = pl.BlockSpec((None, 1, d), lambda i, j: (i, 0, 0))
    tok = pl.BlockSpec((None, ROUTER_TS, ne), lambda i, j: (i, j, 0))
    sel, wts, cnt = pl.pallas_call(
        _router_kernel,
        grid=(b, s // ROUTER_TS),
        in_specs=[
            pl.BlockSpec((None, ROUTER_TS, d), lambda i, j: (i, j, 0)),
            pl.BlockSpec((1, d), lambda i, j: (0, 0)),
            vec, vec,
            pl.BlockSpec((ne, d), lambda i, j: (0, 0)),
            pl.BlockSpec((1, ne), lambda i, j: (0, 0)),
        ],
        out_specs=[tok, tok, pl.BlockSpec((1, ne), lambda i, j: (0, 0))],
        out_shape=[jax.ShapeDtypeStruct((b, s, ne), jnp.int32),
                   jax.ShapeDtypeStruct((b, s, ne), F32),
                   jax.ShapeDtypeStruct((1, ne), F32)],
        scratch_shapes=[pltpu.VMEM((1, ne), F32)],
        compiler_params=_params("arbitrary", "arbitrary"),
        name="router",
    )(x, g.reshape(1, d), shift.reshape(b, 1, d), scale.reshape(b, 1, d),
      router_w.T, router_b.reshape(1, ne))
    return sel.reshape(b * s, ne), wts.reshape(b * s, ne), cnt


MOE_TM = 256
MOE_TT = 256
MOE_UP_TN = 1408
MOE_DOWN_TN = 2048


def _moe_plan(sel, cnt, tokens):
    ne = cnt.shape[-1]
    counts = cnt[0].astype(jnp.int32)
    padded = (counts + MOE_TM - 1) // MOE_TM * MOE_TM
    ends = jnp.cumsum(padded)
    starts = ends - padded
    pos1 = starts[sel[:, 0]] + sel[:, 2]
    pos2 = starts[sel[:, 1]] + sel[:, 3]
    nt = tokens // MOE_TT
    pos = jnp.concatenate([pos1.reshape(nt, MOE_TT), pos2.reshape(nt, MOE_TT)], axis=1)
    max_tiles = 2 * tokens // MOE_TM + ne
    tile_start = jnp.arange(max_tiles, dtype=jnp.int32) * MOE_TM
    tile_expert = jnp.minimum(
        jnp.sum((tile_start[:, None] >= ends[None, :]).astype(jnp.int32), axis=1), ne - 1)
    num_tiles = (ends[-1] // MOE_TM).reshape(1)
    zero_tiles = jnp.concatenate([
        jnp.maximum(ends // MOE_TM - 1, 0),
        jnp.minimum(num_tiles[0] + jnp.arange(ne, dtype=jnp.int32), max_tiles - 1)])
    nonempty = counts > 0
    ar = jnp.arange(ne, dtype=jnp.int32)
    order = (ar[:, None] + 1 + ar[None, :]) % ne
    nxt = order[ar, jnp.argmax(nonempty[order], axis=1)].astype(jnp.int32)
    gidx = jnp.cumsum(nonempty.astype(jnp.int32)) - 1
    groups = jnp.concatenate([nxt, gidx, jnp.sum(nonempty.astype(jnp.int32)).reshape(1)])
    return (pos.reshape(nt, 1, 2 * MOE_TT), tile_expert, num_tiles, max_tiles, zero_tiles,
            groups)


def _row_copies(pos_ref, t, make):
    tt = pos_ref.shape[1] // 2
    return make(0, t, pos_ref[0, t]), make(1, t, pos_ref[0, tt + t])


def _issue_and_drain(pos_ref, make):
    tt = pos_ref.shape[1] // 2

    def issue(t, carry):
        for k, cp in enumerate(_row_copies(pos_ref, t, make)):
            cp.start(priority=k)
        return carry

    def drain(t, carry):
        for cp in _row_copies(pos_ref, t, make):
            cp.wait()
        return carry

    lax.fori_loop(0, tt, issue, 0, unroll=4)
    lax.fori_loop(0, tt, drain, 0, unroll=4)


def _pack_pairs(lo, hi):
    lo_bits = pltpu.bitcast(lo.astype(BF16).astype(F32), jnp.uint32)
    hi_bits = pltpu.bitcast(hi.astype(BF16).astype(F32), jnp.uint32)
    return jnp.bitwise_or(jnp.bitwise_and(hi_bits, jnp.uint32(0xFFFF0000)),
                          jnp.right_shift(lo_bits, jnp.uint32(16)))


def _unpack_pairs(words):
    lo = pltpu.bitcast(jnp.left_shift(words, jnp.uint32(16)), F32)
    hi = pltpu.bitcast(jnp.bitwise_and(words, jnp.uint32(0xFFFF0000)), F32)
    return lo, hi


def _moe_dispatch_kernel(pos_ref, zt_ref, x_ref, g_ref, sh_ref, sc_ref, xs_ref, hbuf_ref, sem):
    @pl.when(jnp.logical_and(pl.program_id(0) == 0, pl.program_id(1) == 0))
    def _():
        hbuf_ref[...] = jnp.zeros_like(hbuf_ref)
        for k in range(zt_ref.shape[0]):
            cp = pltpu.make_async_copy(
                hbuf_ref, xs_ref.at[pl.ds(pl.multiple_of(zt_ref[k] * MOE_TM, MOE_TM), MOE_TM), :],
                sem)
            cp.start()
            cp.wait()

    h = _rms(x_ref[...], g_ref[...]) * (1.0 + sc_ref[...]) + sh_ref[...]
    half = h.shape[1] // 2
    hbuf_ref[...] = _pack_pairs(h[:, :half], h[:, half:])

    def make(k, t, p):
        return pltpu.make_async_copy(hbuf_ref.at[pl.ds(t, 1), :], xs_ref.at[pl.ds(p, 1), :], sem)

    _issue_and_drain(pos_ref, make)


def _moe_dispatch(x, g, shift, scale, plan):
    pos, _, _, max_tiles, zero_tiles, _ = plan
    b, s, d = x.shape
    per_b = s // MOE_TT
    assert MOE_TT == MOE_TM
    vec = pl.BlockSpec((None, 1, d), lambda i, j: (i, 0, 0))
    return pl.pallas_call(
        _moe_dispatch_kernel,
        grid=(b, per_b),
        in_specs=[
            pl.BlockSpec((None, 1, 2 * MOE_TT), lambda i, j: (i * per_b + j, 0, 0),
                         memory_space=pltpu.SMEM),
            pl.BlockSpec(memory_space=pltpu.SMEM),
            pl.BlockSpec((None, MOE_TT, d), lambda i, j: (i, j, 0)),
            pl.BlockSpec((1, d), lambda i, j: (0, 0)),
            vec, vec,
        ],
        out_specs=pl.BlockSpec(memory_space=pl.ANY),
        out_shape=jax.ShapeDtypeStruct((max_tiles * MOE_TM, d // 2), jnp.uint32),
        scratch_shapes=[pltpu.VMEM((MOE_TT, d // 2), jnp.uint32), pltpu.SemaphoreType.DMA(())],
        compiler_params=_params("arbitrary", "arbitrary"),
        name="moe_dispatch",
    )(pos, zero_tiles, x, g.reshape(1, d), shift.reshape(b, 1, d), scale.reshape(b, 1, d))


def _moe_mm_kernel(te_ref, nt_ref, grp_ref, a_ref, *rest, mode, li, tn):
    nw = 2 if mode == "up" else 1
    w_hbm = rest[:nw]
    o_ref = rest[nw]
    wb = rest[nw + 1:2 * nw + 1]
    stage = rest[2 * nw + 1:3 * nw + 1]
    wsem = rest[3 * nw + 1]
    j = pl.program_id(0)
    i = pl.program_id(1)
    ne = (grp_ref.shape[0] - 1) // 2
    expert = te_ref[i]
    active = i < nt_ref[0]
    new_expert = jnp.logical_or(i == 0, expert != te_ref[jnp.maximum(i - 1, 0)])

    def weight_copies(ex, jx):
        cols = pl.ds(pl.multiple_of(jx * tn, LANES), tn)
        return [pltpu.make_async_copy(w_hbm[k].at[li, ex, :, cols], stage[k], wsem.at[k])
                for k in range(nw)]

    @pl.when(jnp.logical_and(active, new_expert))
    def _():
        @pl.when(jnp.logical_and(j == 0, grp_ref[ne + expert] == 0))
        def _():
            for cp in weight_copies(expert, j):
                cp.start()

        for cp in weight_copies(expert, j):
            cp.wait()
        for k in range(nw):
            _cast_weight(stage[k], wb[k])
        nxt = grp_ref[expert]
        nxt_j = j + jnp.where(nxt <= expert, 1, 0)

        @pl.when(nxt_j < pl.num_programs(0))
        def _():
            for cp in weight_copies(nxt, nxt_j):
                cp.start()

    @pl.when(active)
    def _():
        if mode == "down":
            p = lax.dot_general(wb[0][...], a_ref[...], (((0,), (0,)), ((), ())),
                                preferred_element_type=F32).T
            half = p.shape[1] // 2
            o_ref[...] = _pack_pairs(p[:, :half], p[:, half:])
        else:
            lo, hi = _unpack_pairs(a_ref[...])
            lo, hi = lo.astype(BF16), hi.astype(BF16)
            half = lo.shape[1]
            nt_dims = (((0,), (1,)), ((), ()))

            def proj(w_ref):
                return (lax.dot_general(w_ref[0:half, :], lo, nt_dims, preferred_element_type=F32)
                        + lax.dot_general(w_ref[half:, :], hi, nt_dims,
                                          preferred_element_type=F32))

            u = proj(wb[0])
            o_ref[...] = (u * _sigmoid(u) * proj(wb[1])).astype(o_ref.dtype)

    @pl.when(jnp.logical_not(active))
    def _():
        o_ref[...] = jnp.zeros_like(o_ref)


def _moe_mm(a, ws, li, plan, tn, mode):
    _, tile_expert, num_tiles, max_tiles, _, groups = plan
    k, n = ws[0].shape[-2], ws[0].shape[-1]
    rows = max_tiles * MOE_TM
    nw = len(ws)

    def tile(i, nt):
        return jnp.minimum(i, nt[0] - 1)

    if mode == "down":
        a_spec = pl.BlockSpec((k, MOE_TM), lambda j, i, te, nt, gr: (0, tile(i, nt)))
        out_spec = pl.BlockSpec((MOE_TM, tn // 2), lambda j, i, te, nt, gr: (i, j))
        out_shape = jax.ShapeDtypeStruct((rows, n // 2), jnp.uint32)
    else:
        a_spec = pl.BlockSpec((MOE_TM, k // 2), lambda j, i, te, nt, gr: (tile(i, nt), 0))
        out_spec = pl.BlockSpec((tn, MOE_TM), lambda j, i, te, nt, gr: (j, i))
        out_shape = jax.ShapeDtypeStruct((n, rows), BF16)
    return pl.pallas_call(
        functools.partial(_moe_mm_kernel, mode=mode, li=li, tn=tn),
        grid_spec=pltpu.PrefetchScalarGridSpec(
            num_scalar_prefetch=3,
            grid=(n // tn, max_tiles),
            in_specs=[a_spec] + [pl.BlockSpec(memory_space=pl.ANY)] * nw,
            out_specs=out_spec,
            scratch_shapes=([pltpu.VMEM((k, tn), BF16)] * nw + [pltpu.VMEM((k, tn), F32)] * nw
                            + [pltpu.SemaphoreType.DMA((nw,))]),
        ),
        out_shape=out_shape,
        compiler_params=_params("arbitrary", "arbitrary"),
        name=f"moe_mm_{mode}",
    )(tile_expert, num_tiles, groups, a, *ws)


def _moe_combine_kernel(pos_ref, x_ref, g_ref, wts_ref, y_ref, *rest, final):
    if final:
        ng_ref, o_ref, ybuf_ref, sem = rest
    else:
        ng_ref, sh_ref, sc_ref, o_ref, h_ref, ybuf_ref, sem = rest

    def make(k, t, p):
        return pltpu.make_async_copy(y_ref.at[pl.ds(p, 1), :], ybuf_ref.at[k, pl.ds(t, 1), :], sem)

    _issue_and_drain(pos_ref, make)

    def rows(k):
        lo, hi = _unpack_pairs(ybuf_ref[k])
        hw = MOE_DOWN_TN // 2
        parts = []
        for c in range(lo.shape[1] // hw):
            parts += [lo[:, c * hw:(c + 1) * hw], hi[:, c * hw:(c + 1) * hw]]
        return jnp.concatenate(parts, axis=1)

    f = wts_ref[:, 0:1] * rows(0) + wts_ref[:, 1:2] * rows(1)
    xn = x_ref[...] + g_ref[...] * f
    yn = _rms(xn, ng_ref[...])
    if final:
        o_ref[...] = yn
    else:
        o_ref[...] = xn
        h_ref[...] = (yn * (1.0 + sc_ref[...]) + sh_ref[...]).astype(h_ref.dtype)


def _moe_combine(x, gate, wts, y, pos, seq, norm_g, shift=None, scale=None):
    m, d = x.shape
    nb = gate.shape[0]
    ne = wts.shape[-1]
    final = shift is None
    vec = pl.BlockSpec((None, 1, d), lambda i: (i * MOE_TT // seq, 0, 0))
    row = pl.BlockSpec((MOE_TT, d), lambda i: (i, 0))
    in_specs = [
        pl.BlockSpec((None, 1, 2 * MOE_TT), lambda i: (i, 0, 0), memory_space=pltpu.SMEM),
        row, vec,
        pl.BlockSpec((MOE_TT, ne), lambda i: (i, 0)),
        pl.BlockSpec(memory_space=pl.ANY),
        pl.BlockSpec((1, d), lambda i: (0, 0)),
    ]
    args = [pos, x, gate.reshape(nb, 1, d), wts, y, norm_g.reshape(1, d)]
    if final:
        out_specs, out_shape = row, jax.ShapeDtypeStruct((m, d), F32)
    else:
        in_specs += [vec, vec]
        args += [shift.reshape(nb, 1, d), scale.reshape(nb, 1, d)]
        out_specs = [row, row]
        out_shape = [jax.ShapeDtypeStruct((m, d), F32), jax.ShapeDtypeStruct((m, d), BF16)]
    return pl.pallas_call(
        functools.partial(_moe_combine_kernel, final=final),
        grid=(m // MOE_TT,),
        in_specs=in_specs,
        out_specs=out_specs,
        out_shape=out_shape,
        scratch_shapes=[pltpu.VMEM((2, MOE_TT, d // 2), jnp.uint32), pltpu.SemaphoreType.DMA(())],
        compiler_params=_params("arbitrary"),
        name="moe_combine",
    )(*args)


def kernel(x, c, ada_w, ada_b, norm_mix_g, norm_ffn_g, final_norm_g, w_in, ssm_a_re, ssm_a_im, ssm_log_dt, ssm_b_re, ssm_b_im, ssm_c_re, ssm_c_im, ssm_d, w_glu, b_glu, w_branch, w_out, ffn_w1, ffn_w3, ffn_w2, router_w, router_b, moe_w1, moe_w3, moe_w2):
    batch, seq, d = x.shape
    depth = ada_w.shape[0]
    tokens = batch * seq
    ssm_width = w_glu.shape[1]
    per_res = seq // TOKEN_RESIDUES
    tn_in = ATTN_OUT
    assert ssm_width == tn_in and w_in.shape[-1] == (3 * N_GROUPS + 1) * tn_in + 2 * d

    def to_residue_major(t):
        f = t.shape[-1]
        return t.reshape(batch, per_res, TOKEN_RESIDUES, f).transpose(0, 2, 1, 3).reshape(batch, seq, f)

    mod = _ada_modulation(c, ada_w, ada_b).reshape(depth, batch, 6, d)
    x2 = to_residue_major(x).reshape(tokens, d)
    h = None
    out = None
    for layer in range(depth):
        sh1, sc1, g1, sh2, sc2, g2 = (mod[layer, :, i] for i in range(6))

        if h is None:
            h = _norm_mod(x2.reshape(batch, seq, d), norm_mix_g[layer], sh1, sc1)
        qkv0 = _mm_plain(h, w_in, (layer,), lambda j: N_GROUPS * j, 3, F32, tm=1024, tn=tn_in)
        rest = _mm_plain(h, w_in, (layer,),
                         lambda j: j + 1 + jnp.where(j >= 2, 1, 0) + jnp.where(j >= 4, 1, 0),
                         11, BF16, tm=2048, tn=tn_in)
        attn = _dilation_mixture_attention(qkv0, rest, batch, seq)
        tables = _ssm_tables(ssm_a_re[layer], ssm_a_im[layer], ssm_log_dt[layer],
                             ssm_b_re[layer], ssm_b_im[layer], ssm_c_re[layer], ssm_c_im[layer])
        ssm = _s5_ssm(rest, 6, batch, seq, tables, ssm_d[layer], w_glu[layer], b_glu[layer])
        merged = _mm_branch(attn, ssm, w_branch, layer, rest, 7 * tn_in, tm=1024, tn=1024)
        h = None

        li = layer // 2
        if layer % 2 == 0:
            x2, hf = _mm_resid_norm(merged, w_out, layer, x2, g1, norm_ffn_g[layer], sh2, sc2,
                                    seq, tm=256)
            act = _mm_swiglu(hf, ffn_w1, ffn_w3, (li,), tm=2048, tn=512)
            x2 = _mm_resid(act, ffn_w2, (li,), x2, g2, seq, tm=512, tn=512)
        else:
            x2 = _mm_resid(merged, w_out, (layer,), x2, g1, seq, tm=1024, tn=1024)
            x3 = x2.reshape(batch, seq, d)
            sel, wts, cnt = _router(x3, norm_ffn_g[layer], sh2, sc2, router_w[li], router_b[li])
            plan = _moe_plan(sel, cnt, tokens)
            xs = _moe_dispatch(x3, norm_ffn_g[layer], sh2, sc2, plan)
            act = _moe_mm(xs, (moe_w1, moe_w3), li, plan, MOE_UP_TN, "up")
            y = _moe_mm(act, (moe_w2,), li, plan, MOE_DOWN_TN, "down")
            if layer + 1 < depth:
                x2, h = _moe_combine(x2, g2, wts, y, plan[0], seq, norm_mix_g[layer + 1],
                                     mod[layer + 1, :, 0], mod[layer + 1, :, 1])
            else:
                out = _moe_combine(x2, g2, wts, y, plan[0], seq, final_norm_g)
    if out is None:
        out = _final_norm(x2.reshape(batch, seq, d), final_norm_g)
    return out.reshape(batch, TOKEN_RESIDUES, per_res, d).transpose(0, 2, 1, 3).reshape(batch, seq, d)
```

```python
import functools
import math

import jax
import jax.numpy as jnp
from jax import lax
from jax.experimental import pallas as pl
from jax.experimental.pallas import tpu as pltpu

F32 = jnp.float32
BF16 = jnp.bfloat16

LANES = 128
SUBLANES = 8
VMEM_LIMIT_BYTES = 56 * 1024 * 1024

ATTN_PATTERNS = ((128, 1), (512, 4), (2048, 16))
N_GROUPS = len(ATTN_PATTERNS)
HEADS = 8
HEAD_DIM = 128
ATTN_OUT = HEADS * HEAD_DIM
QKV_WIDTH = N_GROUPS * ATTN_OUT
ATTN_BLOCK = 128
SSM_GROUP_CH = 16
SSM_STATE = 64
N_EXPERTS = 8
RMS_EPS = 1e-6
NEG_BIG = -1e30


def _params(*semantics):
    return pltpu.CompilerParams(dimension_semantics=semantics,
                                vmem_limit_bytes=VMEM_LIMIT_BYTES)


def _sigmoid(v):
    return 1.0 / (1.0 + jnp.exp(-v))


ADA_KR = 256
ADA_CN = 1536


def _ada_kernel(ct_ref, w_ref, o_ref):
    kstep = pl.program_id(1)
    kr = w_ref.shape[0]
    ct = ct_ref[pl.ds(pl.multiple_of(kstep * kr, kr), kr), :]
    act = (ct * _sigmoid(ct)).astype(BF16)

    @pl.when(kstep == 0)
    def _():
        o_ref[...] = jnp.zeros_like(o_ref)

    for c0 in range(0, w_ref.shape[1], ADA_CN):
        o_ref[c0:c0 + ADA_CN, :] += lax.dot_general(
            w_ref[:, c0:c0 + ADA_CN].astype(BF16), act, (((0,), (0,)), ((), ())),
            preferred_element_type=F32)


def _ada_modulation(c, ada_w, ada_b):
    depth, k, n = ada_w.shape
    nb = c.shape[0]
    ct = jnp.zeros((k, LANES), F32).at[:, :nb].set(c.T)
    out_t = pl.pallas_call(
        _ada_kernel,
        grid=(depth, k // ADA_KR),
        in_specs=[
            pl.BlockSpec((k, LANES), lambda l, i: (0, 0)),
            pl.BlockSpec((None, ADA_KR, n), lambda l, i: (l, i, 0)),
        ],
        out_specs=pl.BlockSpec((None, n, LANES), lambda l, i: (l, 0, 0)),
        out_shape=jax.ShapeDtypeStruct((depth, n, LANES), F32),
        compiler_params=_params("arbitrary", "arbitrary"),
        name="ada_modulation",
    )(ct, ada_w)
    return out_t[:, :, :nb].transpose(0, 2, 1) + ada_b[:, None, :]


NORM_TS = 512


def _rms(x, g):
    return x * lax.rsqrt(jnp.mean(x * x, axis=-1, keepdims=True) + RMS_EPS) * g


def _norm_mod_kernel(x_ref, g_ref, sh_ref, sc_ref, o_ref):
    y = _rms(x_ref[...], g_ref[...])
    o_ref[...] = (y * (1.0 + sc_ref[...]) + sh_ref[...]).astype(o_ref.dtype)


def _norm_mod(x, g, shift, scale):
    b, s, d = x.shape
    vec = pl.BlockSpec((None, 1, d), lambda i, j: (i, 0, 0))
    out = pl.pallas_call(
        _norm_mod_kernel,
        grid=(b, s // NORM_TS),
        in_specs=[
            pl.BlockSpec((None, NORM_TS, d), lambda i, j: (i, j, 0)),
            pl.BlockSpec((1, d), lambda i, j: (0, 0)),
            vec, vec,
        ],
        out_specs=pl.BlockSpec((None, NORM_TS, d), lambda i, j: (i, j, 0)),
        out_shape=jax.ShapeDtypeStruct((b, s, d), BF16),
        compiler_params=_params("arbitrary", "arbitrary"),
        name="norm_mod",
    )(x, g.reshape(1, d), shift.reshape(b, 1, d), scale.reshape(b, 1, d))
    return out.reshape(b * s, d)


def _final_norm_kernel(x_ref, g_ref, o_ref):
    o_ref[...] = _rms(x_ref[...], g_ref[...])


def _final_norm(x, g):
    b, s, d = x.shape
    return pl.pallas_call(
        _final_norm_kernel,
        grid=(b, s // NORM_TS),
        in_specs=[
            pl.BlockSpec((None, NORM_TS, d), lambda i, j: (i, j, 0)),
            pl.BlockSpec((1, d), lambda i, j: (0, 0)),
        ],
        out_specs=pl.BlockSpec((None, NORM_TS, d), lambda i, j: (i, j, 0)),
        out_shape=jax.ShapeDtypeStruct((b, s, d), F32),
        compiler_params=_params("arbitrary", "arbitrary"),
        name="final_norm",
    )(x, g.reshape(1, d))


CAST_ROWS = 256


def _cast_weight(w_ref, wb_ref):
    def body(i, carry):
        r = pl.multiple_of(i * CAST_ROWS, CAST_ROWS)
        wb_ref[pl.ds(r, CAST_ROWS), :] = w_ref[pl.ds(r, CAST_ROWS), :].astype(BF16)
        return carry
    lax.fori_loop(0, w_ref.shape[0] // CAST_ROWS, body, 0)


def _dot(a, b):
    return jnp.dot(a, b, preferred_element_type=F32)


def _mm_plain_kernel(a_ref, w_ref, o_ref, wb_ref):
    @pl.when(pl.program_id(1) == 0)
    def _():
        _cast_weight(w_ref, wb_ref)
    o_ref[...] = _dot(a_ref[...], wb_ref[...]).astype(o_ref.dtype)


def _mm_plain(a, w, w_idx, col_tile, n_tiles, out_dtype, tm, tn):
    m, k = a.shape
    lead = (None,) * len(w_idx)
    return pl.pallas_call(
        _mm_plain_kernel,
        grid=(n_tiles, m // tm),
        in_specs=[
            pl.BlockSpec((tm, k), lambda j, i: (i, 0)),
            pl.BlockSpec(lead + (k, tn), lambda j, i: w_idx + (0, col_tile(j))),
        ],
        out_specs=pl.BlockSpec((tm, tn), lambda j, i: (i, j)),
        out_shape=jax.ShapeDtypeStruct((m, n_tiles * tn), out_dtype),
        scratch_shapes=[pltpu.VMEM((k, tn), BF16)],
        compiler_params=_params("arbitrary", "arbitrary"),
        name="mm_plain",
    )(a, w)


def _mm_swiglu_kernel(a_ref, w1_ref, w3_ref, o_ref, wb1_ref, wb3_ref):
    @pl.when(pl.program_id(1) == 0)
    def _():
        _cast_weight(w1_ref, wb1_ref)
        _cast_weight(w3_ref, wb3_ref)
    a = a_ref[...]
    u = _dot(a, wb1_ref[...])
    v = _dot(a, wb3_ref[...])
    o_ref[...] = (u * _sigmoid(u) * v).astype(o_ref.dtype)


def _mm_swiglu(a, w1, w3, w_idx, tm, tn):
    m, k = a.shape
    n = w1.shape[-1]
    lead = (None,) * len(w_idx)
    wspec = pl.BlockSpec(lead + (k, tn), lambda j, i: w_idx + (0, j))
    return pl.pallas_call(
        _mm_swiglu_kernel,
        grid=(n // tn, m // tm),
        in_specs=[pl.BlockSpec((tm, k), lambda j, i: (i, 0)), wspec, wspec],
        out_specs=pl.BlockSpec((tm, tn), lambda j, i: (i, j)),
        out_shape=jax.ShapeDtypeStruct((m, n), BF16),
        scratch_shapes=[pltpu.VMEM((k, tn), BF16), pltpu.VMEM((k, tn), BF16)],
        compiler_params=_params("arbitrary", "arbitrary"),
        name="mm_swiglu",
    )(a, w1, w3)


def _mm_resid_kernel(a_ref, w_ref, x_ref, g_ref, o_ref, wb_ref):
    @pl.when(pl.program_id(1) == 0)
    def _():
        _cast_weight(w_ref, wb_ref)
    o_ref[...] = x_ref[...] + g_ref[...] * _dot(a_ref[...], wb_ref[...])


def _mm_resid(a, w, w_idx, x, gate, seq, tm, tn):
    m, k = a.shape
    n = w.shape[-1]
    nb = gate.shape[0]
    lead = (None,) * len(w_idx)
    return pl.pallas_call(
        _mm_resid_kernel,
        grid=(n // tn, m // tm),
        in_specs=[
            pl.BlockSpec((tm, k), lambda j, i: (i, 0)),
            pl.BlockSpec(lead + (k, tn), lambda j, i: w_idx + (0, j)),
            pl.BlockSpec((tm, tn), lambda j, i: (i, j)),
            pl.BlockSpec((None, 1, tn), lambda j, i: (i * tm // seq, 0, j)),
        ],
        out_specs=pl.BlockSpec((tm, tn), lambda j, i: (i, j)),
        out_shape=jax.ShapeDtypeStruct((m, n), F32),
        scratch_shapes=[pltpu.VMEM((k, tn), BF16)],
        compiler_params=_params("arbitrary", "arbitrary"),
        name="mm_resid",
    )(a, w, x, gate.reshape(nb, 1, n))


def _mm_resid_norm_kernel(a_ref, w_ref, x_ref, g_ref, ng_ref, sh_ref, sc_ref,
                          xo_ref, ho_ref, wb_ref):
    @pl.when(pl.program_id(0) == 0)
    def _():
        _cast_weight(w_ref, wb_ref)
    xn = x_ref[...] + g_ref[...] * _dot(a_ref[...], wb_ref[...])
    xo_ref[...] = xn
    yn = _rms(xn, ng_ref[...])
    ho_ref[...] = (yn * (1.0 + sc_ref[...]) + sh_ref[...]).astype(ho_ref.dtype)


def _mm_resid_norm(a, w, layer, x, gate, norm_g, shift, scale, seq, tm):
    m, k = a.shape
    n = w.shape[-1]
    nb = gate.shape[0]
    vec = pl.BlockSpec((None, 1, n), lambda i: (i * tm // seq, 0, 0))
    row = pl.BlockSpec((tm, n), lambda i: (i, 0))
    return pl.pallas_call(
        _mm_resid_norm_kernel,
        grid=(m // tm,),
        in_specs=[
            pl.BlockSpec((tm, k), lambda i: (i, 0)),
            pl.BlockSpec((None, k, n), lambda i: (layer, 0, 0), pipeline_mode=pl.Buffered(1)),
            row, vec,
            pl.BlockSpec((1, n), lambda i: (0, 0)),
            vec, vec,
        ],
        out_specs=[row, row],
        out_shape=[jax.ShapeDtypeStruct((m, n), F32), jax.ShapeDtypeStruct((m, n), BF16)],
        scratch_shapes=[pltpu.VMEM((k, n), BF16)],
        compiler_params=_params("arbitrary"),
        name="mm_resid_norm",
    )(a, w, x, gate.reshape(nb, 1, n), norm_g.reshape(1, n),
      shift.reshape(nb, 1, n), scale.reshape(nb, 1, n))


def _mm_branch_kernel(at_ref, ss_ref, wa_ref, ws_ref, ga_ref, gs_ref, o_ref,
                      wba_ref, wbs_ref):
    @pl.when(pl.program_id(1) == 0)
    def _():
        _cast_weight(wa_ref, wba_ref)
        _cast_weight(ws_ref, wbs_ref)
    pa = _dot(at_ref[...], wba_ref[...])
    ps = _dot(ss_ref[...], wbs_ref[...])
    ga = _sigmoid(ga_ref[...].astype(F32))
    gs = _sigmoid(gs_ref[...].astype(F32))
    o_ref[...] = (ga * pa + gs * ps).astype(o_ref.dtype)


def _mm_branch(attn, ssm, w_branch, layer, proj, gate_col, tm, tn):
    m, k = attn.shape
    n = w_branch.shape[-1]
    g0 = gate_col // tn
    g1 = (gate_col + n) // tn
    return pl.pallas_call(
        _mm_branch_kernel,
        grid=(n // tn, m // tm),
        in_specs=[
            pl.BlockSpec((tm, k), lambda j, i: (i, 0)),
            pl.BlockSpec((tm, k), lambda j, i: (i, 0)),
            pl.BlockSpec((None, k, tn), lambda j, i: (layer, 0, j)),
            pl.BlockSpec((None, k, tn), lambda j, i: (layer, 1, j)),
            pl.BlockSpec((tm, tn), lambda j, i: (i, g0 + j)),
            pl.BlockSpec((tm, tn), lambda j, i: (i, g1 + j)),
        ],
        out_specs=pl.BlockSpec((tm, tn), lambda j, i: (i, j)),
        out_shape=jax.ShapeDtypeStruct((m, n), BF16),
        scratch_shapes=[pltpu.VMEM((k, tn), BF16), pltpu.VMEM((k, tn), BF16)],
        compiler_params=_params("arbitrary", "arbitrary"),
        name="mm_branch",
    )(attn, ssm, w_branch, w_branch, proj, proj)


TOKEN_RESIDUES = 16
ATTN_BB = 4


def _dot_nt(a, b):
    return lax.dot_general(a, b, (((1,), (1,)), ((), ())), preferred_element_type=F32)


def _attn_kernel(*refs, with_prev):
    if with_prev:
        q_ref, kc_ref, vc_ref, bc_ref, bp_ref, o_ref, lse_ref, kp_ref, vp_ref = refs

        @pl.when(pl.program_id(2) == 0)
        def _():
            kp_ref[...] = jnp.zeros_like(kp_ref)
            vp_ref[...] = jnp.zeros_like(vp_ref)
    else:
        q_ref, kc_ref, vc_ref, bc_ref, o_ref, lse_ref = refs
    blk = ATTN_BLOCK
    bb, nc, rpc = q_ref.shape[0], q_ref.shape[1], q_ref.shape[2]
    scale = HEAD_DIM ** -0.5
    lane = lax.broadcasted_iota(jnp.int32, (blk, LANES), 1)

    def heads(ref, bi):
        x = ref[bi].reshape(blk, ATTN_OUT)
        return jnp.stack([x[:, h * HEAD_DIM:(h + 1) * HEAD_DIM] for h in range(HEADS)]
                         ).astype(BF16)

    def qk(q, k):
        return lax.dot_general(q, k, (((2,), (2,)), ((0,), (0,))), preferred_element_type=F32)

    def pv(p, v):
        return lax.dot_general(p.astype(BF16), v, (((2,), (1,)), ((0,), (0,))),
                               preferred_element_type=F32)

    for bi in range(bb):
        q = heads(q_ref, bi)
        k_c = heads(kc_ref, bi)
        v_c = heads(vc_ref, bi)
        s_c = qk(q, k_c) * scale + bc_ref[...]
        m = jnp.max(s_c, axis=2, keepdims=True)
        if with_prev:
            bias_p = jnp.where(pl.program_id(2) > 0, bp_ref[...], NEG_BIG)
            s_p = qk(q, kp_ref[bi]) * scale + bias_p
            m = jnp.maximum(m, jnp.max(s_p, axis=2, keepdims=True))
        p_c = jnp.exp(s_c - m)
        l = jnp.sum(p_c, axis=2, keepdims=True)
        o = pv(p_c, v_c)
        if with_prev:
            p_p = jnp.exp(s_p - m)
            l = l + jnp.sum(p_p, axis=2, keepdims=True)
            o = o + pv(p_p, vp_ref[bi])
            kp_ref[bi] = k_c
            vp_ref[bi] = v_c
        o = o / l
        lse = m + jnp.log(l)
        lse_tile = jnp.zeros((blk, LANES), F32)
        for h in range(HEADS):
            sl = slice(h * HEAD_DIM, (h + 1) * HEAD_DIM)
            o_ref[bi, :, :, sl] = o[h].reshape(nc, rpc, HEAD_DIM).astype(o_ref.dtype)
            lse_tile = jnp.where(lane == h, lse[h], lse_tile)
        lse_ref[bi] = lse_tile.reshape(nc, rpc, LANES)


def _attn_bias(dilation, nc, back):
    rpc = ATTN_BLOCK // nc
    i = jnp.arange(ATTN_BLOCK, dtype=jnp.int32)
    off = nc * (i % rpc) + i // rpc
    dist = off[:, None] - off[None, :] + back * ATTN_BLOCK
    valid = jnp.logical_and(dist >= 0, dist <= ATTN_BLOCK)
    slopes = 2.0 ** (-8.0 * (jnp.arange(HEADS, dtype=F32) + 1.0) / HEADS)
    pen = slopes[:, None, None] * (dist * dilation).astype(F32)[None]
    return jnp.where(valid[None], -pen, NEG_BIG)


def _attention_group(qkv, qkv_cols, batch, seq, gi):
    window, dilation = ATTN_PATTERNS[gi]
    assert window // dilation == ATTN_BLOCK
    cols = qkv.shape[-1]
    per_res = seq // TOKEN_RESIDUES
    nc = TOKEN_RESIDUES // dilation
    rpc = ATTN_BLOCK // nc
    nblk = seq // dilation // ATTN_BLOCK
    with_prev = nblk > 1

    def shape5(c):
        return (batch, nc, dilation, per_res, c)

    def spec(c, col):
        return pl.BlockSpec((ATTN_BB, nc, None, rpc, c), lambda b, r, n: (b, 0, r, n, col))

    bias_spec = pl.BlockSpec((HEADS, ATTN_BLOCK, ATTN_BLOCK), lambda b, r, n: (0, 0, 0))
    in_specs = [spec(ATTN_OUT, col) for col in qkv_cols]
    carry = pltpu.VMEM((ATTN_BB, HEADS, ATTN_BLOCK, HEAD_DIM), BF16)
    view = qkv.reshape(shape5(cols))
    args = [view] * len(in_specs) + [_attn_bias(dilation, nc, 0)]
    in_specs.append(bias_spec)
    if with_prev:
        args.append(_attn_bias(dilation, nc, 1))
        in_specs.append(bias_spec)
    o, lse = pl.pallas_call(
        functools.partial(_attn_kernel, with_prev=with_prev),
        grid=(batch // ATTN_BB, dilation, nblk),
        in_specs=in_specs,
        out_specs=[spec(ATTN_OUT, 0), spec(LANES, 0)],
        out_shape=[
            jax.ShapeDtypeStruct(shape5(ATTN_OUT), qkv.dtype),
            jax.ShapeDtypeStruct(shape5(LANES), F32),
        ],
        scratch_shapes=[carry, carry] if with_prev else [],
        compiler_params=_params("arbitrary", "arbitrary", "arbitrary"),
        name=f"attn_g{gi}",
    )(*args)
    return o.reshape(batch * seq, ATTN_OUT), lse.reshape(batch * seq, LANES)


COMBINE_TS = 512


def _combine_kernel(o0_ref, o1_ref, o2_ref, l0_ref, l1_ref, l2_ref, out_ref):
    a0, a1, a2 = l0_ref[...], l1_ref[...], l2_ref[...]
    m = jnp.maximum(jnp.maximum(a0, a1), a2)
    e0 = jnp.exp(a0 - m)
    e1 = jnp.exp(a1 - m)
    e2 = jnp.exp(a2 - m)
    inv = 1.0 / (e0 + e1 + e2)
    w0, w1, w2 = e0 * inv, e1 * inv, e2 * inv
    for h in range(HEADS):
        sl = slice(h * HEAD_DIM, (h + 1) * HEAD_DIM)
        mix = (w0[:, h:h + 1] * o0_ref[:, sl].astype(F32)
               + w1[:, h:h + 1] * o1_ref[:, sl].astype(F32)
               + w2[:, h:h + 1] * o2_ref[:, sl].astype(F32))
        out_ref[:, sl] = mix.astype(out_ref.dtype)


def _dilation_mixture_attention(qkv0, rest, batch, seq):
    outs, lses = [], []
    for gi in range(N_GROUPS):
        if gi == 0:
            o, lse = _attention_group(qkv0, (0, 1, 2), batch, seq, gi)
        else:
            o, lse = _attention_group(rest, (gi - 1, gi + 1, gi + 3), batch, seq, gi)
        outs.append(o)
        lses.append(lse)
    m = batch * seq
    ospec = pl.BlockSpec((COMBINE_TS, ATTN_OUT), lambda i: (i, 0))
    lspec = pl.BlockSpec((COMBINE_TS, LANES), lambda i: (i, 0))
    return pl.pallas_call(
        _combine_kernel,
        grid=(m // COMBINE_TS,),
        in_specs=[ospec] * 3 + [lspec] * 3,
        out_specs=ospec,
        out_shape=jax.ShapeDtypeStruct((m, ATTN_OUT), BF16),
        compiler_params=_params("arbitrary"),
        name="attn_combine",
    )(*outs, *lses)


SSM_LT = 128
SSM_PASSES = 2
SSM_JBLK = 256


def _gelu_tanh(y):
    return 0.5 * y * (1.0 + jnp.tanh(math.sqrt(2.0 / math.pi) * (y + 0.044715 * (y * y * y))))


def _ssm_kernel(u_ref, pin_ref, pout_ref, bblk_ref, cblk_ref, are_ref, aim_ref, d_ref,
                wglu_ref, bglu_ref, o_ref, sre_ref, sim_ref, hre_ref, him_ref, *, nbatch):
    width = u_ref.shape[-1]
    rows = pin_ref.shape[1]
    njb = width // SSM_JBLK
    jstates = bblk_ref.shape[2] // 2
    pairs = rows // SUBLANES
    per_tile = SUBLANES // nbatch

    @pl.when(pl.program_id(0) == 0)
    def _():
        hre_ref[...] = jnp.zeros_like(hre_ref)
        him_ref[...] = jnp.zeros_like(him_ref)

    u_rm = u_ref[...].reshape(pout_ref.shape[1], width)
    out_rm = None
    for p in range(pin_ref.shape[0]):
        u_f32 = _dot(pin_ref[p], u_rm)
        u_tb = u_f32.astype(BF16)
        ys = []
        for j in range(njb):
            bu = _dot(u_tb[:, j * SSM_JBLK:(j + 1) * SSM_JBLK], bblk_ref[j])
            sre_ref[...] = bu[:, :jstates].reshape(pairs, SUBLANES, jstates)
            sim_ref[...] = bu[:, jstates:].reshape(pairs, SUBLANES, jstates)
            cs = slice(j * jstates, (j + 1) * jstates)
            ar = are_ref[0:nbatch, cs]
            ai = aim_ref[0:nbatch, cs]

            def step(k, carry):
                hr, hi = carry
                for t in range(per_tile):
                    rs = slice(t * nbatch, (t + 1) * nbatch)
                    nr = ar * hr - ai * hi + sre_ref[k, rs, :]
                    ni = ar * hi + ai * hr + sim_ref[k, rs, :]
                    sre_ref[k, rs, :] = nr
                    sim_ref[k, rs, :] = ni
                    hr, hi = nr, ni
                return hr, hi

            hr, hi = lax.fori_loop(0, pairs, step,
                                   (hre_ref[0:nbatch, cs], him_ref[0:nbatch, cs]), unroll=2)
            hre_ref[0:nbatch, cs] = hr
            him_ref[0:nbatch, cs] = hi

            h_re = sre_ref[...].reshape(rows, jstates).astype(BF16)
            h_im = sim_ref[...].reshape(rows, jstates).astype(BF16)
            ys.append(_dot(h_re, cblk_ref[j, 0:jstates, :]) + _dot(h_im, cblk_ref[j, jstates:, :]))

        y = jnp.concatenate(ys, axis=1) + d_ref[...] * u_f32
        z = _gelu_tanh(y).astype(BF16)
        g = _dot(z, wglu_ref[...]) + bglu_ref[...]
        o_tb = (g[:, :width] * _sigmoid(g[:, width:])).astype(BF16)
        back = _dot(pout_ref[p], o_tb)
        out_rm = back if out_rm is None else out_rm + back
    o_ref[...] = out_rm.astype(o_ref.dtype).reshape(o_ref.shape)


def _ssm_tables(a_re, a_im, log_dt, b_re, b_im, c_re, c_im):
    groups, nst = a_re.shape
    gpb = SSM_JBLK // SSM_GROUP_CH
    njb = groups // gpb
    lam = lax.complex(a_re.astype(F32), a_im.astype(F32))
    dt = jnp.exp(log_dt.astype(F32))[:, None]
    a_bar = jnp.exp(lam * dt)
    b_mat = lax.complex(b_re.astype(F32), b_im.astype(F32))
    b_bar = ((a_bar - 1.0) / lam)[:, :, None] * b_mat
    eye = jnp.eye(gpb, dtype=F32)

    def in_blocks(t):
        t = t.reshape(njb, gpb, nst, SSM_GROUP_CH)
        return jnp.einsum('jgnc,gh->jgchn', t, eye).reshape(njb, gpb * SSM_GROUP_CH, gpb * nst)

    def out_blocks(t):
        t = t.reshape(njb, gpb, SSM_GROUP_CH, nst)
        return jnp.einsum('jgcn,gh->jhngc', t, eye).reshape(njb, gpb * nst, gpb * SSM_GROUP_CH)

    bblk = jnp.concatenate([in_blocks(jnp.real(b_bar)), in_blocks(jnp.imag(b_bar))], axis=2)
    cblk = jnp.concatenate([out_blocks(c_re.astype(F32)), out_blocks(-c_im.astype(F32))], axis=1)
    are = jnp.broadcast_to(jnp.real(a_bar).reshape(1, groups * nst), (SUBLANES, groups * nst))
    aim = jnp.broadcast_to(jnp.imag(a_bar).reshape(1, groups * nst), (SUBLANES, groups * nst))
    return bblk.astype(BF16), cblk.astype(BF16), are, aim


def _ssm_row_perms(nbatch):
    per_res = SSM_PASSES * SSM_LT // TOKEN_RESIDUES
    col = jnp.arange(nbatch * TOKEN_RESIDUES * per_res, dtype=jnp.int32)
    b = col // (TOKEN_RESIDUES * per_res)
    t_local = TOKEN_RESIDUES * (col % per_res) + (col // per_res) % TOKEN_RESIDUES
    row = (t_local % SSM_LT) * nbatch + b
    pin = jnp.logical_and(
        (t_local // SSM_LT)[None, None, :] == jnp.arange(SSM_PASSES, dtype=jnp.int32)[:, None, None],
        row[None, None, :] == jnp.arange(SSM_LT * nbatch, dtype=jnp.int32)[None, :, None])
    pin = pin.astype(BF16)
    return pin, pin.transpose(0, 2, 1)


def _s5_ssm(src, u_col, nbatch, seq, tables, d_skip, w_glu, b_glu):
    bblk, cblk, are, aim = tables
    width = w_glu.shape[0]
    nstate = are.shape[1]
    jstates = bblk.shape[2] // 2
    rows = SSM_LT * nbatch
    step_rows = SSM_PASSES * SSM_LT // TOKEN_RESIDUES
    per_res = seq // TOKEN_RESIDUES
    pin, pout = _ssm_row_perms(nbatch)
    d2 = d_skip.reshape(1, width).astype(F32)
    wg = w_glu.astype(BF16)
    bg = b_glu.reshape(1, 2 * width).astype(F32)

    def const(a):
        return pl.BlockSpec(a.shape, lambda i: (0,) * a.ndim, pipeline_mode=pl.Buffered(1))

    blk = (nbatch, TOKEN_RESIDUES, step_rows, width)
    out = pl.pallas_call(
        functools.partial(_ssm_kernel, nbatch=nbatch),
        grid=(per_res // step_rows,),
        in_specs=[pl.BlockSpec(blk, lambda i: (0, 0, i, u_col)),
                  const(pin), const(pout), const(bblk), const(cblk), const(are), const(aim),
                  const(d2), const(wg), const(bg)],
        out_specs=pl.BlockSpec(blk, lambda i: (0, 0, i, 0)),
        out_shape=jax.ShapeDtypeStruct((nbatch, TOKEN_RESIDUES, per_res, width), BF16),
        scratch_shapes=[
            pltpu.VMEM((rows // SUBLANES, SUBLANES, jstates), F32),
            pltpu.VMEM((rows // SUBLANES, SUBLANES, jstates), F32),
            pltpu.VMEM((SUBLANES, nstate), F32),
            pltpu.VMEM((SUBLANES, nstate), F32),
        ],
        compiler_params=_params("arbitrary"),
        name="s5_ssm",
    )(src.reshape(nbatch, TOKEN_RESIDUES, per_res, src.shape[-1]),
      pin, pout, bblk, cblk, are, aim, d2, wg, bg)
    return out.reshape(nbatch * seq, width)


ROUTER_TS = 256


def _router_kernel(x_ref, g_ref, sh_ref, sc_ref, rw_ref, rb_ref,
                   sel_ref, wts_ref, cnt_ref, carry_ref):
    @pl.when(jnp.logical_and(pl.program_id(0) == 0, pl.program_id(1) == 0))
    def _():
        carry_ref[...] = jnp.zeros_like(carry_ref)

    h = _rms(x_ref[...], g_ref[...]) * (1.0 + sc_ref[...]) + sh_ref[...]
    ts, ne = h.shape[0], rw_ref.shape[0]
    idx = lax.broadcasted_iota(jnp.int32, (ts, ne), 1)
    logits = jnp.broadcast_to(rb_ref[...], (ts, ne))
    for ex in range(ne):
        col = jnp.sum(h * rw_ref[ex:ex + 1, :], axis=1, keepdims=True)
        logits = logits + jnp.where(idx == ex, col, 0.0)
    m1 = jnp.max(logits, axis=1, keepdims=True)
    i1 = jnp.min(jnp.where(logits == m1, idx, ne), axis=1, keepdims=True)
    rest = jnp.where(idx == i1, -jnp.inf, logits)
    m2 = jnp.max(rest, axis=1, keepdims=True)
    i2 = jnp.min(jnp.where(rest == m2, idx, ne), axis=1, keepdims=True)
    e = jnp.exp(m2 - m1)
    w1 = 1.0 / (1.0 + e)
    w2 = e / (1.0 + e)

    onehot = jnp.where(idx == i1, 1.0, 0.0) + jnp.where(idx == i2, 1.0, 0.0)
    row = lax.broadcasted_iota(jnp.int32, (ts, ts), 0)
    col = lax.broadcasted_iota(jnp.int32, (ts, ts), 1)
    lower = jnp.where(col < row, 1.0, 0.0).astype(BF16)
    before = carry_ref[...] + _dot(lower, onehot.astype(BF16))
    r1 = jnp.sum(jnp.where(idx == i1, before, 0.0), axis=1, keepdims=True).astype(jnp.int32)
    r2 = jnp.sum(jnp.where(idx == i2, before, 0.0), axis=1, keepdims=True).astype(jnp.int32)
    total = carry_ref[...] + jnp.sum(onehot, axis=0, keepdims=True)
    carry_ref[...] = total
    cnt_ref[...] = total
    sel_ref[...] = jnp.where(idx == 0, i1, jnp.where(idx == 1, i2, jnp.where(
        idx == 2, r1, jnp.where(idx == 3, r2, 0))))
    wts_ref[...] = jnp.where(idx == 0, w1, jnp.where(idx == 1, w2, 0.0))


def _router(x, g, shift, scale, router_w, router_b):
    b, s, d = x.shape
    ne = router_w.shape[-1]
    vec = pl.BlockSpec((None, 1, d), lambda i, j: (i, 0, 0))
    tok = pl.BlockSpec((None, ROUTER_TS, ne), lambda i, j: (i, j, 0))
    sel, wts, cnt = pl.pallas_call(
        _router_kernel,
        grid=(b, s // ROUTER_TS),
        in_specs=[
            pl.BlockSpec((None, ROUTER_TS, d), lambda i, j: (i, j, 0)),
            pl.BlockSpec((1, d), lambda i, j: (0, 0)),
            vec, vec,
            pl.BlockSpec((ne, d), lambda i, j: (0, 0)),
            pl.BlockSpec((1, ne), lambda i, j: (0, 0)),
        ],
        out_specs=[tok, tok, pl.BlockSpec((1, ne), lambda i, j: (0, 0))],
        out_shape=[jax.ShapeDtypeStruct((b, s, ne), jnp.int32),
                   jax.ShapeDtypeStruct((b, s, ne), F32),
                   jax.ShapeDtypeStruct((1, ne), F32)],
        scratch_shapes=[pltpu.VMEM((1, ne), F32)],
        compiler_params=_params("arbitrary", "arbitrary"),
        name="router",
    )(x, g.reshape(1, d), shift.reshape(b, 1, d), scale.reshape(b, 1, d),
      router_w.T, router_b.reshape(1, ne))
    return sel.reshape(b * s, ne), wts.reshape(b * s, ne), cnt


MOE_TM = 256
MOE_TT = 256
MOE_UP_TN = 1408
MOE_DOWN_TN = 2048


def _moe_plan(sel, cnt, tokens):
    ne = cnt.shape[-1]
    counts = cnt[0].astype(jnp.int32)
    padded = (counts + MOE_TM - 1) // MOE_TM * MOE_TM
    ends = jnp.cumsum(padded)
    starts = ends - padded
    pos1 = starts[sel[:, 0]] + sel[:, 2]
    pos2 = starts[sel[:, 1]] + sel[:, 3]
    nt = tokens // MOE_TT
    pos = jnp.concatenate([pos1.reshape(nt, MOE_TT), pos2.reshape(nt, MOE_TT)], axis=1)
    max_tiles = 2 * tokens // MOE_TM + ne
    tile_start = jnp.arange(max_tiles, dtype=jnp.int32) * MOE_TM
    tile_expert = jnp.minimum(
        jnp.sum((tile_start[:, None] >= ends[None, :]).astype(jnp.int32), axis=1), ne - 1)
    num_tiles = (ends[-1] // MOE_TM).reshape(1)
    zero_tiles = jnp.concatenate([
        jnp.maximum(ends // MOE_TM - 1, 0),
        jnp.minimum(num_tiles[0] + jnp.arange(ne, dtype=jnp.int32), max_tiles - 1)])
    nonempty = counts > 0
    ar = jnp.arange(ne, dtype=jnp.int32)
    order = (ar[:, None] + 1 + ar[None, :]) % ne
    nxt = order[ar, jnp.argmax(nonempty[order], axis=1)].astype(jnp.int32)
    gidx = jnp.cumsum(nonempty.astype(jnp.int32)) - 1
    groups = jnp.concatenate([nxt, gidx, jnp.sum(nonempty.astype(jnp.int32)).reshape(1)])
    return (pos.reshape(nt, 1, 2 * MOE_TT), tile_expert, num_tiles, max_tiles, zero_tiles,
            groups)


def _row_copies(pos_ref, t, make):
    tt = pos_ref.shape[1] // 2
    return make(0, t, pos_ref[0, t]), make(1, t, pos_ref[0, tt + t])


def _issue_and_drain(pos_ref, make):
    tt = pos_ref.shape[1] // 2

    def issue(t, carry):
        for k, cp in enumerate(_row_copies(pos_ref, t, make)):
            cp.start(priority=k)
        return carry

    def drain(t, carry):
        for cp in _row_copies(pos_ref, t, make):
            cp.wait()
        return carry

    lax.fori_loop(0, tt, issue, 0, unroll=4)
    lax.fori_loop(0, tt, drain, 0, unroll=4)


def _pack_pairs(lo, hi):
    lo_bits = pltpu.bitcast(lo.astype(BF16).astype(F32), jnp.uint32)
    hi_bits = pltpu.bitcast(hi.astype(BF16).astype(F32), jnp.uint32)
    return jnp.bitwise_or(jnp.bitwise_and(hi_bits, jnp.uint32(0xFFFF0000)),
                          jnp.right_shift(lo_bits, jnp.uint32(16)))


def _unpack_pairs(words):
    lo = pltpu.bitcast(jnp.left_shift(words, jnp.uint32(16)), F32)
    hi = pltpu.bitcast(jnp.bitwise_and(words, jnp.uint32(0xFFFF0000)), F32)
    return lo, hi


def _moe_dispatch_kernel(pos_ref, zt_ref, x_ref, g_ref, sh_ref, sc_ref, xs_ref, hbuf_ref, sem):
    @pl.when(jnp.logical_and(pl.program_id(0) == 0, pl.program_id(1) == 0))
    def _():
        hbuf_ref[...] = jnp.zeros_like(hbuf_ref)
        for k in range(zt_ref.shape[0]):
            cp = pltpu.make_async_copy(
                hbuf_ref, xs_ref.at[pl.ds(pl.multiple_of(zt_ref[k] * MOE_TM, MOE_TM), MOE_TM), :],
                sem)
            cp.start()
            cp.wait()

    h = _rms(x_ref[...], g_ref[...]) * (1.0 + sc_ref[...]) + sh_ref[...]
    half = h.shape[1] // 2
    hbuf_ref[...] = _pack_pairs(h[:, :half], h[:, half:])

    def make(k, t, p):
        return pltpu.make_async_copy(hbuf_ref.at[pl.ds(t, 1), :], xs_ref.at[pl.ds(p, 1), :], sem)

    _issue_and_drain(pos_ref, make)


def _moe_dispatch(x, g, shift, scale, plan):
    pos, _, _, max_tiles, zero_tiles, _ = plan
    b, s, d = x.shape
    per_b = s // MOE_TT
    assert MOE_TT == MOE_TM
    vec = pl.BlockSpec((None, 1, d), lambda i, j: (i, 0, 0))
    return pl.pallas_call(
        _moe_dispatch_kernel,
        grid=(b, per_b),
        in_specs=[
            pl.BlockSpec((None, 1, 2 * MOE_TT), lambda i, j: (i * per_b + j, 0, 0),
                         memory_space=pltpu.SMEM),
            pl.BlockSpec(memory_space=pltpu.SMEM),
            pl.BlockSpec((None, MOE_TT, d), lambda i, j: (i, j, 0)),
            pl.BlockSpec((1, d), lambda i, j: (0, 0)),
            vec, vec,
        ],
        out_specs=pl.BlockSpec(memory_space=pl.ANY),
        out_shape=jax.ShapeDtypeStruct((max_tiles * MOE_TM, d // 2), jnp.uint32),
        scratch_shapes=[pltpu.VMEM((MOE_TT, d // 2), jnp.uint32), pltpu.SemaphoreType.DMA(())],
        compiler_params=_params("arbitrary", "arbitrary"),
        name="moe_dispatch",
    )(pos, zero_tiles, x, g.reshape(1, d), shift.reshape(b, 1, d), scale.reshape(b, 1, d))


def _moe_mm_kernel(te_ref, nt_ref, grp_ref, a_ref, *rest, mode, li, tn):
    nw = 2 if mode == "up" else 1
    w_hbm = rest[:nw]
    o_ref = rest[nw]
    wb = rest[nw + 1:2 * nw + 1]
    stage = rest[2 * nw + 1:3 * nw + 1]
    wsem = rest[3 * nw + 1]
    j = pl.program_id(0)
    i = pl.program_id(1)
    ne = (grp_ref.shape[0] - 1) // 2
    expert = te_ref[i]
    active = i < nt_ref[0]
    new_expert = jnp.logical_or(i == 0, expert != te_ref[jnp.maximum(i - 1, 0)])

    def weight_copies(ex, jx):
        cols = pl.ds(pl.multiple_of(jx * tn, LANES), tn)
        return [pltpu.make_async_copy(w_hbm[k].at[li, ex, :, cols], stage[k], wsem.at[k])
                for k in range(nw)]

    @pl.when(jnp.logical_and(active, new_expert))
    def _():
        @pl.when(jnp.logical_and(j == 0, grp_ref[ne + expert] == 0))
        def _():
            for cp in weight_copies(expert, j):
                cp.start()

        for cp in weight_copies(expert, j):
            cp.wait()
        for k in range(nw):
            _cast_weight(stage[k], wb[k])
        nxt = grp_ref[expert]
        nxt_j = j + jnp.where(nxt <= expert, 1, 0)

        @pl.when(nxt_j < pl.num_programs(0))
        def _():
            for cp in weight_copies(nxt, nxt_j):
                cp.start()

    @pl.when(active)
    def _():
        if mode == "down":
            p = lax.dot_general(wb[0][...], a_ref[...], (((0,), (0,)), ((), ())),
                                preferred_element_type=F32).T
            half = p.shape[1] // 2
            o_ref[...] = _pack_pairs(p[:, :half], p[:, half:])
        else:
            lo, hi = _unpack_pairs(a_ref[...])
            lo, hi = lo.astype(BF16), hi.astype(BF16)
            half = lo.shape[1]
            nt_dims = (((0,), (1,)), ((), ()))

            def proj(w_ref):
                return (lax.dot_general(w_ref[0:half, :], lo, nt_dims, preferred_element_type=F32)
                        + lax.dot_general(w_ref[half:, :], hi, nt_dims,
                                          preferred_element_type=F32))

            u = proj(wb[0])
            o_ref[...] = (u * _sigmoid(u) * proj(wb[1])).astype(o_ref.dtype)

    @pl.when(jnp.logical_not(active))
    def _():
        o_ref[...] = jnp.zeros_like(o_ref)


def _moe_mm(a, ws, li, plan, tn, mode):
    _, tile_expert, num_tiles, max_tiles, _, groups = plan
    k, n = ws[0].shape[-2], ws[0].shape[-1]
    rows = max_tiles * MOE_TM
    nw = len(ws)

    def tile(i, nt):
        return jnp.minimum(i, nt[0] - 1)

    if mode == "down":
        a_spec = pl.BlockSpec((k, MOE_TM), lambda j, i, te, nt, gr: (0, tile(i, nt)))
        out_spec = pl.BlockSpec((MOE_TM, tn // 2), lambda j, i, te, nt, gr: (i, j))
        out_shape = jax.ShapeDtypeStruct((rows, n // 2), jnp.uint32)
    else:
        a_spec = pl.BlockSpec((MOE_TM, k // 2), lambda j, i, te, nt, gr: (tile(i, nt), 0))
        out_spec = pl.BlockSpec((tn, MOE_TM), lambda j, i, te, nt, gr: (j, i))
        out_shape = jax.ShapeDtypeStruct((n, rows), BF16)
    return pl.pallas_call(
        functools.partial(_moe_mm_kernel, mode=mode, li=li, tn=tn),
        grid_spec=pltpu.PrefetchScalarGridSpec(
            num_scalar_prefetch=3,
            grid=(n // tn, max_tiles),
            in_specs=[a_spec] + [pl.BlockSpec(memory_space=pl.ANY)] * nw,
            out_specs=out_spec,
            scratch_shapes=([pltpu.VMEM((k, tn), BF16)] * nw + [pltpu.VMEM((k, tn), F32)] * nw
                            + [pltpu.SemaphoreType.DMA((nw,))]),
        ),
        out_shape=out_shape,
        compiler_params=_params("arbitrary", "arbitrary"),
        name=f"moe_mm_{mode}",
    )(tile_expert, num_tiles, groups, a, *ws)


def _moe_combine_kernel(pos_ref, x_ref, g_ref, wts_ref, y_ref, *rest, final):
    if final:
        ng_ref, o_ref, ybuf_ref, sem = rest
    else:
        ng_ref, sh_ref, sc_ref, o_ref, h_ref, ybuf_ref, sem = rest

    def make(k, t, p):
        return pltpu.make_async_copy(y_ref.at[pl.ds(p, 1), :], ybuf_ref.at[k, pl.ds(t, 1), :], sem)

    _issue_and_drain(pos_ref, make)

    def rows(k):
        lo, hi = _unpack_pairs(ybuf_ref[k])
        hw = MOE_DOWN_TN // 2
        parts = []
        for c in range(lo.shape[1] // hw):
            parts += [lo[:, c * hw:(c + 1) * hw], hi[:, c * hw:(c + 1) * hw]]
        return jnp.concatenate(parts, axis=1)

    f = wts_ref[:, 0:1] * rows(0) + wts_ref[:, 1:2] * rows(1)
    xn = x_ref[...] + g_ref[...] * f
    yn = _rms(xn, ng_ref[...])
    if final:
        o_ref[...] = yn
    else:
        o_ref[...] = xn
        h_ref[...] = (yn * (1.0 + sc_ref[...]) + sh_ref[...]).astype(h_ref.dtype)


def _moe_combine(x, gate, wts, y, pos, seq, norm_g, shift=None, scale=None):
    m, d = x.shape
    nb = gate.shape[0]
    ne = wts.shape[-1]
    final = shift is None
    vec = pl.BlockSpec((None, 1, d), lambda i: (i * MOE_TT // seq, 0, 0))
    row = pl.BlockSpec((MOE_TT, d), lambda i: (i, 0))
    in_specs = [
        pl.BlockSpec((None, 1, 2 * MOE_TT), lambda i: (i, 0, 0), memory_space=pltpu.SMEM),
        row, vec,
        pl.BlockSpec((MOE_TT, ne), lambda i: (i, 0)),
        pl.BlockSpec(memory_space=pl.ANY),
        pl.BlockSpec((1, d), lambda i: (0, 0)),
    ]
    args = [pos, x, gate.reshape(nb, 1, d), wts, y, norm_g.reshape(1, d)]
    if final:
        out_specs, out_shape = row, jax.ShapeDtypeStruct((m, d), F32)
    else:
        in_specs += [vec, vec]
        args += [shift.reshape(nb, 1, d), scale.reshape(nb, 1, d)]
        out_specs = [row, row]
        out_shape = [jax.ShapeDtypeStruct((m, d), F32), jax.ShapeDtypeStruct((m, d), BF16)]
    return pl.pallas_call(
        functools.partial(_moe_combine_kernel, final=final),
        grid=(m // MOE_TT,),
        in_specs=in_specs,
        out_specs=out_specs,
        out_shape=out_shape,
        scratch_shapes=[pltpu.VMEM((2, MOE_TT, d // 2), jnp.uint32), pltpu.SemaphoreType.DMA(())],
        compiler_params=_params("arbitrary"),
        name="moe_combine",
    )(*args)


def kernel(x, c, ada_w, ada_b, norm_mix_g, norm_ffn_g, final_norm_g, w_in, ssm_a_re, ssm_a_im, ssm_log_dt, ssm_b_re, ssm_b_im, ssm_c_re, ssm_c_im, ssm_d, w_glu, b_glu, w_branch, w_out, ffn_w1, ffn_w3, ffn_w2, router_w, router_b, moe_w1, moe_w3, moe_w2):
    batch, seq, d = x.shape
    depth = ada_w.shape[0]
    tokens = batch * seq
    ssm_width = w_glu.shape[1]
    per_res = seq // TOKEN_RESIDUES
    tn_in = ATTN_OUT
    assert ssm_width == tn_in and w_in.shape[-1] == (3 * N_GROUPS + 1) * tn_in + 2 * d

    def to_residue_major(t):
        f = t.shape[-1]
        return t.reshape(batch, per_res, TOKEN_RESIDUES, f).transpose(0, 2, 1, 3).reshape(batch, seq, f)

    mod = _ada_modulation(c, ada_w, ada_b).reshape(depth, batch, 6, d)
    x2 = to_residue_major(x).reshape(tokens, d)
    h = None
    out = None
    for layer in range(depth):
        sh1, sc1, g1, sh2, sc2, g2 = (mod[layer, :, i] for i in range(6))

        if h is None:
            h = _norm_mod(x2.reshape(batch, seq, d), norm_mix_g[layer], sh1, sc1)
        qkv0 = _mm_plain(h, w_in, (layer,), lambda j: N_GROUPS * j, 3, F32, tm=1024, tn=tn_in)
        rest = _mm_plain(h, w_in, (layer,),
                         lambda j: j + 1 + jnp.where(j >= 2, 1, 0) + jnp.where(j >= 4, 1, 0),
                         11, BF16, tm=2048, tn=tn_in)
        attn = _dilation_mixture_attention(qkv0, rest, batch, seq)
        tables = _ssm_tables(ssm_a_re[layer], ssm_a_im[layer], ssm_log_dt[layer],
                             ssm_b_re[layer], ssm_b_im[layer], ssm_c_re[layer], ssm_c_im[layer])
        ssm = _s5_ssm(rest, 6, batch, seq, tables, ssm_d[layer], w_glu[layer], b_glu[layer])
        merged = _mm_branch(attn, ssm, w_branch, layer, rest, 7 * tn_in, tm=1024, tn=1024)
        h = None

        li = layer // 2
        if layer % 2 == 0:
            x2, hf = _mm_resid_norm(merged, w_out, layer, x2, g1, norm_ffn_g[layer], sh2, sc2,
                                    seq, tm=256)
            act = _mm_swiglu(hf, ffn_w1, ffn_w3, (li,), tm=2048, tn=512)
            x2 = _mm_resid(act, ffn_w2, (li,), x2, g2, seq, tm=512, tn=512)
        else:
            x2 = _mm_resid(merged, w_out, (layer,), x2, g1, seq, tm=1024, tn=1024)
            x3 = x2.reshape(batch, seq, d)
            sel, wts, cnt = _router(x3, norm_ffn_g[layer], sh2, sc2, router_w[li], router_b[li])
            plan = _moe_plan(sel, cnt, tokens)
            xs = _moe_dispatch(x3, norm_ffn_g[layer], sh2, sc2, plan)
            act = _moe_mm(xs, (moe_w1, moe_w3), li, plan, MOE_UP_TN, "up")
            y = _moe_mm(act, (moe_w2,), li, plan, MOE_DOWN_TN, "down")
            if layer + 1 < depth:
                x2, h = _moe_combine(x2, g2, wts, y, plan[0], seq, norm_mix_g[layer + 1],
                                     mod[layer + 1, :, 0], mod[layer + 1, :, 1])
            else:
                out = _moe_combine(x2, g2, wts, y, plan[0], seq, final_norm_g)
    if out is None:
        out = _final_norm(x2.reshape(batch, seq, d), final_norm_g)
    return out.reshape(batch, TOKEN_RESIDUES, per_res, d).transpose(0, 2, 1, 3).reshape(batch, seq, d)
```

```python
import functools
import math

import jax
import jax.numpy as jnp
from jax import lax
from jax.experimental import pallas as pl
from jax.experimental.pallas import tpu as pltpu

F32 = jnp.float32
BF16 = jnp.bfloat16

LANES = 128
SUBLANES = 8
VMEM_LIMIT_BYTES = 56 * 1024 * 1024

ATTN_PATTERNS = ((128, 1), (512, 4), (2048, 16))
N_GROUPS = len(ATTN_PATTERNS)
HEADS = 8
HEAD_DIM = 128
ATTN_OUT = HEADS * HEAD_DIM
QKV_WIDTH = N_GROUPS * ATTN_OUT
ATTN_BLOCK = 128
SSM_GROUP_CH = 16
SSM_STATE = 64
N_EXPERTS = 8
RMS_EPS = 1e-6
NEG_BIG = -1e30


def _params(*semantics):
    return pltpu.CompilerParams(dimension_semantics=semantics,
                                vmem_limit_bytes=VMEM_LIMIT_BYTES)


def _sigmoid(v):
    return 1.0 / (1.0 + jnp.exp(-v))


ADA_KR = 256
ADA_CN = 1536


def _ada_kernel(ct_ref, w_ref, b_ref, o_ref, acc_ref):
    kstep = pl.program_id(1)
    kr = w_ref.shape[0]
    ct = ct_ref[pl.ds(pl.multiple_of(kstep * kr, kr), kr), :]
    act = (ct * _sigmoid(ct)).astype(BF16)

    @pl.when(kstep == 0)
    def _():
        acc_ref[...] = jnp.zeros_like(acc_ref)

    for c0 in range(0, w_ref.shape[1], ADA_CN):
        acc_ref[c0:c0 + ADA_CN, :] += lax.dot_general(
            w_ref[:, c0:c0 + ADA_CN].astype(BF16), act, (((0,), (0,)), ((), ())),
            preferred_element_type=F32)

    @pl.when(kstep == pl.num_programs(1) - 1)
    def _():
        for c0 in range(0, w_ref.shape[1], ADA_CN):
            rows = acc_ref[c0:c0 + ADA_CN, :].T[:o_ref.shape[0], :]
            o_ref[:, c0:c0 + ADA_CN] = rows + b_ref[:, c0:c0 + ADA_CN]


def _ada_modulation(c, ada_w, ada_b):
    depth, k, n = ada_w.shape
    nb = c.shape[0]
    ct = jnp.zeros((k, LANES), F32).at[:, :nb].set(c.T)
    out = pl.pallas_call(
        _ada_kernel,
        grid=(depth, k // ADA_KR),
        in_specs=[
            pl.BlockSpec((k, LANES), lambda l, i: (0, 0)),
            pl.BlockSpec((None, ADA_KR, n), lambda l, i: (l, i, 0)),
            pl.BlockSpec((None, 1, n), lambda l, i: (l, 0, 0)),
        ],
        out_specs=pl.BlockSpec((None, SUBLANES, n), lambda l, i: (l, 0, 0)),
        out_shape=jax.ShapeDtypeStruct((depth, SUBLANES, n), F32),
        scratch_shapes=[pltpu.VMEM((n, LANES), F32)],
        compiler_params=_params("arbitrary", "arbitrary"),
        name="ada_modulation",
    )(ct, ada_w, ada_b.reshape(depth, 1, n))
    return out[:, :nb, :]


NORM_TS = 512


def _rms(x, g):
    return x * lax.rsqrt(jnp.mean(x * x, axis=-1, keepdims=True) + RMS_EPS) * g


def _norm_mod_kernel(x_ref, g_ref, sh_ref, sc_ref, o_ref):
    y = _rms(x_ref[...], g_ref[...])
    o_ref[...] = (y * (1.0 + sc_ref[...]) + sh_ref[...]).astype(o_ref.dtype)


def _norm_mod(x, g, shift, scale):
    b, s, d = x.shape
    vec = pl.BlockSpec((None, 1, d), lambda i, j: (i, 0, 0))
    out = pl.pallas_call(
        _norm_mod_kernel,
        grid=(b, s // NORM_TS),
        in_specs=[
            pl.BlockSpec((None, NORM_TS, d), lambda i, j: (i, j, 0)),
            pl.BlockSpec((1, d), lambda i, j: (0, 0)),
            vec, vec,
        ],
        out_specs=pl.BlockSpec((None, NORM_TS, d), lambda i, j: (i, j, 0)),
        out_shape=jax.ShapeDtypeStruct((b, s, d), BF16),
        compiler_params=_params("arbitrary", "arbitrary"),
        name="norm_mod",
    )(x, g.reshape(1, d), shift.reshape(b, 1, d), scale.reshape(b, 1, d))
    return out.reshape(b * s, d)


def _final_norm_kernel(x_ref, g_ref, o_ref):
    o_ref[...] = _rms(x_ref[...], g_ref[...])


def _final_norm(x, g):
    b, s, d = x.shape
    return pl.pallas_call(
        _final_norm_kernel,
        grid=(b, s // NORM_TS),
        in_specs=[
            pl.BlockSpec((None, NORM_TS, d), lambda i, j: (i, j, 0)),
            pl.BlockSpec((1, d), lambda i, j: (0, 0)),
        ],
        out_specs=pl.BlockSpec((None, NORM_TS, d), lambda i, j: (i, j, 0)),
        out_shape=jax.ShapeDtypeStruct((b, s, d), F32),
        compiler_params=_params("arbitrary", "arbitrary"),
        name="final_norm",
    )(x, g.reshape(1, d))


CAST_ROWS = 256


def _cast_weight(w_ref, wb_ref):
    def body(i, carry):
        r = pl.multiple_of(i * CAST_ROWS, CAST_ROWS)
        wb_ref[pl.ds(r, CAST_ROWS), :] = w_ref[pl.ds(r, CAST_ROWS), :].astype(BF16)
        return carry
    lax.fori_loop(0, w_ref.shape[0] // CAST_ROWS, body, 0)


def _dot(a, b):
    return jnp.dot(a, b, preferred_element_type=F32)


def _mm_plain_kernel(a_ref, w_ref, o_ref, wb_ref):
    @pl.when(pl.program_id(1) == 0)
    def _():
        _cast_weight(w_ref, wb_ref)
    o_ref[...] = _dot(a_ref[...], wb_ref[...]).astype(o_ref.dtype)


def _mm_plain(a, w, w_idx, col_tile, n_tiles, out_dtype, tm, tn):
    m, k = a.shape
    lead = (None,) * len(w_idx)
    return pl.pallas_call(
        _mm_plain_kernel,
        grid=(n_tiles, m // tm),
        in_specs=[
            pl.BlockSpec((tm, k), lambda j, i: (i, 0)),
            pl.BlockSpec(lead + (k, tn), lambda j, i: w_idx + (0, col_tile(j))),
        ],
        out_specs=pl.BlockSpec((tm, tn), lambda j, i: (i, j)),
        out_shape=jax.ShapeDtypeStruct((m, n_tiles * tn), out_dtype),
        scratch_shapes=[pltpu.VMEM((k, tn), BF16)],
        compiler_params=_params("arbitrary", "arbitrary"),
        name="mm_plain",
    )(a, w)


def _mm_swiglu_kernel(a_ref, w1_ref, w3_ref, o_ref, wb1_ref, wb3_ref):
    @pl.when(pl.program_id(1) == 0)
    def _():
        _cast_weight(w1_ref, wb1_ref)
        _cast_weight(w3_ref, wb3_ref)
    a = a_ref[...]
    u = _dot(a, wb1_ref[...])
    v = _dot(a, wb3_ref[...])
    o_ref[...] = (u * _sigmoid(u) * v).astype(o_ref.dtype)


def _mm_swiglu(a, w1, w3, w_idx, tm, tn):
    m, k = a.shape
    n = w1.shape[-1]
    lead = (None,) * len(w_idx)
    wspec = pl.BlockSpec(lead + (k, tn), lambda j, i: w_idx + (0, j))
    return pl.pallas_call(
        _mm_swiglu_kernel,
        grid=(n // tn, m // tm),
        in_specs=[pl.BlockSpec((tm, k), lambda j, i: (i, 0)), wspec, wspec],
        out_specs=pl.BlockSpec((tm, tn), lambda j, i: (i, j)),
        out_shape=jax.ShapeDtypeStruct((m, n), BF16),
        scratch_shapes=[pltpu.VMEM((k, tn), BF16), pltpu.VMEM((k, tn), BF16)],
        compiler_params=_params("arbitrary", "arbitrary"),
        name="mm_swiglu",
    )(a, w1, w3)


def _mm_resid_kernel(a_ref, w_ref, x_ref, g_ref, o_ref, wb_ref):
    @pl.when(pl.program_id(1) == 0)
    def _():
        _cast_weight(w_ref, wb_ref)
    o_ref[...] = x_ref[...] + g_ref[...] * _dot(a_ref[...], wb_ref[...])


def _mm_resid(a, w, w_idx, x, gate, seq, tm, tn):
    m, k = a.shape
    n = w.shape[-1]
    nb = gate.shape[0]
    lead = (None,) * len(w_idx)
    return pl.pallas_call(
        _mm_resid_kernel,
        grid=(n // tn, m // tm),
        in_specs=[
            pl.BlockSpec((tm, k), lambda j, i: (i, 0)),
            pl.BlockSpec(lead + (k, tn), lambda j, i: w_idx + (0, j)),
            pl.BlockSpec((tm, tn), lambda j, i: (i, j)),
            pl.BlockSpec((None, 1, tn), lambda j, i: (i * tm // seq, 0, j)),
        ],
        out_specs=pl.BlockSpec((tm, tn), lambda j, i: (i, j)),
        out_shape=jax.ShapeDtypeStruct((m, n), F32),
        scratch_shapes=[pltpu.VMEM((k, tn), BF16)],
        compiler_params=_params("arbitrary", "arbitrary"),
        name="mm_resid",
    )(a, w, x, gate.reshape(nb, 1, n))


def _mm_resid_norm_kernel(a_ref, w_ref, x_ref, g_ref, ng_ref, sh_ref, sc_ref,
                          xo_ref, ho_ref, wb_ref):
    @pl.when(pl.program_id(0) == 0)
    def _():
        _cast_weight(w_ref, wb_ref)
    xn = x_ref[...] + g_ref[...] * _dot(a_ref[...], wb_ref[...])
    xo_ref[...] = xn
    yn = _rms(xn, ng_ref[...])
    ho_ref[...] = (yn * (1.0 + sc_ref[...]) + sh_ref[...]).astype(ho_ref.dtype)


def _mm_resid_norm(a, w, layer, x, gate, norm_g, shift, scale, seq, tm):
    m, k = a.shape
    n = w.shape[-1]
    nb = gate.shape[0]
    vec = pl.BlockSpec((None, 1, n), lambda i: (i * tm // seq, 0, 0))
    row = pl.BlockSpec((tm, n), lambda i: (i, 0))
    return pl.pallas_call(
        _mm_resid_norm_kernel,
        grid=(m // tm,),
        in_specs=[
            pl.BlockSpec((tm, k), lambda i: (i, 0)),
            pl.BlockSpec((None, k, n), lambda i: (layer, 0, 0), pipeline_mode=pl.Buffered(1)),
            row, vec,
            pl.BlockSpec((1, n), lambda i: (0, 0)),
            vec, vec,
        ],
        out_specs=[row, row],
        out_shape=[jax.ShapeDtypeStruct((m, n), F32), jax.ShapeDtypeStruct((m, n), BF16)],
        scratch_shapes=[pltpu.VMEM((k, n), BF16)],
        compiler_params=_params("arbitrary"),
        name="mm_resid_norm",
    )(a, w, x, gate.reshape(nb, 1, n), norm_g.reshape(1, n),
      shift.reshape(nb, 1, n), scale.reshape(nb, 1, n))


def _mm_branch_kernel(at_ref, ss_ref, wa_ref, ws_ref, ga_ref, gs_ref, o_ref,
                      wba_ref, wbs_ref):
    @pl.when(pl.program_id(1) == 0)
    def _():
        _cast_weight(wa_ref, wba_ref)
        _cast_weight(ws_ref, wbs_ref)
    pa = _dot(at_ref[...], wba_ref[...])
    ps = _dot(ss_ref[...], wbs_ref[...])
    ga = _sigmoid(ga_ref[...].astype(F32))
    gs = _sigmoid(gs_ref[...].astype(F32))
    o_ref[...] = (ga * pa + gs * ps).astype(o_ref.dtype)


def _mm_branch(attn, ssm, w_branch, layer, proj, gate_col, tm, tn):
    m, k = attn.shape
    n = w_branch.shape[-1]
    g0 = gate_col // tn
    g1 = (gate_col + n) // tn
    return pl.pallas_call(
        _mm_branch_kernel,
        grid=(n // tn, m // tm),
        in_specs=[
            pl.BlockSpec((tm, k), lambda j, i: (i, 0)),
            pl.BlockSpec((tm, k), lambda j, i: (i, 0)),
            pl.BlockSpec((None, k, tn), lambda j, i: (layer, 0, j)),
            pl.BlockSpec((None, k, tn), lambda j, i: (layer, 1, j)),
            pl.BlockSpec((tm, tn), lambda j, i: (i, g0 + j)),
            pl.BlockSpec((tm, tn), lambda j, i: (i, g1 + j)),
        ],
        out_specs=pl.BlockSpec((tm, tn), lambda j, i: (i, j)),
        out_shape=jax.ShapeDtypeStruct((m, n), BF16),
        scratch_shapes=[pltpu.VMEM((k, tn), BF16), pltpu.VMEM((k, tn), BF16)],
        compiler_params=_params("arbitrary", "arbitrary"),
        name="mm_branch",
    )(attn, ssm, w_branch, w_branch, proj, proj)


TOKEN_RESIDUES = 16
ATTN_BB = 4


def _dot_nt(a, b):
    return lax.dot_general(a, b, (((1,), (1,)), ((), ())), preferred_element_type=F32)


def _attn_kernel(*refs, with_prev):
    if with_prev:
        q_ref, kc_ref, vc_ref, bc_ref, bp_ref, o_ref, lse_ref, kp_ref, vp_ref = refs

        @pl.when(pl.program_id(2) == 0)
        def _():
            kp_ref[...] = jnp.zeros_like(kp_ref)
            vp_ref[...] = jnp.zeros_like(vp_ref)
    else:
        q_ref, kc_ref, vc_ref, bc_ref, o_ref, lse_ref = refs
    blk = ATTN_BLOCK
    bb, nc, rpc = q_ref.shape[0], q_ref.shape[1], q_ref.shape[2]
    scale = HEAD_DIM ** -0.5
    lane = lax.broadcasted_iota(jnp.int32, (blk, LANES), 1)

    def heads(ref, bi):
        x = ref[bi].reshape(blk, ATTN_OUT)
        return jnp.stack([x[:, h * HEAD_DIM:(h + 1) * HEAD_DIM] for h in range(HEADS)]
                         ).astype(BF16)

    def qk(q, k):
        return lax.dot_general(q, k, (((2,), (2,)), ((0,), (0,))), preferred_element_type=F32)

    def pv(p, v):
        return lax.dot_general(p.astype(BF16), v, (((2,), (1,)), ((0,), (0,))),
                               preferred_element_type=F32)

    for bi in range(bb):
        q = heads(q_ref, bi)
        k_c = heads(kc_ref, bi)
        v_c = heads(vc_ref, bi)
        s_c = qk(q, k_c) * scale + bc_ref[...]
        m = jnp.max(s_c, axis=2, keepdims=True)
        if with_prev:
            bias_p = jnp.where(pl.program_id(2) > 0, bp_ref[...], NEG_BIG)
            s_p = qk(q, kp_ref[bi]) * scale + bias_p
            m = jnp.maximum(m, jnp.max(s_p, axis=2, keepdims=True))
        p_c = jnp.exp(s_c - m)
        l = jnp.sum(p_c, axis=2, keepdims=True)
        o = pv(p_c, v_c)
        if with_prev:
            p_p = jnp.exp(s_p - m)
            l = l + jnp.sum(p_p, axis=2, keepdims=True)
            o = o + pv(p_p, vp_ref[bi])
            kp_ref[bi] = k_c
            vp_ref[bi] = v_c
        o = o / l
        lse = m + jnp.log(l)
        lse_tile = jnp.zeros((blk, LANES), F32)
        for h in range(HEADS):
            sl = slice(h * HEAD_DIM, (h + 1) * HEAD_DIM)
            o_ref[bi, :, :, sl] = o[h].reshape(nc, rpc, HEAD_DIM).astype(o_ref.dtype)
            lse_tile = jnp.where(lane == h, lse[h], lse_tile)
        lse_ref[bi] = lse_tile.reshape(nc, rpc, LANES)


def _attn_bias(dilation, nc, back):
    rpc = ATTN_BLOCK // nc
    i = jnp.arange(ATTN_BLOCK, dtype=jnp.int32)
    off = nc * (i % rpc) + i // rpc
    dist = off[:, None] - off[None, :] + back * ATTN_BLOCK
    valid = jnp.logical_and(dist >= 0, dist <= ATTN_BLOCK)
    slopes = 2.0 ** (-8.0 * (jnp.arange(HEADS, dtype=F32) + 1.0) / HEADS)
    pen = slopes[:, None, None] * (dist * dilation).astype(F32)[None]
    return jnp.where(valid[None], -pen, NEG_BIG)


def _attention_group(qkv, qkv_cols, batch, seq, gi):
    window, dilation = ATTN_PATTERNS[gi]
    assert window // dilation == ATTN_BLOCK
    cols = qkv.shape[-1]
    per_res = seq // TOKEN_RESIDUES
    nc = TOKEN_RESIDUES // dilation
    rpc = ATTN_BLOCK // nc
    nblk = seq // dilation // ATTN_BLOCK
    with_prev = nblk > 1

    def shape5(c):
        return (batch, nc, dilation, per_res, c)

    def spec(c, col):
        return pl.BlockSpec((ATTN_BB, nc, None, rpc, c), lambda b, r, n: (b, 0, r, n, col))

    bias_spec = pl.BlockSpec((HEADS, ATTN_BLOCK, ATTN_BLOCK), lambda b, r, n: (0, 0, 0))
    in_specs = [spec(ATTN_OUT, col) for col in qkv_cols]
    carry = pltpu.VMEM((ATTN_BB, HEADS, ATTN_BLOCK, HEAD_DIM), BF16)
    view = qkv.reshape(shape5(cols))
    args = [view] * len(in_specs) + [_attn_bias(dilation, nc, 0)]
    in_specs.append(bias_spec)
    if with_prev:
        args.append(_attn_bias(dilation, nc, 1))
        in_specs.append(bias_spec)
    o, lse = pl.pallas_call(
        functools.partial(_attn_kernel, with_prev=with_prev),
        grid=(batch // ATTN_BB, dilation, nblk),
        in_specs=in_specs,
        out_specs=[spec(ATTN_OUT, 0), spec(LANES, 0)],
        out_shape=[
            jax.ShapeDtypeStruct(shape5(ATTN_OUT), qkv.dtype),
            jax.ShapeDtypeStruct(shape5(LANES), F32),
        ],
        scratch_shapes=[carry, carry] if with_prev else [],
        compiler_params=_params("arbitrary", "arbitrary", "arbitrary"),
        name=f"attn_g{gi}",
    )(*args)
    return o.reshape(batch * seq, ATTN_OUT), lse.reshape(batch * seq, LANES)


COMBINE_TS = 512


def _combine_kernel(o0_ref, o1_ref, o2_ref, l0_ref, l1_ref, l2_ref, out_ref):
    a0, a1, a2 = l0_ref[...], l1_ref[...], l2_ref[...]
    m = jnp.maximum(jnp.maximum(a0, a1), a2)
    e0 = jnp.exp(a0 - m)
    e1 = jnp.exp(a1 - m)
    e2 = jnp.exp(a2 - m)
    inv = 1.0 / (e0 + e1 + e2)
    w0, w1, w2 = e0 * inv, e1 * inv, e2 * inv
    for h in range(HEADS):
        sl = slice(h * HEAD_DIM, (h + 1) * HEAD_DIM)
        mix = (w0[:, h:h + 1] * o0_ref[:, sl].astype(F32)
               + w1[:, h:h + 1] * o1_ref[:, sl].astype(F32)
               + w2[:, h:h + 1] * o2_ref[:, sl].astype(F32))
        out_ref[:, sl] = mix.astype(out_ref.dtype)


def _dilation_mixture_attention(qkv0, rest, batch, seq):
    outs, lses = [], []
    for gi in range(N_GROUPS):
        if gi == 0:
            o, lse = _attention_group(qkv0, (0, 1, 2), batch, seq, gi)
        else:
            o, lse = _attention_group(rest, (gi - 1, gi + 1, gi + 3), batch, seq, gi)
        outs.append(o)
        lses.append(lse)
    m = batch * seq
    ospec = pl.BlockSpec((COMBINE_TS, ATTN_OUT), lambda i: (i, 0))
    lspec = pl.BlockSpec((COMBINE_TS, LANES), lambda i: (i, 0))
    return pl.pallas_call(
        _combine_kernel,
        grid=(m // COMBINE_TS,),
        in_specs=[ospec] * 3 + [lspec] * 3,
        out_specs=ospec,
        out_shape=jax.ShapeDtypeStruct((m, ATTN_OUT), BF16),
        compiler_params=_params("arbitrary"),
        name="attn_combine",
    )(*outs, *lses)


SSM_LT = 128
SSM_PASSES = 2
SSM_JBLK = 256


def _gelu_tanh(y):
    return 0.5 * y * (1.0 + jnp.tanh(math.sqrt(2.0 / math.pi) * (y + 0.044715 * (y * y * y))))


def _ssm_kernel(u_ref, pin_ref, pout_ref, bblk_ref, cblk_ref, are_ref, aim_ref, d_ref,
                wglu_ref, bglu_ref, o_ref, sre_ref, sim_ref, hre_ref, him_ref, *, nbatch):
    width = u_ref.shape[-1]
    rows = pin_ref.shape[1]
    njb = width // SSM_JBLK
    jstates = bblk_ref.shape[2] // 2
    pairs = rows // SUBLANES
    per_tile = SUBLANES // nbatch

    @pl.when(pl.program_id(0) == 0)
    def _():
        hre_ref[...] = jnp.zeros_like(hre_ref)
        him_ref[...] = jnp.zeros_like(him_ref)

    u_rm = u_ref[...].reshape(pout_ref.shape[1], width)
    out_rm = None
    for p in range(pin_ref.shape[0]):
        u_f32 = _dot(pin_ref[p], u_rm)
        u_tb = u_f32.astype(BF16)
        ys = []
        for j in range(njb):
            bu = _dot(u_tb[:, j * SSM_JBLK:(j + 1) * SSM_JBLK], bblk_ref[j])
            sre_ref[...] = bu[:, :jstates].reshape(pairs, SUBLANES, jstates)
            sim_ref[...] = bu[:, jstates:].reshape(pairs, SUBLANES, jstates)
            cs = slice(j * jstates, (j + 1) * jstates)
            ar = are_ref[0:nbatch, cs]
            ai = aim_ref[0:nbatch, cs]

            def step(k, carry):
                hr, hi = carry
                for t in range(per_tile):
                    rs = slice(t * nbatch, (t + 1) * nbatch)
                    nr = ar * hr - ai * hi + sre_ref[k, rs, :]
                    ni = ar * hi + ai * hr + sim_ref[k, rs, :]
                    sre_ref[k, rs, :] = nr
                    sim_ref[k, rs, :] = ni
                    hr, hi = nr, ni
                return hr, hi

            hr, hi = lax.fori_loop(0, pairs, step,
                                   (hre_ref[0:nbatch, cs], him_ref[0:nbatch, cs]), unroll=2)
            hre_ref[0:nbatch, cs] = hr
            him_ref[0:nbatch, cs] = hi

            h_re = sre_ref[...].reshape(rows, jstates).astype(BF16)
            h_im = sim_ref[...].reshape(rows, jstates).astype(BF16)
            ys.append(_dot(h_re, cblk_ref[j, 0:jstates, :]) + _dot(h_im, cblk_ref[j, jstates:, :]))

        y = jnp.concatenate(ys, axis=1) + d_ref[...] * u_f32
        z = _gelu_tanh(y).astype(BF16)
        g = _dot(z, wglu_ref[...]) + bglu_ref[...]
        o_tb = (g[:, :width] * _sigmoid(g[:, width:])).astype(BF16)
        back = _dot(pout_ref[p], o_tb)
        out_rm = back if out_rm is None else out_rm + back
    o_ref[...] = out_rm.astype(o_ref.dtype).reshape(o_ref.shape)


def _ssm_tables(a_re, a_im, log_dt, b_re, b_im, c_re, c_im):
    groups, nst = a_re.shape
    gpb = SSM_JBLK // SSM_GROUP_CH
    njb = groups // gpb
    lam = lax.complex(a_re.astype(F32), a_im.astype(F32))
    dt = jnp.exp(log_dt.astype(F32))[:, None]
    a_bar = jnp.exp(lam * dt)
    b_mat = lax.complex(b_re.astype(F32), b_im.astype(F32))
    b_bar = ((a_bar - 1.0) / lam)[:, :, None] * b_mat
    same_group = jnp.eye(gpb, dtype=jnp.bool_)

    def block_diag(t):
        r, q = t.shape[1:]
        t = t.astype(BF16).reshape(njb, gpb, r, 1, q)
        wide = jnp.where(same_group[None, :, None, :, None], t, jnp.zeros((), BF16))
        return wide.reshape(njb, gpb * r, gpb * q)

    def in_blocks(t):
        return block_diag(t.transpose(0, 2, 1))

    def out_blocks(t):
        return block_diag(t.transpose(0, 2, 1))

    bblk = jnp.concatenate([in_blocks(jnp.real(b_bar)), in_blocks(jnp.imag(b_bar))], axis=2)
    cblk = jnp.concatenate([out_blocks(c_re.astype(F32)), out_blocks(-c_im.astype(F32))], axis=1)
    are = jnp.broadcast_to(jnp.real(a_bar).reshape(1, groups * nst), (SUBLANES, groups * nst))
    aim = jnp.broadcast_to(jnp.imag(a_bar).reshape(1, groups * nst), (SUBLANES, groups * nst))
    return bblk.astype(BF16), cblk.astype(BF16), are, aim


def _ssm_row_perms(nbatch):
    per_res = SSM_PASSES * SSM_LT // TOKEN_RESIDUES
    col = jnp.arange(nbatch * TOKEN_RESIDUES * per_res, dtype=jnp.int32)
    b = col // (TOKEN_RESIDUES * per_res)
    t_local = TOKEN_RESIDUES * (col % per_res) + (col // per_res) % TOKEN_RESIDUES
    row = (t_local % SSM_LT) * nbatch + b
    pin = jnp.logical_and(
        (t_local // SSM_LT)[None, None, :] == jnp.arange(SSM_PASSES, dtype=jnp.int32)[:, None, None],
        row[None, None, :] == jnp.arange(SSM_LT * nbatch, dtype=jnp.int32)[None, :, None])
    pin = pin.astype(BF16)
    return pin, pin.transpose(0, 2, 1)


def _s5_ssm(src, u_col, nbatch, seq, tables, d_skip, w_glu, b_glu):
    bblk, cblk, are, aim = tables
    width = w_glu.shape[0]
    nstate = are.shape[1]
    jstates = bblk.shape[2] // 2
    rows = SSM_LT * nbatch
    step_rows = SSM_PASSES * SSM_LT // TOKEN_RESIDUES
    per_res = seq // TOKEN_RESIDUES
    pin, pout = _ssm_row_perms(nbatch)
    d2 = d_skip.reshape(1, width).astype(F32)
    wg = w_glu.astype(BF16)
    bg = b_glu.reshape(1, 2 * width).astype(F32)

    def const(a):
        return pl.BlockSpec(a.shape, lambda i: (0,) * a.ndim, pipeline_mode=pl.Buffered(1))

    blk = (nbatch, TOKEN_RESIDUES, step_rows, width)
    out = pl.pallas_call(
        functools.partial(_ssm_kernel, nbatch=nbatch),
        grid=(per_res // step_rows,),
        in_specs=[pl.BlockSpec(blk, lambda i: (0, 0, i, u_col)),
                  const(pin), const(pout), const(bblk), const(cblk), const(are), const(aim),
                  const(d2), const(wg), const(bg)],
        out_specs=pl.BlockSpec(blk, lambda i: (0, 0, i, 0)),
        out_shape=jax.ShapeDtypeStruct((nbatch, TOKEN_RESIDUES, per_res, width), BF16),
        scratch_shapes=[
            pltpu.VMEM((rows // SUBLANES, SUBLANES, jstates), F32),
            pltpu.VMEM((rows // SUBLANES, SUBLANES, jstates), F32),
            pltpu.VMEM((SUBLANES, nstate), F32),
            pltpu.VMEM((SUBLANES, nstate), F32),
        ],
        compiler_params=_params("arbitrary"),
        name="s5_ssm",
    )(src.reshape(nbatch, TOKEN_RESIDUES, per_res, src.shape[-1]),
      pin, pout, bblk, cblk, are, aim, d2, wg, bg)
    return out.reshape(nbatch * seq, width)


ROUTER_TS = 256


def _router_kernel(x_ref, g_ref, sh_ref, sc_ref, rw_ref, rb_ref,
                   sel_ref, wts_ref, cnt_ref, carry_ref):
    @pl.when(jnp.logical_and(pl.program_id(0) == 0, pl.program_id(1) == 0))
    def _():
        carry_ref[...] = jnp.zeros_like(carry_ref)

    h = _rms(x_ref[...], g_ref[...]) * (1.0 + sc_ref[...]) + sh_ref[...]
    ts, ne = h.shape[0], rw_ref.shape[0]
    idx = lax.broadcasted_iota(jnp.int32, (ts, ne), 1)
    logits = jnp.broadcast_to(rb_ref[...], (ts, ne))
    for ex in range(ne):
        col = jnp.sum(h * rw_ref[ex:ex + 1, :], axis=1, keepdims=True)
        logits = logits + jnp.where(idx == ex, col, 0.0)
    m1 = jnp.max(logits, axis=1, keepdims=True)
    i1 = jnp.min(jnp.where(logits == m1, idx, ne), axis=1, keepdims=True)
    rest = jnp.where(idx == i1, -jnp.inf, logits)
    m2 = jnp.max(rest, axis=1, keepdims=True)
    i2 = jnp.min(jnp.where(rest == m2, idx, ne), axis=1, keepdims=True)
    e = jnp.exp(m2 - m1)
    w1 = 1.0 / (1.0 + e)
    w2 = e / (1.0 + e)

    onehot = jnp.where(idx == i1, 1.0, 0.0) + jnp.where(idx == i2, 1.0, 0.0)
    row = lax.broadcasted_iota(jnp.int32, (ts, ts), 0)
    col = lax.broadcasted_iota(jnp.int32, (ts, ts), 1)
    lower = jnp.where(col < row, 1.0, 0.0).astype(BF16)
    before = carry_ref[...] + _dot(lower, onehot.astype(BF16))
    r1 = jnp.sum(jnp.where(idx == i1, before, 0.0), axis=1, keepdims=True).astype(jnp.int32)
    r2 = jnp.sum(jnp.where(idx == i2, before, 0.0), axis=1, keepdims=True).astype(jnp.int32)
    total = carry_ref[...] + jnp.sum(onehot, axis=0, keepdims=True)
    carry_ref[...] = total
    cnt_ref[...] = total
    sel_ref[...] = jnp.where(idx == 0, i1, jnp.where(idx == 1, i2, jnp.where(
        idx == 2, r1, jnp.where(idx == 3, r2, 0))))
    wts_ref[...] = jnp.where(idx == 0, w1, jnp.where(idx == 1, w2, 0.0))


def _router(x, g, shift, scale, router_w, router_b):
    b, s, d = x.shape
    ne = router_w.shape[-1]
    vec = pl.BlockSpec((None, 1, d), lambda i, j: (i, 0, 0))
    tok = pl.BlockSpec((None, ROUTER_TS, ne), lambda i, j: (i, j, 0))
    sel, wts, cnt = pl.pallas_call(
        _router_kernel,
        grid=(b, s // ROUTER_TS),
        in_specs=[
            pl.BlockSpec((None, ROUTER_TS, d), lambda i, j: (i, j, 0)),
            pl.BlockSpec((1, d), lambda i, j: (0, 0)),
            vec, vec,
            pl.BlockSpec((ne, d), lambda i, j: (0, 0)),
            pl.BlockSpec((1, ne), lambda i, j: (0, 0)),
        ],
        out_specs=[tok, tok, pl.BlockSpec((1, ne), lambda i, j: (0, 0))],
        out_shape=[jax.ShapeDtypeStruct((b, s, ne), jnp.int32),
                   jax.ShapeDtypeStruct((b, s, ne), F32),
                   jax.ShapeDtypeStruct((1, ne), F32)],
        scratch_shapes=[pltpu.VMEM((1, ne), F32)],
        compiler_params=_params("arbitrary", "arbitrary"),
        name="router",
    )(x, g.reshape(1, d), shift.reshape(b, 1, d), scale.reshape(b, 1, d),
      router_w.T, router_b.reshape(1, ne))
    return sel.reshape(b * s, ne), wts.reshape(b * s, ne), cnt


MOE_TM = 256
MOE_TT = 256
MOE_UP_TN = 1408
MOE_DOWN_TN = 2048


def _moe_plan(sel, cnt, tokens):
    ne = cnt.shape[-1]
    counts = cnt[0].astype(jnp.int32)
    padded = (counts + MOE_TM - 1) // MOE_TM * MOE_TM
    ends = jnp.cumsum(padded)
    starts = ends - padded
    pos1 = starts[sel[:, 0]] + sel[:, 2]
    pos2 = starts[sel[:, 1]] + sel[:, 3]
    nt = tokens // MOE_TT
    pos = jnp.concatenate([pos1.reshape(nt, MOE_TT), pos2.reshape(nt, MOE_TT)], axis=1)
    max_tiles = 2 * tokens // MOE_TM + ne
    tile_start = jnp.arange(max_tiles, dtype=jnp.int32) * MOE_TM
    tile_expert = jnp.minimum(
        jnp.sum((tile_start[:, None] >= ends[None, :]).astype(jnp.int32), axis=1), ne - 1)
    num_tiles = (ends[-1] // MOE_TM).reshape(1)
    zero_tiles = jnp.concatenate([
        jnp.maximum(ends // MOE_TM - 1, 0),
        jnp.minimum(num_tiles[0] + jnp.arange(ne, dtype=jnp.int32), max_tiles - 1)])
    nonempty = counts > 0
    ar = jnp.arange(ne, dtype=jnp.int32)
    order = (ar[:, None] + 1 + ar[None, :]) % ne
    nxt = order[ar, jnp.argmax(nonempty[order], axis=1)].astype(jnp.int32)
    gidx = jnp.cumsum(nonempty.astype(jnp.int32)) - 1
    groups = jnp.concatenate([nxt, gidx, jnp.sum(nonempty.astype(jnp.int32)).reshape(1)])
    return (pos.reshape(nt, 1, 2 * MOE_TT), tile_expert, num_tiles, max_tiles, zero_tiles,
            groups)


def _row_copies(pos_ref, t, make):
    tt = pos_ref.shape[1] // 2
    return make(0, t, pos_ref[0, t]), make(1, t, pos_ref[0, tt + t])


def _issue_and_drain(pos_ref, make):
    tt = pos_ref.shape[1] // 2

    def issue(t, carry):
        for k, cp in enumerate(_row_copies(pos_ref, t, make)):
            cp.start(priority=k)
        return carry

    def drain(t, carry):
        for cp in _row_copies(pos_ref, t, make):
            cp.wait()
        return carry

    lax.fori_loop(0, tt, issue, 0, unroll=4)
    lax.fori_loop(0, tt, drain, 0, unroll=4)


def _pack_pairs(lo, hi):
    lo_bits = pltpu.bitcast(lo.astype(BF16).astype(F32), jnp.uint32)
    hi_bits = pltpu.bitcast(hi.astype(BF16).astype(F32), jnp.uint32)
    return jnp.bitwise_or(jnp.bitwise_and(hi_bits, jnp.uint32(0xFFFF0000)),
                          jnp.right_shift(lo_bits, jnp.uint32(16)))


def _unpack_pairs(words):
    lo = pltpu.bitcast(jnp.left_shift(words, jnp.uint32(16)), F32)
    hi = pltpu.bitcast(jnp.bitwise_and(words, jnp.uint32(0xFFFF0000)), F32)
    return lo, hi


def _moe_dispatch_kernel(pos_ref, zt_ref, x_ref, g_ref, sh_ref, sc_ref, xs_ref, hbuf_ref, sem):
    @pl.when(jnp.logical_and(pl.program_id(0) == 0, pl.program_id(1) == 0))
    def _():
        hbuf_ref[...] = jnp.zeros_like(hbuf_ref)
        for k in range(zt_ref.shape[0]):
            cp = pltpu.make_async_copy(
                hbuf_ref, xs_ref.at[pl.ds(pl.multiple_of(zt_ref[k] * MOE_TM, MOE_TM), MOE_TM), :],
                sem)
            cp.start()
            cp.wait()

    h = _rms(x_ref[...], g_ref[...]) * (1.0 + sc_ref[...]) + sh_ref[...]
    half = h.shape[1] // 2
    hbuf_ref[...] = _pack_pairs(h[:, :half], h[:, half:])

    def make(k, t, p):
        return pltpu.make_async_copy(hbuf_ref.at[pl.ds(t, 1), :], xs_ref.at[pl.ds(p, 1), :], sem)

    _issue_and_drain(pos_ref, make)


def _moe_dispatch(x, g, shift, scale, plan):
    pos, _, _, max_tiles, zero_tiles, _ = plan
    b, s, d = x.shape
    per_b = s // MOE_TT
    assert MOE_TT == MOE_TM
    vec = pl.BlockSpec((None, 1, d), lambda i, j: (i, 0, 0))
    return pl.pallas_call(
        _moe_dispatch_kernel,
        grid=(b, per_b),
        in_specs=[
            pl.BlockSpec((None, 1, 2 * MOE_TT), lambda i, j: (i * per_b + j, 0, 0),
                         memory_space=pltpu.SMEM),
            pl.BlockSpec(memory_space=pltpu.SMEM),
            pl.BlockSpec((None, MOE_TT, d), lambda i, j: (i, j, 0)),
            pl.BlockSpec((1, d), lambda i, j: (0, 0)),
            vec, vec,
        ],
        out_specs=pl.BlockSpec(memory_space=pl.ANY),
        out_shape=jax.ShapeDtypeStruct((max_tiles * MOE_TM, d // 2), jnp.uint32),
        scratch_shapes=[pltpu.VMEM((MOE_TT, d // 2), jnp.uint32), pltpu.SemaphoreType.DMA(())],
        compiler_params=_params("arbitrary", "arbitrary"),
        name="moe_dispatch",
    )(pos, zero_tiles, x, g.reshape(1, d), shift.reshape(b, 1, d), scale.reshape(b, 1, d))


def _moe_mm_kernel(te_ref, nt_ref, grp_ref, a_ref, *rest, mode, li, tn):
    nw = 2 if mode == "up" else 1
    w_hbm = rest[:nw]
    o_ref = rest[nw]
    wb = rest[nw + 1:2 * nw + 1]
    stage = rest[2 * nw + 1:3 * nw + 1]
    wsem = rest[3 * nw + 1]
    j = pl.program_id(0)
    i = pl.program_id(1)
    ne = (grp_ref.shape[0] - 1) // 2
    expert = te_ref[i]
    active = i < nt_ref[0]
    new_expert = jnp.logical_or(i == 0, expert != te_ref[jnp.maximum(i - 1, 0)])

    def weight_copies(ex, jx):
        cols = pl.ds(pl.multiple_of(jx * tn, LANES), tn)
        return [pltpu.make_async_copy(w_hbm[k].at[li, ex, :, cols], stage[k], wsem.at[k])
                for k in range(nw)]

    @pl.when(jnp.logical_and(active, new_expert))
    def _():
        @pl.when(jnp.logical_and(j == 0, grp_ref[ne + expert] == 0))
        def _():
            for cp in weight_copies(expert, j):
                cp.start()

        for cp in weight_copies(expert, j):
            cp.wait()
        for k in range(nw):
            _cast_weight(stage[k], wb[k])
        nxt = grp_ref[expert]
        nxt_j = j + jnp.where(nxt <= expert, 1, 0)

        @pl.when(nxt_j < pl.num_programs(0))
        def _():
            for cp in weight_copies(nxt, nxt_j):
                cp.start()

    @pl.when(active)
    def _():
        if mode == "down":
            p = lax.dot_general(wb[0][...], a_ref[...], (((0,), (0,)), ((), ())),
                                preferred_element_type=F32).T
            half = p.shape[1] // 2
            o_ref[...] = _pack_pairs(p[:, :half], p[:, half:])
        else:
            lo, hi = _unpack_pairs(a_ref[...])
            lo, hi = lo.astype(BF16), hi.astype(BF16)
            half = lo.shape[1]
            nt_dims = (((0,), (1,)), ((), ()))

            def proj(w_ref):
                return (lax.dot_general(w_ref[0:half, :], lo, nt_dims, preferred_element_type=F32)
                        + lax.dot_general(w_ref[half:, :], hi, nt_dims,
                                          preferred_element_type=F32))

            u = proj(wb[0])
            o_ref[...] = (u * _sigmoid(u) * proj(wb[1])).astype(o_ref.dtype)

    @pl.when(jnp.logical_not(active))
    def _():
        o_ref[...] = jnp.zeros_like(o_ref)


def _moe_mm(a, ws, li, plan, tn, mode):
    _, tile_expert, num_tiles, max_tiles, _, groups = plan
    k, n = ws[0].shape[-2], ws[0].shape[-1]
    rows = max_tiles * MOE_TM
    nw = len(ws)

    def tile(i, nt):
        return jnp.minimum(i, nt[0] - 1)

    if mode == "down":
        a_spec = pl.BlockSpec((k, MOE_TM), lambda j, i, te, nt, gr: (0, tile(i, nt)))
        out_spec = pl.BlockSpec((MOE_TM, tn // 2), lambda j, i, te, nt, gr: (i, j))
        out_shape = jax.ShapeDtypeStruct((rows, n // 2), jnp.uint32)
    else:
        a_spec = pl.BlockSpec((MOE_TM, k // 2), lambda j, i, te, nt, gr: (tile(i, nt), 0))
        out_spec = pl.BlockSpec((tn, MOE_TM), lambda j, i, te, nt, gr: (j, i))
        out_shape = jax.ShapeDtypeStruct((n, rows), BF16)
    return pl.pallas_call(
        functools.partial(_moe_mm_kernel, mode=mode, li=li, tn=tn),
        grid_spec=pltpu.PrefetchScalarGridSpec(
            num_scalar_prefetch=3,
            grid=(n // tn, max_tiles),
            in_specs=[a_spec] + [pl.BlockSpec(memory_space=pl.ANY)] * nw,
            out_specs=out_spec,
            scratch_shapes=([pltpu.VMEM((k, tn), BF16)] * nw + [pltpu.VMEM((k, tn), F32)] * nw
                            + [pltpu.SemaphoreType.DMA((nw,))]),
        ),
        out_shape=out_shape,
        compiler_params=_params("arbitrary", "arbitrary"),
        name=f"moe_mm_{mode}",
    )(tile_expert, num_tiles, groups, a, *ws)


def _moe_combine_kernel(pos_ref, x_ref, g_ref, wts_ref, y_ref, *rest, final):
    if final:
        ng_ref, o_ref, ybuf_ref, sem = rest
    else:
        ng_ref, sh_ref, sc_ref, o_ref, h_ref, ybuf_ref, sem = rest

    def make(k, t, p):
        return pltpu.make_async_copy(y_ref.at[pl.ds(p, 1), :], ybuf_ref.at[k, pl.ds(t, 1), :], sem)

    _issue_and_drain(pos_ref, make)

    def rows(k):
        lo, hi = _unpack_pairs(ybuf_ref[k])
        hw = MOE_DOWN_TN // 2
        parts = []
        for c in range(lo.shape[1] // hw):
            parts += [lo[:, c * hw:(c + 1) * hw], hi[:, c * hw:(c + 1) * hw]]
        return jnp.concatenate(parts, axis=1)

    f = wts_ref[:, 0:1] * rows(0) + wts_ref[:, 1:2] * rows(1)
    xn = x_ref[...] + g_ref[...] * f
    yn = _rms(xn, ng_ref[...])
    if final:
        o_ref[...] = yn
    else:
        o_ref[...] = xn
        h_ref[...] = (yn * (1.0 + sc_ref[...]) + sh_ref[...]).astype(h_ref.dtype)


def _moe_combine(x, gate, wts, y, pos, seq, norm_g, shift=None, scale=None):
    m, d = x.shape
    nb = gate.shape[0]
    ne = wts.shape[-1]
    final = shift is None
    vec = pl.BlockSpec((None, 1, d), lambda i: (i * MOE_TT // seq, 0, 0))
    row = pl.BlockSpec((MOE_TT, d), lambda i: (i, 0))
    in_specs = [
        pl.BlockSpec((None, 1, 2 * MOE_TT), lambda i: (i, 0, 0), memory_space=pltpu.SMEM),
        row, vec,
        pl.BlockSpec((MOE_TT, ne), lambda i: (i, 0)),
        pl.BlockSpec(memory_space=pl.ANY),
        pl.BlockSpec((1, d), lambda i: (0, 0)),
    ]
    args = [pos, x, gate.reshape(nb, 1, d), wts, y, norm_g.reshape(1, d)]
    if final:
        out_specs, out_shape = row, jax.ShapeDtypeStruct((m, d), F32)
    else:
        in_specs += [vec, vec]
        args += [shift.reshape(nb, 1, d), scale.reshape(nb, 1, d)]
        out_specs = [row, row]
        out_shape = [jax.ShapeDtypeStruct((m, d), F32), jax.ShapeDtypeStruct((m, d), BF16)]
    return pl.pallas_call(
        functools.partial(_moe_combine_kernel, final=final),
        grid=(m // MOE_TT,),
        in_specs=in_specs,
        out_specs=out_specs,
        out_shape=out_shape,
        scratch_shapes=[pltpu.VMEM((2, MOE_TT, d // 2), jnp.uint32), pltpu.SemaphoreType.DMA(())],
        compiler_params=_params("arbitrary"),
        name="moe_combine",
    )(*args)


def kernel(x, c, ada_w, ada_b, norm_mix_g, norm_ffn_g, final_norm_g, w_in, ssm_a_re, ssm_a_im, ssm_log_dt, ssm_b_re, ssm_b_im, ssm_c_re, ssm_c_im, ssm_d, w_glu, b_glu, w_branch, w_out, ffn_w1, ffn_w3, ffn_w2, router_w, router_b, moe_w1, moe_w3, moe_w2):
    batch, seq, d = x.shape
    depth = ada_w.shape[0]
    tokens = batch * seq
    ssm_width = w_glu.shape[1]
    per_res = seq // TOKEN_RESIDUES
    tn_in = ATTN_OUT
    assert ssm_width == tn_in and w_in.shape[-1] == (3 * N_GROUPS + 1) * tn_in + 2 * d

    def to_residue_major(t):
        f = t.shape[-1]
        return t.reshape(batch, per_res, TOKEN_RESIDUES, f).transpose(0, 2, 1, 3).reshape(batch, seq, f)

    mod = _ada_modulation(c, ada_w, ada_b).reshape(depth, batch, 6, d)
    x2 = to_residue_major(x).reshape(tokens, d)
    h = None
    out = None
    for layer in range(depth):
        sh1, sc1, g1, sh2, sc2, g2 = (mod[layer, :, i] for i in range(6))

        if h is None:
            h = _norm_mod(x2.reshape(batch, seq, d), norm_mix_g[layer], sh1, sc1)
        qkv0 = _mm_plain(h, w_in, (layer,), lambda j: N_GROUPS * j, 3, F32, tm=1024, tn=tn_in)
        rest = _mm_plain(h, w_in, (layer,),
                         lambda j: j + 1 + jnp.where(j >= 2, 1, 0) + jnp.where(j >= 4, 1, 0),
                         11, BF16, tm=2048, tn=tn_in)
        attn = _dilation_mixture_attention(qkv0, rest, batch, seq)
        tables = _ssm_tables(ssm_a_re[layer], ssm_a_im[layer], ssm_log_dt[layer],
                             ssm_b_re[layer], ssm_b_im[layer], ssm_c_re[layer], ssm_c_im[layer])
        ssm = _s5_ssm(rest, 6, batch, seq, tables, ssm_d[layer], w_glu[layer], b_glu[layer])
        merged = _mm_branch(attn, ssm, w_branch, layer, rest, 7 * tn_in, tm=1024, tn=1024)
        h = None

        li = layer // 2
        if layer % 2 == 0:
            x2, hf = _mm_resid_norm(merged, w_out, layer, x2, g1, norm_ffn_g[layer], sh2, sc2,
                                    seq, tm=256)
            act = _mm_swiglu(hf, ffn_w1, ffn_w3, (li,), tm=2048, tn=512)
            x2 = _mm_resid(act, ffn_w2, (li,), x2, g2, seq, tm=512, tn=512)
        else:
            x2 = _mm_resid(merged, w_out, (layer,), x2, g1, seq, tm=1024, tn=1024)
            x3 = x2.reshape(batch, seq, d)
            sel, wts, cnt = _router(x3, norm_ffn_g[layer], sh2, sc2, router_w[li], router_b[li])
            plan = _moe_plan(sel, cnt, tokens)
            xs = _moe_dispatch(x3, norm_ffn_g[layer], sh2, sc2, plan)
            act = _moe_mm(xs, (moe_w1, moe_w3), li, plan, MOE_UP_TN, "up")
            y = _moe_mm(act, (moe_w2,), li, plan, MOE_DOWN_TN, "down")
            if layer + 1 < depth:
                x2, h = _moe_combine(x2, g2, wts, y, plan[0], seq, norm_mix_g[layer + 1],
                                     mod[layer + 1, :, 0], mod[layer + 1, :, 1])
            else:
                out = _moe_combine(x2, g2, wts, y, plan[0], seq, final_norm_g)
    if out is None:
        out = _final_norm(x2.reshape(batch, seq, d), final_norm_g)
    return out.reshape(batch, TOKEN_RESIDUES, per_res, d).transpose(0, 2, 1, 3).reshape(batch, seq, d)
```

```python
import functools
import math

import jax
import jax.numpy as jnp
from jax import lax
from jax.experimental import pallas as pl
from jax.experimental.pallas import tpu as pltpu

F32 = jnp.float32
BF16 = jnp.bfloat16

LANES = 128
SUBLANES = 8
V7X_VMEM_BYTES = 64 * 1024 * 1024
VMEM_LIMIT_BYTES = V7X_VMEM_BYTES * 7 // 8

ATTN_PATTERNS = ((128, 1), (512, 4), (2048, 16))
N_GROUPS = len(ATTN_PATTERNS)
HEADS = 8
HEAD_DIM = 128
ATTN_OUT = HEADS * HEAD_DIM
ATTN_BLOCK = 128
SSM_GROUP_CH = 16
RMS_EPS = 1e-6
NEG_BIG = -1e30

W_IN_TILE_F32 = (1024, ATTN_OUT)
W_IN_TILE = (2048, ATTN_OUT)
BRANCH_TILE = (1024, 1024)
W_OUT_TILE = (1024, 1024)
W_OUT_NORM_ROWS = 256
SWIGLU_TILE = (2048, 512)
FFN_DOWN_TILE = (512, 512)


def _params(*semantics):
    return pltpu.CompilerParams(dimension_semantics=semantics,
                                vmem_limit_bytes=VMEM_LIMIT_BYTES)


def _sigmoid(v):
    return 1.0 / (1.0 + jnp.exp(-v))


ADA_KR = 256
ADA_CN = 1536


def _ada_kernel(ct_ref, w_ref, b_ref, o_ref, acc_ref):
    kstep = pl.program_id(1)
    kr = w_ref.shape[0]
    ct = ct_ref[pl.ds(pl.multiple_of(kstep * kr, kr), kr), :]
    act = (ct * _sigmoid(ct)).astype(BF16)

    @pl.when(kstep == 0)
    def _():
        acc_ref[...] = jnp.zeros_like(acc_ref)

    for c0 in range(0, w_ref.shape[1], ADA_CN):
        acc_ref[c0:c0 + ADA_CN, :] += lax.dot_general(
            w_ref[:, c0:c0 + ADA_CN].astype(BF16), act, (((0,), (0,)), ((), ())),
            preferred_element_type=F32)

    @pl.when(kstep == pl.num_programs(1) - 1)
    def _():
        for c0 in range(0, w_ref.shape[1], ADA_CN):
            rows = acc_ref[c0:c0 + ADA_CN, :].T[:o_ref.shape[0], :]
            o_ref[:, c0:c0 + ADA_CN] = rows + b_ref[:, c0:c0 + ADA_CN]


def _ada_modulation(c, ada_w, ada_b):
    depth, k, n = ada_w.shape
    nb = c.shape[0]
    ct = jnp.zeros((k, LANES), F32).at[:, :nb].set(c.T)
    out = pl.pallas_call(
        _ada_kernel,
        grid=(depth, k // ADA_KR),
        in_specs=[
            pl.BlockSpec((k, LANES), lambda l, i: (0, 0)),
            pl.BlockSpec((None, ADA_KR, n), lambda l, i: (l, i, 0)),
            pl.BlockSpec((None, 1, n), lambda l, i: (l, 0, 0)),
        ],
        out_specs=pl.BlockSpec((None, SUBLANES, n), lambda l, i: (l, 0, 0)),
        out_shape=jax.ShapeDtypeStruct((depth, SUBLANES, n), F32),
        scratch_shapes=[pltpu.VMEM((n, LANES), F32)],
        compiler_params=_params("arbitrary", "arbitrary"),
        name="ada_modulation",
    )(ct, ada_w, ada_b.reshape(depth, 1, n))
    return out[:, :nb, :]


NORM_TS = 512


def _rms(x, g):
    return x * lax.rsqrt(jnp.mean(x * x, axis=-1, keepdims=True) + RMS_EPS) * g


def _norm_mod_kernel(x_ref, g_ref, sh_ref, sc_ref, o_ref):
    y = _rms(x_ref[...], g_ref[...])
    o_ref[...] = (y * (1.0 + sc_ref[...]) + sh_ref[...]).astype(o_ref.dtype)


def _norm_mod(x, g, shift, scale):
    b, s, d = x.shape
    vec = pl.BlockSpec((None, 1, d), lambda i, j: (i, 0, 0))
    out = pl.pallas_call(
        _norm_mod_kernel,
        grid=(b, s // NORM_TS),
        in_specs=[
            pl.BlockSpec((None, NORM_TS, d), lambda i, j: (i, j, 0)),
            pl.BlockSpec((1, d), lambda i, j: (0, 0)),
            vec, vec,
        ],
        out_specs=pl.BlockSpec((None, NORM_TS, d), lambda i, j: (i, j, 0)),
        out_shape=jax.ShapeDtypeStruct((b, s, d), BF16),
        compiler_params=_params("arbitrary", "arbitrary"),
        name="norm_mod",
    )(x, g.reshape(1, d), shift.reshape(b, 1, d), scale.reshape(b, 1, d))
    return out.reshape(b * s, d)


def _final_norm_kernel(x_ref, g_ref, o_ref):
    o_ref[...] = _rms(x_ref[...], g_ref[...])


def _final_norm(x, g):
    b, s, d = x.shape
    return pl.pallas_call(
        _final_norm_kernel,
        grid=(b, s // NORM_TS),
        in_specs=[
            pl.BlockSpec((None, NORM_TS, d), lambda i, j: (i, j, 0)),
            pl.BlockSpec((1, d), lambda i, j: (0, 0)),
        ],
        out_specs=pl.BlockSpec((None, NORM_TS, d), lambda i, j: (i, j, 0)),
        out_shape=jax.ShapeDtypeStruct((b, s, d), F32),
        compiler_params=_params("arbitrary", "arbitrary"),
        name="final_norm",
    )(x, g.reshape(1, d))


CAST_ROWS = 256


def _cast_weight(w_ref, wb_ref):
    def body(i, carry):
        r = pl.multiple_of(i * CAST_ROWS, CAST_ROWS)
        wb_ref[pl.ds(r, CAST_ROWS), :] = w_ref[pl.ds(r, CAST_ROWS), :].astype(BF16)
        return carry
    lax.fori_loop(0, w_ref.shape[0] // CAST_ROWS, body, 0)


def _dot(a, b):
    return jnp.dot(a, b, preferred_element_type=F32)


def _mm_plain_kernel(a_ref, w_ref, o_ref, wb_ref):
    @pl.when(pl.program_id(1) == 0)
    def _():
        _cast_weight(w_ref, wb_ref)
    o_ref[...] = _dot(a_ref[...], wb_ref[...]).astype(o_ref.dtype)


def _mm_plain(a, w, w_idx, col_tile, n_tiles, out_dtype, tm, tn):
    m, k = a.shape
    lead = (None,) * len(w_idx)
    return pl.pallas_call(
        _mm_plain_kernel,
        grid=(n_tiles, m // tm),
        in_specs=[
            pl.BlockSpec((tm, k), lambda j, i: (i, 0)),
            pl.BlockSpec(lead + (k, tn), lambda j, i: w_idx + (0, col_tile(j))),
        ],
        out_specs=pl.BlockSpec((tm, tn), lambda j, i: (i, j)),
        out_shape=jax.ShapeDtypeStruct((m, n_tiles * tn), out_dtype),
        scratch_shapes=[pltpu.VMEM((k, tn), BF16)],
        compiler_params=_params("arbitrary", "arbitrary"),
        name="mm_plain",
    )(a, w)


def _mm_swiglu_kernel(a_ref, w1_ref, w3_ref, o_ref, wb1_ref, wb3_ref):
    @pl.when(pl.program_id(1) == 0)
    def _():
        _cast_weight(w1_ref, wb1_ref)
        _cast_weight(w3_ref, wb3_ref)
    a = a_ref[...]
    u = _dot(a, wb1_ref[...])
    v = _dot(a, wb3_ref[...])
    o_ref[...] = (u * _sigmoid(u) * v).astype(o_ref.dtype)


def _mm_swiglu(a, w1, w3, w_idx, tm, tn):
    m, k = a.shape
    n = w1.shape[-1]
    lead = (None,) * len(w_idx)
    wspec = pl.BlockSpec(lead + (k, tn), lambda j, i: w_idx + (0, j))
    return pl.pallas_call(
        _mm_swiglu_kernel,
        grid=(n // tn, m // tm),
        in_specs=[pl.BlockSpec((tm, k), lambda j, i: (i, 0)), wspec, wspec],
        out_specs=pl.BlockSpec((tm, tn), lambda j, i: (i, j)),
        out_shape=jax.ShapeDtypeStruct((m, n), BF16),
        scratch_shapes=[pltpu.VMEM((k, tn), BF16), pltpu.VMEM((k, tn), BF16)],
        compiler_params=_params("arbitrary", "arbitrary"),
        name="mm_swiglu",
    )(a, w1, w3)


def _mm_resid_kernel(a_ref, w_ref, x_ref, g_ref, o_ref, wb_ref):
    @pl.when(pl.program_id(1) == 0)
    def _():
        _cast_weight(w_ref, wb_ref)
    o_ref[...] = x_ref[...] + g_ref[...] * _dot(a_ref[...], wb_ref[...])


def _mm_resid(a, w, w_idx, x, gate, seq, tm, tn):
    m, k = a.shape
    n = w.shape[-1]
    nb = gate.shape[0]
    lead = (None,) * len(w_idx)
    return pl.pallas_call(
        _mm_resid_kernel,
        grid=(n // tn, m // tm),
        in_specs=[
            pl.BlockSpec((tm, k), lambda j, i: (i, 0)),
            pl.BlockSpec(lead + (k, tn), lambda j, i: w_idx + (0, j)),
            pl.BlockSpec((tm, tn), lambda j, i: (i, j)),
            pl.BlockSpec((None, 1, tn), lambda j, i: (i * tm // seq, 0, j)),
        ],
        out_specs=pl.BlockSpec((tm, tn), lambda j, i: (i, j)),
        out_shape=jax.ShapeDtypeStruct((m, n), F32),
        scratch_shapes=[pltpu.VMEM((k, tn), BF16)],
        compiler_params=_params("arbitrary", "arbitrary"),
        name="mm_resid",
    )(a, w, x, gate.reshape(nb, 1, n))


def _mm_resid_norm_kernel(a_ref, w_ref, x_ref, g_ref, ng_ref, sh_ref, sc_ref,
                          xo_ref, ho_ref, wb_ref):
    @pl.when(pl.program_id(0) == 0)
    def _():
        _cast_weight(w_ref, wb_ref)
    xn = x_ref[...] + g_ref[...] * _dot(a_ref[...], wb_ref[...])
    xo_ref[...] = xn
    yn = _rms(xn, ng_ref[...])
    ho_ref[...] = (yn * (1.0 + sc_ref[...]) + sh_ref[...]).astype(ho_ref.dtype)


def _mm_resid_norm(a, w, layer, x, gate, norm_g, shift, scale, seq, tm):
    m, k = a.shape
    n = w.shape[-1]
    nb = gate.shape[0]
    vec = pl.BlockSpec((None, 1, n), lambda i: (i * tm // seq, 0, 0))
    row = pl.BlockSpec((tm, n), lambda i: (i, 0))
    return pl.pallas_call(
        _mm_resid_norm_kernel,
        grid=(m // tm,),
        in_specs=[
            pl.BlockSpec((tm, k), lambda i: (i, 0)),
            pl.BlockSpec((None, k, n), lambda i: (layer, 0, 0), pipeline_mode=pl.Buffered(1)),
            row, vec,
            pl.BlockSpec((1, n), lambda i: (0, 0)),
            vec, vec,
        ],
        out_specs=[row, row],
        out_shape=[jax.ShapeDtypeStruct((m, n), F32), jax.ShapeDtypeStruct((m, n), BF16)],
        scratch_shapes=[pltpu.VMEM((k, n), BF16)],
        compiler_params=_params("arbitrary"),
        name="mm_resid_norm",
    )(a, w, x, gate.reshape(nb, 1, n), norm_g.reshape(1, n),
      shift.reshape(nb, 1, n), scale.reshape(nb, 1, n))


def _mm_branch_kernel(at_ref, ss_ref, wa_ref, ws_ref, ga_ref, gs_ref, o_ref,
                      wba_ref, wbs_ref):
    @pl.when(pl.program_id(1) == 0)
    def _():
        _cast_weight(wa_ref, wba_ref)
        _cast_weight(ws_ref, wbs_ref)
    pa = _dot(at_ref[...], wba_ref[...])
    ps = _dot(ss_ref[...], wbs_ref[...])
    ga = _sigmoid(ga_ref[...].astype(F32))
    gs = _sigmoid(gs_ref[...].astype(F32))
    o_ref[...] = (ga * pa + gs * ps).astype(o_ref.dtype)


def _mm_branch(attn, ssm, w_branch, layer, proj, gate_col, tm, tn):
    m, k = attn.shape
    n = w_branch.shape[-1]
    g0 = gate_col // tn
    g1 = (gate_col + n) // tn
    return pl.pallas_call(
        _mm_branch_kernel,
        grid=(n // tn, m // tm),
        in_specs=[
            pl.BlockSpec((tm, k), lambda j, i: (i, 0)),
            pl.BlockSpec((tm, k), lambda j, i: (i, 0)),
            pl.BlockSpec((None, k, tn), lambda j, i: (layer, 0, j)),
            pl.BlockSpec((None, k, tn), lambda j, i: (layer, 1, j)),
            pl.BlockSpec((tm, tn), lambda j, i: (i, g0 + j)),
            pl.BlockSpec((tm, tn), lambda j, i: (i, g1 + j)),
        ],
        out_specs=pl.BlockSpec((tm, tn), lambda j, i: (i, j)),
        out_shape=jax.ShapeDtypeStruct((m, n), BF16),
        scratch_shapes=[pltpu.VMEM((k, tn), BF16), pltpu.VMEM((k, tn), BF16)],
        compiler_params=_params("arbitrary", "arbitrary"),
        name="mm_branch",
    )(attn, ssm, w_branch, w_branch, proj, proj)


TOKEN_RESIDUES = 16
ATTN_BB = 4


def _attn_kernel(*refs, with_prev):
    if with_prev:
        q_ref, kc_ref, vc_ref, bc_ref, bp_ref, o_ref, lse_ref, kp_ref, vp_ref = refs

        @pl.when(pl.program_id(2) == 0)
        def _():
            kp_ref[...] = jnp.zeros_like(kp_ref)
            vp_ref[...] = jnp.zeros_like(vp_ref)
    else:
        q_ref, kc_ref, vc_ref, bc_ref, o_ref, lse_ref = refs
    blk = ATTN_BLOCK
    bb, nc, rpc = q_ref.shape[0], q_ref.shape[1], q_ref.shape[2]
    scale = HEAD_DIM ** -0.5
    lane = lax.broadcasted_iota(jnp.int32, (blk, LANES), 1)

    def heads(ref, bi):
        x = ref[bi].reshape(blk, ATTN_OUT)
        return jnp.stack([x[:, h * HEAD_DIM:(h + 1) * HEAD_DIM] for h in range(HEADS)]
                         ).astype(BF16)

    def qk(q, k):
        return lax.dot_general(q, k, (((2,), (2,)), ((0,), (0,))), preferred_element_type=F32)

    def pv(p, v):
        return lax.dot_general(p.astype(BF16), v, (((2,), (1,)), ((0,), (0,))),
                               preferred_element_type=F32)

    for bi in range(bb):
        q = heads(q_ref, bi)
        k_c = heads(kc_ref, bi)
        v_c = heads(vc_ref, bi)
        s_c = qk(q, k_c) * scale + bc_ref[...]
        m = jnp.max(s_c, axis=2, keepdims=True)
        if with_prev:
            bias_p = jnp.where(pl.program_id(2) > 0, bp_ref[...], NEG_BIG)
            s_p = qk(q, kp_ref[bi]) * scale + bias_p
            m = jnp.maximum(m, jnp.max(s_p, axis=2, keepdims=True))
        p_c = jnp.exp(s_c - m)
        l = jnp.sum(p_c, axis=2, keepdims=True)
        o = pv(p_c, v_c)
        if with_prev:
            p_p = jnp.exp(s_p - m)
            l = l + jnp.sum(p_p, axis=2, keepdims=True)
            o = o + pv(p_p, vp_ref[bi])
            kp_ref[bi] = k_c
            vp_ref[bi] = v_c
        o = o / l
        lse = m + jnp.log(l)
        lse_tile = jnp.zeros((blk, LANES), F32)
        for h in range(HEADS):
            sl = slice(h * HEAD_DIM, (h + 1) * HEAD_DIM)
            o_ref[bi, :, :, sl] = o[h].reshape(nc, rpc, HEAD_DIM).astype(o_ref.dtype)
            lse_tile = jnp.where(lane == h, lse[h], lse_tile)
        lse_ref[bi] = lse_tile.reshape(nc, rpc, LANES)


def _attn_bias(dilation, nc, back):
    rpc = ATTN_BLOCK // nc
    i = jnp.arange(ATTN_BLOCK, dtype=jnp.int32)
    off = nc * (i % rpc) + i // rpc
    dist = off[:, None] - off[None, :] + back * ATTN_BLOCK
    valid = jnp.logical_and(dist >= 0, dist <= ATTN_BLOCK)
    slopes = 2.0 ** (-8.0 * (jnp.arange(HEADS, dtype=F32) + 1.0) / HEADS)
    pen = slopes[:, None, None] * (dist * dilation).astype(F32)[None]
    return jnp.where(valid[None], -pen, NEG_BIG)


def _attention_group(qkv, qkv_cols, batch, seq, gi):
    window, dilation = ATTN_PATTERNS[gi]
    assert window // dilation == ATTN_BLOCK
    cols = qkv.shape[-1]
    per_res = seq // TOKEN_RESIDUES
    nc = TOKEN_RESIDUES // dilation
    rpc = ATTN_BLOCK // nc
    nblk = seq // dilation // ATTN_BLOCK
    with_prev = nblk > 1

    def shape5(c):
        return (batch, nc, dilation, per_res, c)

    def spec(c, col):
        return pl.BlockSpec((ATTN_BB, nc, None, rpc, c), lambda b, r, n: (b, 0, r, n, col))

    bias_spec = pl.BlockSpec((HEADS, ATTN_BLOCK, ATTN_BLOCK), lambda b, r, n: (0, 0, 0))
    in_specs = [spec(ATTN_OUT, col) for col in qkv_cols]
    carry = pltpu.VMEM((ATTN_BB, HEADS, ATTN_BLOCK, HEAD_DIM), BF16)
    view = qkv.reshape(shape5(cols))
    args = [view] * len(in_specs) + [_attn_bias(dilation, nc, 0)]
    in_specs.append(bias_spec)
    if with_prev:
        args.append(_attn_bias(dilation, nc, 1))
        in_specs.append(bias_spec)
    o, lse = pl.pallas_call(
        functools.partial(_attn_kernel, with_prev=with_prev),
        grid=(batch // ATTN_BB, dilation, nblk),
        in_specs=in_specs,
        out_specs=[spec(ATTN_OUT, 0), spec(LANES, 0)],
        out_shape=[
            jax.ShapeDtypeStruct(shape5(ATTN_OUT), qkv.dtype),
            jax.ShapeDtypeStruct(shape5(LANES), F32),
        ],
        scratch_shapes=[carry, carry] if with_prev else [],
        compiler_params=_params("arbitrary", "arbitrary", "arbitrary"),
        name=f"attn_g{gi}",
    )(*args)
    return o.reshape(batch * seq, ATTN_OUT), lse.reshape(batch * seq, LANES)


COMBINE_TS = 512


def _combine_kernel(o0_ref, o1_ref, o2_ref, l0_ref, l1_ref, l2_ref, out_ref):
    a0, a1, a2 = l0_ref[...], l1_ref[...], l2_ref[...]
    m = jnp.maximum(jnp.maximum(a0, a1), a2)
    e0 = jnp.exp(a0 - m)
    e1 = jnp.exp(a1 - m)
    e2 = jnp.exp(a2 - m)
    inv = 1.0 / (e0 + e1 + e2)
    w0, w1, w2 = e0 * inv, e1 * inv, e2 * inv
    for h in range(HEADS):
        sl = slice(h * HEAD_DIM, (h + 1) * HEAD_DIM)
        mix = (w0[:, h:h + 1] * o0_ref[:, sl].astype(F32)
               + w1[:, h:h + 1] * o1_ref[:, sl].astype(F32)
               + w2[:, h:h + 1] * o2_ref[:, sl].astype(F32))
        out_ref[:, sl] = mix.astype(out_ref.dtype)


def _dilation_mixture_attention(qkv0, rest, batch, seq):
    outs, lses = [], []
    for gi in range(N_GROUPS):
        if gi == 0:
            o, lse = _attention_group(qkv0, (0, 1, 2), batch, seq, gi)
        else:
            o, lse = _attention_group(rest, (gi - 1, gi + 1, gi + 3), batch, seq, gi)
        outs.append(o)
        lses.append(lse)
    m = batch * seq
    ospec = pl.BlockSpec((COMBINE_TS, ATTN_OUT), lambda i: (i, 0))
    lspec = pl.BlockSpec((COMBINE_TS, LANES), lambda i: (i, 0))
    return pl.pallas_call(
        _combine_kernel,
        grid=(m // COMBINE_TS,),
        in_specs=[ospec] * 3 + [lspec] * 3,
        out_specs=ospec,
        out_shape=jax.ShapeDtypeStruct((m, ATTN_OUT), BF16),
        compiler_params=_params("arbitrary"),
        name="attn_combine",
    )(*outs, *lses)


SSM_LT = 128
SSM_PASSES = 2
SSM_JBLK = 256


def _gelu_tanh(y):
    return 0.5 * y * (1.0 + jnp.tanh(math.sqrt(2.0 / math.pi) * (y + 0.044715 * (y * y * y))))


def _ssm_kernel(u_ref, pin_ref, pout_ref, bblk_ref, cblk_ref, are_ref, aim_ref, d_ref,
                wglu_ref, bglu_ref, o_ref, sre_ref, sim_ref, hre_ref, him_ref, *, nbatch):
    width = u_ref.shape[-1]
    rows = pin_ref.shape[1]
    njb = width // SSM_JBLK
    jstates = bblk_ref.shape[2] // 2
    pairs = rows // SUBLANES
    per_tile = SUBLANES // nbatch

    @pl.when(pl.program_id(0) == 0)
    def _():
        hre_ref[...] = jnp.zeros_like(hre_ref)
        him_ref[...] = jnp.zeros_like(him_ref)

    u_rm = u_ref[...].reshape(pout_ref.shape[1], width)
    out_rm = None
    for p in range(pin_ref.shape[0]):
        u_f32 = _dot(pin_ref[p], u_rm)
        u_tb = u_f32.astype(BF16)
        ys = []
        for j in range(njb):
            bu = _dot(u_tb[:, j * SSM_JBLK:(j + 1) * SSM_JBLK], bblk_ref[j])
            sre_ref[...] = bu[:, :jstates].reshape(pairs, SUBLANES, jstates)
            sim_ref[...] = bu[:, jstates:].reshape(pairs, SUBLANES, jstates)
            cs = slice(j * jstates, (j + 1) * jstates)
            ar = are_ref[0:nbatch, cs]
            ai = aim_ref[0:nbatch, cs]

            def step(k, carry):
                hr, hi = carry
                for t in range(per_tile):
                    rs = slice(t * nbatch, (t + 1) * nbatch)
                    nr = ar * hr - ai * hi + sre_ref[k, rs, :]
                    ni = ar * hi + ai * hr + sim_ref[k, rs, :]
                    sre_ref[k, rs, :] = nr
                    sim_ref[k, rs, :] = ni
                    hr, hi = nr, ni
                return hr, hi

            hr, hi = lax.fori_loop(0, pairs, step,
                                   (hre_ref[0:nbatch, cs], him_ref[0:nbatch, cs]), unroll=4)
            hre_ref[0:nbatch, cs] = hr
            him_ref[0:nbatch, cs] = hi

            h_re = sre_ref[...].reshape(rows, jstates).astype(BF16)
            h_im = sim_ref[...].reshape(rows, jstates).astype(BF16)
            ys.append(_dot(h_re, cblk_ref[j, 0:jstates, :]) + _dot(h_im, cblk_ref[j, jstates:, :]))

        y = jnp.concatenate(ys, axis=1) + d_ref[...] * u_f32
        z = _gelu_tanh(y).astype(BF16)
        g = _dot(z, wglu_ref[...]) + bglu_ref[...]
        o_tb = (g[:, :width] * _sigmoid(g[:, width:])).astype(BF16)
        back = _dot(pout_ref[p], o_tb)
        out_rm = back if out_rm is None else out_rm + back
    o_ref[...] = out_rm.astype(o_ref.dtype).reshape(o_ref.shape)


def _ssm_tables(a_re, a_im, log_dt, b_re, b_im, c_re, c_im):
    groups, nst = a_re.shape
    gpb = SSM_JBLK // SSM_GROUP_CH
    njb = groups // gpb
    lam = lax.complex(a_re.astype(F32), a_im.astype(F32))
    dt = jnp.exp(log_dt.astype(F32))[:, None]
    a_bar = jnp.exp(lam * dt)
    b_mat = lax.complex(b_re.astype(F32), b_im.astype(F32))
    b_bar = ((a_bar - 1.0) / lam)[:, :, None] * b_mat
    same_group = jnp.eye(gpb, dtype=jnp.bool_)

    def block_diag(t):
        r, q = t.shape[1:]
        t = t.astype(BF16).reshape(njb, gpb, r, 1, q)
        wide = jnp.where(same_group[None, :, None, :, None], t, jnp.zeros((), BF16))
        return wide.reshape(njb, gpb * r, gpb * q)

    def in_blocks(t):
        return block_diag(t.transpose(0, 2, 1))

    def out_blocks(t):
        return block_diag(t.transpose(0, 2, 1))

    bblk = jnp.concatenate([in_blocks(jnp.real(b_bar)), in_blocks(jnp.imag(b_bar))], axis=2)
    cblk = jnp.concatenate([out_blocks(c_re.astype(F32)), out_blocks(-c_im.astype(F32))], axis=1)
    are = jnp.broadcast_to(jnp.real(a_bar).reshape(1, groups * nst), (SUBLANES, groups * nst))
    aim = jnp.broadcast_to(jnp.imag(a_bar).reshape(1, groups * nst), (SUBLANES, groups * nst))
    return bblk.astype(BF16), cblk.astype(BF16), are, aim


def _ssm_row_perms(nbatch):
    per_res = SSM_PASSES * SSM_LT // TOKEN_RESIDUES
    col = jnp.arange(nbatch * TOKEN_RESIDUES * per_res, dtype=jnp.int32)
    b = col // (TOKEN_RESIDUES * per_res)
    t_local = TOKEN_RESIDUES * (col % per_res) + (col // per_res) % TOKEN_RESIDUES
    row = (t_local % SSM_LT) * nbatch + b
    pin = jnp.logical_and(
        (t_local // SSM_LT)[None, None, :] == jnp.arange(SSM_PASSES, dtype=jnp.int32)[:, None, None],
        row[None, None, :] == jnp.arange(SSM_LT * nbatch, dtype=jnp.int32)[None, :, None])
    pin = pin.astype(BF16)
    return pin, pin.transpose(0, 2, 1)


def _s5_ssm(src, u_col, nbatch, seq, tables, d_skip, w_glu, b_glu):
    bblk, cblk, are, aim = tables
    width = w_glu.shape[0]
    nstate = are.shape[1]
    jstates = bblk.shape[2] // 2
    rows = SSM_LT * nbatch
    step_rows = SSM_PASSES * SSM_LT // TOKEN_RESIDUES
    per_res = seq // TOKEN_RESIDUES
    pin, pout = _ssm_row_perms(nbatch)
    d2 = d_skip.reshape(1, width).astype(F32)
    wg = w_glu.astype(BF16)
    bg = b_glu.reshape(1, 2 * width).astype(F32)

    def const(a):
        return pl.BlockSpec(a.shape, lambda i: (0,) * a.ndim, pipeline_mode=pl.Buffered(1))

    blk = (nbatch, TOKEN_RESIDUES, step_rows, width)
    out = pl.pallas_call(
        functools.partial(_ssm_kernel, nbatch=nbatch),
        grid=(per_res // step_rows,),
        in_specs=[pl.BlockSpec(blk, lambda i: (0, 0, i, u_col)),
                  const(pin), const(pout), const(bblk), const(cblk), const(are), const(aim),
                  const(d2), const(wg), const(bg)],
        out_specs=pl.BlockSpec(blk, lambda i: (0, 0, i, 0)),
        out_shape=jax.ShapeDtypeStruct((nbatch, TOKEN_RESIDUES, per_res, width), BF16),
        scratch_shapes=[
            pltpu.VMEM((rows // SUBLANES, SUBLANES, jstates), F32),
            pltpu.VMEM((rows // SUBLANES, SUBLANES, jstates), F32),
            pltpu.VMEM((SUBLANES, nstate), F32),
            pltpu.VMEM((SUBLANES, nstate), F32),
        ],
        compiler_params=_params("arbitrary"),
        name="s5_ssm",
    )(src.reshape(nbatch, TOKEN_RESIDUES, per_res, src.shape[-1]),
      pin, pout, bblk, cblk, are, aim, d2, wg, bg)
    return out.reshape(nbatch * seq, width)


ROUTER_TS = 256


def _router_kernel(x_ref, g_ref, sh_ref, sc_ref, rw_ref, rb_ref,
                   sel_ref, wts_ref, cnt_ref, carry_ref):
    @pl.when(jnp.logical_and(pl.program_id(0) == 0, pl.program_id(1) == 0))
    def _():
        carry_ref[...] = jnp.zeros_like(carry_ref)

    h = _rms(x_ref[...], g_ref[...]) * (1.0 + sc_ref[...]) + sh_ref[...]
    ts, ne = h.shape[0], rw_ref.shape[0]
    idx = lax.broadcasted_iota(jnp.int32, (ts, ne), 1)
    logits = jnp.broadcast_to(rb_ref[...], (ts, ne))
    for ex in range(ne):
        col = jnp.sum(h * rw_ref[ex:ex + 1, :], axis=1, keepdims=True)
        logits = logits + jnp.where(idx == ex, col, 0.0)
    m1 = jnp.max(logits, axis=1, keepdims=True)
    i1 = jnp.min(jnp.where(logits == m1, idx, ne), axis=1, keepdims=True)
    rest = jnp.where(idx == i1, -jnp.inf, logits)
    m2 = jnp.max(rest, axis=1, keepdims=True)
    i2 = jnp.min(jnp.where(rest == m2, idx, ne), axis=1, keepdims=True)
    e = jnp.exp(m2 - m1)
    w1 = 1.0 / (1.0 + e)
    w2 = e / (1.0 + e)

    onehot = jnp.where(idx == i1, 1.0, 0.0) + jnp.where(idx == i2, 1.0, 0.0)
    row = lax.broadcasted_iota(jnp.int32, (ts, ts), 0)
    col = lax.broadcasted_iota(jnp.int32, (ts, ts), 1)
    lower = jnp.where(col < row, 1.0, 0.0).astype(BF16)
    before = carry_ref[...] + _dot(lower, onehot.astype(BF16))
    r1 = jnp.sum(jnp.where(idx == i1, before, 0.0), axis=1, keepdims=True).astype(jnp.int32)
    r2 = jnp.sum(jnp.where(idx == i2, before, 0.0), axis=1, keepdims=True).astype(jnp.int32)
    total = carry_ref[...] + jnp.sum(onehot, axis=0, keepdims=True)
    carry_ref[...] = total
    cnt_ref[...] = total
    sel_ref[...] = jnp.where(idx == 0, i1, jnp.where(idx == 1, i2, jnp.where(
        idx == 2, r1, jnp.where(idx == 3, r2, 0))))
    wts_ref[...] = jnp.where(idx == 0, w1, jnp.where(idx == 1, w2, 0.0))


def _router(x, g, shift, scale, router_w, router_b):
    b, s, d = x.shape
    ne = router_w.shape[-1]
    vec = pl.BlockSpec((None, 1, d), lambda i, j: (i, 0, 0))
    tok = pl.BlockSpec((None, ROUTER_TS, ne), lambda i, j: (i, j, 0))
    sel, wts, cnt = pl.pallas_call(
        _router_kernel,
        grid=(b, s // ROUTER_TS),
        in_specs=[
            pl.BlockSpec((None, ROUTER_TS, d), lambda i, j: (i, j, 0)),
            pl.BlockSpec((1, d), lambda i, j: (0, 0)),
            vec, vec,
            pl.BlockSpec((ne, d), lambda i, j: (0, 0)),
            pl.BlockSpec((1, ne), lambda i, j: (0, 0)),
        ],
        out_specs=[tok, tok, pl.BlockSpec((1, ne), lambda i, j: (0, 0))],
        out_shape=[jax.ShapeDtypeStruct((b, s, ne), jnp.int32),
                   jax.ShapeDtypeStruct((b, s, ne), F32),
                   jax.ShapeDtypeStruct((1, ne), F32)],
        scratch_shapes=[pltpu.VMEM((1, ne), F32)],
        compiler_params=_params("arbitrary", "arbitrary"),
        name="router",
    )(x, g.reshape(1, d), shift.reshape(b, 1, d), scale.reshape(b, 1, d),
      router_w.T, router_b.reshape(1, ne))
    return sel.reshape(b * s, ne), wts.reshape(b * s, ne), cnt


MOE_TM = 256
MOE_TT = 256
MOE_UP_TN = 1408
MOE_DOWN_TN = 2048


def _moe_plan(sel, cnt, tokens):
    ne = cnt.shape[-1]
    counts = cnt[0].astype(jnp.int32)
    padded = (counts + MOE_TM - 1) // MOE_TM * MOE_TM
    ends = jnp.cumsum(padded)
    starts = ends - padded
    pos1 = starts[sel[:, 0]] + sel[:, 2]
    pos2 = starts[sel[:, 1]] + sel[:, 3]
    nt = tokens // MOE_TT
    pos = jnp.concatenate([pos1.reshape(nt, MOE_TT), pos2.reshape(nt, MOE_TT)], axis=1)
    max_tiles = 2 * tokens // MOE_TM + ne
    tile_start = jnp.arange(max_tiles, dtype=jnp.int32) * MOE_TM
    tile_expert = jnp.minimum(
        jnp.sum((tile_start[:, None] >= ends[None, :]).astype(jnp.int32), axis=1), ne - 1)
    num_tiles = (ends[-1] // MOE_TM).reshape(1)
    zero_tiles = jnp.concatenate([
        jnp.maximum(ends // MOE_TM - 1, 0),
        jnp.minimum(num_tiles[0] + jnp.arange(ne, dtype=jnp.int32), max_tiles - 1)])
    nonempty = counts > 0
    ar = jnp.arange(ne, dtype=jnp.int32)
    order = (ar[:, None] + 1 + ar[None, :]) % ne
    nxt = order[ar, jnp.argmax(nonempty[order], axis=1)].astype(jnp.int32)
    gidx = jnp.cumsum(nonempty.astype(jnp.int32)) - 1
    groups = jnp.concatenate([nxt, gidx, jnp.sum(nonempty.astype(jnp.int32)).reshape(1)])
    return (pos.reshape(nt, 1, 2 * MOE_TT), tile_expert, num_tiles, max_tiles, zero_tiles,
            groups)


def _row_copies(pos_ref, t, make):
    tt = pos_ref.shape[1] // 2
    return make(0, t, pos_ref[0, t]), make(1, t, pos_ref[0, tt + t])


def _issue_and_drain(pos_ref, make):
    tt = pos_ref.shape[1] // 2

    def issue(t, carry):
        for k, cp in enumerate(_row_copies(pos_ref, t, make)):
            cp.start(priority=k)
        return carry

    def drain(t, carry):
        for cp in _row_copies(pos_ref, t, make):
            cp.wait()
        return carry

    lax.fori_loop(0, tt, issue, 0, unroll=4)
    lax.fori_loop(0, tt, drain, 0, unroll=4)


def _pack_pairs(lo, hi):
    lo_bits = pltpu.bitcast(lo.astype(BF16).astype(F32), jnp.uint32)
    hi_bits = pltpu.bitcast(hi.astype(BF16).astype(F32), jnp.uint32)
    return jnp.bitwise_or(jnp.bitwise_and(hi_bits, jnp.uint32(0xFFFF0000)),
                          jnp.right_shift(lo_bits, jnp.uint32(16)))


def _unpack_pairs(words):
    lo = pltpu.bitcast(jnp.left_shift(words, jnp.uint32(16)), F32)
    hi = pltpu.bitcast(jnp.bitwise_and(words, jnp.uint32(0xFFFF0000)), F32)
    return lo, hi


def _moe_dispatch_kernel(pos_ref, zt_ref, x_ref, g_ref, sh_ref, sc_ref, xs_ref, hbuf_ref, sem):
    @pl.when(jnp.logical_and(pl.program_id(0) == 0, pl.program_id(1) == 0))
    def _():
        hbuf_ref[...] = jnp.zeros_like(hbuf_ref)
        for k in range(zt_ref.shape[0]):
            cp = pltpu.make_async_copy(
                hbuf_ref, xs_ref.at[pl.ds(pl.multiple_of(zt_ref[k] * MOE_TM, MOE_TM), MOE_TM), :],
                sem)
            cp.start()
            cp.wait()

    h = _rms(x_ref[...], g_ref[...]) * (1.0 + sc_ref[...]) + sh_ref[...]
    half = h.shape[1] // 2
    hbuf_ref[...] = _pack_pairs(h[:, :half], h[:, half:])

    def make(k, t, p):
        return pltpu.make_async_copy(hbuf_ref.at[pl.ds(t, 1), :], xs_ref.at[pl.ds(p, 1), :], sem)

    _issue_and_drain(pos_ref, make)


def _moe_dispatch(x, g, shift, scale, plan):
    pos, _, _, max_tiles, zero_tiles, _ = plan
    b, s, d = x.shape
    per_b = s // MOE_TT
    assert MOE_TT == MOE_TM
    vec = pl.BlockSpec((None, 1, d), lambda i, j: (i, 0, 0))
    return pl.pallas_call(
        _moe_dispatch_kernel,
        grid=(b, per_b),
        in_specs=[
            pl.BlockSpec((None, 1, 2 * MOE_TT), lambda i, j: (i * per_b + j, 0, 0),
                         memory_space=pltpu.SMEM),
            pl.BlockSpec(memory_space=pltpu.SMEM),
            pl.BlockSpec((None, MOE_TT, d), lambda i, j: (i, j, 0)),
            pl.BlockSpec((1, d), lambda i, j: (0, 0)),
            vec, vec,
        ],
        out_specs=pl.BlockSpec(memory_space=pl.ANY),
        out_shape=jax.ShapeDtypeStruct((max_tiles * MOE_TM, d // 2), jnp.uint32),
        scratch_shapes=[pltpu.VMEM((MOE_TT, d // 2), jnp.uint32), pltpu.SemaphoreType.DMA(())],
        compiler_params=_params("arbitrary", "arbitrary"),
        name="moe_dispatch",
    )(pos, zero_tiles, x, g.reshape(1, d), shift.reshape(b, 1, d), scale.reshape(b, 1, d))


def _moe_mm_kernel(te_ref, nt_ref, grp_ref, a_ref, *rest, mode, li, tn):
    nw = 2 if mode == "up" else 1
    w_hbm = rest[:nw]
    o_ref = rest[nw]
    wb = rest[nw + 1:2 * nw + 1]
    stage = rest[2 * nw + 1:3 * nw + 1]
    wsem = rest[3 * nw + 1]
    j = pl.program_id(0)
    i = pl.program_id(1)
    ne = (grp_ref.shape[0] - 1) // 2
    expert = te_ref[i]
    active = i < nt_ref[0]
    new_expert = jnp.logical_or(i == 0, expert != te_ref[jnp.maximum(i - 1, 0)])

    def weight_copies(ex, jx):
        cols = pl.ds(pl.multiple_of(jx * tn, LANES), tn)
        return [pltpu.make_async_copy(w_hbm[k].at[li, ex, :, cols], stage[k], wsem.at[k])
                for k in range(nw)]

    @pl.when(jnp.logical_and(active, new_expert))
    def _():
        @pl.when(jnp.logical_and(j == 0, grp_ref[ne + expert] == 0))
        def _():
            for cp in weight_copies(expert, j):
                cp.start()

        for cp in weight_copies(expert, j):
            cp.wait()
        for k in range(nw):
            _cast_weight(stage[k], wb[k])
        nxt = grp_ref[expert]
        nxt_j = j + jnp.where(nxt <= expert, 1, 0)

        @pl.when(nxt_j < pl.num_programs(0))
        def _():
            for cp in weight_copies(nxt, nxt_j):
                cp.start()

    @pl.when(active)
    def _():
        if mode == "down":
            p = lax.dot_general(wb[0][...], a_ref[...], (((0,), (0,)), ((), ())),
                                preferred_element_type=F32).T
            half = p.shape[1] // 2
            o_ref[...] = _pack_pairs(p[:, :half], p[:, half:])
        else:
            lo, hi = _unpack_pairs(a_ref[...])
            lo, hi = lo.astype(BF16), hi.astype(BF16)
            half = lo.shape[1]
            nt_dims = (((0,), (1,)), ((), ()))

            def proj(w_ref):
                return (lax.dot_general(w_ref[0:half, :], lo, nt_dims, preferred_element_type=F32)
                        + lax.dot_general(w_ref[half:, :], hi, nt_dims,
                                          preferred_element_type=F32))

            u = proj(wb[0])
            o_ref[...] = (u * _sigmoid(u) * proj(wb[1])).astype(o_ref.dtype)

    @pl.when(jnp.logical_not(active))
    def _():
        o_ref[...] = jnp.zeros_like(o_ref)


def _moe_mm(a, ws, li, plan, tn, mode):
    _, tile_expert, num_tiles, max_tiles, _, groups = plan
    k, n = ws[0].shape[-2], ws[0].shape[-1]
    rows = max_tiles * MOE_TM
    nw = len(ws)

    def tile(i, nt):
        return jnp.minimum(i, nt[0] - 1)

    if mode == "down":
        a_spec = pl.BlockSpec((k, MOE_TM), lambda j, i, te, nt, gr: (0, tile(i, nt)))
        out_spec = pl.BlockSpec((MOE_TM, tn // 2), lambda j, i, te, nt, gr: (i, j))
        out_shape = jax.ShapeDtypeStruct((rows, n // 2), jnp.uint32)
    else:
        a_spec = pl.BlockSpec((MOE_TM, k // 2), lambda j, i, te, nt, gr: (tile(i, nt), 0))
        out_spec = pl.BlockSpec((tn, MOE_TM), lambda j, i, te, nt, gr: (j, i))
        out_shape = jax.ShapeDtypeStruct((n, rows), BF16)
    return pl.pallas_call(
        functools.partial(_moe_mm_kernel, mode=mode, li=li, tn=tn),
        grid_spec=pltpu.PrefetchScalarGridSpec(
            num_scalar_prefetch=3,
            grid=(n // tn, max_tiles),
            in_specs=[a_spec] + [pl.BlockSpec(memory_space=pl.ANY)] * nw,
            out_specs=out_spec,
            scratch_shapes=([pltpu.VMEM((k, tn), BF16)] * nw + [pltpu.VMEM((k, tn), F32)] * nw
                            + [pltpu.SemaphoreType.DMA((nw,))]),
        ),
        out_shape=out_shape,
        compiler_params=_params("arbitrary", "arbitrary"),
        name=f"moe_mm_{mode}",
    )(tile_expert, num_tiles, groups, a, *ws)


def _moe_combine_kernel(pos_ref, x_ref, g_ref, wts_ref, y_ref, *rest, final):
    if final:
        ng_ref, o_ref, ybuf_ref, sem = rest
    else:
        ng_ref, sh_ref, sc_ref, o_ref, h_ref, ybuf_ref, sem = rest

    def make(k, t, p):
        return pltpu.make_async_copy(y_ref.at[pl.ds(p, 1), :], ybuf_ref.at[k, pl.ds(t, 1), :], sem)

    _issue_and_drain(pos_ref, make)

    def rows(k):
        lo, hi = _unpack_pairs(ybuf_ref[k])
        hw = MOE_DOWN_TN // 2
        parts = []
        for c in range(lo.shape[1] // hw):
            parts += [lo[:, c * hw:(c + 1) * hw], hi[:, c * hw:(c + 1) * hw]]
        return jnp.concatenate(parts, axis=1)

    f = wts_ref[:, 0:1] * rows(0) + wts_ref[:, 1:2] * rows(1)
    xn = x_ref[...] + g_ref[...] * f
    yn = _rms(xn, ng_ref[...])
    if final:
        o_ref[...] = yn
    else:
        o_ref[...] = xn
        h_ref[...] = (yn * (1.0 + sc_ref[...]) + sh_ref[...]).astype(h_ref.dtype)


def _moe_combine(x, gate, wts, y, pos, seq, norm_g, shift=None, scale=None):
    m, d = x.shape
    nb = gate.shape[0]
    ne = wts.shape[-1]
    final = shift is None
    vec = pl.BlockSpec((None, 1, d), lambda i: (i * MOE_TT // seq, 0, 0))
    row = pl.BlockSpec((MOE_TT, d), lambda i: (i, 0))
    in_specs = [
        pl.BlockSpec((None, 1, 2 * MOE_TT), lambda i: (i, 0, 0), memory_space=pltpu.SMEM),
        row, vec,
        pl.BlockSpec((MOE_TT, ne), lambda i: (i, 0)),
        pl.BlockSpec(memory_space=pl.ANY),
        pl.BlockSpec((1, d), lambda i: (0, 0)),
    ]
    args = [pos, x, gate.reshape(nb, 1, d), wts, y, norm_g.reshape(1, d)]
    if final:
        out_specs, out_shape = row, jax.ShapeDtypeStruct((m, d), F32)
    else:
        in_specs += [vec, vec]
        args += [shift.reshape(nb, 1, d), scale.reshape(nb, 1, d)]
        out_specs = [row, row]
        out_shape = [jax.ShapeDtypeStruct((m, d), F32), jax.ShapeDtypeStruct((m, d), BF16)]
    return pl.pallas_call(
        functools.partial(_moe_combine_kernel, final=final),
        grid=(m // MOE_TT,),
        in_specs=in_specs,
        out_specs=out_specs,
        out_shape=out_shape,
        scratch_shapes=[pltpu.VMEM((2, MOE_TT, d // 2), jnp.uint32), pltpu.SemaphoreType.DMA(())],
        compiler_params=_params("arbitrary"),
        name="moe_combine",
    )(*args)


def kernel(x, c, ada_w, ada_b, norm_mix_g, norm_ffn_g, final_norm_g, w_in, ssm_a_re, ssm_a_im, ssm_log_dt, ssm_b_re, ssm_b_im, ssm_c_re, ssm_c_im, ssm_d, w_glu, b_glu, w_branch, w_out, ffn_w1, ffn_w3, ffn_w2, router_w, router_b, moe_w1, moe_w3, moe_w2):
    batch, seq, d = x.shape
    depth = ada_w.shape[0]
    tokens = batch * seq
    ssm_width = w_glu.shape[1]
    per_res = seq // TOKEN_RESIDUES
    tn_in = ATTN_OUT
    assert ssm_width == tn_in and w_in.shape[-1] == (3 * N_GROUPS + 1) * tn_in + 2 * d

    def to_residue_major(t):
        f = t.shape[-1]
        return t.reshape(batch, per_res, TOKEN_RESIDUES, f).transpose(0, 2, 1, 3).reshape(batch, seq, f)

    mod = _ada_modulation(c, ada_w, ada_b).reshape(depth, batch, 6, d)
    x2 = to_residue_major(x).reshape(tokens, d)
    h = None
    out = None
    for layer in range(depth):
        sh1, sc1, g1, sh2, sc2, g2 = (mod[layer, :, i] for i in range(6))

        if h is None:
            h = _norm_mod(x2.reshape(batch, seq, d), norm_mix_g[layer], sh1, sc1)
        qkv0 = _mm_plain(h, w_in, (layer,), lambda j: N_GROUPS * j, 3, F32, *W_IN_TILE_F32)
        rest = _mm_plain(h, w_in, (layer,),
                         lambda j: j + 1 + jnp.where(j >= 2, 1, 0) + jnp.where(j >= 4, 1, 0),
                         11, BF16, *W_IN_TILE)
        attn = _dilation_mixture_attention(qkv0, rest, batch, seq)
        tables = _ssm_tables(ssm_a_re[layer], ssm_a_im[layer], ssm_log_dt[layer],
                             ssm_b_re[layer], ssm_b_im[layer], ssm_c_re[layer], ssm_c_im[layer])
        ssm = _s5_ssm(rest, 6, batch, seq, tables, ssm_d[layer], w_glu[layer], b_glu[layer])
        merged = _mm_branch(attn, ssm, w_branch, layer, rest, 7 * tn_in, *BRANCH_TILE)
        h = None

        li = layer // 2
        if layer % 2 == 0:
            x2, hf = _mm_resid_norm(merged, w_out, layer, x2, g1, norm_ffn_g[layer], sh2, sc2,
                                    seq, W_OUT_NORM_ROWS)
            act = _mm_swiglu(hf, ffn_w1, ffn_w3, (li,), *SWIGLU_TILE)
            x2 = _mm_resid(act, ffn_w2, (li,), x2, g2, seq, *FFN_DOWN_TILE)
        else:
            x2 = _mm_resid(merged, w_out, (layer,), x2, g1, seq, *W_OUT_TILE)
            x3 = x2.reshape(batch, seq, d)
            sel, wts, cnt = _router(x3, norm_ffn_g[layer], sh2, sc2, router_w[li], router_b[li])
            plan = _moe_plan(sel, cnt, tokens)
            xs = _moe_dispatch(x3, norm_ffn_g[layer], sh2, sc2, plan)
            act = _moe_mm(xs, (moe_w1, moe_w3), li, plan, MOE_UP_TN, "up")
            y = _moe_mm(act, (moe_w2,), li, plan, MOE_DOWN_TN, "down")
            if layer + 1 < depth:
                x2, h = _moe_combine(x2, g2, wts, y, plan[0], seq, norm_mix_g[layer + 1],
                                     mod[layer + 1, :, 0], mod[layer + 1, :, 1])
            else:
                out = _moe_combine(x2, g2, wts, y, plan[0], seq, final_norm_g)
    if out is None:
        out = _final_norm(x2.reshape(batch, seq, d), final_norm_g)
    return out.reshape(batch, TOKEN_RESIDUES, per_res, d).transpose(0, 2, 1, 3).reshape(batch, seq, d)
```

```python
import functools
import math

import jax
import jax.numpy as jnp
from jax import lax
from jax.experimental import pallas as pl
from jax.experimental.pallas import tpu as pltpu

F32 = jnp.float32
BF16 = jnp.bfloat16

LANES = 128
SUBLANES = 8
V7X_VMEM_BYTES = 64 * 1024 * 1024
VMEM_LIMIT_BYTES = V7X_VMEM_BYTES * 7 // 8

ATTN_PATTERNS = ((128, 1), (512, 4), (2048, 16))
N_GROUPS = len(ATTN_PATTERNS)
HEADS = 8
HEAD_DIM = 128
ATTN_OUT = HEADS * HEAD_DIM
ATTN_BLOCK = 128
SSM_GROUP_CH = 16
RMS_EPS = 1e-6
NEG_BIG = -1e30

W_IN_TILE_F32 = (1024, ATTN_OUT)
W_IN_TILE = (2048, ATTN_OUT)
BRANCH_ROWS = 256
W_OUT_TILE = (1024, 1024)
W_OUT_NORM_ROWS = 256
SWIGLU_TILE = (2048, 512)
FFN_DOWN_TILE = (512, 512)


def _params(*semantics):
    return pltpu.CompilerParams(dimension_semantics=semantics,
                                vmem_limit_bytes=VMEM_LIMIT_BYTES)


def _sigmoid(v):
    return 1.0 / (1.0 + jnp.exp(-v))


ADA_KR = 256
ADA_CN = 1536


def _ada_kernel(ct_ref, w_ref, b_ref, o_ref, acc_ref):
    kstep = pl.program_id(1)
    kr = w_ref.shape[0]
    ct = ct_ref[pl.ds(pl.multiple_of(kstep * kr, kr), kr), :]
    act = (ct * _sigmoid(ct)).astype(BF16)

    @pl.when(kstep == 0)
    def _():
        acc_ref[...] = jnp.zeros_like(acc_ref)

    for c0 in range(0, w_ref.shape[1], ADA_CN):
        acc_ref[c0:c0 + ADA_CN, :] += lax.dot_general(
            w_ref[:, c0:c0 + ADA_CN].astype(BF16), act, (((0,), (0,)), ((), ())),
            preferred_element_type=F32)

    @pl.when(kstep == pl.num_programs(1) - 1)
    def _():
        for c0 in range(0, w_ref.shape[1], ADA_CN):
            rows = acc_ref[c0:c0 + ADA_CN, :].T[:o_ref.shape[0], :]
            o_ref[:, c0:c0 + ADA_CN] = rows + b_ref[:, c0:c0 + ADA_CN]


def _ada_modulation(c, ada_w, ada_b):
    depth, k, n = ada_w.shape
    nb = c.shape[0]
    ct = jnp.zeros((k, LANES), F32).at[:, :nb].set(c.T)
    out = pl.pallas_call(
        _ada_kernel,
        grid=(depth, k // ADA_KR),
        in_specs=[
            pl.BlockSpec((k, LANES), lambda l, i: (0, 0)),
            pl.BlockSpec((None, ADA_KR, n), lambda l, i: (l, i, 0)),
            pl.BlockSpec((None, 1, n), lambda l, i: (l, 0, 0)),
        ],
        out_specs=pl.BlockSpec((None, SUBLANES, n), lambda l, i: (l, 0, 0)),
        out_shape=jax.ShapeDtypeStruct((depth, SUBLANES, n), F32),
        scratch_shapes=[pltpu.VMEM((n, LANES), F32)],
        compiler_params=_params("arbitrary", "arbitrary"),
        name="ada_modulation",
    )(ct, ada_w, ada_b.reshape(depth, 1, n))
    return out[:, :nb, :]


NORM_TS = 512


def _rms(x, g):
    return x * lax.rsqrt(jnp.mean(x * x, axis=-1, keepdims=True) + RMS_EPS) * g


def _norm_mod_kernel(x_ref, g_ref, sh_ref, sc_ref, o_ref):
    y = _rms(x_ref[...], g_ref[...])
    o_ref[...] = (y * (1.0 + sc_ref[...]) + sh_ref[...]).astype(o_ref.dtype)


def _norm_mod(x, g, shift, scale):
    b, s, d = x.shape
    vec = pl.BlockSpec((None, 1, d), lambda i, j: (i, 0, 0))
    out = pl.pallas_call(
        _norm_mod_kernel,
        grid=(b, s // NORM_TS),
        in_specs=[
            pl.BlockSpec((None, NORM_TS, d), lambda i, j: (i, j, 0)),
            pl.BlockSpec((1, d), lambda i, j: (0, 0)),
            vec, vec,
        ],
        out_specs=pl.BlockSpec((None, NORM_TS, d), lambda i, j: (i, j, 0)),
        out_shape=jax.ShapeDtypeStruct((b, s, d), BF16),
        compiler_params=_params("arbitrary", "arbitrary"),
        name="norm_mod",
    )(x, g.reshape(1, d), shift.reshape(b, 1, d), scale.reshape(b, 1, d))
    return out.reshape(b * s, d)


def _final_norm_kernel(x_ref, g_ref, o_ref):
    o_ref[...] = _rms(x_ref[...], g_ref[...])


def _final_norm(x, g):
    b, s, d = x.shape
    return pl.pallas_call(
        _final_norm_kernel,
        grid=(b, s // NORM_TS),
        in_specs=[
            pl.BlockSpec((None, NORM_TS, d), lambda i, j: (i, j, 0)),
            pl.BlockSpec((1, d), lambda i, j: (0, 0)),
        ],
        out_specs=pl.BlockSpec((None, NORM_TS, d), lambda i, j: (i, j, 0)),
        out_shape=jax.ShapeDtypeStruct((b, s, d), F32),
        compiler_params=_params("arbitrary", "arbitrary"),
        name="final_norm",
    )(x, g.reshape(1, d))


CAST_ROWS = 256


def _cast_weight(w_ref, wb_ref):
    def body(i, carry):
        r = pl.multiple_of(i * CAST_ROWS, CAST_ROWS)
        wb_ref[pl.ds(r, CAST_ROWS), :] = w_ref[pl.ds(r, CAST_ROWS), :].astype(BF16)
        return carry
    lax.fori_loop(0, w_ref.shape[0] // CAST_ROWS, body, 0)


def _dot(a, b):
    return jnp.dot(a, b, preferred_element_type=F32)


def _mm_plain_kernel(a_ref, w_ref, o_ref, wb_ref):
    @pl.when(pl.program_id(1) == 0)
    def _():
        _cast_weight(w_ref, wb_ref)
    o_ref[...] = _dot(a_ref[...], wb_ref[...]).astype(o_ref.dtype)


def _mm_plain(a, w, w_idx, col_tile, n_tiles, out_dtype, tm, tn):
    m, k = a.shape
    lead = (None,) * len(w_idx)
    return pl.pallas_call(
        _mm_plain_kernel,
        grid=(n_tiles, m // tm),
        in_specs=[
            pl.BlockSpec((tm, k), lambda j, i: (i, 0)),
            pl.BlockSpec(lead + (k, tn), lambda j, i: w_idx + (0, col_tile(j))),
        ],
        out_specs=pl.BlockSpec((tm, tn), lambda j, i: (i, j)),
        out_shape=jax.ShapeDtypeStruct((m, n_tiles * tn), out_dtype),
        scratch_shapes=[pltpu.VMEM((k, tn), BF16)],
        compiler_params=_params("arbitrary", "arbitrary"),
        name="mm_plain",
    )(a, w)


def _mm_swiglu_kernel(a_ref, w1_ref, w3_ref, o_ref, wb1_ref, wb3_ref):
    @pl.when(pl.program_id(1) == 0)
    def _():
        _cast_weight(w1_ref, wb1_ref)
        _cast_weight(w3_ref, wb3_ref)
    a = a_ref[...]
    u = _dot(a, wb1_ref[...])
    v = _dot(a, wb3_ref[...])
    o_ref[...] = (u * _sigmoid(u) * v).astype(o_ref.dtype)


def _mm_swiglu(a, w1, w3, w_idx, tm, tn):
    m, k = a.shape
    n = w1.shape[-1]
    lead = (None,) * len(w_idx)
    wspec = pl.BlockSpec(lead + (k, tn), lambda j, i: w_idx + (0, j))
    return pl.pallas_call(
        _mm_swiglu_kernel,
        grid=(n // tn, m // tm),
        in_specs=[pl.BlockSpec((tm, k), lambda j, i: (i, 0)), wspec, wspec],
        out_specs=pl.BlockSpec((tm, tn), lambda j, i: (i, j)),
        out_shape=jax.ShapeDtypeStruct((m, n), BF16),
        scratch_shapes=[pltpu.VMEM((k, tn), BF16), pltpu.VMEM((k, tn), BF16)],
        compiler_params=_params("arbitrary", "arbitrary"),
        name="mm_swiglu",
    )(a, w1, w3)


def _mm_resid_kernel(a_ref, w_ref, x_ref, g_ref, o_ref, wb_ref):
    @pl.when(pl.program_id(1) == 0)
    def _():
        _cast_weight(w_ref, wb_ref)
    o_ref[...] = x_ref[...] + g_ref[...] * _dot(a_ref[...], wb_ref[...])


def _mm_resid(a, w, w_idx, x, gate, seq, tm, tn):
    m, k = a.shape
    n = w.shape[-1]
    nb = gate.shape[0]
    lead = (None,) * len(w_idx)
    return pl.pallas_call(
        _mm_resid_kernel,
        grid=(n // tn, m // tm),
        in_specs=[
            pl.BlockSpec((tm, k), lambda j, i: (i, 0)),
            pl.BlockSpec(lead + (k, tn), lambda j, i: w_idx + (0, j)),
            pl.BlockSpec((tm, tn), lambda j, i: (i, j)),
            pl.BlockSpec((None, 1, tn), lambda j, i: (i * tm // seq, 0, j)),
        ],
        out_specs=pl.BlockSpec((tm, tn), lambda j, i: (i, j)),
        out_shape=jax.ShapeDtypeStruct((m, n), F32),
        scratch_shapes=[pltpu.VMEM((k, tn), BF16)],
        compiler_params=_params("arbitrary", "arbitrary"),
        name="mm_resid",
    )(a, w, x, gate.reshape(nb, 1, n))


def _mm_resid_norm_kernel(a_ref, w_ref, x_ref, g_ref, ng_ref, sh_ref, sc_ref,
                          xo_ref, ho_ref, wb_ref):
    @pl.when(pl.program_id(0) == 0)
    def _():
        _cast_weight(w_ref, wb_ref)
    xn = x_ref[...] + g_ref[...] * _dot(a_ref[...], wb_ref[...])
    xo_ref[...] = xn
    yn = _rms(xn, ng_ref[...])
    ho_ref[...] = (yn * (1.0 + sc_ref[...]) + sh_ref[...]).astype(ho_ref.dtype)


def _mm_resid_norm(a, w, layer, x, gate, norm_g, shift, scale, seq, tm):
    m, k = a.shape
    n = w.shape[-1]
    nb = gate.shape[0]
    vec = pl.BlockSpec((None, 1, n), lambda i: (i * tm // seq, 0, 0))
    row = pl.BlockSpec((tm, n), lambda i: (i, 0))
    return pl.pallas_call(
        _mm_resid_norm_kernel,
        grid=(m // tm,),
        in_specs=[
            pl.BlockSpec((tm, k), lambda i: (i, 0)),
            pl.BlockSpec((None, k, n), lambda i: (layer, 0, 0), pipeline_mode=pl.Buffered(1)),
            row, vec,
            pl.BlockSpec((1, n), lambda i: (0, 0)),
            vec, vec,
        ],
        out_specs=[row, row],
        out_shape=[jax.ShapeDtypeStruct((m, n), F32), jax.ShapeDtypeStruct((m, n), BF16)],
        scratch_shapes=[pltpu.VMEM((k, n), BF16)],
        compiler_params=_params("arbitrary"),
        name="mm_resid_norm",
    )(a, w, x, gate.reshape(nb, 1, n), norm_g.reshape(1, n),
      shift.reshape(nb, 1, n), scale.reshape(nb, 1, n))


def _mix_groups(o_refs, l_refs):
    a0, a1, a2 = (r[...] for r in l_refs)
    m = jnp.maximum(jnp.maximum(a0, a1), a2)
    e0 = jnp.exp(a0 - m)
    e1 = jnp.exp(a1 - m)
    e2 = jnp.exp(a2 - m)
    inv = 1.0 / (e0 + e1 + e2)
    weights = (e0 * inv, e1 * inv, e2 * inv)
    cols = []
    for h in range(HEADS):
        sl = slice(h * HEAD_DIM, (h + 1) * HEAD_DIM)
        cols.append(sum(w[:, h:h + 1] * o[:, sl].astype(F32) for w, o in zip(weights, o_refs)))
    return jnp.concatenate(cols, axis=1)


def _mm_branch_kernel(o0_ref, o1_ref, o2_ref, l0_ref, l1_ref, l2_ref, ss_ref, wa_ref, ws_ref,
                      ga0_ref, ga1_ref, gs0_ref, gs1_ref, o_ref, wba_ref, wbs_ref):
    @pl.when(pl.program_id(0) == 0)
    def _():
        _cast_weight(wa_ref, wba_ref)
        _cast_weight(ws_ref, wbs_ref)
    attn = _mix_groups((o0_ref, o1_ref, o2_ref), (l0_ref, l1_ref, l2_ref)).astype(BF16)
    ssm = ss_ref[...]
    tn = ga0_ref.shape[1]
    for half, (ga_ref, gs_ref) in enumerate(((ga0_ref, gs0_ref), (ga1_ref, gs1_ref))):
        sl = slice(half * tn, (half + 1) * tn)
        pa = _dot(attn, wba_ref[:, sl])
        ps = _dot(ssm, wbs_ref[:, sl])
        ga = _sigmoid(ga_ref[...].astype(F32))
        gs = _sigmoid(gs_ref[...].astype(F32))
        o_ref[:, sl] = (ga * pa + gs * ps).astype(o_ref.dtype)


def _mm_branch(outs, lses, ssm, w_branch, layer, proj, gate_tile, tm):
    m, k = ssm.shape
    n = w_branch.shape[-1]
    assert n == 2 * ATTN_OUT
    row = lambda width: pl.BlockSpec((tm, width), lambda i: (i, 0))
    gate = lambda t: pl.BlockSpec((tm, ATTN_OUT), lambda i: (i, gate_tile + t))
    weight = lambda half: pl.BlockSpec((None, k, n), lambda i: (layer, half, 0),
                                       pipeline_mode=pl.Buffered(1))
    return pl.pallas_call(
        _mm_branch_kernel,
        grid=(m // tm,),
        in_specs=[row(ATTN_OUT)] * 3 + [row(LANES)] * 3 + [row(k), weight(0), weight(1),
                                                          gate(0), gate(1), gate(2), gate(3)],
        out_specs=row(n),
        out_shape=jax.ShapeDtypeStruct((m, n), BF16),
        scratch_shapes=[pltpu.VMEM((k, n), BF16), pltpu.VMEM((k, n), BF16)],
        compiler_params=_params("arbitrary"),
        name="mm_branch",
    )(*outs, *lses, ssm, w_branch, w_branch, proj, proj, proj, proj)


TOKEN_RESIDUES = 16
ATTN_BB = 4


def _attn_kernel(*refs, with_prev):
    if with_prev:
        q_ref, kc_ref, vc_ref, bc_ref, bp_ref, o_ref, lse_ref, kp_ref, vp_ref = refs

        @pl.when(pl.program_id(2) == 0)
        def _():
            kp_ref[...] = jnp.zeros_like(kp_ref)
            vp_ref[...] = jnp.zeros_like(vp_ref)
    else:
        q_ref, kc_ref, vc_ref, bc_ref, o_ref, lse_ref = refs
    blk = ATTN_BLOCK
    bb, nc, rpc = q_ref.shape[0], q_ref.shape[1], q_ref.shape[2]
    scale = HEAD_DIM ** -0.5
    lane = lax.broadcasted_iota(jnp.int32, (blk, LANES), 1)

    def heads(ref, bi):
        x = ref[bi].reshape(blk, ATTN_OUT)
        return jnp.stack([x[:, h * HEAD_DIM:(h + 1) * HEAD_DIM] for h in range(HEADS)]
                         ).astype(BF16)

    def qk(q, k):
        return lax.dot_general(q, k, (((2,), (2,)), ((0,), (0,))), preferred_element_type=F32)

    def pv(p, v):
        return lax.dot_general(p.astype(BF16), v, (((2,), (1,)), ((0,), (0,))),
                               preferred_element_type=F32)

    for bi in range(bb):
        q = heads(q_ref, bi)
        k_c = heads(kc_ref, bi)
        v_c = heads(vc_ref, bi)
        s_c = qk(q, k_c) * scale + bc_ref[...]
        m = jnp.max(s_c, axis=2, keepdims=True)
        if with_prev:
            bias_p = jnp.where(pl.program_id(2) > 0, bp_ref[...], NEG_BIG)
            s_p = qk(q, kp_ref[bi]) * scale + bias_p
            m = jnp.maximum(m, jnp.max(s_p, axis=2, keepdims=True))
        p_c = jnp.exp(s_c - m)
        l = jnp.sum(p_c, axis=2, keepdims=True)
        o = pv(p_c, v_c)
        if with_prev:
            p_p = jnp.exp(s_p - m)
            l = l + jnp.sum(p_p, axis=2, keepdims=True)
            o = o + pv(p_p, vp_ref[bi])
            kp_ref[bi] = k_c
            vp_ref[bi] = v_c
        o = o / l
        lse = m + jnp.log(l)
        lse_tile = jnp.zeros((blk, LANES), F32)
        for h in range(HEADS):
            sl = slice(h * HEAD_DIM, (h + 1) * HEAD_DIM)
            o_ref[bi, :, :, sl] = o[h].reshape(nc, rpc, HEAD_DIM).astype(o_ref.dtype)
            lse_tile = jnp.where(lane == h, lse[h], lse_tile)
        lse_ref[bi] = lse_tile.reshape(nc, rpc, LANES)


def _attn_bias(dilation, nc, back):
    rpc = ATTN_BLOCK // nc
    i = jnp.arange(ATTN_BLOCK, dtype=jnp.int32)
    off = nc * (i % rpc) + i // rpc
    dist = off[:, None] - off[None, :] + back * ATTN_BLOCK
    valid = jnp.logical_and(dist >= 0, dist <= ATTN_BLOCK)
    slopes = 2.0 ** (-8.0 * (jnp.arange(HEADS, dtype=F32) + 1.0) / HEADS)
    pen = slopes[:, None, None] * (dist * dilation).astype(F32)[None]
    return jnp.where(valid[None], -pen, NEG_BIG)


def _attention_group(qkv, qkv_cols, batch, seq, gi):
    window, dilation = ATTN_PATTERNS[gi]
    assert window // dilation == ATTN_BLOCK
    cols = qkv.shape[-1]
    per_res = seq // TOKEN_RESIDUES
    nc = TOKEN_RESIDUES // dilation
    rpc = ATTN_BLOCK // nc
    nblk = seq // dilation // ATTN_BLOCK
    with_prev = nblk > 1

    def shape5(c):
        return (batch, nc, dilation, per_res, c)

    def spec(c, col):
        return pl.BlockSpec((ATTN_BB, nc, None, rpc, c), lambda b, r, n: (b, 0, r, n, col))

    bias_spec = pl.BlockSpec((HEADS, ATTN_BLOCK, ATTN_BLOCK), lambda b, r, n: (0, 0, 0))
    in_specs = [spec(ATTN_OUT, col) for col in qkv_cols]
    carry = pltpu.VMEM((ATTN_BB, HEADS, ATTN_BLOCK, HEAD_DIM), BF16)
    view = qkv.reshape(shape5(cols))
    args = [view] * len(in_specs) + [_attn_bias(dilation, nc, 0)]
    in_specs.append(bias_spec)
    if with_prev:
        args.append(_attn_bias(dilation, nc, 1))
        in_specs.append(bias_spec)
    o, lse = pl.pallas_call(
        functools.partial(_attn_kernel, with_prev=with_prev),
        grid=(batch // ATTN_BB, dilation, nblk),
        in_specs=in_specs,
        out_specs=[spec(ATTN_OUT, 0), spec(LANES, 0)],
        out_shape=[
            jax.ShapeDtypeStruct(shape5(ATTN_OUT), qkv.dtype),
            jax.ShapeDtypeStruct(shape5(LANES), F32),
        ],
        scratch_shapes=[carry, carry] if with_prev else [],
        compiler_params=_params("arbitrary", "arbitrary", "arbitrary"),
        name=f"attn_g{gi}",
    )(*args)
    return o.reshape(batch * seq, ATTN_OUT), lse.reshape(batch * seq, LANES)


def _attention_groups(qkv0, rest, batch, seq):
    outs, lses = [], []
    for gi in range(N_GROUPS):
        if gi == 0:
            o, lse = _attention_group(qkv0, (0, 1, 2), batch, seq, gi)
        else:
            o, lse = _attention_group(rest, (gi - 1, gi + 1, gi + 3), batch, seq, gi)
        outs.append(o)
        lses.append(lse)
    return outs, lses


SSM_LT = 128
SSM_PASSES = 2
SSM_JBLK = 256


def _gelu_tanh(y):
    return 0.5 * y * (1.0 + jnp.tanh(math.sqrt(2.0 / math.pi) * (y + 0.044715 * (y * y * y))))


def _ssm_kernel(u_ref, pin_ref, pout_ref, bblk_ref, cblk_ref, are_ref, aim_ref, d_ref,
                wglu_ref, bglu_ref, o_ref, sre_ref, sim_ref, hre_ref, him_ref, *, nbatch):
    width = u_ref.shape[-1]
    rows = pin_ref.shape[1]
    njb = width // SSM_JBLK
    jstates = bblk_ref.shape[2] // 2
    pairs = rows // SUBLANES
    per_tile = SUBLANES // nbatch

    @pl.when(pl.program_id(0) == 0)
    def _():
        hre_ref[...] = jnp.zeros_like(hre_ref)
        him_ref[...] = jnp.zeros_like(him_ref)

    u_rm = u_ref[...].reshape(pout_ref.shape[1], width)
    out_rm = None
    for p in range(pin_ref.shape[0]):
        u_f32 = _dot(pin_ref[p], u_rm)
        u_tb = u_f32.astype(BF16)
        ys = []
        for j in range(njb):
            bu = _dot(u_tb[:, j * SSM_JBLK:(j + 1) * SSM_JBLK], bblk_ref[j])
            sre_ref[...] = bu[:, :jstates].reshape(pairs, SUBLANES, jstates)
            sim_ref[...] = bu[:, jstates:].reshape(pairs, SUBLANES, jstates)
            cs = slice(j * jstates, (j + 1) * jstates)
            ar = are_ref[0:nbatch, cs]
            ai = aim_ref[0:nbatch, cs]

            def step(k, carry):
                hr, hi = carry
                for t in range(per_tile):
                    rs = slice(t * nbatch, (t + 1) * nbatch)
                    nr = ar * hr - ai * hi + sre_ref[k, rs, :]
                    ni = ar * hi + ai * hr + sim_ref[k, rs, :]
                    sre_ref[k, rs, :] = nr
                    sim_ref[k, rs, :] = ni
                    hr, hi = nr, ni
                return hr, hi

            hr, hi = lax.fori_loop(0, pairs, step,
                                   (hre_ref[0:nbatch, cs], him_ref[0:nbatch, cs]), unroll=4)
            hre_ref[0:nbatch, cs] = hr
            him_ref[0:nbatch, cs] = hi

            h_re = sre_ref[...].reshape(rows, jstates).astype(BF16)
            h_im = sim_ref[...].reshape(rows, jstates).astype(BF16)
            ys.append(_dot(h_re, cblk_ref[j, 0:jstates, :]) + _dot(h_im, cblk_ref[j, jstates:, :]))

        y = jnp.concatenate(ys, axis=1) + d_ref[...] * u_f32
        z = _gelu_tanh(y).astype(BF16)
        g = _dot(z, wglu_ref[...]) + bglu_ref[...]
        o_tb = (g[:, :width] * _sigmoid(g[:, width:])).astype(BF16)
        back = _dot(pout_ref[p], o_tb)
        out_rm = back if out_rm is None else out_rm + back
    o_ref[...] = out_rm.astype(o_ref.dtype).reshape(o_ref.shape)


def _ssm_tables(a_re, a_im, log_dt, b_re, b_im, c_re, c_im):
    groups, nst = a_re.shape
    gpb = SSM_JBLK // SSM_GROUP_CH
    njb = groups // gpb
    lam = lax.complex(a_re.astype(F32), a_im.astype(F32))
    dt = jnp.exp(log_dt.astype(F32))[:, None]
    a_bar = jnp.exp(lam * dt)
    b_mat = lax.complex(b_re.astype(F32), b_im.astype(F32))
    b_bar = ((a_bar - 1.0) / lam)[:, :, None] * b_mat
    same_group = jnp.eye(gpb, dtype=jnp.bool_)

    def block_diag(t):
        r, q = t.shape[1:]
        t = t.astype(BF16).reshape(njb, gpb, r, 1, q)
        wide = jnp.where(same_group[None, :, None, :, None], t, jnp.zeros((), BF16))
        return wide.reshape(njb, gpb * r, gpb * q)

    def in_blocks(t):
        return block_diag(t.transpose(0, 2, 1))

    def out_blocks(t):
        return block_diag(t.transpose(0, 2, 1))

    bblk = jnp.concatenate([in_blocks(jnp.real(b_bar)), in_blocks(jnp.imag(b_bar))], axis=2)
    cblk = jnp.concatenate([out_blocks(c_re.astype(F32)), out_blocks(-c_im.astype(F32))], axis=1)
    are = jnp.broadcast_to(jnp.real(a_bar).reshape(1, groups * nst), (SUBLANES, groups * nst))
    aim = jnp.broadcast_to(jnp.imag(a_bar).reshape(1, groups * nst), (SUBLANES, groups * nst))
    return bblk.astype(BF16), cblk.astype(BF16), are, aim


def _ssm_row_perms(nbatch):
    per_res = SSM_PASSES * SSM_LT // TOKEN_RESIDUES
    col = jnp.arange(nbatch * TOKEN_RESIDUES * per_res, dtype=jnp.int32)
    b = col // (TOKEN_RESIDUES * per_res)
    t_local = TOKEN_RESIDUES * (col % per_res) + (col // per_res) % TOKEN_RESIDUES
    row = (t_local % SSM_LT) * nbatch + b
    pin = jnp.logical_and(
        (t_local // SSM_LT)[None, None, :] == jnp.arange(SSM_PASSES, dtype=jnp.int32)[:, None, None],
        row[None, None, :] == jnp.arange(SSM_LT * nbatch, dtype=jnp.int32)[None, :, None])
    pin = pin.astype(BF16)
    return pin, pin.transpose(0, 2, 1)


def _s5_ssm(src, u_col, nbatch, seq, tables, d_skip, w_glu, b_glu):
    bblk, cblk, are, aim = tables
    width = w_glu.shape[0]
    nstate = are.shape[1]
    jstates = bblk.shape[2] // 2
    rows = SSM_LT * nbatch
    step_rows = SSM_PASSES * SSM_LT // TOKEN_RESIDUES
    per_res = seq // TOKEN_RESIDUES
    pin, pout = _ssm_row_perms(nbatch)
    d2 = d_skip.reshape(1, width).astype(F32)
    wg = w_glu.astype(BF16)
    bg = b_glu.reshape(1, 2 * width).astype(F32)

    def const(a):
        return pl.BlockSpec(a.shape, lambda i: (0,) * a.ndim, pipeline_mode=pl.Buffered(1))

    blk = (nbatch, TOKEN_RESIDUES, step_rows, width)
    out = pl.pallas_call(
        functools.partial(_ssm_kernel, nbatch=nbatch),
        grid=(per_res // step_rows,),
        in_specs=[pl.BlockSpec(blk, lambda i: (0, 0, i, u_col)),
                  const(pin), const(pout), const(bblk), const(cblk), const(are), const(aim),
                  const(d2), const(wg), const(bg)],
        out_specs=pl.BlockSpec(blk, lambda i: (0, 0, i, 0)),
        out_shape=jax.ShapeDtypeStruct((nbatch, TOKEN_RESIDUES, per_res, width), BF16),
        scratch_shapes=[
            pltpu.VMEM((rows // SUBLANES, SUBLANES, jstates), F32),
            pltpu.VMEM((rows // SUBLANES, SUBLANES, jstates), F32),
            pltpu.VMEM((SUBLANES, nstate), F32),
            pltpu.VMEM((SUBLANES, nstate), F32),
        ],
        compiler_params=_params("arbitrary"),
        name="s5_ssm",
    )(src.reshape(nbatch, TOKEN_RESIDUES, per_res, src.shape[-1]),
      pin, pout, bblk, cblk, are, aim, d2, wg, bg)
    return out.reshape(nbatch * seq, width)


ROUTER_TS = 256


def _router_kernel(x_ref, g_ref, sh_ref, sc_ref, rw_ref, rb_ref,
                   sel_ref, wts_ref, cnt_ref, carry_ref):
    @pl.when(jnp.logical_and(pl.program_id(0) == 0, pl.program_id(1) == 0))
    def _():
        carry_ref[...] = jnp.zeros_like(carry_ref)

    h = _rms(x_ref[...], g_ref[...]) * (1.0 + sc_ref[...]) + sh_ref[...]
    ts, ne = h.shape[0], rw_ref.shape[0]
    idx = lax.broadcasted_iota(jnp.int32, (ts, ne), 1)
    logits = jnp.broadcast_to(rb_ref[...], (ts, ne))
    for ex in range(ne):
        col = jnp.sum(h * rw_ref[ex:ex + 1, :], axis=1, keepdims=True)
        logits = logits + jnp.where(idx == ex, col, 0.0)
    m1 = jnp.max(logits, axis=1, keepdims=True)
    i1 = jnp.min(jnp.where(logits == m1, idx, ne), axis=1, keepdims=True)
    rest = jnp.where(idx == i1, -jnp.inf, logits)
    m2 = jnp.max(rest, axis=1, keepdims=True)
    i2 = jnp.min(jnp.where(rest == m2, idx, ne), axis=1, keepdims=True)
    e = jnp.exp(m2 - m1)
    w1 = 1.0 / (1.0 + e)
    w2 = e / (1.0 + e)

    onehot = jnp.where(idx == i1, 1.0, 0.0) + jnp.where(idx == i2, 1.0, 0.0)
    row = lax.broadcasted_iota(jnp.int32, (ts, ts), 0)
    col = lax.broadcasted_iota(jnp.int32, (ts, ts), 1)
    lower = jnp.where(col < row, 1.0, 0.0).astype(BF16)
    before = carry_ref[...] + _dot(lower, onehot.astype(BF16))
    r1 = jnp.sum(jnp.where(idx == i1, before, 0.0), axis=1, keepdims=True).astype(jnp.int32)
    r2 = jnp.sum(jnp.where(idx == i2, before, 0.0), axis=1, keepdims=True).astype(jnp.int32)
    total = carry_ref[...] + jnp.sum(onehot, axis=0, keepdims=True)
    carry_ref[...] = total
    cnt_ref[...] = total
    sel_ref[...] = jnp.where(idx == 0, i1, jnp.where(idx == 1, i2, jnp.where(
        idx == 2, r1, jnp.where(idx == 3, r2, 0))))
    wts_ref[...] = jnp.where(idx == 0, w1, jnp.where(idx == 1, w2, 0.0))


def _router(x, g, shift, scale, router_w, router_b):
    b, s, d = x.shape
    ne = router_w.shape[-1]
    vec = pl.BlockSpec((None, 1, d), lambda i, j: (i, 0, 0))
    tok = pl.BlockSpec((None, ROUTER_TS, ne), lambda i, j: (i, j, 0))
    sel, wts, cnt = pl.pallas_call(
        _router_kernel,
        grid=(b, s // ROUTER_TS),
        in_specs=[
            pl.BlockSpec((None, ROUTER_TS, d), lambda i, j: (i, j, 0)),
            pl.BlockSpec((1, d), lambda i, j: (0, 0)),
            vec, vec,
            pl.BlockSpec((ne, d), lambda i, j: (0, 0)),
            pl.BlockSpec((1, ne), lambda i, j: (0, 0)),
        ],
        out_specs=[tok, tok, pl.BlockSpec((1, ne), lambda i, j: (0, 0))],
        out_shape=[jax.ShapeDtypeStruct((b, s, ne), jnp.int32),
                   jax.ShapeDtypeStruct((b, s, ne), F32),
                   jax.ShapeDtypeStruct((1, ne), F32)],
        scratch_shapes=[pltpu.VMEM((1, ne), F32)],
        compiler_params=_params("arbitrary", "arbitrary"),
        name="router",
    )(x, g.reshape(1, d), shift.reshape(b, 1, d), scale.reshape(b, 1, d),
      router_w.T, router_b.reshape(1, ne))
    return sel.reshape(b * s, ne), wts.reshape(b * s, ne), cnt


MOE_TM = 256
MOE_TT = 256
MOE_UP_TN = 1408
MOE_DOWN_TN = 2048


def _moe_plan(sel, cnt, tokens):
    ne = cnt.shape[-1]
    counts = cnt[0].astype(jnp.int32)
    padded = (counts + MOE_TM - 1) // MOE_TM * MOE_TM
    ends = jnp.cumsum(padded)
    starts = ends - padded
    pos1 = starts[sel[:, 0]] + sel[:, 2]
    pos2 = starts[sel[:, 1]] + sel[:, 3]
    nt = tokens // MOE_TT
    pos = jnp.concatenate([pos1.reshape(nt, MOE_TT), pos2.reshape(nt, MOE_TT)], axis=1)
    max_tiles = 2 * tokens // MOE_TM + ne
    tile_start = jnp.arange(max_tiles, dtype=jnp.int32) * MOE_TM
    tile_expert = jnp.minimum(
        jnp.sum((tile_start[:, None] >= ends[None, :]).astype(jnp.int32), axis=1), ne - 1)
    num_tiles = (ends[-1] // MOE_TM).reshape(1)
    zero_tiles = jnp.concatenate([
        jnp.maximum(ends // MOE_TM - 1, 0),
        jnp.minimum(num_tiles[0] + jnp.arange(ne, dtype=jnp.int32), max_tiles - 1)])
    nonempty = counts > 0
    ar = jnp.arange(ne, dtype=jnp.int32)
    order = (ar[:, None] + 1 + ar[None, :]) % ne
    nxt = order[ar, jnp.argmax(nonempty[order], axis=1)].astype(jnp.int32)
    gidx = jnp.cumsum(nonempty.astype(jnp.int32)) - 1
    groups = jnp.concatenate([nxt, gidx, jnp.sum(nonempty.astype(jnp.int32)).reshape(1)])
    return (pos.reshape(nt, 1, 2 * MOE_TT), tile_expert, num_tiles, max_tiles, zero_tiles,
            groups)


def _row_copies(pos_ref, t, make):
    tt = pos_ref.shape[1] // 2
    return make(0, t, pos_ref[0, t]), make(1, t, pos_ref[0, tt + t])


def _issue_and_drain(pos_ref, make):
    tt = pos_ref.shape[1] // 2

    def issue(t, carry):
        for k, cp in enumerate(_row_copies(pos_ref, t, make)):
            cp.start(priority=k)
        return carry

    def drain(t, carry):
        for cp in _row_copies(pos_ref, t, make):
            cp.wait()
        return carry

    lax.fori_loop(0, tt, issue, 0, unroll=4)
    lax.fori_loop(0, tt, drain, 0, unroll=4)


def _pack_pairs(lo, hi):
    lo_bits = pltpu.bitcast(lo.astype(BF16).astype(F32), jnp.uint32)
    hi_bits = pltpu.bitcast(hi.astype(BF16).astype(F32), jnp.uint32)
    return jnp.bitwise_or(jnp.bitwise_and(hi_bits, jnp.uint32(0xFFFF0000)),
                          jnp.right_shift(lo_bits, jnp.uint32(16)))


def _unpack_pairs(words):
    lo = pltpu.bitcast(jnp.left_shift(words, jnp.uint32(16)), F32)
    hi = pltpu.bitcast(jnp.bitwise_and(words, jnp.uint32(0xFFFF0000)), F32)
    return lo, hi


def _moe_dispatch_kernel(pos_ref, zt_ref, x_ref, g_ref, sh_ref, sc_ref, xs_ref, hbuf_ref, sem):
    @pl.when(jnp.logical_and(pl.program_id(0) == 0, pl.program_id(1) == 0))
    def _():
        hbuf_ref[...] = jnp.zeros_like(hbuf_ref)
        for k in range(zt_ref.shape[0]):
            cp = pltpu.make_async_copy(
                hbuf_ref, xs_ref.at[pl.ds(pl.multiple_of(zt_ref[k] * MOE_TM, MOE_TM), MOE_TM), :],
                sem)
            cp.start()
            cp.wait()

    h = _rms(x_ref[...], g_ref[...]) * (1.0 + sc_ref[...]) + sh_ref[...]
    half = h.shape[1] // 2
    hbuf_ref[...] = _pack_pairs(h[:, :half], h[:, half:])

    def make(k, t, p):
        return pltpu.make_async_copy(hbuf_ref.at[pl.ds(t, 1), :], xs_ref.at[pl.ds(p, 1), :], sem)

    _issue_and_drain(pos_ref, make)


def _moe_dispatch(x, g, shift, scale, plan):
    pos, _, _, max_tiles, zero_tiles, _ = plan
    b, s, d = x.shape
    per_b = s // MOE_TT
    assert MOE_TT == MOE_TM
    vec = pl.BlockSpec((None, 1, d), lambda i, j: (i, 0, 0))
    return pl.pallas_call(
        _moe_dispatch_kernel,
        grid=(b, per_b),
        in_specs=[
            pl.BlockSpec((None, 1, 2 * MOE_TT), lambda i, j: (i * per_b + j, 0, 0),
                         memory_space=pltpu.SMEM),
            pl.BlockSpec(memory_space=pltpu.SMEM),
            pl.BlockSpec((None, MOE_TT, d), lambda i, j: (i, j, 0)),
            pl.BlockSpec((1, d), lambda i, j: (0, 0)),
            vec, vec,
        ],
        out_specs=pl.BlockSpec(memory_space=pl.ANY),
        out_shape=jax.ShapeDtypeStruct((max_tiles * MOE_TM, d // 2), jnp.uint32),
        scratch_shapes=[pltpu.VMEM((MOE_TT, d // 2), jnp.uint32), pltpu.SemaphoreType.DMA(())],
        compiler_params=_params("arbitrary", "arbitrary"),
        name="moe_dispatch",
    )(pos, zero_tiles, x, g.reshape(1, d), shift.reshape(b, 1, d), scale.reshape(b, 1, d))


def _moe_mm_kernel(te_ref, nt_ref, grp_ref, a_ref, *rest, mode, li, tn):
    nw = 2 if mode == "up" else 1
    w_hbm = rest[:nw]
    o_ref = rest[nw]
    wb = rest[nw + 1:2 * nw + 1]
    stage = rest[2 * nw + 1:3 * nw + 1]
    wsem = rest[3 * nw + 1]
    j = pl.program_id(0)
    i = pl.program_id(1)
    ne = (grp_ref.shape[0] - 1) // 2
    expert = te_ref[i]
    active = i < nt_ref[0]
    new_expert = jnp.logical_or(i == 0, expert != te_ref[jnp.maximum(i - 1, 0)])

    def weight_copies(ex, jx):
        cols = pl.ds(pl.multiple_of(jx * tn, LANES), tn)
        return [pltpu.make_async_copy(w_hbm[k].at[li, ex, :, cols], stage[k], wsem.at[k])
                for k in range(nw)]

    @pl.when(jnp.logical_and(active, new_expert))
    def _():
        @pl.when(jnp.logical_and(j == 0, grp_ref[ne + expert] == 0))
        def _():
            for cp in weight_copies(expert, j):
                cp.start()

        for cp in weight_copies(expert, j):
            cp.wait()
        for k in range(nw):
            _cast_weight(stage[k], wb[k])
        nxt = grp_ref[expert]
        nxt_j = j + jnp.where(nxt <= expert, 1, 0)

        @pl.when(nxt_j < pl.num_programs(0))
        def _():
            for cp in weight_copies(nxt, nxt_j):
                cp.start()

    @pl.when(active)
    def _():
        if mode == "down":
            p = lax.dot_general(wb[0][...], a_ref[...], (((0,), (0,)), ((), ())),
                                preferred_element_type=F32).T
            half = p.shape[1] // 2
            o_ref[...] = _pack_pairs(p[:, :half], p[:, half:])
        else:
            lo, hi = _unpack_pairs(a_ref[...])
            lo, hi = lo.astype(BF16), hi.astype(BF16)
            half = lo.shape[1]
            nt_dims = (((0,), (1,)), ((), ()))

            def proj(w_ref):
                return (lax.dot_general(w_ref[0:half, :], lo, nt_dims, preferred_element_type=F32)
                        + lax.dot_general(w_ref[half:, :], hi, nt_dims,
                                          preferred_element_type=F32))

            u = proj(wb[0])
            o_ref[...] = (u * _sigmoid(u) * proj(wb[1])).astype(o_ref.dtype)

    @pl.when(jnp.logical_not(active))
    def _():
        o_ref[...] = jnp.zeros_like(o_ref)


def _moe_mm(a, ws, li, plan, tn, mode):
    _, tile_expert, num_tiles, max_tiles, _, groups = plan
    k, n = ws[0].shape[-2], ws[0].shape[-1]
    rows = max_tiles * MOE_TM
    nw = len(ws)

    def tile(i, nt):
        return jnp.minimum(i, nt[0] - 1)

    if mode == "down":
        a_spec = pl.BlockSpec((k, MOE_TM), lambda j, i, te, nt, gr: (0, tile(i, nt)))
        out_spec = pl.BlockSpec((MOE_TM, tn // 2), lambda j, i, te, nt, gr: (i, j))
        out_shape = jax.ShapeDtypeStruct((rows, n // 2), jnp.uint32)
    else:
        a_spec = pl.BlockSpec((MOE_TM, k // 2), lambda j, i, te, nt, gr: (tile(i, nt), 0))
        out_spec = pl.BlockSpec((tn, MOE_TM), lambda j, i, te, nt, gr: (j, i))
        out_shape = jax.ShapeDtypeStruct((n, rows), BF16)
    return pl.pallas_call(
        functools.partial(_moe_mm_kernel, mode=mode, li=li, tn=tn),
        grid_spec=pltpu.PrefetchScalarGridSpec(
            num_scalar_prefetch=3,
            grid=(n // tn, max_tiles),
            in_specs=[a_spec] + [pl.BlockSpec(memory_space=pl.ANY)] * nw,
            out_specs=out_spec,
            scratch_shapes=([pltpu.VMEM((k, tn), BF16)] * nw + [pltpu.VMEM((k, tn), F32)] * nw
                            + [pltpu.SemaphoreType.DMA((nw,))]),
        ),
        out_shape=out_shape,
        compiler_params=_params("arbitrary", "arbitrary"),
        name=f"moe_mm_{mode}",
    )(tile_expert, num_tiles, groups, a, *ws)


def _moe_combine_kernel(pos_ref, x_ref, g_ref, wts_ref, y_ref, *rest, final):
    if final:
        ng_ref, o_ref, ybuf_ref, sem = rest
    else:
        ng_ref, sh_ref, sc_ref, o_ref, h_ref, ybuf_ref, sem = rest

    def make(k, t, p):
        return pltpu.make_async_copy(y_ref.at[pl.ds(p, 1), :], ybuf_ref.at[k, pl.ds(t, 1), :], sem)

    _issue_and_drain(pos_ref, make)

    def rows(k):
        lo, hi = _unpack_pairs(ybuf_ref[k])
        hw = MOE_DOWN_TN // 2
        parts = []
        for c in range(lo.shape[1] // hw):
            parts += [lo[:, c * hw:(c + 1) * hw], hi[:, c * hw:(c + 1) * hw]]
        return jnp.concatenate(parts, axis=1)

    f = wts_ref[:, 0:1] * rows(0) + wts_ref[:, 1:2] * rows(1)
    xn = x_ref[...] + g_ref[...] * f
    yn = _rms(xn, ng_ref[...])
    if final:
        o_ref[...] = yn
    else:
        o_ref[...] = xn
        h_ref[...] = (yn * (1.0 + sc_ref[...]) + sh_ref[...]).astype(h_ref.dtype)


def _moe_combine(x, gate, wts, y, pos, seq, norm_g, shift=None, scale=None):
    m, d = x.shape
    nb = gate.shape[0]
    ne = wts.shape[-1]
    final = shift is None
    vec = pl.BlockSpec((None, 1, d), lambda i: (i * MOE_TT // seq, 0, 0))
    row = pl.BlockSpec((MOE_TT, d), lambda i: (i, 0))
    in_specs = [
        pl.BlockSpec((None, 1, 2 * MOE_TT), lambda i: (i, 0, 0), memory_space=pltpu.SMEM),
        row, vec,
        pl.BlockSpec((MOE_TT, ne), lambda i: (i, 0)),
        pl.BlockSpec(memory_space=pl.ANY),
        pl.BlockSpec((1, d), lambda i: (0, 0)),
    ]
    args = [pos, x, gate.reshape(nb, 1, d), wts, y, norm_g.reshape(1, d)]
    if final:
        out_specs, out_shape = row, jax.ShapeDtypeStruct((m, d), F32)
    else:
        in_specs += [vec, vec]
        args += [shift.reshape(nb, 1, d), scale.reshape(nb, 1, d)]
        out_specs = [row, row]
        out_shape = [jax.ShapeDtypeStruct((m, d), F32), jax.ShapeDtypeStruct((m, d), BF16)]
    return pl.pallas_call(
        functools.partial(_moe_combine_kernel, final=final),
        grid=(m // MOE_TT,),
        in_specs=in_specs,
        out_specs=out_specs,
        out_shape=out_shape,
        scratch_shapes=[pltpu.VMEM((2, MOE_TT, d // 2), jnp.uint32), pltpu.SemaphoreType.DMA(())],
        compiler_params=_params("arbitrary"),
        name="moe_combine",
    )(*args)


def kernel(x, c, ada_w, ada_b, norm_mix_g, norm_ffn_g, final_norm_g, w_in, ssm_a_re, ssm_a_im, ssm_log_dt, ssm_b_re, ssm_b_im, ssm_c_re, ssm_c_im, ssm_d, w_glu, b_glu, w_branch, w_out, ffn_w1, ffn_w3, ffn_w2, router_w, router_b, moe_w1, moe_w3, moe_w2):
    batch, seq, d = x.shape
    depth = ada_w.shape[0]
    tokens = batch * seq
    ssm_width = w_glu.shape[1]
    per_res = seq // TOKEN_RESIDUES
    tn_in = ATTN_OUT
    assert ssm_width == tn_in and w_in.shape[-1] == (3 * N_GROUPS + 1) * tn_in + 2 * d

    def to_residue_major(t):
        f = t.shape[-1]
        return t.reshape(batch, per_res, TOKEN_RESIDUES, f).transpose(0, 2, 1, 3).reshape(batch, seq, f)

    mod = _ada_modulation(c, ada_w, ada_b).reshape(depth, batch, 6, d)
    x2 = to_residue_major(x).reshape(tokens, d)
    h = None
    out = None
    for layer in range(depth):
        sh1, sc1, g1, sh2, sc2, g2 = (mod[layer, :, i] for i in range(6))

        if h is None:
            h = _norm_mod(x2.reshape(batch, seq, d), norm_mix_g[layer], sh1, sc1)
        qkv0 = _mm_plain(h, w_in, (layer,), lambda j: N_GROUPS * j, 3, F32, *W_IN_TILE_F32)
        rest = _mm_plain(h, w_in, (layer,),
                         lambda j: j + 1 + jnp.where(j >= 2, 1, 0) + jnp.where(j >= 4, 1, 0),
                         11, BF16, *W_IN_TILE)
        outs, lses = _attention_groups(qkv0, rest, batch, seq)
        tables = _ssm_tables(ssm_a_re[layer], ssm_a_im[layer], ssm_log_dt[layer],
                             ssm_b_re[layer], ssm_b_im[layer], ssm_c_re[layer], ssm_c_im[layer])
        ssm = _s5_ssm(rest, 6, batch, seq, tables, ssm_d[layer], w_glu[layer], b_glu[layer])
        merged = _mm_branch(outs, lses, ssm, w_branch, layer, rest, 7, BRANCH_ROWS)
        h = None

        li = layer // 2
        if layer % 2 == 0:
            x2, hf = _mm_resid_norm(merged, w_out, layer, x2, g1, norm_ffn_g[layer], sh2, sc2,
                                    seq, W_OUT_NORM_ROWS)
            act = _mm_swiglu(hf, ffn_w1, ffn_w3, (li,), *SWIGLU_TILE)
            x2 = _mm_resid(act, ffn_w2, (li,), x2, g2, seq, *FFN_DOWN_TILE)
        else:
            x2 = _mm_resid(merged, w_out, (layer,), x2, g1, seq, *W_OUT_TILE)
            x3 = x2.reshape(batch, seq, d)
            sel, wts, cnt = _router(x3, norm_ffn_g[layer], sh2, sc2, router_w[li], router_b[li])
            plan = _moe_plan(sel, cnt, tokens)
            xs = _moe_dispatch(x3, norm_ffn_g[layer], sh2, sc2, plan)
            act = _moe_mm(xs, (moe_w1, moe_w3), li, plan, MOE_UP_TN, "up")
            y = _moe_mm(act, (moe_w2,), li, plan, MOE_DOWN_TN, "down")
            if layer + 1 < depth:
                x2, h = _moe_combine(x2, g2, wts, y, plan[0], seq, norm_mix_g[layer + 1],
                                     mod[layer + 1, :, 0], mod[layer + 1, :, 1])
            else:
                out = _moe_combine(x2, g2, wts, y, plan[0], seq, final_norm_g)
    if out is None:
        out = _final_norm(x2.reshape(batch, seq, d), final_norm_g)
    return out.reshape(batch, TOKEN_RESIDUES, per_res, d).transpose(0, 2, 1, 3).reshape(batch, seq, d)
```

```python
import functools
import math

import jax
import jax.numpy as jnp
from jax import lax
from jax.experimental import pallas as pl
from jax.experimental.pallas import tpu as pltpu

F32 = jnp.float32
BF16 = jnp.bfloat16

LANES = 128
SUBLANES = 8
V7X_VMEM_BYTES = 64 * 1024 * 1024
VMEM_LIMIT_BYTES = V7X_VMEM_BYTES * 7 // 8

ATTN_PATTERNS = ((128, 1), (512, 4), (2048, 16))
N_GROUPS = len(ATTN_PATTERNS)
HEADS = 8
HEAD_DIM = 128
ATTN_OUT = HEADS * HEAD_DIM
ATTN_BLOCK = 128
SSM_GROUP_CH = 16
RMS_EPS = 1e-6
NEG_BIG = -1e30

W_IN_TILE_F32 = (1024, ATTN_OUT)
W_IN_TILE = (2048, ATTN_OUT)
BRANCH_ROWS = 512
W_OUT_TILE = (1024, 1024)
W_OUT_NORM_ROWS = 512
SWIGLU_TILE = (2048, 512)
FFN_DOWN_TILE = (512, 512)


def _params(*semantics):
    return pltpu.CompilerParams(dimension_semantics=semantics,
                                vmem_limit_bytes=VMEM_LIMIT_BYTES)


def _sigmoid(v):
    return 1.0 / (1.0 + jnp.exp(-v))


ADA_KR = 256
ADA_CN = 1536


def _ada_kernel(ct_ref, w_ref, b_ref, o_ref, acc_ref):
    kstep = pl.program_id(1)
    kr = w_ref.shape[0]
    ct = ct_ref[pl.ds(pl.multiple_of(kstep * kr, kr), kr), :]
    act = (ct * _sigmoid(ct)).astype(BF16)

    @pl.when(kstep == 0)
    def _():
        acc_ref[...] = jnp.zeros_like(acc_ref)

    for c0 in range(0, w_ref.shape[1], ADA_CN):
        acc_ref[c0:c0 + ADA_CN, :] += lax.dot_general(
            w_ref[:, c0:c0 + ADA_CN].astype(BF16), act, (((0,), (0,)), ((), ())),
            preferred_element_type=F32)

    @pl.when(kstep == pl.num_programs(1) - 1)
    def _():
        for c0 in range(0, w_ref.shape[1], ADA_CN):
            rows = acc_ref[c0:c0 + ADA_CN, :].T[:o_ref.shape[0], :]
            o_ref[:, c0:c0 + ADA_CN] = rows + b_ref[:, c0:c0 + ADA_CN]


def _ada_modulation(c, ada_w, ada_b):
    depth, k, n = ada_w.shape
    nb = c.shape[0]
    ct = jnp.zeros((k, LANES), F32).at[:, :nb].set(c.T)
    out = pl.pallas_call(
        _ada_kernel,
        grid=(depth, k // ADA_KR),
        in_specs=[
            pl.BlockSpec((k, LANES), lambda l, i: (0, 0)),
            pl.BlockSpec((None, ADA_KR, n), lambda l, i: (l, i, 0)),
            pl.BlockSpec((None, 1, n), lambda l, i: (l, 0, 0)),
        ],
        out_specs=pl.BlockSpec((None, SUBLANES, n), lambda l, i: (l, 0, 0)),
        out_shape=jax.ShapeDtypeStruct((depth, SUBLANES, n), F32),
        scratch_shapes=[pltpu.VMEM((n, LANES), F32)],
        compiler_params=_params("arbitrary", "arbitrary"),
        name="ada_modulation",
    )(ct, ada_w, ada_b.reshape(depth, 1, n))
    return out[:, :nb, :]


NORM_TS = 512


def _rms(x, g):
    return x * lax.rsqrt(jnp.mean(x * x, axis=-1, keepdims=True) + RMS_EPS) * g


def _norm_mod_kernel(x_ref, g_ref, sh_ref, sc_ref, o_ref):
    y = _rms(x_ref[...], g_ref[...])
    o_ref[...] = (y * (1.0 + sc_ref[...]) + sh_ref[...]).astype(o_ref.dtype)


def _norm_mod(x, g, shift, scale):
    b, s, d = x.shape
    vec = pl.BlockSpec((None, 1, d), lambda i, j: (i, 0, 0))
    out = pl.pallas_call(
        _norm_mod_kernel,
        grid=(b, s // NORM_TS),
        in_specs=[
            pl.BlockSpec((None, NORM_TS, d), lambda i, j: (i, j, 0)),
            pl.BlockSpec((1, d), lambda i, j: (0, 0)),
            vec, vec,
        ],
        out_specs=pl.BlockSpec((None, NORM_TS, d), lambda i, j: (i, j, 0)),
        out_shape=jax.ShapeDtypeStruct((b, s, d), BF16),
        compiler_params=_params("arbitrary", "arbitrary"),
        name="norm_mod",
    )(x, g.reshape(1, d), shift.reshape(b, 1, d), scale.reshape(b, 1, d))
    return out.reshape(b * s, d)


def _final_norm_kernel(x_ref, g_ref, o_ref):
    o_ref[...] = _rms(x_ref[...], g_ref[...])


def _final_norm(x, g):
    b, s, d = x.shape
    return pl.pallas_call(
        _final_norm_kernel,
        grid=(b, s // NORM_TS),
        in_specs=[
            pl.BlockSpec((None, NORM_TS, d), lambda i, j: (i, j, 0)),
            pl.BlockSpec((1, d), lambda i, j: (0, 0)),
        ],
        out_specs=pl.BlockSpec((None, NORM_TS, d), lambda i, j: (i, j, 0)),
        out_shape=jax.ShapeDtypeStruct((b, s, d), F32),
        compiler_params=_params("arbitrary", "arbitrary"),
        name="final_norm",
    )(x, g.reshape(1, d))


CAST_ROWS = 256


def _cast_weight(w_ref, wb_ref):
    def body(i, carry):
        r = pl.multiple_of(i * CAST_ROWS, CAST_ROWS)
        wb_ref[pl.ds(r, CAST_ROWS), :] = w_ref[pl.ds(r, CAST_ROWS), :].astype(BF16)
        return carry
    lax.fori_loop(0, w_ref.shape[0] // CAST_ROWS, body, 0)


def _dot(a, b):
    return jnp.dot(a, b, preferred_element_type=F32)


def _mm_plain_kernel(a_ref, w_ref, o_ref, wb_ref):
    @pl.when(pl.program_id(1) == 0)
    def _():
        _cast_weight(w_ref, wb_ref)
    o_ref[...] = _dot(a_ref[...], wb_ref[...]).astype(o_ref.dtype)


def _mm_plain(a, w, w_idx, col_tile, n_tiles, out_dtype, tm, tn):
    m, k = a.shape
    lead = (None,) * len(w_idx)
    return pl.pallas_call(
        _mm_plain_kernel,
        grid=(n_tiles, m // tm),
        in_specs=[
            pl.BlockSpec((tm, k), lambda j, i: (i, 0)),
            pl.BlockSpec(lead + (k, tn), lambda j, i: w_idx + (0, col_tile(j))),
        ],
        out_specs=pl.BlockSpec((tm, tn), lambda j, i: (i, j)),
        out_shape=jax.ShapeDtypeStruct((m, n_tiles * tn), out_dtype),
        scratch_shapes=[pltpu.VMEM((k, tn), BF16)],
        compiler_params=_params("arbitrary", "arbitrary"),
        name="mm_plain",
    )(a, w)


def _mm_swiglu_kernel(a_ref, w1_ref, w3_ref, o_ref, wb1_ref, wb3_ref):
    @pl.when(pl.program_id(1) == 0)
    def _():
        _cast_weight(w1_ref, wb1_ref)
        _cast_weight(w3_ref, wb3_ref)
    a = a_ref[...]
    u = _dot(a, wb1_ref[...])
    v = _dot(a, wb3_ref[...])
    o_ref[...] = (u * _sigmoid(u) * v).astype(o_ref.dtype)


def _mm_swiglu(a, w1, w3, w_idx, tm, tn):
    m, k = a.shape
    n = w1.shape[-1]
    lead = (None,) * len(w_idx)
    wspec = pl.BlockSpec(lead + (k, tn), lambda j, i: w_idx + (0, j))
    return pl.pallas_call(
        _mm_swiglu_kernel,
        grid=(n // tn, m // tm),
        in_specs=[pl.BlockSpec((tm, k), lambda j, i: (i, 0)), wspec, wspec],
        out_specs=pl.BlockSpec((tm, tn), lambda j, i: (i, j)),
        out_shape=jax.ShapeDtypeStruct((m, n), BF16),
        scratch_shapes=[pltpu.VMEM((k, tn), BF16), pltpu.VMEM((k, tn), BF16)],
        compiler_params=_params("arbitrary", "arbitrary"),
        name="mm_swiglu",
    )(a, w1, w3)


def _mm_resid_kernel(a_ref, w_ref, x_ref, g_ref, o_ref, wb_ref):
    @pl.when(pl.program_id(1) == 0)
    def _():
        _cast_weight(w_ref, wb_ref)
    o_ref[...] = x_ref[...] + g_ref[...] * _dot(a_ref[...], wb_ref[...])


def _mm_resid(a, w, w_idx, x, gate, seq, tm, tn):
    m, k = a.shape
    n = w.shape[-1]
    nb = gate.shape[0]
    lead = (None,) * len(w_idx)
    return pl.pallas_call(
        _mm_resid_kernel,
        grid=(n // tn, m // tm),
        in_specs=[
            pl.BlockSpec((tm, k), lambda j, i: (i, 0)),
            pl.BlockSpec(lead + (k, tn), lambda j, i: w_idx + (0, j)),
            pl.BlockSpec((tm, tn), lambda j, i: (i, j)),
            pl.BlockSpec((None, 1, tn), lambda j, i: (i * tm // seq, 0, j)),
        ],
        out_specs=pl.BlockSpec((tm, tn), lambda j, i: (i, j)),
        out_shape=jax.ShapeDtypeStruct((m, n), F32),
        scratch_shapes=[pltpu.VMEM((k, tn), BF16)],
        compiler_params=_params("arbitrary", "arbitrary"),
        name="mm_resid",
    )(a, w, x, gate.reshape(nb, 1, n))


def _mm_resid_norm_kernel(a_ref, w_ref, x_ref, g_ref, ng_ref, sh_ref, sc_ref,
                          xo_ref, ho_ref, wb_ref):
    @pl.when(pl.program_id(0) == 0)
    def _():
        _cast_weight(w_ref, wb_ref)
    xn = x_ref[...] + g_ref[...] * _dot(a_ref[...], wb_ref[...])
    xo_ref[...] = xn
    yn = _rms(xn, ng_ref[...])
    ho_ref[...] = (yn * (1.0 + sc_ref[...]) + sh_ref[...]).astype(ho_ref.dtype)


def _mm_resid_norm(a, w, layer, x, gate, norm_g, shift, scale, seq, tm):
    m, k = a.shape
    n = w.shape[-1]
    nb = gate.shape[0]
    vec = pl.BlockSpec((None, 1, n), lambda i: (i * tm // seq, 0, 0))
    row = pl.BlockSpec((tm, n), lambda i: (i, 0))
    return pl.pallas_call(
        _mm_resid_norm_kernel,
        grid=(m // tm,),
        in_specs=[
            pl.BlockSpec((tm, k), lambda i: (i, 0)),
            pl.BlockSpec((None, k, n), lambda i: (layer, 0, 0), pipeline_mode=pl.Buffered(1)),
            row, vec,
            pl.BlockSpec((1, n), lambda i: (0, 0)),
            vec, vec,
        ],
        out_specs=[row, row],
        out_shape=[jax.ShapeDtypeStruct((m, n), F32), jax.ShapeDtypeStruct((m, n), BF16)],
        scratch_shapes=[pltpu.VMEM((k, n), BF16)],
        compiler_params=_params("arbitrary"),
        name="mm_resid_norm",
    )(a, w, x, gate.reshape(nb, 1, n), norm_g.reshape(1, n),
      shift.reshape(nb, 1, n), scale.reshape(nb, 1, n))


def _mix_groups(o_refs, l_refs):
    a0, a1, a2 = (r[...] for r in l_refs)
    m = jnp.maximum(jnp.maximum(a0, a1), a2)
    e0 = jnp.exp(a0 - m)
    e1 = jnp.exp(a1 - m)
    e2 = jnp.exp(a2 - m)
    inv = 1.0 / (e0 + e1 + e2)
    weights = (e0 * inv, e1 * inv, e2 * inv)
    cols = []
    for h in range(HEADS):
        sl = slice(h * HEAD_DIM, (h + 1) * HEAD_DIM)
        cols.append(sum(w[:, h:h + 1] * o[:, sl].astype(F32) for w, o in zip(weights, o_refs)))
    return jnp.concatenate(cols, axis=1)


def _mm_branch_kernel(o0_ref, o1_ref, o2_ref, l0_ref, l1_ref, l2_ref, ss_ref, wa_ref, ws_ref,
                      ga0_ref, ga1_ref, gs0_ref, gs1_ref, o_ref, wba_ref, wbs_ref):
    @pl.when(pl.program_id(0) == 0)
    def _():
        _cast_weight(wa_ref, wba_ref)
        _cast_weight(ws_ref, wbs_ref)
    attn = _mix_groups((o0_ref, o1_ref, o2_ref), (l0_ref, l1_ref, l2_ref)).astype(BF16)
    ssm = ss_ref[...]
    tn = ga0_ref.shape[1]
    for half, (ga_ref, gs_ref) in enumerate(((ga0_ref, gs0_ref), (ga1_ref, gs1_ref))):
        sl = slice(half * tn, (half + 1) * tn)
        pa = _dot(attn, wba_ref[:, sl])
        ps = _dot(ssm, wbs_ref[:, sl])
        ga = _sigmoid(ga_ref[...].astype(F32))
        gs = _sigmoid(gs_ref[...].astype(F32))
        o_ref[:, sl] = (ga * pa + gs * ps).astype(o_ref.dtype)


def _mm_branch(outs, lses, ssm, w_branch, layer, proj, gate_tile, tm):
    m, k = ssm.shape
    n = w_branch.shape[-1]
    assert n == 2 * ATTN_OUT
    row = lambda width: pl.BlockSpec((tm, width), lambda i: (i, 0))
    gate = lambda t: pl.BlockSpec((tm, ATTN_OUT), lambda i: (i, gate_tile + t))
    weight = lambda half: pl.BlockSpec((None, k, n), lambda i: (layer, half, 0),
                                       pipeline_mode=pl.Buffered(1))
    return pl.pallas_call(
        _mm_branch_kernel,
        grid=(m // tm,),
        in_specs=[row(ATTN_OUT)] * 3 + [row(LANES)] * 3 + [row(k), weight(0), weight(1),
                                                          gate(0), gate(1), gate(2), gate(3)],
        out_specs=row(n),
        out_shape=jax.ShapeDtypeStruct((m, n), BF16),
        scratch_shapes=[pltpu.VMEM((k, n), BF16), pltpu.VMEM((k, n), BF16)],
        compiler_params=_params("arbitrary"),
        name="mm_branch",
    )(*outs, *lses, ssm, w_branch, w_branch, proj, proj, proj, proj)


TOKEN_RESIDUES = 16
ATTN_BB = 4


def _attn_kernel(*refs, with_prev):
    if with_prev:
        q_ref, kc_ref, vc_ref, bc_ref, bp_ref, o_ref, lse_ref, kp_ref, vp_ref = refs

        @pl.when(pl.program_id(2) == 0)
        def _():
            kp_ref[...] = jnp.zeros_like(kp_ref)
            vp_ref[...] = jnp.zeros_like(vp_ref)
    else:
        q_ref, kc_ref, vc_ref, bc_ref, o_ref, lse_ref = refs
    blk = ATTN_BLOCK
    bb, nc, rpc = q_ref.shape[0], q_ref.shape[1], q_ref.shape[2]
    scale = HEAD_DIM ** -0.5
    lane = lax.broadcasted_iota(jnp.int32, (blk, LANES), 1)

    def heads(ref, bi):
        x = ref[bi].reshape(blk, ATTN_OUT)
        return jnp.stack([x[:, h * HEAD_DIM:(h + 1) * HEAD_DIM] for h in range(HEADS)]
                         ).astype(BF16)

    def qk(q, k):
        return lax.dot_general(q, k, (((2,), (2,)), ((0,), (0,))), preferred_element_type=F32)

    def pv(p, v):
        return lax.dot_general(p.astype(BF16), v, (((2,), (1,)), ((0,), (0,))),
                               preferred_element_type=F32)

    for bi in range(bb):
        q = heads(q_ref, bi)
        k_c = heads(kc_ref, bi)
        v_c = heads(vc_ref, bi)
        s_c = qk(q, k_c) * scale + bc_ref[...]
        m = jnp.max(s_c, axis=2, keepdims=True)
        if with_prev:
            bias_p = jnp.where(pl.program_id(2) > 0, bp_ref[...], NEG_BIG)
            s_p = qk(q, kp_ref[bi]) * scale + bias_p
            m = jnp.maximum(m, jnp.max(s_p, axis=2, keepdims=True))
        p_c = jnp.exp(s_c - m)
        l = jnp.sum(p_c, axis=2, keepdims=True)
        o = pv(p_c, v_c)
        if with_prev:
            p_p = jnp.exp(s_p - m)
            l = l + jnp.sum(p_p, axis=2, keepdims=True)
            o = o + pv(p_p, vp_ref[bi])
            kp_ref[bi] = k_c
            vp_ref[bi] = v_c
        o = o / l
        lse = m + jnp.log(l)
        lse_tile = jnp.zeros((blk, LANES), F32)
        for h in range(HEADS):
            sl = slice(h * HEAD_DIM, (h + 1) * HEAD_DIM)
            o_ref[bi, :, :, sl] = o[h].reshape(nc, rpc, HEAD_DIM).astype(o_ref.dtype)
            lse_tile = jnp.where(lane == h, lse[h], lse_tile)
        lse_ref[bi] = lse_tile.reshape(nc, rpc, LANES)


def _attn_bias(dilation, nc, back):
    rpc = ATTN_BLOCK // nc
    i = jnp.arange(ATTN_BLOCK, dtype=jnp.int32)
    off = nc * (i % rpc) + i // rpc
    dist = off[:, None] - off[None, :] + back * ATTN_BLOCK
    valid = jnp.logical_and(dist >= 0, dist <= ATTN_BLOCK)
    slopes = 2.0 ** (-8.0 * (jnp.arange(HEADS, dtype=F32) + 1.0) / HEADS)
    pen = slopes[:, None, None] * (dist * dilation).astype(F32)[None]
    return jnp.where(valid[None], -pen, NEG_BIG)


def _attention_group(qkv, qkv_cols, batch, seq, gi):
    window, dilation = ATTN_PATTERNS[gi]
    assert window // dilation == ATTN_BLOCK
    cols = qkv.shape[-1]
    per_res = seq // TOKEN_RESIDUES
    nc = TOKEN_RESIDUES // dilation
    rpc = ATTN_BLOCK // nc
    nblk = seq // dilation // ATTN_BLOCK
    with_prev = nblk > 1

    def shape5(c):
        return (batch, nc, dilation, per_res, c)

    def spec(c, col):
        return pl.BlockSpec((ATTN_BB, nc, None, rpc, c), lambda b, r, n: (b, 0, r, n, col))

    bias_spec = pl.BlockSpec((HEADS, ATTN_BLOCK, ATTN_BLOCK), lambda b, r, n: (0, 0, 0))
    in_specs = [spec(ATTN_OUT, col) for col in qkv_cols]
    carry = pltpu.VMEM((ATTN_BB, HEADS, ATTN_BLOCK, HEAD_DIM), BF16)
    view = qkv.reshape(shape5(cols))
    args = [view] * len(in_specs) + [_attn_bias(dilation, nc, 0)]
    in_specs.append(bias_spec)
    if with_prev:
        args.append(_attn_bias(dilation, nc, 1))
        in_specs.append(bias_spec)
    o, lse = pl.pallas_call(
        functools.partial(_attn_kernel, with_prev=with_prev),
        grid=(batch // ATTN_BB, dilation, nblk),
        in_specs=in_specs,
        out_specs=[spec(ATTN_OUT, 0), spec(LANES, 0)],
        out_shape=[
            jax.ShapeDtypeStruct(shape5(ATTN_OUT), qkv.dtype),
            jax.ShapeDtypeStruct(shape5(LANES), F32),
        ],
        scratch_shapes=[carry, carry] if with_prev else [],
        compiler_params=_params("arbitrary", "arbitrary", "arbitrary"),
        name=f"attn_g{gi}",
    )(*args)
    return o.reshape(batch * seq, ATTN_OUT), lse.reshape(batch * seq, LANES)


def _attention_groups(qkv0, rest, batch, seq):
    outs, lses = [], []
    for gi in range(N_GROUPS):
        if gi == 0:
            o, lse = _attention_group(qkv0, (0, 1, 2), batch, seq, gi)
        else:
            o, lse = _attention_group(rest, (gi - 1, gi + 1, gi + 3), batch, seq, gi)
        outs.append(o)
        lses.append(lse)
    return outs, lses


SSM_LT = 128
SSM_PASSES = 2
SSM_JBLK = 256


def _gelu_tanh(y):
    return 0.5 * y * (1.0 + jnp.tanh(math.sqrt(2.0 / math.pi) * (y + 0.044715 * (y * y * y))))


def _ssm_kernel(u_ref, pin_ref, pout_ref, bblk_ref, cblk_ref, are_ref, aim_ref, d_ref,
                wglu_ref, bglu_ref, o_ref, sre_ref, sim_ref, hre_ref, him_ref, *, nbatch):
    width = u_ref.shape[-1]
    rows = pin_ref.shape[1]
    njb = width // SSM_JBLK
    jstates = bblk_ref.shape[2] // 2
    pairs = rows // SUBLANES
    per_tile = SUBLANES // nbatch

    @pl.when(pl.program_id(0) == 0)
    def _():
        hre_ref[...] = jnp.zeros_like(hre_ref)
        him_ref[...] = jnp.zeros_like(him_ref)

    u_rm = u_ref[...].reshape(pout_ref.shape[1], width)
    out_rm = None
    for p in range(pin_ref.shape[0]):
        u_f32 = _dot(pin_ref[p], u_rm)
        u_tb = u_f32.astype(BF16)
        ys = []
        for j in range(njb):
            bu = _dot(u_tb[:, j * SSM_JBLK:(j + 1) * SSM_JBLK], bblk_ref[j])
            sre_ref[...] = bu[:, :jstates].reshape(pairs, SUBLANES, jstates)
            sim_ref[...] = bu[:, jstates:].reshape(pairs, SUBLANES, jstates)
            cs = slice(j * jstates, (j + 1) * jstates)
            ar = are_ref[0:nbatch, cs]
            ai = aim_ref[0:nbatch, cs]

            def step(k, carry):
                hr, hi = carry
                for t in range(per_tile):
                    rs = slice(t * nbatch, (t + 1) * nbatch)
                    nr = ar * hr - ai * hi + sre_ref[k, rs, :]
                    ni = ar * hi + ai * hr + sim_ref[k, rs, :]
                    sre_ref[k, rs, :] = nr
                    sim_ref[k, rs, :] = ni
                    hr, hi = nr, ni
                return hr, hi

            hr, hi = lax.fori_loop(0, pairs, step,
                                   (hre_ref[0:nbatch, cs], him_ref[0:nbatch, cs]), unroll=4)
            hre_ref[0:nbatch, cs] = hr
            him_ref[0:nbatch, cs] = hi

            h_re = sre_ref[...].reshape(rows, jstates).astype(BF16)
            h_im = sim_ref[...].reshape(rows, jstates).astype(BF16)
            ys.append(_dot(h_re, cblk_ref[j, 0:jstates, :]) + _dot(h_im, cblk_ref[j, jstates:, :]))

        y = jnp.concatenate(ys, axis=1) + d_ref[...] * u_f32
        z = _gelu_tanh(y).astype(BF16)
        g = _dot(z, wglu_ref[...]) + bglu_ref[...]
        o_tb = (g[:, :width] * _sigmoid(g[:, width:])).astype(BF16)
        back = _dot(pout_ref[p], o_tb)
        out_rm = back if out_rm is None else out_rm + back
    o_ref[...] = out_rm.astype(o_ref.dtype).reshape(o_ref.shape)


def _ssm_tables(a_re, a_im, log_dt, b_re, b_im, c_re, c_im):
    groups, nst = a_re.shape
    gpb = SSM_JBLK // SSM_GROUP_CH
    njb = groups // gpb
    lam = lax.complex(a_re.astype(F32), a_im.astype(F32))
    dt = jnp.exp(log_dt.astype(F32))[:, None]
    a_bar = jnp.exp(lam * dt)
    b_mat = lax.complex(b_re.astype(F32), b_im.astype(F32))
    b_bar = ((a_bar - 1.0) / lam)[:, :, None] * b_mat
    same_group = jnp.eye(gpb, dtype=jnp.bool_)

    def block_diag(t):
        r, q = t.shape[1:]
        t = t.astype(BF16).reshape(njb, gpb, r, 1, q)
        wide = jnp.where(same_group[None, :, None, :, None], t, jnp.zeros((), BF16))
        return wide.reshape(njb, gpb * r, gpb * q)

    def in_blocks(t):
        return block_diag(t.transpose(0, 2, 1))

    def out_blocks(t):
        return block_diag(t.transpose(0, 2, 1))

    bblk = jnp.concatenate([in_blocks(jnp.real(b_bar)), in_blocks(jnp.imag(b_bar))], axis=2)
    cblk = jnp.concatenate([out_blocks(c_re.astype(F32)), out_blocks(-c_im.astype(F32))], axis=1)
    are = jnp.broadcast_to(jnp.real(a_bar).reshape(1, groups * nst), (SUBLANES, groups * nst))
    aim = jnp.broadcast_to(jnp.imag(a_bar).reshape(1, groups * nst), (SUBLANES, groups * nst))
    return bblk.astype(BF16), cblk.astype(BF16), are, aim


def _ssm_row_perms(nbatch):
    per_res = SSM_PASSES * SSM_LT // TOKEN_RESIDUES
    col = jnp.arange(nbatch * TOKEN_RESIDUES * per_res, dtype=jnp.int32)
    b = col // (TOKEN_RESIDUES * per_res)
    t_local = TOKEN_RESIDUES * (col % per_res) + (col // per_res) % TOKEN_RESIDUES
    row = (t_local % SSM_LT) * nbatch + b
    pin = jnp.logical_and(
        (t_local // SSM_LT)[None, None, :] == jnp.arange(SSM_PASSES, dtype=jnp.int32)[:, None, None],
        row[None, None, :] == jnp.arange(SSM_LT * nbatch, dtype=jnp.int32)[None, :, None])
    pin = pin.astype(BF16)
    return pin, pin.transpose(0, 2, 1)


def _s5_ssm(src, u_col, nbatch, seq, tables, d_skip, w_glu, b_glu):
    bblk, cblk, are, aim = tables
    width = w_glu.shape[0]
    nstate = are.shape[1]
    jstates = bblk.shape[2] // 2
    rows = SSM_LT * nbatch
    step_rows = SSM_PASSES * SSM_LT // TOKEN_RESIDUES
    per_res = seq // TOKEN_RESIDUES
    pin, pout = _ssm_row_perms(nbatch)
    d2 = d_skip.reshape(1, width).astype(F32)
    wg = w_glu.astype(BF16)
    bg = b_glu.reshape(1, 2 * width).astype(F32)

    def const(a):
        return pl.BlockSpec(a.shape, lambda i: (0,) * a.ndim, pipeline_mode=pl.Buffered(1))

    blk = (nbatch, TOKEN_RESIDUES, step_rows, width)
    out = pl.pallas_call(
        functools.partial(_ssm_kernel, nbatch=nbatch),
        grid=(per_res // step_rows,),
        in_specs=[pl.BlockSpec(blk, lambda i: (0, 0, i, u_col)),
                  const(pin), const(pout), const(bblk), const(cblk), const(are), const(aim),
                  const(d2), const(wg), const(bg)],
        out_specs=pl.BlockSpec(blk, lambda i: (0, 0, i, 0)),
        out_shape=jax.ShapeDtypeStruct((nbatch, TOKEN_RESIDUES, per_res, width), BF16),
        scratch_shapes=[
            pltpu.VMEM((rows // SUBLANES, SUBLANES, jstates), F32),
            pltpu.VMEM((rows // SUBLANES, SUBLANES, jstates), F32),
            pltpu.VMEM((SUBLANES, nstate), F32),
            pltpu.VMEM((SUBLANES, nstate), F32),
        ],
        compiler_params=_params("arbitrary"),
        name="s5_ssm",
    )(src.reshape(nbatch, TOKEN_RESIDUES, per_res, src.shape[-1]),
      pin, pout, bblk, cblk, are, aim, d2, wg, bg)
    return out.reshape(nbatch * seq, width)


ROUTER_TS = 256


def _router_kernel(x_ref, g_ref, sh_ref, sc_ref, rw_ref, rb_ref,
                   sel_ref, wts_ref, cnt_ref, carry_ref):
    @pl.when(jnp.logical_and(pl.program_id(0) == 0, pl.program_id(1) == 0))
    def _():
        carry_ref[...] = jnp.zeros_like(carry_ref)

    h = _rms(x_ref[...], g_ref[...]) * (1.0 + sc_ref[...]) + sh_ref[...]
    ts, ne = h.shape[0], rw_ref.shape[0]
    idx = lax.broadcasted_iota(jnp.int32, (ts, ne), 1)
    logits = jnp.broadcast_to(rb_ref[...], (ts, ne))
    for ex in range(ne):
        col = jnp.sum(h * rw_ref[ex:ex + 1, :], axis=1, keepdims=True)
        logits = logits + jnp.where(idx == ex, col, 0.0)
    m1 = jnp.max(logits, axis=1, keepdims=True)
    i1 = jnp.min(jnp.where(logits == m1, idx, ne), axis=1, keepdims=True)
    rest = jnp.where(idx == i1, -jnp.inf, logits)
    m2 = jnp.max(rest, axis=1, keepdims=True)
    i2 = jnp.min(jnp.where(rest == m2, idx, ne), axis=1, keepdims=True)
    e = jnp.exp(m2 - m1)
    w1 = 1.0 / (1.0 + e)
    w2 = e / (1.0 + e)

    onehot = jnp.where(idx == i1, 1.0, 0.0) + jnp.where(idx == i2, 1.0, 0.0)
    row = lax.broadcasted_iota(jnp.int32, (ts, ts), 0)
    col = lax.broadcasted_iota(jnp.int32, (ts, ts), 1)
    lower = jnp.where(col < row, 1.0, 0.0).astype(BF16)
    before = carry_ref[...] + _dot(lower, onehot.astype(BF16))
    r1 = jnp.sum(jnp.where(idx == i1, before, 0.0), axis=1, keepdims=True).astype(jnp.int32)
    r2 = jnp.sum(jnp.where(idx == i2, before, 0.0), axis=1, keepdims=True).astype(jnp.int32)
    total = carry_ref[...] + jnp.sum(onehot, axis=0, keepdims=True)
    carry_ref[...] = total
    cnt_ref[...] = total
    sel_ref[...] = jnp.where(idx == 0, i1, jnp.where(idx == 1, i2, jnp.where(
        idx == 2, r1, jnp.where(idx == 3, r2, 0))))
    wts_ref[...] = jnp.where(idx == 0, w1, jnp.where(idx == 1, w2, 0.0))


def _router(x, g, shift, scale, router_w, router_b):
    b, s, d = x.shape
    ne = router_w.shape[-1]
    vec = pl.BlockSpec((None, 1, d), lambda i, j: (i, 0, 0))
    tok = pl.BlockSpec((None, ROUTER_TS, ne), lambda i, j: (i, j, 0))
    sel, wts, cnt = pl.pallas_call(
        _router_kernel,
        grid=(b, s // ROUTER_TS),
        in_specs=[
            pl.BlockSpec((None, ROUTER_TS, d), lambda i, j: (i, j, 0)),
            pl.BlockSpec((1, d), lambda i, j: (0, 0)),
            vec, vec,
            pl.BlockSpec((ne, d), lambda i, j: (0, 0)),
            pl.BlockSpec((1, ne), lambda i, j: (0, 0)),
        ],
        out_specs=[tok, tok, pl.BlockSpec((1, ne), lambda i, j: (0, 0))],
        out_shape=[jax.ShapeDtypeStruct((b, s, ne), jnp.int32),
                   jax.ShapeDtypeStruct((b, s, ne), F32),
                   jax.ShapeDtypeStruct((1, ne), F32)],
        scratch_shapes=[pltpu.VMEM((1, ne), F32)],
        compiler_params=_params("arbitrary", "arbitrary"),
        name="router",
    )(x, g.reshape(1, d), shift.reshape(b, 1, d), scale.reshape(b, 1, d),
      router_w.T, router_b.reshape(1, ne))
    return sel.reshape(b * s, ne), wts.reshape(b * s, ne), cnt


MOE_TM = 256
MOE_TT = 256
MOE_UP_TN = 1408
MOE_DOWN_TN = 2048


def _moe_plan(sel, cnt, tokens):
    ne = cnt.shape[-1]
    counts = cnt[0].astype(jnp.int32)
    padded = (counts + MOE_TM - 1) // MOE_TM * MOE_TM
    ends = jnp.cumsum(padded)
    starts = ends - padded
    pos1 = starts[sel[:, 0]] + sel[:, 2]
    pos2 = starts[sel[:, 1]] + sel[:, 3]
    nt = tokens // MOE_TT
    pos = jnp.concatenate([pos1.reshape(nt, MOE_TT), pos2.reshape(nt, MOE_TT)], axis=1)
    max_tiles = 2 * tokens // MOE_TM + ne
    tile_start = jnp.arange(max_tiles, dtype=jnp.int32) * MOE_TM
    tile_expert = jnp.minimum(
        jnp.sum((tile_start[:, None] >= ends[None, :]).astype(jnp.int32), axis=1), ne - 1)
    num_tiles = (ends[-1] // MOE_TM).reshape(1)
    zero_tiles = jnp.concatenate([
        jnp.maximum(ends // MOE_TM - 1, 0),
        jnp.minimum(num_tiles[0] + jnp.arange(ne, dtype=jnp.int32), max_tiles - 1)])
    nonempty = counts > 0
    ar = jnp.arange(ne, dtype=jnp.int32)
    order = (ar[:, None] + 1 + ar[None, :]) % ne
    nxt = order[ar, jnp.argmax(nonempty[order], axis=1)].astype(jnp.int32)
    gidx = jnp.cumsum(nonempty.astype(jnp.int32)) - 1
    groups = jnp.concatenate([nxt, gidx, jnp.sum(nonempty.astype(jnp.int32)).reshape(1)])
    return (pos.reshape(nt, 1, 2 * MOE_TT), tile_expert, num_tiles, max_tiles, zero_tiles,
            groups)


def _row_copies(pos_ref, t, make):
    tt = pos_ref.shape[1] // 2
    return make(0, t, pos_ref[0, t]), make(1, t, pos_ref[0, tt + t])


def _issue_and_drain(pos_ref, make):
    tt = pos_ref.shape[1] // 2

    def issue(t, carry):
        for k, cp in enumerate(_row_copies(pos_ref, t, make)):
            cp.start(priority=k)
        return carry

    def drain(t, carry):
        for cp in _row_copies(pos_ref, t, make):
            cp.wait()
        return carry

    lax.fori_loop(0, tt, issue, 0, unroll=4)
    lax.fori_loop(0, tt, drain, 0, unroll=4)


def _pack_pairs(lo, hi):
    lo_bits = pltpu.bitcast(lo.astype(BF16).astype(F32), jnp.uint32)
    hi_bits = pltpu.bitcast(hi.astype(BF16).astype(F32), jnp.uint32)
    return jnp.bitwise_or(jnp.bitwise_and(hi_bits, jnp.uint32(0xFFFF0000)),
                          jnp.right_shift(lo_bits, jnp.uint32(16)))


def _unpack_pairs(words):
    lo = pltpu.bitcast(jnp.left_shift(words, jnp.uint32(16)), F32)
    hi = pltpu.bitcast(jnp.bitwise_and(words, jnp.uint32(0xFFFF0000)), F32)
    return lo, hi


def _moe_dispatch_kernel(pos_ref, zt_ref, x_ref, g_ref, sh_ref, sc_ref, xs_ref, hbuf_ref, sem):
    @pl.when(jnp.logical_and(pl.program_id(0) == 0, pl.program_id(1) == 0))
    def _():
        hbuf_ref[...] = jnp.zeros_like(hbuf_ref)
        for k in range(zt_ref.shape[0]):
            cp = pltpu.make_async_copy(
                hbuf_ref, xs_ref.at[pl.ds(pl.multiple_of(zt_ref[k] * MOE_TM, MOE_TM), MOE_TM), :],
                sem)
            cp.start()
            cp.wait()

    h = _rms(x_ref[...], g_ref[...]) * (1.0 + sc_ref[...]) + sh_ref[...]
    half = h.shape[1] // 2
    hbuf_ref[...] = _pack_pairs(h[:, :half], h[:, half:])

    def make(k, t, p):
        return pltpu.make_async_copy(hbuf_ref.at[pl.ds(t, 1), :], xs_ref.at[pl.ds(p, 1), :], sem)

    _issue_and_drain(pos_ref, make)


def _moe_dispatch(x, g, shift, scale, plan):
    pos, _, _, max_tiles, zero_tiles, _ = plan
    b, s, d = x.shape
    per_b = s // MOE_TT
    assert MOE_TT == MOE_TM
    vec = pl.BlockSpec((None, 1, d), lambda i, j: (i, 0, 0))
    return pl.pallas_call(
        _moe_dispatch_kernel,
        grid=(b, per_b),
        in_specs=[
            pl.BlockSpec((None, 1, 2 * MOE_TT), lambda i, j: (i * per_b + j, 0, 0),
                         memory_space=pltpu.SMEM),
            pl.BlockSpec(memory_space=pltpu.SMEM),
            pl.BlockSpec((None, MOE_TT, d), lambda i, j: (i, j, 0)),
            pl.BlockSpec((1, d), lambda i, j: (0, 0)),
            vec, vec,
        ],
        out_specs=pl.BlockSpec(memory_space=pl.ANY),
        out_shape=jax.ShapeDtypeStruct((max_tiles * MOE_TM, d // 2), jnp.uint32),
        scratch_shapes=[pltpu.VMEM((MOE_TT, d // 2), jnp.uint32), pltpu.SemaphoreType.DMA(())],
        compiler_params=_params("arbitrary", "arbitrary"),
        name="moe_dispatch",
    )(pos, zero_tiles, x, g.reshape(1, d), shift.reshape(b, 1, d), scale.reshape(b, 1, d))


def _moe_mm_kernel(te_ref, nt_ref, grp_ref, a_ref, *rest, mode, li, tn):
    nw = 2 if mode == "up" else 1
    w_hbm = rest[:nw]
    o_ref = rest[nw]
    wb = rest[nw + 1:2 * nw + 1]
    stage = rest[2 * nw + 1:3 * nw + 1]
    wsem = rest[3 * nw + 1]
    j = pl.program_id(0)
    i = pl.program_id(1)
    ne = (grp_ref.shape[0] - 1) // 2
    expert = te_ref[i]
    active = i < nt_ref[0]
    new_expert = jnp.logical_or(i == 0, expert != te_ref[jnp.maximum(i - 1, 0)])

    def weight_copies(ex, jx):
        cols = pl.ds(pl.multiple_of(jx * tn, LANES), tn)
        return [pltpu.make_async_copy(w_hbm[k].at[li, ex, :, cols], stage[k], wsem.at[k])
                for k in range(nw)]

    @pl.when(jnp.logical_and(active, new_expert))
    def _():
        @pl.when(jnp.logical_and(j == 0, grp_ref[ne + expert] == 0))
        def _():
            for cp in weight_copies(expert, j):
                cp.start()

        for cp in weight_copies(expert, j):
            cp.wait()
        for k in range(nw):
            _cast_weight(stage[k], wb[k])
        nxt = grp_ref[expert]
        nxt_j = j + jnp.where(nxt <= expert, 1, 0)

        @pl.when(nxt_j < pl.num_programs(0))
        def _():
            for cp in weight_copies(nxt, nxt_j):
                cp.start()

    @pl.when(active)
    def _():
        if mode == "down":
            p = lax.dot_general(wb[0][...], a_ref[...], (((0,), (0,)), ((), ())),
                                preferred_element_type=F32).T
            half = p.shape[1] // 2
            o_ref[...] = _pack_pairs(p[:, :half], p[:, half:])
        else:
            lo, hi = _unpack_pairs(a_ref[...])
            lo, hi = lo.astype(BF16), hi.astype(BF16)
            half = lo.shape[1]
            nt_dims = (((0,), (1,)), ((), ()))

            def proj(w_ref):
                return (lax.dot_general(w_ref[0:half, :], lo, nt_dims, preferred_element_type=F32)
                        + lax.dot_general(w_ref[half:, :], hi, nt_dims,
                                          preferred_element_type=F32))

            u = proj(wb[0])
            o_ref[...] = (u * _sigmoid(u) * proj(wb[1])).astype(o_ref.dtype)

    @pl.when(jnp.logical_not(active))
    def _():
        o_ref[...] = jnp.zeros_like(o_ref)


def _moe_mm(a, ws, li, plan, tn, mode):
    _, tile_expert, num_tiles, max_tiles, _, groups = plan
    k, n = ws[0].shape[-2], ws[0].shape[-1]
    rows = max_tiles * MOE_TM
    nw = len(ws)

    def tile(i, nt):
        return jnp.minimum(i, nt[0] - 1)

    if mode == "down":
        a_spec = pl.BlockSpec((k, MOE_TM), lambda j, i, te, nt, gr: (0, tile(i, nt)))
        out_spec = pl.BlockSpec((MOE_TM, tn // 2), lambda j, i, te, nt, gr: (i, j))
        out_shape = jax.ShapeDtypeStruct((rows, n // 2), jnp.uint32)
    else:
        a_spec = pl.BlockSpec((MOE_TM, k // 2), lambda j, i, te, nt, gr: (tile(i, nt), 0))
        out_spec = pl.BlockSpec((tn, MOE_TM), lambda j, i, te, nt, gr: (j, i))
        out_shape = jax.ShapeDtypeStruct((n, rows), BF16)
    return pl.pallas_call(
        functools.partial(_moe_mm_kernel, mode=mode, li=li, tn=tn),
        grid_spec=pltpu.PrefetchScalarGridSpec(
            num_scalar_prefetch=3,
            grid=(n // tn, max_tiles),
            in_specs=[a_spec] + [pl.BlockSpec(memory_space=pl.ANY)] * nw,
            out_specs=out_spec,
            scratch_shapes=([pltpu.VMEM((k, tn), BF16)] * nw + [pltpu.VMEM((k, tn), F32)] * nw
                            + [pltpu.SemaphoreType.DMA((nw,))]),
        ),
        out_shape=out_shape,
        compiler_params=_params("arbitrary", "arbitrary"),
        name=f"moe_mm_{mode}",
    )(tile_expert, num_tiles, groups, a, *ws)


def _moe_combine_kernel(pos_ref, x_ref, g_ref, wts_ref, y_ref, *rest, final):
    if final:
        ng_ref, o_ref, ybuf_ref, sem = rest
    else:
        ng_ref, sh_ref, sc_ref, o_ref, h_ref, ybuf_ref, sem = rest

    def make(k, t, p):
        return pltpu.make_async_copy(y_ref.at[pl.ds(p, 1), :], ybuf_ref.at[k, pl.ds(t, 1), :], sem)

    _issue_and_drain(pos_ref, make)

    def rows(k):
        lo, hi = _unpack_pairs(ybuf_ref[k])
        hw = MOE_DOWN_TN // 2
        parts = []
        for c in range(lo.shape[1] // hw):
            parts += [lo[:, c * hw:(c + 1) * hw], hi[:, c * hw:(c + 1) * hw]]
        return jnp.concatenate(parts, axis=1)

    f = wts_ref[:, 0:1] * rows(0) + wts_ref[:, 1:2] * rows(1)
    xn = x_ref[...] + g_ref[...] * f
    yn = _rms(xn, ng_ref[...])
    if final:
        o_ref[...] = yn
    else:
        o_ref[...] = xn
        h_ref[...] = (yn * (1.0 + sc_ref[...]) + sh_ref[...]).astype(h_ref.dtype)


def _moe_combine(x, gate, wts, y, pos, seq, norm_g, shift=None, scale=None):
    m, d = x.shape
    nb = gate.shape[0]
    ne = wts.shape[-1]
    final = shift is None
    vec = pl.BlockSpec((None, 1, d), lambda i: (i * MOE_TT // seq, 0, 0))
    row = pl.BlockSpec((MOE_TT, d), lambda i: (i, 0))
    in_specs = [
        pl.BlockSpec((None, 1, 2 * MOE_TT), lambda i: (i, 0, 0), memory_space=pltpu.SMEM),
        row, vec,
        pl.BlockSpec((MOE_TT, ne), lambda i: (i, 0)),
        pl.BlockSpec(memory_space=pl.ANY),
        pl.BlockSpec((1, d), lambda i: (0, 0)),
    ]
    args = [pos, x, gate.reshape(nb, 1, d), wts, y, norm_g.reshape(1, d)]
    if final:
        out_specs, out_shape = row, jax.ShapeDtypeStruct((m, d), F32)
    else:
        in_specs += [vec, vec]
        args += [shift.reshape(nb, 1, d), scale.reshape(nb, 1, d)]
        out_specs = [row, row]
        out_shape = [jax.ShapeDtypeStruct((m, d), F32), jax.ShapeDtypeStruct((m, d), BF16)]
    return pl.pallas_call(
        functools.partial(_moe_combine_kernel, final=final),
        grid=(m // MOE_TT,),
        in_specs=in_specs,
        out_specs=out_specs,
        out_shape=out_shape,
        scratch_shapes=[pltpu.VMEM((2, MOE_TT, d // 2), jnp.uint32), pltpu.SemaphoreType.DMA(())],
        compiler_params=_params("arbitrary"),
        name="moe_combine",
    )(*args)


def kernel(x, c, ada_w, ada_b, norm_mix_g, norm_ffn_g, final_norm_g, w_in, ssm_a_re, ssm_a_im, ssm_log_dt, ssm_b_re, ssm_b_im, ssm_c_re, ssm_c_im, ssm_d, w_glu, b_glu, w_branch, w_out, ffn_w1, ffn_w3, ffn_w2, router_w, router_b, moe_w1, moe_w3, moe_w2):
    batch, seq, d = x.shape
    depth = ada_w.shape[0]
    tokens = batch * seq
    ssm_width = w_glu.shape[1]
    per_res = seq // TOKEN_RESIDUES
    tn_in = ATTN_OUT
    assert ssm_width == tn_in and w_in.shape[-1] == (3 * N_GROUPS + 1) * tn_in + 2 * d

    def to_residue_major(t):
        f = t.shape[-1]
        return t.reshape(batch, per_res, TOKEN_RESIDUES, f).transpose(0, 2, 1, 3).reshape(batch, seq, f)

    mod = _ada_modulation(c, ada_w, ada_b).reshape(depth, batch, 6, d)
    x2 = to_residue_major(x).reshape(tokens, d)
    h = None
    out = None
    for layer in range(depth):
        sh1, sc1, g1, sh2, sc2, g2 = (mod[layer, :, i] for i in range(6))

        if h is None:
            h = _norm_mod(x2.reshape(batch, seq, d), norm_mix_g[layer], sh1, sc1)
        qkv0 = _mm_plain(h, w_in, (layer,), lambda j: N_GROUPS * j, 3, F32, *W_IN_TILE_F32)
        rest = _mm_plain(h, w_in, (layer,),
                         lambda j: j + 1 + jnp.where(j >= 2, 1, 0) + jnp.where(j >= 4, 1, 0),
                         11, BF16, *W_IN_TILE)
        outs, lses = _attention_groups(qkv0, rest, batch, seq)
        tables = _ssm_tables(ssm_a_re[layer], ssm_a_im[layer], ssm_log_dt[layer],
                             ssm_b_re[layer], ssm_b_im[layer], ssm_c_re[layer], ssm_c_im[layer])
        ssm = _s5_ssm(rest, 6, batch, seq, tables, ssm_d[layer], w_glu[layer], b_glu[layer])
        merged = _mm_branch(outs, lses, ssm, w_branch, layer, rest, 7, BRANCH_ROWS)
        h = None

        li = layer // 2
        if layer % 2 == 0:
            x2, hf = _mm_resid_norm(merged, w_out, layer, x2, g1, norm_ffn_g[layer], sh2, sc2,
                                    seq, W_OUT_NORM_ROWS)
            act = _mm_swiglu(hf, ffn_w1, ffn_w3, (li,), *SWIGLU_TILE)
            x2 = _mm_resid(act, ffn_w2, (li,), x2, g2, seq, *FFN_DOWN_TILE)
        else:
            x2 = _mm_resid(merged, w_out, (layer,), x2, g1, seq, *W_OUT_TILE)
            x3 = x2.reshape(batch, seq, d)
            sel, wts, cnt = _router(x3, norm_ffn_g[layer], sh2, sc2, router_w[li], router_b[li])
            plan = _moe_plan(sel, cnt, tokens)
            xs = _moe_dispatch(x3, norm_ffn_g[layer], sh2, sc2, plan)
            act = _moe_mm(xs, (moe_w1, moe_w3), li, plan, MOE_UP_TN, "up")
            y = _moe_mm(act, (moe_w2,), li, plan, MOE_DOWN_TN, "down")
            if layer + 1 < depth:
                x2, h = _moe_combine(x2, g2, wts, y, plan[0], seq, norm_mix_g[layer + 1],
                                     mod[layer + 1, :, 0], mod[layer + 1, :, 1])
            else:
                out = _moe_combine(x2, g2, wts, y, plan[0], seq, final_norm_g)
    if out is None:
        out = _final_norm(x2.reshape(batch, seq, d), final_norm_g)
    return out.reshape(batch, TOKEN_RESIDUES, per_res, d).transpose(0, 2, 1, 3).reshape(batch, seq, d)
```

```python
import functools
import math

import jax
import jax.numpy as jnp
from jax import lax
from jax.experimental import pallas as pl
from jax.experimental.pallas import tpu as pltpu

F32 = jnp.float32
BF16 = jnp.bfloat16

LANES = 128
SUBLANES = 8
V7X_VMEM_BYTES = 64 * 1024 * 1024
VMEM_LIMIT_BYTES = V7X_VMEM_BYTES * 7 // 8

ATTN_PATTERNS = ((128, 1), (512, 4), (2048, 16))
N_GROUPS = len(ATTN_PATTERNS)
HEADS = 8
HEAD_DIM = 128
ATTN_OUT = HEADS * HEAD_DIM
ATTN_BLOCK = 128
SSM_GROUP_CH = 16
RMS_EPS = 1e-6
NEG_BIG = -1e30

W_IN_TILE_F32 = (1024, ATTN_OUT)
W_IN_TILE = (2048, ATTN_OUT)
BRANCH_ROWS = 512
W_OUT_TILE = (1024, 1024)
W_OUT_NORM_ROWS = 512
SWIGLU_TILE = (2048, 512)
FFN_DOWN_TILE = (512, 512)


def _params(*semantics):
    return pltpu.CompilerParams(dimension_semantics=semantics,
                                vmem_limit_bytes=VMEM_LIMIT_BYTES)


def _sigmoid(v):
    return 1.0 / (1.0 + jnp.exp(-v))


ADA_KR = 256
ADA_CN = 1536


def _ada_kernel(ct_ref, w_ref, b_ref, o_ref, acc_ref):
    kstep = pl.program_id(1)
    kr = w_ref.shape[0]
    ct = ct_ref[pl.ds(pl.multiple_of(kstep * kr, kr), kr), :]
    act = (ct * _sigmoid(ct)).astype(BF16)

    @pl.when(kstep == 0)
    def _():
        acc_ref[...] = jnp.zeros_like(acc_ref)

    for c0 in range(0, w_ref.shape[1], ADA_CN):
        acc_ref[c0:c0 + ADA_CN, :] += lax.dot_general(
            w_ref[:, c0:c0 + ADA_CN].astype(BF16), act, (((0,), (0,)), ((), ())),
            preferred_element_type=F32)

    @pl.when(kstep == pl.num_programs(1) - 1)
    def _():
        for c0 in range(0, w_ref.shape[1], ADA_CN):
            rows = acc_ref[c0:c0 + ADA_CN, :].T[:o_ref.shape[0], :]
            o_ref[:, c0:c0 + ADA_CN] = rows + b_ref[:, c0:c0 + ADA_CN]


def _ada_modulation(c, ada_w, ada_b):
    depth, k, n = ada_w.shape
    nb = c.shape[0]
    ct = jnp.zeros((k, LANES), F32).at[:, :nb].set(c.T)
    out = pl.pallas_call(
        _ada_kernel,
        grid=(depth, k // ADA_KR),
        in_specs=[
            pl.BlockSpec((k, LANES), lambda l, i: (0, 0)),
            pl.BlockSpec((None, ADA_KR, n), lambda l, i: (l, i, 0)),
            pl.BlockSpec((None, 1, n), lambda l, i: (l, 0, 0)),
        ],
        out_specs=pl.BlockSpec((None, SUBLANES, n), lambda l, i: (l, 0, 0)),
        out_shape=jax.ShapeDtypeStruct((depth, SUBLANES, n), F32),
        scratch_shapes=[pltpu.VMEM((n, LANES), F32)],
        compiler_params=_params("arbitrary", "arbitrary"),
        name="ada_modulation",
    )(ct, ada_w, ada_b.reshape(depth, 1, n))
    return out[:, :nb, :]


NORM_TS = 512


def _rms(x, g):
    return x * lax.rsqrt(jnp.mean(x * x, axis=-1, keepdims=True) + RMS_EPS) * g


def _norm_mod_kernel(x_ref, g_ref, sh_ref, sc_ref, o_ref):
    y = _rms(x_ref[...], g_ref[...])
    o_ref[...] = (y * (1.0 + sc_ref[...]) + sh_ref[...]).astype(o_ref.dtype)


def _norm_mod(x, g, shift, scale):
    b, s, d = x.shape
    vec = pl.BlockSpec((None, 1, d), lambda i, j: (i, 0, 0))
    out = pl.pallas_call(
        _norm_mod_kernel,
        grid=(b, s // NORM_TS),
        in_specs=[
            pl.BlockSpec((None, NORM_TS, d), lambda i, j: (i, j, 0)),
            pl.BlockSpec((1, d), lambda i, j: (0, 0)),
            vec, vec,
        ],
        out_specs=pl.BlockSpec((None, NORM_TS, d), lambda i, j: (i, j, 0)),
        out_shape=jax.ShapeDtypeStruct((b, s, d), BF16),
        compiler_params=_params("arbitrary", "arbitrary"),
        name="norm_mod",
    )(x, g.reshape(1, d), shift.reshape(b, 1, d), scale.reshape(b, 1, d))
    return out.reshape(b * s, d)


def _final_norm_kernel(x_ref, g_ref, o_ref):
    o_ref[...] = _rms(x_ref[...], g_ref[...])


def _final_norm(x, g):
    b, s, d = x.shape
    return pl.pallas_call(
        _final_norm_kernel,
        grid=(b, s // NORM_TS),
        in_specs=[
            pl.BlockSpec((None, NORM_TS, d), lambda i, j: (i, j, 0)),
            pl.BlockSpec((1, d), lambda i, j: (0, 0)),
        ],
        out_specs=pl.BlockSpec((None, NORM_TS, d), lambda i, j: (i, j, 0)),
        out_shape=jax.ShapeDtypeStruct((b, s, d), F32),
        compiler_params=_params("arbitrary", "arbitrary"),
        name="final_norm",
    )(x, g.reshape(1, d))


CAST_ROWS = 256


def _cast_weight(w_ref, wb_ref):
    def body(i, carry):
        r = pl.multiple_of(i * CAST_ROWS, CAST_ROWS)
        wb_ref[pl.ds(r, CAST_ROWS), :] = w_ref[pl.ds(r, CAST_ROWS), :].astype(BF16)
        return carry
    lax.fori_loop(0, w_ref.shape[0] // CAST_ROWS, body, 0)


def _cast_weight_transposed(w_ref, wt_ref):
    for r in range(0, w_ref.shape[0], CAST_ROWS):
        wt_ref[:, r:r + CAST_ROWS] = w_ref[r:r + CAST_ROWS, :].T.astype(BF16)


def _dot(a, b):
    return jnp.dot(a, b, preferred_element_type=F32)


def _mm_plain_kernel(a_ref, w_ref, o_ref, wb_ref):
    @pl.when(pl.program_id(1) == 0)
    def _():
        _cast_weight(w_ref, wb_ref)
    o_ref[...] = _dot(a_ref[...], wb_ref[...]).astype(o_ref.dtype)


def _mm_plain(a, w, w_idx, col_tile, n_tiles, out_dtype, tm, tn):
    m, k = a.shape
    lead = (None,) * len(w_idx)
    return pl.pallas_call(
        _mm_plain_kernel,
        grid=(n_tiles, m // tm),
        in_specs=[
            pl.BlockSpec((tm, k), lambda j, i: (i, 0)),
            pl.BlockSpec(lead + (k, tn), lambda j, i: w_idx + (0, col_tile(j))),
        ],
        out_specs=pl.BlockSpec((tm, tn), lambda j, i: (i, j)),
        out_shape=jax.ShapeDtypeStruct((m, n_tiles * tn), out_dtype),
        scratch_shapes=[pltpu.VMEM((k, tn), BF16)],
        compiler_params=_params("arbitrary", "arbitrary"),
        name="mm_plain",
    )(a, w)


def _mm_swiglu_kernel(a_ref, w1_ref, w3_ref, o_ref, wb1_ref, wb3_ref):
    @pl.when(pl.program_id(1) == 0)
    def _():
        _cast_weight(w1_ref, wb1_ref)
        _cast_weight(w3_ref, wb3_ref)
    a = a_ref[...]
    u = _dot(a, wb1_ref[...])
    v = _dot(a, wb3_ref[...])
    o_ref[...] = (u * _sigmoid(u) * v).astype(o_ref.dtype)


def _mm_swiglu(a, w1, w3, w_idx, tm, tn):
    m, k = a.shape
    n = w1.shape[-1]
    lead = (None,) * len(w_idx)
    wspec = pl.BlockSpec(lead + (k, tn), lambda j, i: w_idx + (0, j))
    return pl.pallas_call(
        _mm_swiglu_kernel,
        grid=(n // tn, m // tm),
        in_specs=[pl.BlockSpec((tm, k), lambda j, i: (i, 0)), wspec, wspec],
        out_specs=pl.BlockSpec((tm, tn), lambda j, i: (i, j)),
        out_shape=jax.ShapeDtypeStruct((m, n), BF16),
        scratch_shapes=[pltpu.VMEM((k, tn), BF16), pltpu.VMEM((k, tn), BF16)],
        compiler_params=_params("arbitrary", "arbitrary"),
        name="mm_swiglu",
    )(a, w1, w3)


def _mm_resid_kernel(a_ref, w_ref, x_ref, g_ref, o_ref, wb_ref):
    @pl.when(pl.program_id(1) == 0)
    def _():
        _cast_weight(w_ref, wb_ref)
    o_ref[...] = x_ref[...] + g_ref[...] * _dot(a_ref[...], wb_ref[...])


def _mm_resid(a, w, w_idx, x, gate, seq, tm, tn):
    m, k = a.shape
    n = w.shape[-1]
    nb = gate.shape[0]
    lead = (None,) * len(w_idx)
    return pl.pallas_call(
        _mm_resid_kernel,
        grid=(n // tn, m // tm),
        in_specs=[
            pl.BlockSpec((tm, k), lambda j, i: (i, 0)),
            pl.BlockSpec(lead + (k, tn), lambda j, i: w_idx + (0, j)),
            pl.BlockSpec((tm, tn), lambda j, i: (i, j)),
            pl.BlockSpec((None, 1, tn), lambda j, i: (i * tm // seq, 0, j)),
        ],
        out_specs=pl.BlockSpec((tm, tn), lambda j, i: (i, j)),
        out_shape=jax.ShapeDtypeStruct((m, n), F32),
        scratch_shapes=[pltpu.VMEM((k, tn), BF16)],
        compiler_params=_params("arbitrary", "arbitrary"),
        name="mm_resid",
    )(a, w, x, gate.reshape(nb, 1, n))


def _mm_resid_norm_kernel(a_ref, w_ref, x_ref, g_ref, ng_ref, sh_ref, sc_ref,
                          xo_ref, ho_ref, wb_ref):
    @pl.when(pl.program_id(0) == 0)
    def _():
        _cast_weight(w_ref, wb_ref)
    xn = x_ref[...] + g_ref[...] * _dot(a_ref[...], wb_ref[...])
    xo_ref[...] = xn
    yn = _rms(xn, ng_ref[...])
    ho_ref[...] = (yn * (1.0 + sc_ref[...]) + sh_ref[...]).astype(ho_ref.dtype)


def _mm_resid_norm(a, w, layer, x, gate, norm_g, shift, scale, seq, tm):
    m, k = a.shape
    n = w.shape[-1]
    nb = gate.shape[0]
    vec = pl.BlockSpec((None, 1, n), lambda i: (i * tm // seq, 0, 0))
    row = pl.BlockSpec((tm, n), lambda i: (i, 0))
    return pl.pallas_call(
        _mm_resid_norm_kernel,
        grid=(m // tm,),
        in_specs=[
            pl.BlockSpec((tm, k), lambda i: (i, 0)),
            pl.BlockSpec((None, k, n), lambda i: (layer, 0, 0), pipeline_mode=pl.Buffered(1)),
            row, vec,
            pl.BlockSpec((1, n), lambda i: (0, 0)),
            vec, vec,
        ],
        out_specs=[row, row],
        out_shape=[jax.ShapeDtypeStruct((m, n), F32), jax.ShapeDtypeStruct((m, n), BF16)],
        scratch_shapes=[pltpu.VMEM((k, n), BF16)],
        compiler_params=_params("arbitrary"),
        name="mm_resid_norm",
    )(a, w, x, gate.reshape(nb, 1, n), norm_g.reshape(1, n),
      shift.reshape(nb, 1, n), scale.reshape(nb, 1, n))


def _mix_groups(o_refs, l_refs):
    a0, a1, a2 = (r[...] for r in l_refs)
    m = jnp.maximum(jnp.maximum(a0, a1), a2)
    e0 = jnp.exp(a0 - m)
    e1 = jnp.exp(a1 - m)
    e2 = jnp.exp(a2 - m)
    inv = 1.0 / (e0 + e1 + e2)
    weights = (e0 * inv, e1 * inv, e2 * inv)
    cols = []
    for h in range(HEADS):
        sl = slice(h * HEAD_DIM, (h + 1) * HEAD_DIM)
        cols.append(sum(w[:, h:h + 1] * o[:, sl].astype(F32) for w, o in zip(weights, o_refs)))
    return jnp.concatenate(cols, axis=1)


def _mm_branch_kernel(o0_ref, o1_ref, o2_ref, l0_ref, l1_ref, l2_ref, ss_ref, wa_ref, ws_ref,
                      ga0_ref, ga1_ref, gs0_ref, gs1_ref, o_ref, wba_ref, wbs_ref):
    @pl.when(pl.program_id(0) == 0)
    def _():
        _cast_weight(wa_ref, wba_ref)
        _cast_weight(ws_ref, wbs_ref)
    attn = _mix_groups((o0_ref, o1_ref, o2_ref), (l0_ref, l1_ref, l2_ref)).astype(BF16)
    ssm = ss_ref[...]
    tn = ga0_ref.shape[1]
    for half, (ga_ref, gs_ref) in enumerate(((ga0_ref, gs0_ref), (ga1_ref, gs1_ref))):
        sl = slice(half * tn, (half + 1) * tn)
        pa = _dot(attn, wba_ref[:, sl])
        ps = _dot(ssm, wbs_ref[:, sl])
        ga = _sigmoid(ga_ref[...].astype(F32))
        gs = _sigmoid(gs_ref[...].astype(F32))
        o_ref[:, sl] = (ga * pa + gs * ps).astype(o_ref.dtype)


def _mm_branch(outs, lses, ssm, w_branch, layer, proj, gate_tile, tm):
    m, k = ssm.shape
    n = w_branch.shape[-1]
    assert n == 2 * ATTN_OUT
    row = lambda width: pl.BlockSpec((tm, width), lambda i: (i, 0))
    gate = lambda t: pl.BlockSpec((tm, ATTN_OUT), lambda i: (i, gate_tile + t))
    weight = lambda half: pl.BlockSpec((None, k, n), lambda i: (layer, half, 0),
                                       pipeline_mode=pl.Buffered(1))
    return pl.pallas_call(
        _mm_branch_kernel,
        grid=(m // tm,),
        in_specs=[row(ATTN_OUT)] * 3 + [row(LANES)] * 3 + [row(k), weight(0), weight(1),
                                                          gate(0), gate(1), gate(2), gate(3)],
        out_specs=row(n),
        out_shape=jax.ShapeDtypeStruct((m, n), BF16),
        scratch_shapes=[pltpu.VMEM((k, n), BF16), pltpu.VMEM((k, n), BF16)],
        compiler_params=_params("arbitrary"),
        name="mm_branch",
    )(*outs, *lses, ssm, w_branch, w_branch, proj, proj, proj, proj)


TOKEN_RESIDUES = 16
ATTN_BB = 4


def _attn_kernel(*refs, with_prev):
    if with_prev:
        q_ref, kc_ref, vc_ref, bc_ref, bp_ref, o_ref, lse_ref, kp_ref, vp_ref = refs

        @pl.when(pl.program_id(2) == 0)
        def _():
            kp_ref[...] = jnp.zeros_like(kp_ref)
            vp_ref[...] = jnp.zeros_like(vp_ref)
    else:
        q_ref, kc_ref, vc_ref, bc_ref, o_ref, lse_ref = refs
    blk = ATTN_BLOCK
    bb, nc, rpc = q_ref.shape[0], q_ref.shape[1], q_ref.shape[2]
    scale = HEAD_DIM ** -0.5
    lane = lax.broadcasted_iota(jnp.int32, (blk, LANES), 1)

    def heads(ref, bi):
        x = ref[bi].reshape(blk, ATTN_OUT)
        return jnp.stack([x[:, h * HEAD_DIM:(h + 1) * HEAD_DIM] for h in range(HEADS)]
                         ).astype(BF16)

    def qk(q, k):
        return lax.dot_general(q, k, (((2,), (2,)), ((0,), (0,))), preferred_element_type=F32)

    def pv(p, v):
        return lax.dot_general(p.astype(BF16), v, (((2,), (1,)), ((0,), (0,))),
                               preferred_element_type=F32)

    for bi in range(bb):
        q = heads(q_ref, bi)
        k_c = heads(kc_ref, bi)
        v_c = heads(vc_ref, bi)
        s_c = qk(q, k_c) * scale + bc_ref[...]
        m = jnp.max(s_c, axis=2, keepdims=True)
        if with_prev:
            bias_p = jnp.where(pl.program_id(2) > 0, bp_ref[...], NEG_BIG)
            s_p = qk(q, kp_ref[bi]) * scale + bias_p
            m = jnp.maximum(m, jnp.max(s_p, axis=2, keepdims=True))
        p_c = jnp.exp(s_c - m)
        l = jnp.sum(p_c, axis=2, keepdims=True)
        o = pv(p_c, v_c)
        if with_prev:
            p_p = jnp.exp(s_p - m)
            l = l + jnp.sum(p_p, axis=2, keepdims=True)
            o = o + pv(p_p, vp_ref[bi])
            kp_ref[bi] = k_c
            vp_ref[bi] = v_c
        o = o / l
        lse = m + jnp.log(l)
        lse_tile = jnp.zeros((blk, LANES), F32)
        for h in range(HEADS):
            sl = slice(h * HEAD_DIM, (h + 1) * HEAD_DIM)
            o_ref[bi, :, :, sl] = o[h].reshape(nc, rpc, HEAD_DIM).astype(o_ref.dtype)
            lse_tile = jnp.where(lane == h, lse[h], lse_tile)
        lse_ref[bi] = lse_tile.reshape(nc, rpc, LANES)


def _attn_bias(dilation, nc, back):
    rpc = ATTN_BLOCK // nc
    i = jnp.arange(ATTN_BLOCK, dtype=jnp.int32)
    off = nc * (i % rpc) + i // rpc
    dist = off[:, None] - off[None, :] + back * ATTN_BLOCK
    valid = jnp.logical_and(dist >= 0, dist <= ATTN_BLOCK)
    slopes = 2.0 ** (-8.0 * (jnp.arange(HEADS, dtype=F32) + 1.0) / HEADS)
    pen = slopes[:, None, None] * (dist * dilation).astype(F32)[None]
    return jnp.where(valid[None], -pen, NEG_BIG)


def _attention_group(qkv, qkv_cols, batch, seq, gi):
    window, dilation = ATTN_PATTERNS[gi]
    assert window // dilation == ATTN_BLOCK
    cols = qkv.shape[-1]
    per_res = seq // TOKEN_RESIDUES
    nc = TOKEN_RESIDUES // dilation
    rpc = ATTN_BLOCK // nc
    nblk = seq // dilation // ATTN_BLOCK
    with_prev = nblk > 1

    def shape5(c):
        return (batch, nc, dilation, per_res, c)

    def spec(c, col):
        return pl.BlockSpec((ATTN_BB, nc, None, rpc, c), lambda b, r, n: (b, 0, r, n, col))

    bias_spec = pl.BlockSpec((HEADS, ATTN_BLOCK, ATTN_BLOCK), lambda b, r, n: (0, 0, 0))
    in_specs = [spec(ATTN_OUT, col) for col in qkv_cols]
    carry = pltpu.VMEM((ATTN_BB, HEADS, ATTN_BLOCK, HEAD_DIM), BF16)
    view = qkv.reshape(shape5(cols))
    args = [view] * len(in_specs) + [_attn_bias(dilation, nc, 0)]
    in_specs.append(bias_spec)
    if with_prev:
        args.append(_attn_bias(dilation, nc, 1))
        in_specs.append(bias_spec)
    o, lse = pl.pallas_call(
        functools.partial(_attn_kernel, with_prev=with_prev),
        grid=(batch // ATTN_BB, dilation, nblk),
        in_specs=in_specs,
        out_specs=[spec(ATTN_OUT, 0), spec(LANES, 0)],
        out_shape=[
            jax.ShapeDtypeStruct(shape5(ATTN_OUT), qkv.dtype),
            jax.ShapeDtypeStruct(shape5(LANES), F32),
        ],
        scratch_shapes=[carry, carry] if with_prev else [],
        compiler_params=_params("arbitrary", "arbitrary", "arbitrary"),
        name=f"attn_g{gi}",
    )(*args)
    return o.reshape(batch * seq, ATTN_OUT), lse.reshape(batch * seq, LANES)


def _attention_groups(qkv0, rest, batch, seq):
    outs, lses = [], []
    for gi in range(N_GROUPS):
        if gi == 0:
            o, lse = _attention_group(qkv0, (0, 1, 2), batch, seq, gi)
        else:
            o, lse = _attention_group(rest, (gi - 1, gi + 1, gi + 3), batch, seq, gi)
        outs.append(o)
        lses.append(lse)
    return outs, lses


SSM_LT = 128
SSM_PASSES = 2
SSM_JBLK = 256


def _gelu_tanh(y):
    return 0.5 * y * (1.0 + jnp.tanh(math.sqrt(2.0 / math.pi) * (y + 0.044715 * (y * y * y))))


def _ssm_kernel(u_ref, pin_ref, pout_ref, bblk_ref, cblk_ref, are_ref, aim_ref, d_ref,
                wglu_ref, bglu_ref, o_ref, sre_ref, sim_ref, hre_ref, him_ref, *, nbatch):
    width = u_ref.shape[-1]
    rows = pin_ref.shape[1]
    njb = width // SSM_JBLK
    jstates = bblk_ref.shape[2] // 2
    pairs = rows // SUBLANES
    per_tile = SUBLANES // nbatch

    @pl.when(pl.program_id(0) == 0)
    def _():
        hre_ref[...] = jnp.zeros_like(hre_ref)
        him_ref[...] = jnp.zeros_like(him_ref)

    u_rm = u_ref[...].reshape(pout_ref.shape[1], width)
    out_rm = None
    for p in range(pin_ref.shape[0]):
        u_f32 = _dot(pin_ref[p], u_rm)
        u_tb = u_f32.astype(BF16)
        ys = []
        for j in range(njb):
            bu = _dot(u_tb[:, j * SSM_JBLK:(j + 1) * SSM_JBLK], bblk_ref[j])
            sre_ref[...] = bu[:, :jstates].reshape(pairs, SUBLANES, jstates)
            sim_ref[...] = bu[:, jstates:].reshape(pairs, SUBLANES, jstates)
            cs = slice(j * jstates, (j + 1) * jstates)
            ar = are_ref[0:nbatch, cs]
            ai = aim_ref[0:nbatch, cs]

            def step(k, carry):
                hr, hi = carry
                for t in range(per_tile):
                    rs = slice(t * nbatch, (t + 1) * nbatch)
                    nr = ar * hr - ai * hi + sre_ref[k, rs, :]
                    ni = ar * hi + ai * hr + sim_ref[k, rs, :]
                    sre_ref[k, rs, :] = nr
                    sim_ref[k, rs, :] = ni
                    hr, hi = nr, ni
                return hr, hi

            hr, hi = lax.fori_loop(0, pairs, step,
                                   (hre_ref[0:nbatch, cs], him_ref[0:nbatch, cs]), unroll=4)
            hre_ref[0:nbatch, cs] = hr
            him_ref[0:nbatch, cs] = hi

            h_re = sre_ref[...].reshape(rows, jstates).astype(BF16)
            h_im = sim_ref[...].reshape(rows, jstates).astype(BF16)
            ys.append(_dot(h_re, cblk_ref[j, 0:jstates, :]) + _dot(h_im, cblk_ref[j, jstates:, :]))

        y = jnp.concatenate(ys, axis=1) + d_ref[...] * u_f32
        z = _gelu_tanh(y).astype(BF16)
        g = _dot(z, wglu_ref[...]) + bglu_ref[...]
        o_tb = (g[:, :width] * _sigmoid(g[:, width:])).astype(BF16)
        back = _dot(pout_ref[p], o_tb)
        out_rm = back if out_rm is None else out_rm + back
    o_ref[...] = out_rm.astype(o_ref.dtype).reshape(o_ref.shape)


def _ssm_tables(a_re, a_im, log_dt, b_re, b_im, c_re, c_im):
    groups, nst = a_re.shape
    gpb = SSM_JBLK // SSM_GROUP_CH
    njb = groups // gpb
    lam = lax.complex(a_re.astype(F32), a_im.astype(F32))
    dt = jnp.exp(log_dt.astype(F32))[:, None]
    a_bar = jnp.exp(lam * dt)
    b_mat = lax.complex(b_re.astype(F32), b_im.astype(F32))
    b_bar = ((a_bar - 1.0) / lam)[:, :, None] * b_mat
    same_group = jnp.eye(gpb, dtype=jnp.bool_)

    def block_diag(t):
        r, q = t.shape[1:]
        t = t.astype(BF16).reshape(njb, gpb, r, 1, q)
        wide = jnp.where(same_group[None, :, None, :, None], t, jnp.zeros((), BF16))
        return wide.reshape(njb, gpb * r, gpb * q)

    def in_blocks(t):
        return block_diag(t.transpose(0, 2, 1))

    def out_blocks(t):
        return block_diag(t.transpose(0, 2, 1))

    bblk = jnp.concatenate([in_blocks(jnp.real(b_bar)), in_blocks(jnp.imag(b_bar))], axis=2)
    cblk = jnp.concatenate([out_blocks(c_re.astype(F32)), out_blocks(-c_im.astype(F32))], axis=1)
    are = jnp.broadcast_to(jnp.real(a_bar).reshape(1, groups * nst), (SUBLANES, groups * nst))
    aim = jnp.broadcast_to(jnp.imag(a_bar).reshape(1, groups * nst), (SUBLANES, groups * nst))
    return bblk.astype(BF16), cblk.astype(BF16), are, aim


def _ssm_row_perms(nbatch):
    per_res = SSM_PASSES * SSM_LT // TOKEN_RESIDUES
    col = jnp.arange(nbatch * TOKEN_RESIDUES * per_res, dtype=jnp.int32)
    b = col // (TOKEN_RESIDUES * per_res)
    t_local = TOKEN_RESIDUES * (col % per_res) + (col // per_res) % TOKEN_RESIDUES
    row = (t_local % SSM_LT) * nbatch + b
    pin = jnp.logical_and(
        (t_local // SSM_LT)[None, None, :] == jnp.arange(SSM_PASSES, dtype=jnp.int32)[:, None, None],
        row[None, None, :] == jnp.arange(SSM_LT * nbatch, dtype=jnp.int32)[None, :, None])
    pin = pin.astype(BF16)
    return pin, pin.transpose(0, 2, 1)


def _s5_ssm(src, u_col, nbatch, seq, tables, d_skip, w_glu, b_glu):
    bblk, cblk, are, aim = tables
    width = w_glu.shape[0]
    nstate = are.shape[1]
    jstates = bblk.shape[2] // 2
    rows = SSM_LT * nbatch
    step_rows = SSM_PASSES * SSM_LT // TOKEN_RESIDUES
    per_res = seq // TOKEN_RESIDUES
    pin, pout = _ssm_row_perms(nbatch)
    d2 = d_skip.reshape(1, width).astype(F32)
    wg = w_glu.astype(BF16)
    bg = b_glu.reshape(1, 2 * width).astype(F32)

    def const(a):
        return pl.BlockSpec(a.shape, lambda i: (0,) * a.ndim, pipeline_mode=pl.Buffered(1))

    blk = (nbatch, TOKEN_RESIDUES, step_rows, width)
    out = pl.pallas_call(
        functools.partial(_ssm_kernel, nbatch=nbatch),
        grid=(per_res // step_rows,),
        in_specs=[pl.BlockSpec(blk, lambda i: (0, 0, i, u_col)),
                  const(pin), const(pout), const(bblk), const(cblk), const(are), const(aim),
                  const(d2), const(wg), const(bg)],
        out_specs=pl.BlockSpec(blk, lambda i: (0, 0, i, 0)),
        out_shape=jax.ShapeDtypeStruct((nbatch, TOKEN_RESIDUES, per_res, width), BF16),
        scratch_shapes=[
            pltpu.VMEM((rows // SUBLANES, SUBLANES, jstates), F32),
            pltpu.VMEM((rows // SUBLANES, SUBLANES, jstates), F32),
            pltpu.VMEM((SUBLANES, nstate), F32),
            pltpu.VMEM((SUBLANES, nstate), F32),
        ],
        compiler_params=_params("arbitrary"),
        name="s5_ssm",
    )(src.reshape(nbatch, TOKEN_RESIDUES, per_res, src.shape[-1]),
      pin, pout, bblk, cblk, are, aim, d2, wg, bg)
    return out.reshape(nbatch * seq, width)


ROUTER_TS = 256


def _router_kernel(x_ref, g_ref, sh_ref, sc_ref, rw_ref, rb_ref,
                   sel_ref, wts_ref, cnt_ref, carry_ref):
    @pl.when(jnp.logical_and(pl.program_id(0) == 0, pl.program_id(1) == 0))
    def _():
        carry_ref[...] = jnp.zeros_like(carry_ref)

    h = _rms(x_ref[...], g_ref[...]) * (1.0 + sc_ref[...]) + sh_ref[...]
    ts, ne = h.shape[0], rw_ref.shape[0]
    idx = lax.broadcasted_iota(jnp.int32, (ts, ne), 1)
    logits = jnp.broadcast_to(rb_ref[...], (ts, ne))
    for ex in range(ne):
        col = jnp.sum(h * rw_ref[ex:ex + 1, :], axis=1, keepdims=True)
        logits = logits + jnp.where(idx == ex, col, 0.0)
    m1 = jnp.max(logits, axis=1, keepdims=True)
    i1 = jnp.min(jnp.where(logits == m1, idx, ne), axis=1, keepdims=True)
    rest = jnp.where(idx == i1, -jnp.inf, logits)
    m2 = jnp.max(rest, axis=1, keepdims=True)
    i2 = jnp.min(jnp.where(rest == m2, idx, ne), axis=1, keepdims=True)
    e = jnp.exp(m2 - m1)
    w1 = 1.0 / (1.0 + e)
    w2 = e / (1.0 + e)

    onehot = jnp.where(idx == i1, 1.0, 0.0) + jnp.where(idx == i2, 1.0, 0.0)
    row = lax.broadcasted_iota(jnp.int32, (ts, ts), 0)
    col = lax.broadcasted_iota(jnp.int32, (ts, ts), 1)
    lower = jnp.where(col < row, 1.0, 0.0).astype(BF16)
    before = carry_ref[...] + _dot(lower, onehot.astype(BF16))
    r1 = jnp.sum(jnp.where(idx == i1, before, 0.0), axis=1, keepdims=True).astype(jnp.int32)
    r2 = jnp.sum(jnp.where(idx == i2, before, 0.0), axis=1, keepdims=True).astype(jnp.int32)
    total = carry_ref[...] + jnp.sum(onehot, axis=0, keepdims=True)
    carry_ref[...] = total
    cnt_ref[...] = total
    sel_ref[...] = jnp.where(idx == 0, i1, jnp.where(idx == 1, i2, jnp.where(
        idx == 2, r1, jnp.where(idx == 3, r2, 0))))
    wts_ref[...] = jnp.where(idx == 0, w1, jnp.where(idx == 1, w2, 0.0))


def _router(x, g, shift, scale, router_w, router_b):
    b, s, d = x.shape
    ne = router_w.shape[-1]
    vec = pl.BlockSpec((None, 1, d), lambda i, j: (i, 0, 0))
    tok = pl.BlockSpec((None, ROUTER_TS, ne), lambda i, j: (i, j, 0))
    sel, wts, cnt = pl.pallas_call(
        _router_kernel,
        grid=(b, s // ROUTER_TS),
        in_specs=[
            pl.BlockSpec((None, ROUTER_TS, d), lambda i, j: (i, j, 0)),
            pl.BlockSpec((1, d), lambda i, j: (0, 0)),
            vec, vec,
            pl.BlockSpec((ne, d), lambda i, j: (0, 0)),
            pl.BlockSpec((1, ne), lambda i, j: (0, 0)),
        ],
        out_specs=[tok, tok, pl.BlockSpec((1, ne), lambda i, j: (0, 0))],
        out_shape=[jax.ShapeDtypeStruct((b, s, ne), jnp.int32),
                   jax.ShapeDtypeStruct((b, s, ne), F32),
                   jax.ShapeDtypeStruct((1, ne), F32)],
        scratch_shapes=[pltpu.VMEM((1, ne), F32)],
        compiler_params=_params("arbitrary", "arbitrary"),
        name="router",
    )(x, g.reshape(1, d), shift.reshape(b, 1, d), scale.reshape(b, 1, d),
      router_w.T, router_b.reshape(1, ne))
    return sel.reshape(b * s, ne), wts.reshape(b * s, ne), cnt


MOE_TM = 256
MOE_TT = 256
MOE_UP_TN = 1408
MOE_DOWN_TN = 2048


def _moe_plan(sel, cnt, tokens):
    ne = cnt.shape[-1]
    counts = cnt[0].astype(jnp.int32)
    padded = (counts + MOE_TM - 1) // MOE_TM * MOE_TM
    ends = jnp.cumsum(padded)
    starts = ends - padded
    pos1 = starts[sel[:, 0]] + sel[:, 2]
    pos2 = starts[sel[:, 1]] + sel[:, 3]
    nt = tokens // MOE_TT
    pos = jnp.concatenate([pos1.reshape(nt, MOE_TT), pos2.reshape(nt, MOE_TT)], axis=1)
    max_tiles = 2 * tokens // MOE_TM + ne
    tile_start = jnp.arange(max_tiles, dtype=jnp.int32) * MOE_TM
    tile_expert = jnp.minimum(
        jnp.sum((tile_start[:, None] >= ends[None, :]).astype(jnp.int32), axis=1), ne - 1)
    num_tiles = (ends[-1] // MOE_TM).reshape(1)
    zero_tiles = jnp.concatenate([
        jnp.maximum(ends // MOE_TM - 1, 0),
        jnp.minimum(num_tiles[0] + jnp.arange(ne, dtype=jnp.int32), max_tiles - 1)])
    nonempty = counts > 0
    ar = jnp.arange(ne, dtype=jnp.int32)
    order = (ar[:, None] + 1 + ar[None, :]) % ne
    nxt = order[ar, jnp.argmax(nonempty[order], axis=1)].astype(jnp.int32)
    gidx = jnp.cumsum(nonempty.astype(jnp.int32)) - 1
    groups = jnp.concatenate([nxt, gidx, jnp.sum(nonempty.astype(jnp.int32)).reshape(1)])
    return (pos.reshape(nt, 1, 2 * MOE_TT), tile_expert, num_tiles, max_tiles, zero_tiles,
            groups)


def _row_copies(pos_ref, t, make):
    tt = pos_ref.shape[1] // 2
    return make(0, t, pos_ref[0, t]), make(1, t, pos_ref[0, tt + t])


def _issue_and_drain(pos_ref, make):
    tt = pos_ref.shape[1] // 2

    def issue(t, carry):
        for k, cp in enumerate(_row_copies(pos_ref, t, make)):
            cp.start(priority=k)
        return carry

    def drain(t, carry):
        for cp in _row_copies(pos_ref, t, make):
            cp.wait()
        return carry

    lax.fori_loop(0, tt, issue, 0, unroll=4)
    lax.fori_loop(0, tt, drain, 0, unroll=4)


def _pack_pairs(lo, hi):
    lo_bits = pltpu.bitcast(lo.astype(BF16).astype(F32), jnp.uint32)
    hi_bits = pltpu.bitcast(hi.astype(BF16).astype(F32), jnp.uint32)
    return jnp.bitwise_or(jnp.bitwise_and(hi_bits, jnp.uint32(0xFFFF0000)),
                          jnp.right_shift(lo_bits, jnp.uint32(16)))


def _unpack_pairs(words):
    lo = pltpu.bitcast(jnp.left_shift(words, jnp.uint32(16)), F32)
    hi = pltpu.bitcast(jnp.bitwise_and(words, jnp.uint32(0xFFFF0000)), F32)
    return lo, hi


def _moe_dispatch_kernel(pos_ref, zt_ref, x_ref, g_ref, sh_ref, sc_ref, xs_ref, hbuf_ref, sem):
    @pl.when(jnp.logical_and(pl.program_id(0) == 0, pl.program_id(1) == 0))
    def _():
        hbuf_ref[...] = jnp.zeros_like(hbuf_ref)
        for k in range(zt_ref.shape[0]):
            cp = pltpu.make_async_copy(
                hbuf_ref, xs_ref.at[pl.ds(pl.multiple_of(zt_ref[k] * MOE_TM, MOE_TM), MOE_TM), :],
                sem)
            cp.start()
            cp.wait()

    h = _rms(x_ref[...], g_ref[...]) * (1.0 + sc_ref[...]) + sh_ref[...]
    half = h.shape[1] // 2
    hbuf_ref[...] = _pack_pairs(h[:, :half], h[:, half:])

    def make(k, t, p):
        return pltpu.make_async_copy(hbuf_ref.at[pl.ds(t, 1), :], xs_ref.at[pl.ds(p, 1), :], sem)

    _issue_and_drain(pos_ref, make)


def _moe_dispatch(x, g, shift, scale, plan):
    pos, _, _, max_tiles, zero_tiles, _ = plan
    b, s, d = x.shape
    per_b = s // MOE_TT
    assert MOE_TT == MOE_TM
    vec = pl.BlockSpec((None, 1, d), lambda i, j: (i, 0, 0))
    return pl.pallas_call(
        _moe_dispatch_kernel,
        grid=(b, per_b),
        in_specs=[
            pl.BlockSpec((None, 1, 2 * MOE_TT), lambda i, j: (i * per_b + j, 0, 0),
                         memory_space=pltpu.SMEM),
            pl.BlockSpec(memory_space=pltpu.SMEM),
            pl.BlockSpec((None, MOE_TT, d), lambda i, j: (i, j, 0)),
            pl.BlockSpec((1, d), lambda i, j: (0, 0)),
            vec, vec,
        ],
        out_specs=pl.BlockSpec(memory_space=pl.ANY),
        out_shape=jax.ShapeDtypeStruct((max_tiles * MOE_TM, d // 2), jnp.uint32),
        scratch_shapes=[pltpu.VMEM((MOE_TT, d // 2), jnp.uint32), pltpu.SemaphoreType.DMA(())],
        compiler_params=_params("arbitrary", "arbitrary"),
        name="moe_dispatch",
    )(pos, zero_tiles, x, g.reshape(1, d), shift.reshape(b, 1, d), scale.reshape(b, 1, d))


def _moe_mm_kernel(te_ref, nt_ref, grp_ref, a_ref, *rest, mode, li, tn):
    nw = 2 if mode == "up" else 1
    w_hbm = rest[:nw]
    o_ref = rest[nw]
    wb = rest[nw + 1:2 * nw + 1]
    stage = rest[2 * nw + 1:3 * nw + 1]
    wsem = rest[3 * nw + 1]
    j = pl.program_id(0)
    i = pl.program_id(1)
    ne = (grp_ref.shape[0] - 1) // 2
    expert = te_ref[i]
    active = i < nt_ref[0]
    new_expert = jnp.logical_or(i == 0, expert != te_ref[jnp.maximum(i - 1, 0)])

    def weight_copies(ex, jx):
        cols = pl.ds(pl.multiple_of(jx * tn, LANES), tn)
        return [pltpu.make_async_copy(w_hbm[k].at[li, ex, :, cols], stage[k], wsem.at[k])
                for k in range(nw)]

    @pl.when(jnp.logical_and(active, new_expert))
    def _():
        @pl.when(jnp.logical_and(j == 0, grp_ref[ne + expert] == 0))
        def _():
            for cp in weight_copies(expert, j):
                cp.start()

        for cp in weight_copies(expert, j):
            cp.wait()
        for k in range(nw):
            _cast_weight_transposed(stage[k], wb[k])
        nxt = grp_ref[expert]
        nxt_j = j + jnp.where(nxt <= expert, 1, 0)

        @pl.when(nxt_j < pl.num_programs(0))
        def _():
            for cp in weight_copies(nxt, nxt_j):
                cp.start()

    @pl.when(active)
    def _():
        if mode == "down":
            p = _dot(wb[0][...], a_ref[...]).T
            half = p.shape[1] // 2
            o_ref[...] = _pack_pairs(p[:, :half], p[:, half:])
        else:
            lo, hi = _unpack_pairs(a_ref[...])
            lo, hi = lo.astype(BF16), hi.astype(BF16)
            half = lo.shape[1]
            nt_dims = (((1,), (1,)), ((), ()))

            def proj(wt_ref):
                return (lax.dot_general(wt_ref[:, 0:half], lo, nt_dims, preferred_element_type=F32)
                        + lax.dot_general(wt_ref[:, half:], hi, nt_dims,
                                          preferred_element_type=F32))

            u = proj(wb[0])
            o_ref[...] = (u * _sigmoid(u) * proj(wb[1])).astype(o_ref.dtype)

    @pl.when(jnp.logical_not(active))
    def _():
        o_ref[...] = jnp.zeros_like(o_ref)


def _moe_mm(a, ws, li, plan, tn, mode):
    _, tile_expert, num_tiles, max_tiles, _, groups = plan
    k, n = ws[0].shape[-2], ws[0].shape[-1]
    rows = max_tiles * MOE_TM
    nw = len(ws)

    def tile(i, nt):
        return jnp.minimum(i, nt[0] - 1)

    if mode == "down":
        a_spec = pl.BlockSpec((k, MOE_TM), lambda j, i, te, nt, gr: (0, tile(i, nt)))
        out_spec = pl.BlockSpec((MOE_TM, tn // 2), lambda j, i, te, nt, gr: (i, j))
        out_shape = jax.ShapeDtypeStruct((rows, n // 2), jnp.uint32)
    else:
        a_spec = pl.BlockSpec((MOE_TM, k // 2), lambda j, i, te, nt, gr: (tile(i, nt), 0))
        out_spec = pl.BlockSpec((tn, MOE_TM), lambda j, i, te, nt, gr: (j, i))
        out_shape = jax.ShapeDtypeStruct((n, rows), BF16)
    return pl.pallas_call(
        functools.partial(_moe_mm_kernel, mode=mode, li=li, tn=tn),
        grid_spec=pltpu.PrefetchScalarGridSpec(
            num_scalar_prefetch=3,
            grid=(n // tn, max_tiles),
            in_specs=[a_spec] + [pl.BlockSpec(memory_space=pl.ANY)] * nw,
            out_specs=out_spec,
            scratch_shapes=([pltpu.VMEM((tn, k), BF16)] * nw + [pltpu.VMEM((k, tn), F32)] * nw
                            + [pltpu.SemaphoreType.DMA((nw,))]),
        ),
        out_shape=out_shape,
        compiler_params=_params("arbitrary", "arbitrary"),
        name=f"moe_mm_{mode}",
    )(tile_expert, num_tiles, groups, a, *ws)


def _moe_combine_kernel(pos_ref, x_ref, g_ref, wts_ref, y_ref, *rest, final):
    if final:
        ng_ref, o_ref, ybuf_ref, sem = rest
    else:
        ng_ref, sh_ref, sc_ref, o_ref, h_ref, ybuf_ref, sem = rest

    def make(k, t, p):
        return pltpu.make_async_copy(y_ref.at[pl.ds(p, 1), :], ybuf_ref.at[k, pl.ds(t, 1), :], sem)

    _issue_and_drain(pos_ref, make)

    def rows(k):
        lo, hi = _unpack_pairs(ybuf_ref[k])
        hw = MOE_DOWN_TN // 2
        parts = []
        for c in range(lo.shape[1] // hw):
            parts += [lo[:, c * hw:(c + 1) * hw], hi[:, c * hw:(c + 1) * hw]]
        return jnp.concatenate(parts, axis=1)

    f = wts_ref[:, 0:1] * rows(0) + wts_ref[:, 1:2] * rows(1)
    xn = x_ref[...] + g_ref[...] * f
    yn = _rms(xn, ng_ref[...])
    if final:
        o_ref[...] = yn
    else:
        o_ref[...] = xn
        h_ref[...] = (yn * (1.0 + sc_ref[...]) + sh_ref[...]).astype(h_ref.dtype)


def _moe_combine(x, gate, wts, y, pos, seq, norm_g, shift=None, scale=None):
    m, d = x.shape
    nb = gate.shape[0]
    ne = wts.shape[-1]
    final = shift is None
    vec = pl.BlockSpec((None, 1, d), lambda i: (i * MOE_TT // seq, 0, 0))
    row = pl.BlockSpec((MOE_TT, d), lambda i: (i, 0))
    in_specs = [
        pl.BlockSpec((None, 1, 2 * MOE_TT), lambda i: (i, 0, 0), memory_space=pltpu.SMEM),
        row, vec,
        pl.BlockSpec((MOE_TT, ne), lambda i: (i, 0)),
        pl.BlockSpec(memory_space=pl.ANY),
        pl.BlockSpec((1, d), lambda i: (0, 0)),
    ]
    args = [pos, x, gate.reshape(nb, 1, d), wts, y, norm_g.reshape(1, d)]
    if final:
        out_specs, out_shape = row, jax.ShapeDtypeStruct((m, d), F32)
    else:
        in_specs += [vec, vec]
        args += [shift.reshape(nb, 1, d), scale.reshape(nb, 1, d)]
        out_specs = [row, row]
        out_shape = [jax.ShapeDtypeStruct((m, d), F32), jax.ShapeDtypeStruct((m, d), BF16)]
    return pl.pallas_call(
        functools.partial(_moe_combine_kernel, final=final),
        grid=(m // MOE_TT,),
        in_specs=in_specs,
        out_specs=out_specs,
        out_shape=out_shape,
        scratch_shapes=[pltpu.VMEM((2, MOE_TT, d // 2), jnp.uint32), pltpu.SemaphoreType.DMA(())],
        compiler_params=_params("arbitrary"),
        name="moe_combine",
    )(*args)


def kernel(x, c, ada_w, ada_b, norm_mix_g, norm_ffn_g, final_norm_g, w_in, ssm_a_re, ssm_a_im, ssm_log_dt, ssm_b_re, ssm_b_im, ssm_c_re, ssm_c_im, ssm_d, w_glu, b_glu, w_branch, w_out, ffn_w1, ffn_w3, ffn_w2, router_w, router_b, moe_w1, moe_w3, moe_w2):
    batch, seq, d = x.shape
    depth = ada_w.shape[0]
    tokens = batch * seq
    ssm_width = w_glu.shape[1]
    per_res = seq // TOKEN_RESIDUES
    tn_in = ATTN_OUT
    assert ssm_width == tn_in and w_in.shape[-1] == (3 * N_GROUPS + 1) * tn_in + 2 * d

    def to_residue_major(t):
        f = t.shape[-1]
        return t.reshape(batch, per_res, TOKEN_RESIDUES, f).transpose(0, 2, 1, 3).reshape(batch, seq, f)

    mod = _ada_modulation(c, ada_w, ada_b).reshape(depth, batch, 6, d)
    x2 = to_residue_major(x).reshape(tokens, d)
    h = None
    out = None
    for layer in range(depth):
        sh1, sc1, g1, sh2, sc2, g2 = (mod[layer, :, i] for i in range(6))

        if h is None:
            h = _norm_mod(x2.reshape(batch, seq, d), norm_mix_g[layer], sh1, sc1)
        qkv0 = _mm_plain(h, w_in, (layer,), lambda j: N_GROUPS * j, 3, F32, *W_IN_TILE_F32)
        rest = _mm_plain(h, w_in, (layer,),
                         lambda j: j + 1 + jnp.where(j >= 2, 1, 0) + jnp.where(j >= 4, 1, 0),
                         11, BF16, *W_IN_TILE)
        outs, lses = _attention_groups(qkv0, rest, batch, seq)
        tables = _ssm_tables(ssm_a_re[layer], ssm_a_im[layer], ssm_log_dt[layer],
                             ssm_b_re[layer], ssm_b_im[layer], ssm_c_re[layer], ssm_c_im[layer])
        ssm = _s5_ssm(rest, 6, batch, seq, tables, ssm_d[layer], w_glu[layer], b_glu[layer])
        merged = _mm_branch(outs, lses, ssm, w_branch, layer, rest, 7, BRANCH_ROWS)
        h = None

        li = layer // 2
        if layer % 2 == 0:
            x2, hf = _mm_resid_norm(merged, w_out, layer, x2, g1, norm_ffn_g[layer], sh2, sc2,
                                    seq, W_OUT_NORM_ROWS)
            act = _mm_swiglu(hf, ffn_w1, ffn_w3, (li,), *SWIGLU_TILE)
            x2 = _mm_resid(act, ffn_w2, (li,), x2, g2, seq, *FFN_DOWN_TILE)
        else:
            x2 = _mm_resid(merged, w_out, (layer,), x2, g1, seq, *W_OUT_TILE)
            x3 = x2.reshape(batch, seq, d)
            sel, wts, cnt = _router(x3, norm_ffn_g[layer], sh2, sc2, router_w[li], router_b[li])
            plan = _moe_plan(sel, cnt, tokens)
            xs = _moe_dispatch(x3, norm_ffn_g[layer], sh2, sc2, plan)
            act = _moe_mm(xs, (moe_w1, moe_w3), li, plan, MOE_UP_TN, "up")
            y = _moe_mm(act, (moe_w2,), li, plan, MOE_DOWN_TN, "down")
            if layer + 1 < depth:
                x2, h = _moe_combine(x2, g2, wts, y, plan[0], seq, norm_mix_g[layer + 1],
                                     mod[layer + 1, :, 0], mod[layer + 1, :, 1])
            else:
                out = _moe_combine(x2, g2, wts, y, plan[0], seq, final_norm_g)
    if out is None:
        out = _final_norm(x2.reshape(batch, seq, d), final_norm_g)
    return out.reshape(batch, TOKEN_RESIDUES, per_res, d).transpose(0, 2, 1, 3).reshape(batch, seq, d)
```

```python
import functools
import math

import jax
import jax.numpy as jnp
from jax import lax
from jax.experimental import pallas as pl
from jax.experimental.pallas import tpu as pltpu

F32 = jnp.float32
BF16 = jnp.bfloat16

LANES = 128
SUBLANES = 8
V7X_VMEM_BYTES = 64 * 1024 * 1024
VMEM_LIMIT_BYTES = V7X_VMEM_BYTES * 7 // 8

ATTN_PATTERNS = ((128, 1), (512, 4), (2048, 16))
N_GROUPS = len(ATTN_PATTERNS)
HEADS = 8
HEAD_DIM = 128
ATTN_OUT = HEADS * HEAD_DIM
ATTN_BLOCK = 128
SSM_GROUP_CH = 16
RMS_EPS = 1e-6
NEG_BIG = -1e30

W_IN_TILE_F32 = (1024, ATTN_OUT)
W_IN_TILE = (2048, ATTN_OUT)
BRANCH_ROWS = 512
W_OUT_TILE = (1024, 1024)
W_OUT_NORM_ROWS = 512
SWIGLU_TILE = (2048, 512)
FFN_DOWN_TILE = (512, 512)


def _params(*semantics):
    return pltpu.CompilerParams(dimension_semantics=semantics,
                                vmem_limit_bytes=VMEM_LIMIT_BYTES)


def _sigmoid(v):
    return 1.0 / (1.0 + jnp.exp(-v))


ADA_KR = 256
ADA_CN = 1536


def _ada_kernel(ct_ref, w_ref, b_ref, o_ref, acc_ref):
    kstep = pl.program_id(1)
    kr = w_ref.shape[0]
    ct = ct_ref[pl.ds(pl.multiple_of(kstep * kr, kr), kr), :]
    act = (ct * _sigmoid(ct)).astype(BF16)

    @pl.when(kstep == 0)
    def _():
        acc_ref[...] = jnp.zeros_like(acc_ref)

    for c0 in range(0, w_ref.shape[1], ADA_CN):
        acc_ref[c0:c0 + ADA_CN, :] += lax.dot_general(
            w_ref[:, c0:c0 + ADA_CN].astype(BF16), act, (((0,), (0,)), ((), ())),
            preferred_element_type=F32)

    @pl.when(kstep == pl.num_programs(1) - 1)
    def _():
        for c0 in range(0, w_ref.shape[1], ADA_CN):
            rows = acc_ref[c0:c0 + ADA_CN, :].T[:o_ref.shape[0], :]
            o_ref[:, c0:c0 + ADA_CN] = rows + b_ref[:, c0:c0 + ADA_CN]


def _ada_modulation(c, ada_w, ada_b):
    depth, k, n = ada_w.shape
    nb = c.shape[0]
    ct = jnp.zeros((k, LANES), F32).at[:, :nb].set(c.T)
    out = pl.pallas_call(
        _ada_kernel,
        grid=(depth, k // ADA_KR),
        in_specs=[
            pl.BlockSpec((k, LANES), lambda l, i: (0, 0)),
            pl.BlockSpec((None, ADA_KR, n), lambda l, i: (l, i, 0)),
            pl.BlockSpec((None, 1, n), lambda l, i: (l, 0, 0)),
        ],
        out_specs=pl.BlockSpec((None, SUBLANES, n), lambda l, i: (l, 0, 0)),
        out_shape=jax.ShapeDtypeStruct((depth, SUBLANES, n), F32),
        scratch_shapes=[pltpu.VMEM((n, LANES), F32)],
        compiler_params=_params("arbitrary", "arbitrary"),
        name="ada_modulation",
    )(ct, ada_w, ada_b.reshape(depth, 1, n))
    return out[:, :nb, :]


NORM_TS = 512


def _rms(x, g):
    return x * lax.rsqrt(jnp.mean(x * x, axis=-1, keepdims=True) + RMS_EPS) * g


def _norm_mod_kernel(x_ref, g_ref, sh_ref, sc_ref, o_ref):
    y = _rms(x_ref[...], g_ref[...])
    o_ref[...] = (y * (1.0 + sc_ref[...]) + sh_ref[...]).astype(o_ref.dtype)


def _norm_mod(x, g, shift, scale):
    b, s, d = x.shape
    vec = pl.BlockSpec((None, 1, d), lambda i, j: (i, 0, 0))
    out = pl.pallas_call(
        _norm_mod_kernel,
        grid=(b, s // NORM_TS),
        in_specs=[
            pl.BlockSpec((None, NORM_TS, d), lambda i, j: (i, j, 0)),
            pl.BlockSpec((1, d), lambda i, j: (0, 0)),
            vec, vec,
        ],
        out_specs=pl.BlockSpec((None, NORM_TS, d), lambda i, j: (i, j, 0)),
        out_shape=jax.ShapeDtypeStruct((b, s, d), BF16),
        compiler_params=_params("arbitrary", "arbitrary"),
        name="norm_mod",
    )(x, g.reshape(1, d), shift.reshape(b, 1, d), scale.reshape(b, 1, d))
    return out.reshape(b * s, d)


def _final_norm_kernel(x_ref, g_ref, o_ref):
    o_ref[...] = _rms(x_ref[...], g_ref[...])


def _final_norm(x, g):
    b, s, d = x.shape
    return pl.pallas_call(
        _final_norm_kernel,
        grid=(b, s // NORM_TS),
        in_specs=[
            pl.BlockSpec((None, NORM_TS, d), lambda i, j: (i, j, 0)),
            pl.BlockSpec((1, d), lambda i, j: (0, 0)),
        ],
        out_specs=pl.BlockSpec((None, NORM_TS, d), lambda i, j: (i, j, 0)),
        out_shape=jax.ShapeDtypeStruct((b, s, d), F32),
        compiler_params=_params("arbitrary", "arbitrary"),
        name="final_norm",
    )(x, g.reshape(1, d))


CAST_ROWS = 256


def _cast_weight(w_ref, wb_ref):
    def body(i, carry):
        r = pl.multiple_of(i * CAST_ROWS, CAST_ROWS)
        wb_ref[pl.ds(r, CAST_ROWS), :] = w_ref[pl.ds(r, CAST_ROWS), :].astype(BF16)
        return carry
    lax.fori_loop(0, w_ref.shape[0] // CAST_ROWS, body, 0)


def _dot(a, b):
    return jnp.dot(a, b, preferred_element_type=F32)


def _mm_plain_kernel(a_ref, w_ref, o_ref, wb_ref):
    @pl.when(pl.program_id(1) == 0)
    def _():
        _cast_weight(w_ref, wb_ref)
    o_ref[...] = _dot(a_ref[...], wb_ref[...]).astype(o_ref.dtype)


def _mm_plain(a, w, w_idx, col_tile, n_tiles, out_dtype, tm, tn):
    m, k = a.shape
    lead = (None,) * len(w_idx)
    return pl.pallas_call(
        _mm_plain_kernel,
        grid=(n_tiles, m // tm),
        in_specs=[
            pl.BlockSpec((tm, k), lambda j, i: (i, 0)),
            pl.BlockSpec(lead + (k, tn), lambda j, i: w_idx + (0, col_tile(j))),
        ],
        out_specs=pl.BlockSpec((tm, tn), lambda j, i: (i, j)),
        out_shape=jax.ShapeDtypeStruct((m, n_tiles * tn), out_dtype),
        scratch_shapes=[pltpu.VMEM((k, tn), BF16)],
        compiler_params=_params("arbitrary", "arbitrary"),
        name="mm_plain",
    )(a, w)


def _mm_swiglu_kernel(a_ref, w1_ref, w3_ref, o_ref, wb1_ref, wb3_ref):
    @pl.when(pl.program_id(1) == 0)
    def _():
        _cast_weight(w1_ref, wb1_ref)
        _cast_weight(w3_ref, wb3_ref)
    a = a_ref[...]
    u = _dot(a, wb1_ref[...])
    v = _dot(a, wb3_ref[...])
    o_ref[...] = (u * _sigmoid(u) * v).astype(o_ref.dtype)


def _mm_swiglu(a, w1, w3, w_idx, tm, tn):
    m, k = a.shape
    n = w1.shape[-1]
    lead = (None,) * len(w_idx)
    wspec = pl.BlockSpec(lead + (k, tn), lambda j, i: w_idx + (0, j))
    return pl.pallas_call(
        _mm_swiglu_kernel,
        grid=(n // tn, m // tm),
        in_specs=[pl.BlockSpec((tm, k), lambda j, i: (i, 0)), wspec, wspec],
        out_specs=pl.BlockSpec((tm, tn), lambda j, i: (i, j)),
        out_shape=jax.ShapeDtypeStruct((m, n), BF16),
        scratch_shapes=[pltpu.VMEM((k, tn), BF16), pltpu.VMEM((k, tn), BF16)],
        compiler_params=_params("arbitrary", "arbitrary"),
        name="mm_swiglu",
    )(a, w1, w3)


def _mm_resid_kernel(a_ref, w_ref, x_ref, g_ref, o_ref, wb_ref):
    @pl.when(pl.program_id(1) == 0)
    def _():
        _cast_weight(w_ref, wb_ref)
    o_ref[...] = x_ref[...] + g_ref[...] * _dot(a_ref[...], wb_ref[...])


def _mm_resid(a, w, w_idx, x, gate, seq, tm, tn):
    m, k = a.shape
    n = w.shape[-1]
    nb = gate.shape[0]
    lead = (None,) * len(w_idx)
    return pl.pallas_call(
        _mm_resid_kernel,
        grid=(n // tn, m // tm),
        in_specs=[
            pl.BlockSpec((tm, k), lambda j, i: (i, 0)),
            pl.BlockSpec(lead + (k, tn), lambda j, i: w_idx + (0, j)),
            pl.BlockSpec((tm, tn), lambda j, i: (i, j)),
            pl.BlockSpec((None, 1, tn), lambda j, i: (i * tm // seq, 0, j)),
        ],
        out_specs=pl.BlockSpec((tm, tn), lambda j, i: (i, j)),
        out_shape=jax.ShapeDtypeStruct((m, n), F32),
        scratch_shapes=[pltpu.VMEM((k, tn), BF16)],
        compiler_params=_params("arbitrary", "arbitrary"),
        name="mm_resid",
    )(a, w, x, gate.reshape(nb, 1, n))


def _mm_resid_norm_kernel(a_ref, w_ref, x_ref, g_ref, ng_ref, sh_ref, sc_ref,
                          xo_ref, ho_ref, wb_ref):
    @pl.when(pl.program_id(0) == 0)
    def _():
        _cast_weight(w_ref, wb_ref)
    xn = x_ref[...] + g_ref[...] * _dot(a_ref[...], wb_ref[...])
    xo_ref[...] = xn
    yn = _rms(xn, ng_ref[...])
    ho_ref[...] = (yn * (1.0 + sc_ref[...]) + sh_ref[...]).astype(ho_ref.dtype)


def _mm_resid_norm(a, w, layer, x, gate, norm_g, shift, scale, seq, tm):
    m, k = a.shape
    n = w.shape[-1]
    nb = gate.shape[0]
    vec = pl.BlockSpec((None, 1, n), lambda i: (i * tm // seq, 0, 0))
    row = pl.BlockSpec((tm, n), lambda i: (i, 0))
    return pl.pallas_call(
        _mm_resid_norm_kernel,
        grid=(m // tm,),
        in_specs=[
            pl.BlockSpec((tm, k), lambda i: (i, 0)),
            pl.BlockSpec((None, k, n), lambda i: (layer, 0, 0), pipeline_mode=pl.Buffered(1)),
            row, vec,
            pl.BlockSpec((1, n), lambda i: (0, 0)),
            vec, vec,
        ],
        out_specs=[row, row],
        out_shape=[jax.ShapeDtypeStruct((m, n), F32), jax.ShapeDtypeStruct((m, n), BF16)],
        scratch_shapes=[pltpu.VMEM((k, n), BF16)],
        compiler_params=_params("arbitrary"),
        name="mm_resid_norm",
    )(a, w, x, gate.reshape(nb, 1, n), norm_g.reshape(1, n),
      shift.reshape(nb, 1, n), scale.reshape(nb, 1, n))


def _mix_groups(o_refs, l_refs):
    a0, a1, a2 = (r[...] for r in l_refs)
    m = jnp.maximum(jnp.maximum(a0, a1), a2)
    e0 = jnp.exp(a0 - m)
    e1 = jnp.exp(a1 - m)
    e2 = jnp.exp(a2 - m)
    inv = 1.0 / (e0 + e1 + e2)
    weights = (e0 * inv, e1 * inv, e2 * inv)
    cols = []
    for h in range(HEADS):
        sl = slice(h * HEAD_DIM, (h + 1) * HEAD_DIM)
        cols.append(sum(w[:, h:h + 1] * o[:, sl].astype(F32) for w, o in zip(weights, o_refs)))
    return jnp.concatenate(cols, axis=1)


def _mm_branch_kernel(o0_ref, o1_ref, o2_ref, l0_ref, l1_ref, l2_ref, ss_ref, wa_ref, ws_ref,
                      ga0_ref, ga1_ref, gs0_ref, gs1_ref, o_ref, wba_ref, wbs_ref):
    @pl.when(pl.program_id(0) == 0)
    def _():
        _cast_weight(wa_ref, wba_ref)
        _cast_weight(ws_ref, wbs_ref)
    attn = _mix_groups((o0_ref, o1_ref, o2_ref), (l0_ref, l1_ref, l2_ref)).astype(BF16)
    ssm = ss_ref[...]
    tn = ga0_ref.shape[1]
    for half, (ga_ref, gs_ref) in enumerate(((ga0_ref, gs0_ref), (ga1_ref, gs1_ref))):
        sl = slice(half * tn, (half + 1) * tn)
        pa = _dot(attn, wba_ref[:, sl])
        ps = _dot(ssm, wbs_ref[:, sl])
        ga = _sigmoid(ga_ref[...].astype(F32))
        gs = _sigmoid(gs_ref[...].astype(F32))
        o_ref[:, sl] = (ga * pa + gs * ps).astype(o_ref.dtype)


def _mm_branch(outs, lses, ssm, w_branch, layer, proj, gate_tile, tm):
    m, k = ssm.shape
    n = w_branch.shape[-1]
    assert n == 2 * ATTN_OUT
    row = lambda width: pl.BlockSpec((tm, width), lambda i: (i, 0))
    gate = lambda t: pl.BlockSpec((tm, ATTN_OUT), lambda i: (i, gate_tile + t))
    weight = lambda half: pl.BlockSpec((None, k, n), lambda i: (layer, half, 0),
                                       pipeline_mode=pl.Buffered(1))
    return pl.pallas_call(
        _mm_branch_kernel,
        grid=(m // tm,),
        in_specs=[row(ATTN_OUT)] * 3 + [row(LANES)] * 3 + [row(k), weight(0), weight(1),
                                                          gate(0), gate(1), gate(2), gate(3)],
        out_specs=row(n),
        out_shape=jax.ShapeDtypeStruct((m, n), BF16),
        scratch_shapes=[pltpu.VMEM((k, n), BF16), pltpu.VMEM((k, n), BF16)],
        compiler_params=_params("arbitrary"),
        name="mm_branch",
    )(*outs, *lses, ssm, w_branch, w_branch, proj, proj, proj, proj)


TOKEN_RESIDUES = 16
ATTN_BB = 4
ATTN_RES_NO_PREV = 2


def _attn_kernel(*refs, with_prev):
    if with_prev:
        q_ref, kc_ref, vc_ref, bc_ref, bp_ref, o_ref, lse_ref, kp_ref, vp_ref = refs

        @pl.when(pl.program_id(2) == 0)
        def _():
            kp_ref[...] = jnp.zeros_like(kp_ref)
            vp_ref[...] = jnp.zeros_like(vp_ref)
    else:
        q_ref, kc_ref, vc_ref, bc_ref, o_ref, lse_ref = refs
    blk = ATTN_BLOCK
    bb, nc, res, rpc = q_ref.shape[:4]
    assert res == 1 or not with_prev
    scale = HEAD_DIM ** -0.5
    lane = lax.broadcasted_iota(jnp.int32, (blk, LANES), 1)

    def heads(ref, bi, ri):
        x = ref[bi, :, ri].reshape(blk, ATTN_OUT)
        return jnp.stack([x[:, h * HEAD_DIM:(h + 1) * HEAD_DIM] for h in range(HEADS)]
                         ).astype(BF16)

    def qk(q, k):
        return lax.dot_general(q, k, (((2,), (2,)), ((0,), (0,))), preferred_element_type=F32)

    def pv(p, v):
        return lax.dot_general(p.astype(BF16), v, (((2,), (1,)), ((0,), (0,))),
                               preferred_element_type=F32)

    for bi, ri in [(b, r) for b in range(bb) for r in range(res)]:
        q = heads(q_ref, bi, ri)
        k_c = heads(kc_ref, bi, ri)
        v_c = heads(vc_ref, bi, ri)
        s_c = qk(q, k_c) * scale + bc_ref[...]
        m = jnp.max(s_c, axis=2, keepdims=True)
        if with_prev:
            bias_p = jnp.where(pl.program_id(2) > 0, bp_ref[...], NEG_BIG)
            s_p = qk(q, kp_ref[bi]) * scale + bias_p
            m = jnp.maximum(m, jnp.max(s_p, axis=2, keepdims=True))
        p_c = jnp.exp(s_c - m)
        l = jnp.sum(p_c, axis=2, keepdims=True)
        o = pv(p_c, v_c)
        if with_prev:
            p_p = jnp.exp(s_p - m)
            l = l + jnp.sum(p_p, axis=2, keepdims=True)
            o = o + pv(p_p, vp_ref[bi])
            kp_ref[bi] = k_c
            vp_ref[bi] = v_c
        o = o / l
        lse = m + jnp.log(l)
        lse_tile = jnp.zeros((blk, LANES), F32)
        for h in range(HEADS):
            sl = slice(h * HEAD_DIM, (h + 1) * HEAD_DIM)
            o_ref[bi, :, ri, :, sl] = o[h].reshape(nc, rpc, HEAD_DIM).astype(o_ref.dtype)
            lse_tile = jnp.where(lane == h, lse[h], lse_tile)
        lse_ref[bi, :, ri] = lse_tile.reshape(nc, rpc, LANES)


def _attn_bias(dilation, nc, back):
    rpc = ATTN_BLOCK // nc
    i = jnp.arange(ATTN_BLOCK, dtype=jnp.int32)
    off = nc * (i % rpc) + i // rpc
    dist = off[:, None] - off[None, :] + back * ATTN_BLOCK
    valid = jnp.logical_and(dist >= 0, dist <= ATTN_BLOCK)
    slopes = 2.0 ** (-8.0 * (jnp.arange(HEADS, dtype=F32) + 1.0) / HEADS)
    pen = slopes[:, None, None] * (dist * dilation).astype(F32)[None]
    return jnp.where(valid[None], -pen, NEG_BIG)


def _attention_group(qkv, qkv_cols, batch, seq, gi):
    window, dilation = ATTN_PATTERNS[gi]
    assert window // dilation == ATTN_BLOCK
    cols = qkv.shape[-1]
    per_res = seq // TOKEN_RESIDUES
    nc = TOKEN_RESIDUES // dilation
    rpc = ATTN_BLOCK // nc
    nblk = seq // dilation // ATTN_BLOCK
    with_prev = nblk > 1
    res = 1 if with_prev else ATTN_RES_NO_PREV

    def shape5(c):
        return (batch, nc, dilation, per_res, c)

    def spec(c, col):
        return pl.BlockSpec((ATTN_BB, nc, res, rpc, c), lambda b, r, n: (b, 0, r, n, col))

    bias_spec = pl.BlockSpec((HEADS, ATTN_BLOCK, ATTN_BLOCK), lambda b, r, n: (0, 0, 0))
    in_specs = [spec(ATTN_OUT, col) for col in qkv_cols]
    carry = pltpu.VMEM((ATTN_BB, HEADS, ATTN_BLOCK, HEAD_DIM), BF16)
    view = qkv.reshape(shape5(cols))
    args = [view] * len(in_specs) + [_attn_bias(dilation, nc, 0)]
    in_specs.append(bias_spec)
    if with_prev:
        args.append(_attn_bias(dilation, nc, 1))
        in_specs.append(bias_spec)
    o, lse = pl.pallas_call(
        functools.partial(_attn_kernel, with_prev=with_prev),
        grid=(batch // ATTN_BB, dilation // res, nblk),
        in_specs=in_specs,
        out_specs=[spec(ATTN_OUT, 0), spec(LANES, 0)],
        out_shape=[
            jax.ShapeDtypeStruct(shape5(ATTN_OUT), qkv.dtype),
            jax.ShapeDtypeStruct(shape5(LANES), F32),
        ],
        scratch_shapes=[carry, carry] if with_prev else [],
        compiler_params=_params("arbitrary", "arbitrary", "arbitrary"),
        name=f"attn_g{gi}",
    )(*args)
    return o.reshape(batch * seq, ATTN_OUT), lse.reshape(batch * seq, LANES)


def _attention_groups(qkv0, rest, batch, seq):
    outs, lses = [], []
    for gi in range(N_GROUPS):
        if gi == 0:
            o, lse = _attention_group(qkv0, (0, 1, 2), batch, seq, gi)
        else:
            o, lse = _attention_group(rest, (gi - 1, gi + 1, gi + 3), batch, seq, gi)
        outs.append(o)
        lses.append(lse)
    return outs, lses


SSM_LT = 128
SSM_PASSES = 2
SSM_JBLK = 256


def _gelu_tanh(y):
    return 0.5 * y * (1.0 + jnp.tanh(math.sqrt(2.0 / math.pi) * (y + 0.044715 * (y * y * y))))


def _ssm_kernel(u_ref, pin_ref, pout_ref, bblk_ref, cblk_ref, are_ref, aim_ref, d_ref,
                wglu_ref, bglu_ref, o_ref, sre_ref, sim_ref, hre_ref, him_ref, *, nbatch):
    width = u_ref.shape[-1]
    rows = pin_ref.shape[1]
    njb = width // SSM_JBLK
    jstates = bblk_ref.shape[2] // 2
    pairs = rows // SUBLANES
    per_tile = SUBLANES // nbatch

    @pl.when(pl.program_id(0) == 0)
    def _():
        hre_ref[...] = jnp.zeros_like(hre_ref)
        him_ref[...] = jnp.zeros_like(him_ref)

    u_rm = u_ref[...].reshape(pout_ref.shape[1], width)
    out_rm = None
    for p in range(pin_ref.shape[0]):
        u_f32 = _dot(pin_ref[p], u_rm)
        u_tb = u_f32.astype(BF16)
        ys = []
        for j in range(njb):
            bu = _dot(u_tb[:, j * SSM_JBLK:(j + 1) * SSM_JBLK], bblk_ref[j])
            sre_ref[...] = bu[:, :jstates].reshape(pairs, SUBLANES, jstates)
            sim_ref[...] = bu[:, jstates:].reshape(pairs, SUBLANES, jstates)
            cs = slice(j * jstates, (j + 1) * jstates)
            ar = are_ref[0:nbatch, cs]
            ai = aim_ref[0:nbatch, cs]

            def step(k, carry):
                hr, hi = carry
                for t in range(per_tile):
                    rs = slice(t * nbatch, (t + 1) * nbatch)
                    nr = ar * hr - ai * hi + sre_ref[k, rs, :]
                    ni = ar * hi + ai * hr + sim_ref[k, rs, :]
                    sre_ref[k, rs, :] = nr
                    sim_ref[k, rs, :] = ni
                    hr, hi = nr, ni
                return hr, hi

            hr, hi = lax.fori_loop(0, pairs, step,
                                   (hre_ref[0:nbatch, cs], him_ref[0:nbatch, cs]), unroll=4)
            hre_ref[0:nbatch, cs] = hr
            him_ref[0:nbatch, cs] = hi

            h_re = sre_ref[...].reshape(rows, jstates).astype(BF16)
            h_im = sim_ref[...].reshape(rows, jstates).astype(BF16)
            ys.append(_dot(h_re, cblk_ref[j, 0:jstates, :]) + _dot(h_im, cblk_ref[j, jstates:, :]))

        y = jnp.concatenate(ys, axis=1) + d_ref[...] * u_f32
        z = _gelu_tanh(y).astype(BF16)
        g = _dot(z, wglu_ref[...]) + bglu_ref[...]
        o_tb = (g[:, :width] * _sigmoid(g[:, width:])).astype(BF16)
        back = _dot(pout_ref[p], o_tb)
        out_rm = back if out_rm is None else out_rm + back
    o_ref[...] = out_rm.astype(o_ref.dtype).reshape(o_ref.shape)


def _ssm_tables(a_re, a_im, log_dt, b_re, b_im, c_re, c_im):
    groups, nst = a_re.shape
    gpb = SSM_JBLK // SSM_GROUP_CH
    njb = groups // gpb
    lam = lax.complex(a_re.astype(F32), a_im.astype(F32))
    dt = jnp.exp(log_dt.astype(F32))[:, None]
    a_bar = jnp.exp(lam * dt)
    b_mat = lax.complex(b_re.astype(F32), b_im.astype(F32))
    b_bar = ((a_bar - 1.0) / lam)[:, :, None] * b_mat
    same_group = jnp.eye(gpb, dtype=jnp.bool_)

    def block_diag(t):
        r, q = t.shape[1:]
        t = t.astype(BF16).reshape(njb, gpb, r, 1, q)
        wide = jnp.where(same_group[None, :, None, :, None], t, jnp.zeros((), BF16))
        return wide.reshape(njb, gpb * r, gpb * q)

    def in_blocks(t):
        return block_diag(t.transpose(0, 2, 1))

    def out_blocks(t):
        return block_diag(t.transpose(0, 2, 1))

    bblk = jnp.concatenate([in_blocks(jnp.real(b_bar)), in_blocks(jnp.imag(b_bar))], axis=2)
    cblk = jnp.concatenate([out_blocks(c_re.astype(F32)), out_blocks(-c_im.astype(F32))], axis=1)
    are = jnp.broadcast_to(jnp.real(a_bar).reshape(1, groups * nst), (SUBLANES, groups * nst))
    aim = jnp.broadcast_to(jnp.imag(a_bar).reshape(1, groups * nst), (SUBLANES, groups * nst))
    return bblk.astype(BF16), cblk.astype(BF16), are, aim


def _ssm_row_perms(nbatch):
    per_res = SSM_PASSES * SSM_LT // TOKEN_RESIDUES
    col = jnp.arange(nbatch * TOKEN_RESIDUES * per_res, dtype=jnp.int32)
    b = col // (TOKEN_RESIDUES * per_res)
    t_local = TOKEN_RESIDUES * (col % per_res) + (col // per_res) % TOKEN_RESIDUES
    row = (t_local % SSM_LT) * nbatch + b
    pin = jnp.logical_and(
        (t_local // SSM_LT)[None, None, :] == jnp.arange(SSM_PASSES, dtype=jnp.int32)[:, None, None],
        row[None, None, :] == jnp.arange(SSM_LT * nbatch, dtype=jnp.int32)[None, :, None])
    pin = pin.astype(BF16)
    return pin, pin.transpose(0, 2, 1)


def _s5_ssm(src, u_col, nbatch, seq, tables, d_skip, w_glu, b_glu):
    bblk, cblk, are, aim = tables
    width = w_glu.shape[0]
    nstate = are.shape[1]
    jstates = bblk.shape[2] // 2
    rows = SSM_LT * nbatch
    step_rows = SSM_PASSES * SSM_LT // TOKEN_RESIDUES
    per_res = seq // TOKEN_RESIDUES
    pin, pout = _ssm_row_perms(nbatch)
    d2 = d_skip.reshape(1, width).astype(F32)
    wg = w_glu.astype(BF16)
    bg = b_glu.reshape(1, 2 * width).astype(F32)

    def const(a):
        return pl.BlockSpec(a.shape, lambda i: (0,) * a.ndim, pipeline_mode=pl.Buffered(1))

    blk = (nbatch, TOKEN_RESIDUES, step_rows, width)
    out = pl.pallas_call(
        functools.partial(_ssm_kernel, nbatch=nbatch),
        grid=(per_res // step_rows,),
        in_specs=[pl.BlockSpec(blk, lambda i: (0, 0, i, u_col)),
                  const(pin), const(pout), const(bblk), const(cblk), const(are), const(aim),
                  const(d2), const(wg), const(bg)],
        out_specs=pl.BlockSpec(blk, lambda i: (0, 0, i, 0)),
        out_shape=jax.ShapeDtypeStruct((nbatch, TOKEN_RESIDUES, per_res, width), BF16),
        scratch_shapes=[
            pltpu.VMEM((rows // SUBLANES, SUBLANES, jstates), F32),
            pltpu.VMEM((rows // SUBLANES, SUBLANES, jstates), F32),
            pltpu.VMEM((SUBLANES, nstate), F32),
            pltpu.VMEM((SUBLANES, nstate), F32),
        ],
        compiler_params=_params("arbitrary"),
        name="s5_ssm",
    )(src.reshape(nbatch, TOKEN_RESIDUES, per_res, src.shape[-1]),
      pin, pout, bblk, cblk, are, aim, d2, wg, bg)
    return out.reshape(nbatch * seq, width)


ROUTER_TS = 256


def _router_kernel(x_ref, g_ref, sh_ref, sc_ref, rw_ref, rb_ref,
                   sel_ref, wts_ref, cnt_ref, carry_ref):
    @pl.when(jnp.logical_and(pl.program_id(0) == 0, pl.program_id(1) == 0))
    def _():
        carry_ref[...] = jnp.zeros_like(carry_ref)

    h = _rms(x_ref[...], g_ref[...]) * (1.0 + sc_ref[...]) + sh_ref[...]
    ts, ne = h.shape[0], rw_ref.shape[0]
    idx = lax.broadcasted_iota(jnp.int32, (ts, ne), 1)
    logits = jnp.broadcast_to(rb_ref[...], (ts, ne))
    for ex in range(ne):
        col = jnp.sum(h * rw_ref[ex:ex + 1, :], axis=1, keepdims=True)
        logits = logits + jnp.where(idx == ex, col, 0.0)
    m1 = jnp.max(logits, axis=1, keepdims=True)
    i1 = jnp.min(jnp.where(logits == m1, idx, ne), axis=1, keepdims=True)
    rest = jnp.where(idx == i1, -jnp.inf, logits)
    m2 = jnp.max(rest, axis=1, keepdims=True)
    i2 = jnp.min(jnp.where(rest == m2, idx, ne), axis=1, keepdims=True)
    e = jnp.exp(m2 - m1)
    w1 = 1.0 / (1.0 + e)
    w2 = e / (1.0 + e)

    onehot = jnp.where(idx == i1, 1.0, 0.0) + jnp.where(idx == i2, 1.0, 0.0)
    row = lax.broadcasted_iota(jnp.int32, (ts, ts), 0)
    col = lax.broadcasted_iota(jnp.int32, (ts, ts), 1)
    lower = jnp.where(col < row, 1.0, 0.0).astype(BF16)
    before = carry_ref[...] + _dot(lower, onehot.astype(BF16))
    r1 = jnp.sum(jnp.where(idx == i1, before, 0.0), axis=1, keepdims=True).astype(jnp.int32)
    r2 = jnp.sum(jnp.where(idx == i2, before, 0.0), axis=1, keepdims=True).astype(jnp.int32)
    total = carry_ref[...] + jnp.sum(onehot, axis=0, keepdims=True)
    carry_ref[...] = total
    cnt_ref[...] = total
    sel_ref[...] = jnp.where(idx == 0, i1, jnp.where(idx == 1, i2, jnp.where(
        idx == 2, r1, jnp.where(idx == 3, r2, 0))))
    wts_ref[...] = jnp.where(idx == 0, w1, jnp.where(idx == 1, w2, 0.0))


def _router(x, g, shift, scale, router_w, router_b):
    b, s, d = x.shape
    ne = router_w.shape[-1]
    vec = pl.BlockSpec((None, 1, d), lambda i, j: (i, 0, 0))
    tok = pl.BlockSpec((None, ROUTER_TS, ne), lambda i, j: (i, j, 0))
    sel, wts, cnt = pl.pallas_call(
        _router_kernel,
        grid=(b, s // ROUTER_TS),
        in_specs=[
            pl.BlockSpec((None, ROUTER_TS, d), lambda i, j: (i, j, 0)),
            pl.BlockSpec((1, d), lambda i, j: (0, 0)),
            vec, vec,
            pl.BlockSpec((ne, d), lambda i, j: (0, 0)),
            pl.BlockSpec((1, ne), lambda i, j: (0, 0)),
        ],
        out_specs=[tok, tok, pl.BlockSpec((1, ne), lambda i, j: (0, 0))],
        out_shape=[jax.ShapeDtypeStruct((b, s, ne), jnp.int32),
                   jax.ShapeDtypeStruct((b, s, ne), F32),
                   jax.ShapeDtypeStruct((1, ne), F32)],
        scratch_shapes=[pltpu.VMEM((1, ne), F32)],
        compiler_params=_params("arbitrary", "arbitrary"),
        name="router",
    )(x, g.reshape(1, d), shift.reshape(b, 1, d), scale.reshape(b, 1, d),
      router_w.T, router_b.reshape(1, ne))
    return sel.reshape(b * s, ne), wts.reshape(b * s, ne), cnt


MOE_TM = 256
MOE_TT = 256
MOE_UP_TN = 1408
MOE_DOWN_TN = 2048


def _moe_plan(sel, cnt, tokens):
    ne = cnt.shape[-1]
    counts = cnt[0].astype(jnp.int32)
    padded = (counts + MOE_TM - 1) // MOE_TM * MOE_TM
    ends = jnp.cumsum(padded)
    starts = ends - padded
    pos1 = starts[sel[:, 0]] + sel[:, 2]
    pos2 = starts[sel[:, 1]] + sel[:, 3]
    nt = tokens // MOE_TT
    pos = jnp.concatenate([pos1.reshape(nt, MOE_TT), pos2.reshape(nt, MOE_TT)], axis=1)
    max_tiles = 2 * tokens // MOE_TM + ne
    tile_start = jnp.arange(max_tiles, dtype=jnp.int32) * MOE_TM
    tile_expert = jnp.minimum(
        jnp.sum((tile_start[:, None] >= ends[None, :]).astype(jnp.int32), axis=1), ne - 1)
    num_tiles = (ends[-1] // MOE_TM).reshape(1)
    zero_tiles = jnp.concatenate([
        jnp.maximum(ends // MOE_TM - 1, 0),
        jnp.minimum(num_tiles[0] + jnp.arange(ne, dtype=jnp.int32), max_tiles - 1)])
    nonempty = counts > 0
    ar = jnp.arange(ne, dtype=jnp.int32)
    order = (ar[:, None] + 1 + ar[None, :]) % ne
    nxt = order[ar, jnp.argmax(nonempty[order], axis=1)].astype(jnp.int32)
    gidx = jnp.cumsum(nonempty.astype(jnp.int32)) - 1
    groups = jnp.concatenate([nxt, gidx, jnp.sum(nonempty.astype(jnp.int32)).reshape(1)])
    return (pos.reshape(nt, 1, 2 * MOE_TT), tile_expert, num_tiles, max_tiles, zero_tiles,
            groups)


def _row_copies(pos_ref, t, make):
    tt = pos_ref.shape[1] // 2
    return make(0, t, pos_ref[0, t]), make(1, t, pos_ref[0, tt + t])


def _issue_and_drain(pos_ref, make):
    tt = pos_ref.shape[1] // 2

    def issue(t, carry):
        for k, cp in enumerate(_row_copies(pos_ref, t, make)):
            cp.start(priority=k)
        return carry

    def drain(t, carry):
        for cp in _row_copies(pos_ref, t, make):
            cp.wait()
        return carry

    lax.fori_loop(0, tt, issue, 0, unroll=4)
    lax.fori_loop(0, tt, drain, 0, unroll=4)


def _pack_pairs(lo, hi):
    lo_bits = pltpu.bitcast(lo.astype(BF16).astype(F32), jnp.uint32)
    hi_bits = pltpu.bitcast(hi.astype(BF16).astype(F32), jnp.uint32)
    return jnp.bitwise_or(jnp.bitwise_and(hi_bits, jnp.uint32(0xFFFF0000)),
                          jnp.right_shift(lo_bits, jnp.uint32(16)))


def _unpack_pairs(words):
    lo = pltpu.bitcast(jnp.left_shift(words, jnp.uint32(16)), F32)
    hi = pltpu.bitcast(jnp.bitwise_and(words, jnp.uint32(0xFFFF0000)), F32)
    return lo, hi


def _moe_dispatch_kernel(pos_ref, zt_ref, x_ref, g_ref, sh_ref, sc_ref, xs_ref, hbuf_ref, sem):
    @pl.when(jnp.logical_and(pl.program_id(0) == 0, pl.program_id(1) == 0))
    def _():
        hbuf_ref[...] = jnp.zeros_like(hbuf_ref)
        for k in range(zt_ref.shape[0]):
            cp = pltpu.make_async_copy(
                hbuf_ref, xs_ref.at[pl.ds(pl.multiple_of(zt_ref[k] * MOE_TM, MOE_TM), MOE_TM), :],
                sem)
            cp.start()
            cp.wait()

    h = _rms(x_ref[...], g_ref[...]) * (1.0 + sc_ref[...]) + sh_ref[...]
    half = h.shape[1] // 2
    hbuf_ref[...] = _pack_pairs(h[:, :half], h[:, half:])

    def make(k, t, p):
        return pltpu.make_async_copy(hbuf_ref.at[pl.ds(t, 1), :], xs_ref.at[pl.ds(p, 1), :], sem)

    _issue_and_drain(pos_ref, make)


def _moe_dispatch(x, g, shift, scale, plan):
    pos, _, _, max_tiles, zero_tiles, _ = plan
    b, s, d = x.shape
    per_b = s // MOE_TT
    assert MOE_TT == MOE_TM
    vec = pl.BlockSpec((None, 1, d), lambda i, j: (i, 0, 0))
    return pl.pallas_call(
        _moe_dispatch_kernel,
        grid=(b, per_b),
        in_specs=[
            pl.BlockSpec((None, 1, 2 * MOE_TT), lambda i, j: (i * per_b + j, 0, 0),
                         memory_space=pltpu.SMEM),
            pl.BlockSpec(memory_space=pltpu.SMEM),
            pl.BlockSpec((None, MOE_TT, d), lambda i, j: (i, j, 0)),
            pl.BlockSpec((1, d), lambda i, j: (0, 0)),
            vec, vec,
        ],
        out_specs=pl.BlockSpec(memory_space=pl.ANY),
        out_shape=jax.ShapeDtypeStruct((max_tiles * MOE_TM, d // 2), jnp.uint32),
        scratch_shapes=[pltpu.VMEM((MOE_TT, d // 2), jnp.uint32), pltpu.SemaphoreType.DMA(())],
        compiler_params=_params("arbitrary", "arbitrary"),
        name="moe_dispatch",
    )(pos, zero_tiles, x, g.reshape(1, d), shift.reshape(b, 1, d), scale.reshape(b, 1, d))


def _moe_mm_kernel(te_ref, nt_ref, grp_ref, a_ref, *rest, mode, li, tn):
    nw = 2 if mode == "up" else 1
    w_hbm = rest[:nw]
    o_ref = rest[nw]
    wb = rest[nw + 1:2 * nw + 1]
    stage = rest[2 * nw + 1:3 * nw + 1]
    wsem = rest[3 * nw + 1]
    j = pl.program_id(0)
    i = pl.program_id(1)
    ne = (grp_ref.shape[0] - 1) // 2
    expert = te_ref[i]
    active = i < nt_ref[0]
    new_expert = jnp.logical_or(i == 0, expert != te_ref[jnp.maximum(i - 1, 0)])

    def weight_copies(ex, jx):
        cols = pl.ds(pl.multiple_of(jx * tn, LANES), tn)
        return [pltpu.make_async_copy(w_hbm[k].at[li, ex, :, cols], stage[k], wsem.at[k])
                for k in range(nw)]

    @pl.when(jnp.logical_and(active, new_expert))
    def _():
        @pl.when(jnp.logical_and(j == 0, grp_ref[ne + expert] == 0))
        def _():
            for cp in weight_copies(expert, j):
                cp.start()

        for cp in weight_copies(expert, j):
            cp.wait()
        for k in range(nw):
            _cast_weight(stage[k], wb[k])
        nxt = grp_ref[expert]
        nxt_j = j + jnp.where(nxt <= expert, 1, 0)

        @pl.when(nxt_j < pl.num_programs(0))
        def _():
            for cp in weight_copies(nxt, nxt_j):
                cp.start()

    @pl.when(active)
    def _():
        if mode == "down":
            p = lax.dot_general(wb[0][...], a_ref[...], (((0,), (0,)), ((), ())),
                                preferred_element_type=F32).T
            half = p.shape[1] // 2
            o_ref[...] = _pack_pairs(p[:, :half], p[:, half:])
        else:
            lo, hi = _unpack_pairs(a_ref[...])
            lo, hi = lo.astype(BF16), hi.astype(BF16)
            half = lo.shape[1]
            nt_dims = (((0,), (1,)), ((), ()))

            def proj(w_ref):
                return (lax.dot_general(w_ref[0:half, :], lo, nt_dims, preferred_element_type=F32)
                        + lax.dot_general(w_ref[half:, :], hi, nt_dims,
                                          preferred_element_type=F32))

            u = proj(wb[0])
            o_ref[...] = (u * _sigmoid(u) * proj(wb[1])).astype(o_ref.dtype)

    @pl.when(jnp.logical_not(active))
    def _():
        o_ref[...] = jnp.zeros_like(o_ref)


def _moe_mm(a, ws, li, plan, tn, mode):
    _, tile_expert, num_tiles, max_tiles, _, groups = plan
    k, n = ws[0].shape[-2], ws[0].shape[-1]
    rows = max_tiles * MOE_TM
    nw = len(ws)

    def tile(i, nt):
        return jnp.minimum(i, nt[0] - 1)

    if mode == "down":
        a_spec = pl.BlockSpec((k, MOE_TM), lambda j, i, te, nt, gr: (0, tile(i, nt)))
        out_spec = pl.BlockSpec((MOE_TM, tn // 2), lambda j, i, te, nt, gr: (i, j))
        out_shape = jax.ShapeDtypeStruct((rows, n // 2), jnp.uint32)
    else:
        a_spec = pl.BlockSpec((MOE_TM, k // 2), lambda j, i, te, nt, gr: (tile(i, nt), 0))
        out_spec = pl.BlockSpec((tn, MOE_TM), lambda j, i, te, nt, gr: (j, i))
        out_shape = jax.ShapeDtypeStruct((n, rows), BF16)
    return pl.pallas_call(
        functools.partial(_moe_mm_kernel, mode=mode, li=li, tn=tn),
        grid_spec=pltpu.PrefetchScalarGridSpec(
            num_scalar_prefetch=3,
            grid=(n // tn, max_tiles),
            in_specs=[a_spec] + [pl.BlockSpec(memory_space=pl.ANY)] * nw,
            out_specs=out_spec,
            scratch_shapes=([pltpu.VMEM((k, tn), BF16)] * nw + [pltpu.VMEM((k, tn), F32)] * nw
                            + [pltpu.SemaphoreType.DMA((nw,))]),
        ),
        out_shape=out_shape,
        compiler_params=_params("arbitrary", "arbitrary"),
        name=f"moe_mm_{mode}",
    )(tile_expert, num_tiles, groups, a, *ws)


def _moe_combine_kernel(pos_ref, x_ref, g_ref, wts_ref, y_ref, *rest, final):
    if final:
        ng_ref, o_ref, ybuf_ref, sem = rest
    else:
        ng_ref, sh_ref, sc_ref, o_ref, h_ref, ybuf_ref, sem = rest

    def make(k, t, p):
        return pltpu.make_async_copy(y_ref.at[pl.ds(p, 1), :], ybuf_ref.at[k, pl.ds(t, 1), :], sem)

    _issue_and_drain(pos_ref, make)

    def rows(k):
        lo, hi = _unpack_pairs(ybuf_ref[k])
        hw = MOE_DOWN_TN // 2
        parts = []
        for c in range(lo.shape[1] // hw):
            parts += [lo[:, c * hw:(c + 1) * hw], hi[:, c * hw:(c + 1) * hw]]
        return jnp.concatenate(parts, axis=1)

    f = wts_ref[:, 0:1] * rows(0) + wts_ref[:, 1:2] * rows(1)
    xn = x_ref[...] + g_ref[...] * f
    yn = _rms(xn, ng_ref[...])
    if final:
        o_ref[...] = yn
    else:
        o_ref[...] = xn
        h_ref[...] = (yn * (1.0 + sc_ref[...]) + sh_ref[...]).astype(h_ref.dtype)


def _moe_combine(x, gate, wts, y, pos, seq, norm_g, shift=None, scale=None):
    m, d = x.shape
    nb = gate.shape[0]
    ne = wts.shape[-1]
    final = shift is None
    vec = pl.BlockSpec((None, 1, d), lambda i: (i * MOE_TT // seq, 0, 0))
    row = pl.BlockSpec((MOE_TT, d), lambda i: (i, 0))
    in_specs = [
        pl.BlockSpec((None, 1, 2 * MOE_TT), lambda i: (i, 0, 0), memory_space=pltpu.SMEM),
        row, vec,
        pl.BlockSpec((MOE_TT, ne), lambda i: (i, 0)),
        pl.BlockSpec(memory_space=pl.ANY),
        pl.BlockSpec((1, d), lambda i: (0, 0)),
    ]
    args = [pos, x, gate.reshape(nb, 1, d), wts, y, norm_g.reshape(1, d)]
    if final:
        out_specs, out_shape = row, jax.ShapeDtypeStruct((m, d), F32)
    else:
        in_specs += [vec, vec]
        args += [shift.reshape(nb, 1, d), scale.reshape(nb, 1, d)]
        out_specs = [row, row]
        out_shape = [jax.ShapeDtypeStruct((m, d), F32), jax.ShapeDtypeStruct((m, d), BF16)]
    return pl.pallas_call(
        functools.partial(_moe_combine_kernel, final=final),
        grid=(m // MOE_TT,),
        in_specs=in_specs,
        out_specs=out_specs,
        out_shape=out_shape,
        scratch_shapes=[pltpu.VMEM((2, MOE_TT, d // 2), jnp.uint32), pltpu.SemaphoreType.DMA(())],
        compiler_params=_params("arbitrary"),
        name="moe_combine",
    )(*args)


def kernel(x, c, ada_w, ada_b, norm_mix_g, norm_ffn_g, final_norm_g, w_in, ssm_a_re, ssm_a_im, ssm_log_dt, ssm_b_re, ssm_b_im, ssm_c_re, ssm_c_im, ssm_d, w_glu, b_glu, w_branch, w_out, ffn_w1, ffn_w3, ffn_w2, router_w, router_b, moe_w1, moe_w3, moe_w2):
    batch, seq, d = x.shape
    depth = ada_w.shape[0]
    tokens = batch * seq
    ssm_width = w_glu.shape[1]
    per_res = seq // TOKEN_RESIDUES
    tn_in = ATTN_OUT
    assert ssm_width == tn_in and w_in.shape[-1] == (3 * N_GROUPS + 1) * tn_in + 2 * d

    def to_residue_major(t):
        f = t.shape[-1]
        return t.reshape(batch, per_res, TOKEN_RESIDUES, f).transpose(0, 2, 1, 3).reshape(batch, seq, f)

    mod = _ada_modulation(c, ada_w, ada_b).reshape(depth, batch, 6, d)
    x2 = to_residue_major(x).reshape(tokens, d)
    h = None
    out = None
    for layer in range(depth):
        sh1, sc1, g1, sh2, sc2, g2 = (mod[layer, :, i] for i in range(6))

        if h is None:
            h = _norm_mod(x2.reshape(batch, seq, d), norm_mix_g[layer], sh1, sc1)
        qkv0 = _mm_plain(h, w_in, (layer,), lambda j: N_GROUPS * j, 3, F32, *W_IN_TILE_F32)
        rest = _mm_plain(h, w_in, (layer,),
                         lambda j: j + 1 + jnp.where(j >= 2, 1, 0) + jnp.where(j >= 4, 1, 0),
                         11, BF16, *W_IN_TILE)
        outs, lses = _attention_groups(qkv0, rest, batch, seq)
        tables = _ssm_tables(ssm_a_re[layer], ssm_a_im[layer], ssm_log_dt[layer],
                             ssm_b_re[layer], ssm_b_im[layer], ssm_c_re[layer], ssm_c_im[layer])
        ssm = _s5_ssm(rest, 6, batch, seq, tables, ssm_d[layer], w_glu[layer], b_glu[layer])
        merged = _mm_branch(outs, lses, ssm, w_branch, layer, rest, 7, BRANCH_ROWS)
        h = None

        li = layer // 2
        if layer % 2 == 0:
            x2, hf = _mm_resid_norm(merged, w_out, layer, x2, g1, norm_ffn_g[layer], sh2, sc2,
                                    seq, W_OUT_NORM_ROWS)
            act = _mm_swiglu(hf, ffn_w1, ffn_w3, (li,), *SWIGLU_TILE)
            x2 = _mm_resid(act, ffn_w2, (li,), x2, g2, seq, *FFN_DOWN_TILE)
        else:
            x2 = _mm_resid(merged, w_out, (layer,), x2, g1, seq, *W_OUT_TILE)
            x3 = x2.reshape(batch, seq, d)
            sel, wts, cnt = _router(x3, norm_ffn_g[layer], sh2, sc2, router_w[li], router_b[li])
            plan = _moe_plan(sel, cnt, tokens)
            xs = _moe_dispatch(x3, norm_ffn_g[layer], sh2, sc2, plan)
            act = _moe_mm(xs, (moe_w1, moe_w3), li, plan, MOE_UP_TN, "up")
            y = _moe_mm(act, (moe_w2,), li, plan, MOE_DOWN_TN, "down")
            if layer + 1 < depth:
                x2, h = _moe_combine(x2, g2, wts, y, plan[0], seq, norm_mix_g[layer + 1],
                                     mod[layer + 1, :, 0], mod[layer + 1, :, 1])
            else:
                out = _moe_combine(x2, g2, wts, y, plan[0], seq, final_norm_g)
    if out is None:
        out = _final_norm(x2.reshape(batch, seq, d), final_norm_g)
    return out.reshape(batch, TOKEN_RESIDUES, per_res, d).transpose(0, 2, 1, 3).reshape(batch, seq, d)
```
